```python
import jax, jax.numpy as jnp
from jax import lax
import numpy as np

D_MODEL = 1024
BATCH = 4
SEQ = 8192
DEPTH = 2

HEAD_DIM = 64
ROPE_THETA = 500000.0
ROT_DIM = HEAD_DIM // 4
GRID_W = 64
EPS = 1e-6
NEG_INF = -1e30
QBLK = 128

A_HEADS = 3 * D_MODEL // (4 * HEAD_DIM)
A_WIDTH = A_HEADS * HEAD_DIM
A_PATTERNS = ((128, 1), (512, 4), (2048, 16))
B_GROUP_DIM = 64
B_GROUPS = D_MODEL // (4 * B_GROUP_DIM)
B_WIDTH = B_GROUPS * B_GROUP_DIM
B_CHUNK = 128
EVEN_IN = 3 * A_WIDTH + 2 * B_WIDTH
C_HEADS = D_MODEL // (2 * HEAD_DIM)
C_Q_RANK = D_MODEL // 4
C_KV_RANK = D_MODEL // 8
C_NOPE = HEAD_DIM
C_ROPE = HEAD_DIM // 2
C_V = HEAD_DIM
D_HEADS = D_MODEL // (2 * HEAD_DIM)
D_KV_HEADS = 2
D_GROUP = D_HEADS // D_KV_HEADS
D_THETA = 10000.0
ODD_IN = C_Q_RANK + C_KV_RANK + C_ROPE + (D_HEADS + 2 * D_KV_HEADS) * HEAD_DIM
N_EXPERTS = 16
EC_FACTOR = 2
EXPERT_FF = 512

kernel_name = "hybrid_dilated_gmlp_mla_axialgqa_ecmoe_encoder"


def rmsnorm(x, g):
    xf = x.astype(jnp.float32)
    y = xf * lax.rsqrt(jnp.mean(xf * xf, axis=-1, keepdims=True) + EPS)
    return (y * g.astype(jnp.float32)).astype(x.dtype)


def rope(x, pos, theta):
    r = x.shape[-1]
    half = r // 2
    inv = jnp.power(jnp.float32(theta), -jnp.arange(half, dtype=jnp.float32) * (2.0 / r))
    ang = pos.astype(jnp.float32)[:, None] * inv[None, :]
    shape = (pos.shape[0],) + (1,) * (x.ndim - 3) + (half,)
    cos = jnp.cos(ang).reshape(shape)
    sin = jnp.sin(ang).reshape(shape)
    xf = x.astype(jnp.float32)
    x1, x2 = xf[..., :half], xf[..., half:]
    return jnp.concatenate([x1 * cos - x2 * sin, x1 * sin + x2 * cos], axis=-1).astype(x.dtype)


def partial_rope(x, pos):
    return jnp.concatenate([rope(x[..., :ROT_DIM], pos, ROPE_THETA), x[..., ROT_DIM:]], axis=-1)


def axial_rope(x, row, col):
    half = x.shape[-1] // 2
    return jnp.concatenate([rope(x[..., :half], row, D_THETA), rope(x[..., half:], col, D_THETA)], axis=-1)


def dilated_band_attention(q, k, v, window, dilation):
    b, s, h, dh = q.shape
    r = window // (2 * dilation)
    blk = r
    L = s // dilation
    nb = -(-L // blk)
    lp = nb * blk

    def residue(t):
        return t.reshape(b, L, dilation, h, dh).transpose(0, 2, 1, 3, 4)

    qr = jnp.pad(residue(q), ((0, 0), (0, 0), (0, lp - L), (0, 0), (0, 0)))
    qr = qr.reshape(b, dilation, nb, blk, h, dh)

    def band(t):
        tp = jnp.pad(residue(t), ((0, 0), (0, 0), (blk, lp - L + blk), (0, 0), (0, 0)))
        tp = tp.reshape(b, dilation, nb + 2, blk, h, dh)
        return jnp.concatenate([tp[:, :, 0:nb], tp[:, :, 1:nb + 1], tp[:, :, 2:nb + 2]], axis=3)

    kb, vb = band(k), band(v)
    scores = jnp.einsum('bdnqhc,bdnkhc->bdnhqk', qr, kb).astype(jnp.float32) * (dh ** -0.5)
    qi = jnp.arange(blk)[:, None]
    ki = jnp.arange(3 * blk)[None, :]
    in_band = jnp.abs(ki - blk - qi) <= r
    kpos = jnp.arange(nb)[:, None] * blk - blk + jnp.arange(3 * blk)[None, :]
    valid = in_band[None] & ((kpos >= 0) & (kpos < L))[:, None, :]
    scores = jnp.where(valid[None, None, :, None], scores, NEG_INF)
    lse = jax.nn.logsumexp(scores, axis=-1)
    p = jnp.exp(scores - lse[..., None])
    out = jnp.einsum('bdnhqk,bdnkhc->bdnqhc', p.astype(v.dtype), vb)
    out = out.reshape(b, dilation, lp, h, dh)[:, :, :L].transpose(0, 2, 1, 3, 4).reshape(b, s, h, dh)
    lse = lse.transpose(0, 1, 2, 4, 3).reshape(b, dilation, lp, h)[:, :, :L]
    lse = lse.transpose(0, 2, 1, 3).reshape(b, s, h)
    return out, lse


def dilated_mixture(q, k, v):
    results = [dilated_band_attention(q, k, v, w, d) for (w, d) in A_PATTERNS]
    outs = jnp.stack([o for (o, _) in results], axis=0)
    lses = jnp.stack([l for (_, l) in results], axis=0)
    wts = jax.nn.softmax(lses, axis=0)
    return jnp.einsum('pbsh,pbshc->bshc', wts.astype(q.dtype), outs)


def spatial_gating(z, v_norm, w_s, b_s):
    b, s, _ = z.shape
    z = jax.nn.gelu(z)
    u, vv = jnp.split(z, 2, axis=-1)
    u = u.reshape(b, s, B_GROUPS, B_GROUP_DIM)
    vv = rmsnorm(vv.reshape(b, s, B_GROUPS, B_GROUP_DIM), v_norm)
    vv = vv.reshape(b, s // B_CHUNK, B_CHUNK, B_GROUPS, B_GROUP_DIM)
    mixed = jnp.einsum('gpq,bnqgc->bnpgc', w_s, vv) + b_s.T[None, None, :, :, None]
    return (u * mixed.reshape(b, s, B_GROUPS, B_GROUP_DIM)).reshape(b, s, B_WIDTH)


def block_attention(q, k, v):
    b, s, kvh, g, dk = q.shape
    nq = s // QBLK
    scale = dk ** -0.5
    qb = q.reshape(b, nq, QBLK, kvh, g, dk).transpose(1, 0, 2, 3, 4, 5)

    def one(qblk):
        sc = jnp.einsum('bqkgc,bskc->bkgqs', qblk, k).astype(jnp.float32) * scale
        p = jax.nn.softmax(sc, axis=-1)
        return jnp.einsum('bkgqs,bskc->bqkgc', p.astype(v.dtype), v)

    out = lax.map(one, qb)
    return out.transpose(1, 0, 2, 3, 4, 5).reshape(b, s, kvh * g * v.shape[-1])


def mixer_even(h, pos, w_in, gmlp_norm, w_s, b_s, w_out):
    b, s, _ = h.shape
    proj = h @ w_in
    q, k, v, z = jnp.split(proj, [A_WIDTH, 2 * A_WIDTH, 3 * A_WIDTH], axis=-1)
    q = partial_rope(q.reshape(b, s, A_HEADS, HEAD_DIM), pos)
    k = partial_rope(k.reshape(b, s, A_HEADS, HEAD_DIM), pos)
    v = v.reshape(b, s, A_HEADS, HEAD_DIM)
    a_out = dilated_mixture(q, k, v).reshape(b, s, A_WIDTH)
    g_out = spatial_gating(z, gmlp_norm, w_s, b_s)
    return jnp.concatenate([a_out, g_out], axis=-1) @ w_out


def mixer_odd(h, pos, row, col, w_in, cq_norm, w_cq_up, ckv_norm, w_ckv_up, dq_norm, dk_norm, w_out):
    b, s, _ = h.shape
    proj = h @ w_in
    o1 = C_Q_RANK
    o2 = o1 + C_KV_RANK
    o3 = o2 + C_ROPE
    o4 = o3 + D_HEADS * HEAD_DIM
    o5 = o4 + D_KV_HEADS * HEAD_DIM
    cq, ckv, ck_rope, dq, dk, dv = jnp.split(proj, [o1, o2, o3, o4, o5], axis=-1)
    cq = (rmsnorm(cq, cq_norm) @ w_cq_up).reshape(b, s, C_HEADS, C_NOPE + C_ROPE)
    q_nope, q_rope = cq[..., :C_NOPE], rope(cq[..., C_NOPE:], pos, ROPE_THETA)
    kv = (rmsnorm(ckv, ckv_norm) @ w_ckv_up).reshape(b, s, C_HEADS, C_NOPE + C_V)
    k_nope, v_c = kv[..., :C_NOPE], kv[..., C_NOPE:]
    k_rope = rope(ck_rope, pos, ROPE_THETA)
    q_c = jnp.concatenate([q_nope, q_rope], axis=-1)[:, :, :, None]
    k_c = jnp.concatenate([k_nope, jnp.broadcast_to(k_rope[:, :, None], (b, s, C_HEADS, C_ROPE))], axis=-1)
    c_out = block_attention(q_c, k_c, v_c)
    dq = axial_rope(rmsnorm(dq.reshape(b, s, D_KV_HEADS, D_GROUP, HEAD_DIM), dq_norm), row, col)
    dk = axial_rope(rmsnorm(dk.reshape(b, s, D_KV_HEADS, HEAD_DIM), dk_norm), row, col)
    dv = dv.reshape(b, s, D_KV_HEADS, HEAD_DIM)
    d_out = block_attention(dq, dk, dv)
    return jnp.concatenate([c_out, d_out], axis=-1) @ w_out


def ec_moe(h, w_router, w_gate, w_up, w_down):
    b, s, d = h.shape
    cap = EC_FACTOR * s // N_EXPERTS
    logits = jnp.einsum('bsd,de->bse', h, w_router).astype(jnp.float32)
    aff = jax.nn.softmax(logits, axis=-1)
    gates, idx = lax.top_k(aff.transpose(0, 2, 1), cap)
    xs = jax.vmap(lambda hb, ib: hb[ib])(h, idx)
    g = jnp.einsum('becd,edf->becf', xs, w_gate)
    u = jnp.einsum('becd,edf->becf', xs, w_up)
    y = jnp.einsum('becf,efd->becd', jax.nn.silu(g) * u, w_down)
    y = y * gates[..., None].astype(y.dtype)
    return jax.vmap(lambda ib, yb: jnp.zeros((s, d), yb.dtype).at[ib.reshape(-1)].add(yb.reshape(-1, d)))(idx, y)


def setup_inputs(seed: int = 0) -> dict:
    key = jax.random.key(seed)
    ks = jax.random.split(key, 21)
    n_even = (DEPTH + 1) // 2
    n_odd = DEPTH // 2

    def normal(k, shape, scale):
        return jax.random.normal(k, shape, jnp.float32) * scale

    def gain(k, shape):
        return 1.0 + 0.02 * jax.random.normal(k, shape, jnp.float32)

    return {
        "x": normal(ks[0], (BATCH, SEQ, D_MODEL), 1.0),
        "norm_mix": gain(ks[1], (DEPTH, D_MODEL)),
        "norm_ffn": gain(ks[2], (DEPTH, D_MODEL)),
        "even_w_in": normal(ks[3], (n_even, D_MODEL, EVEN_IN), D_MODEL ** -0.5),
        "even_gmlp_norm": gain(ks[4], (n_even, B_GROUPS, B_GROUP_DIM)),
        "even_w_spatial": normal(ks[5], (n_even, B_GROUPS, B_CHUNK, B_CHUNK), B_CHUNK ** -0.5),
        "even_b_spatial": normal(ks[6], (n_even, B_GROUPS, B_CHUNK), 0.02),
        "even_w_out": normal(ks[7], (n_even, A_WIDTH + B_WIDTH, D_MODEL), (A_WIDTH + B_WIDTH) ** -0.5),
        "odd_w_in": normal(ks[8], (n_odd, D_MODEL, ODD_IN), D_MODEL ** -0.5),
        "odd_cq_norm": gain(ks[9], (n_odd, C_Q_RANK)),
        "odd_w_cq_up": normal(ks[10], (n_odd, C_Q_RANK, C_HEADS * (C_NOPE + C_ROPE)), C_Q_RANK ** -0.5),
        "odd_ckv_norm": gain(ks[11], (n_odd, C_KV_RANK)),
        "odd_w_ckv_up": normal(ks[12], (n_odd, C_KV_RANK, C_HEADS * (C_NOPE + C_V)), C_KV_RANK ** -0.5),
        "odd_dq_norm": gain(ks[13], (n_odd, HEAD_DIM)),
        "odd_dk_norm": gain(ks[14], (n_odd, HEAD_DIM)),
        "odd_w_out": normal(ks[15], (n_odd, C_HEADS * C_V + D_HEADS * HEAD_DIM, D_MODEL), (C_HEADS * C_V + D_HEADS * HEAD_DIM) ** -0.5),
        "moe_w_router": normal(ks[16], (DEPTH, D_MODEL, N_EXPERTS), D_MODEL ** -0.5),
        "moe_w_gate": normal(ks[17], (DEPTH, N_EXPERTS, D_MODEL, EXPERT_FF), D_MODEL ** -0.5),
        "moe_w_up": normal(ks[18], (DEPTH, N_EXPERTS, D_MODEL, EXPERT_FF), D_MODEL ** -0.5),
        "moe_w_down": normal(ks[19], (DEPTH, N_EXPERTS, EXPERT_FF, D_MODEL), EXPERT_FF ** -0.5),
        "final_norm": gain(ks[20], (D_MODEL,)),
    }


def reference(x, norm_mix, norm_ffn, even_w_in, even_gmlp_norm, even_w_spatial, even_b_spatial, even_w_out,
              odd_w_in, odd_cq_norm, odd_w_cq_up, odd_ckv_norm, odd_w_ckv_up, odd_dq_norm, odd_dk_norm, odd_w_out,
              moe_w_router, moe_w_gate, moe_w_up, moe_w_down, final_norm):
    b, s, _ = x.shape
    pos = jnp.arange(s, dtype=jnp.int32)
    rows = s // GRID_W
    row = jnp.repeat(jnp.arange(rows, dtype=jnp.int32), GRID_W)
    col = jnp.tile(jnp.arange(GRID_W, dtype=jnp.int32), rows)
    for i in range(DEPTH):
        j = i // 2
        h = rmsnorm(x, norm_mix[i])
        if i % 2 == 0:
            mix = mixer_even(h, pos, even_w_in[j], even_gmlp_norm[j], even_w_spatial[j], even_b_spatial[j], even_w_out[j])
        else:
            mix = mixer_odd(h, pos, row, col, odd_w_in[j], odd_cq_norm[j], odd_w_cq_up[j], odd_ckv_norm[j],
                            odd_w_ckv_up[j], odd_dq_norm[j], odd_dk_norm[j], odd_w_out[j])
        x = x + mix
        x = x + ec_moe(rmsnorm(x, norm_ffn[i]), moe_w_router[i], moe_w_gate[i], moe_w_up[i], moe_w_down[i])
    return rmsnorm(x, final_norm)
```

```python
import functools
import math

import jax
import jax.numpy as jnp
from jax import lax
from jax.experimental import pallas as pl
from jax.experimental.pallas import tpu as pltpu

F32 = jnp.float32
BF16 = jnp.bfloat16
I32 = jnp.int32

EPS = 1e-6
NEG = -1e30
LOG2E = 1.4426950408889634

D_MODEL = 1024
HEAD_DIM = 64
ROPE_THETA = 500000.0
ROT_DIM = 16
GRID_W = 64
A_HEADS = 12
A_WIDTH = 768
A_DILATIONS = (1, 4, 16)
A_RADIUS = 64
B_WIDTH = 256
B_GROUPS = 4
B_CHUNK = 128
C_HEADS = 8
C_Q_RANK = 256
C_KV_RANK = 128
C_NOPE = 64
C_ROPE = 32
D_HEADS = 8
D_KV_HEADS = 2
D_THETA = 10000.0
N_EXPERTS = 16
EC_FACTOR = 2
EXPERT_FF = 512

LANES = 128
VMEM_LIMIT = 48 * 1024 * 1024

_NT = (((1,), (1,)), ((), ()))


def _cparams(sem):
    return pltpu.CompilerParams(dimension_semantics=sem, vmem_limit_bytes=VMEM_LIMIT)


def _rms_scale(x):
    return lax.rsqrt(jnp.mean(x * x, axis=-1, keepdims=True) + EPS)


def _rope3(a, c_ref, s1_ref, s2_ref, shift):
    return (a * c_ref[...] + pltpu.roll(a, LANES - shift, 1) * s1_ref[...]
            + pltpu.roll(a, shift, 1) * s2_ref[...])


def _split_dot(x, w_bf16):
    hi = x.astype(BF16)
    lo = (x - hi.astype(F32)).astype(BF16)
    return (jnp.dot(hi, w_bf16, preferred_element_type=F32)
            + jnp.dot(lo, w_bf16, preferred_element_type=F32))


def _proj_even_kernel(x_ref, g_ref, w_ref, cq_ref, s1q_ref, s2q_ref, ck_ref, s1k_ref, s2k_ref,
                      gn_ref, gmat_ref, ws_ref, bs_ref,
                      q_ref, k_ref, v_ref, go_ref):
    x = x_ref[...]
    y = (x * _rms_scale(x) * g_ref[...]).astype(BF16)
    tm = x.shape[0]

    aq = jnp.dot(y, w_ref[:, 0:A_WIDTH], preferred_element_type=F32)
    for j in range(A_WIDTH // LANES):
        sl = slice(j * LANES, (j + 1) * LANES)
        q_ref[:, sl] = _rope3(aq[:, sl], cq_ref, s1q_ref, s2q_ref, ROT_DIM // 2)
    ak = jnp.dot(y, w_ref[:, A_WIDTH:2 * A_WIDTH], preferred_element_type=F32)
    for j in range(A_WIDTH // LANES):
        sl = slice(j * LANES, (j + 1) * LANES)
        k_ref[:, sl] = _rope3(ak[:, sl], ck_ref, s1k_ref, s2k_ref, ROT_DIM // 2)
    v_ref[...] = jnp.dot(y, w_ref[:, 2 * A_WIDTH:3 * A_WIDTH], preferred_element_type=F32)

    z = jnp.dot(y, w_ref[:, 3 * A_WIDTH:3 * A_WIDTH + 2 * B_WIDTH], preferred_element_type=F32)
    ge = jax.nn.gelu(z)
    u = ge[:, :B_WIDTH]
    vv = ge[:, B_WIDTH:]
    ss = _split_dot(vv * vv, gmat_ref[...])
    vn = (vv * lax.rsqrt(ss + EPS) * gn_ref[...]).astype(BF16)
    grp = lax.broadcasted_iota(I32, (B_CHUNK, B_WIDTH), 1) // (B_WIDTH // B_GROUPS)
    for c in range(tm // B_CHUNK):
        rows = slice(c * B_CHUNK, (c + 1) * B_CHUNK)
        vc = vn[rows]
        mg = [jnp.dot(ws_ref[g], vc, preferred_element_type=F32) for g in range(B_GROUPS)]
        mixed = jnp.where(grp == 0, mg[0], jnp.where(grp == 1, mg[1], jnp.where(grp == 2, mg[2], mg[3])))
        go_ref[rows, :] = (u[rows] * (mixed + bs_ref[...])).astype(BF16)


def _rope_tables(pos, theta, r, lane_off, period, scale):
    half = r // 2
    inv = jnp.power(jnp.float32(theta), -jnp.arange(half, dtype=F32) * (2.0 / r))
    ang = pos.astype(F32)[:, None] * inv[None, :]
    cos, sin = jnp.cos(ang), jnp.sin(ang)
    o = (jnp.arange(LANES) % period) - lane_off
    in_lo = (o >= 0) & (o < half)
    in_hi = (o >= half) & (o < r)
    idx = jnp.clip(jnp.where(in_hi, o - half, o), 0, half - 1)
    c = jnp.where((in_lo | in_hi)[None, :], cos[:, idx], 1.0)
    s1 = jnp.where(in_lo[None, :], -sin[:, idx], 0.0)
    s2 = jnp.where(in_hi[None, :], sin[:, idx], 0.0)
    return c * scale, s1 * scale, s2 * scale


def _proj_even(x2d, seq, g_mix, w_in, gmlp_norm, w_s, b_s, tm=512):
    t = x2d.shape[0]
    nblk = seq // tm
    pos = jnp.arange(seq, dtype=I32)
    qscale = HEAD_DIM ** -0.5 * LOG2E
    cq, s1q, s2q = _rope_tables(pos, ROPE_THETA, ROT_DIM, 0, HEAD_DIM, qscale)
    ck, s1k, s2k = _rope_tables(pos, ROPE_THETA, ROT_DIM, 0, HEAD_DIM, 1.0)
    gdim = B_WIDTH // B_GROUPS
    gid = jnp.arange(B_WIDTH) // gdim
    gmat = jnp.where(gid[:, None] == gid[None, :], 1.0 / gdim, 0.0).astype(BF16)
    bias = jnp.repeat(b_s.T, gdim, axis=1)
    row = lambda i: (i, 0)
    tab = lambda i: (i % nblk, 0)
    full = lambda i: (0, 0)
    tspec = pl.BlockSpec((tm, LANES), tab)
    return pl.pallas_call(
        _proj_even_kernel,
        grid=(t // tm,),
        in_specs=[
            pl.BlockSpec((tm, D_MODEL), row),
            pl.BlockSpec((1, D_MODEL), full),
            pl.BlockSpec(w_in.shape, full),
            tspec, tspec, tspec, tspec, tspec, tspec,
            pl.BlockSpec((1, B_WIDTH), full),
            pl.BlockSpec((B_WIDTH, B_WIDTH), full),
            pl.BlockSpec((B_GROUPS, B_CHUNK, B_CHUNK), lambda i: (0, 0, 0)),
            pl.BlockSpec((B_CHUNK, B_WIDTH), full),
        ],
        out_specs=[
            pl.BlockSpec((tm, A_WIDTH), row),
            pl.BlockSpec((tm, A_WIDTH), row),
            pl.BlockSpec((tm, A_WIDTH), row),
            pl.BlockSpec((tm, B_WIDTH), row),
        ],
        out_shape=[
            jax.ShapeDtypeStruct((t, A_WIDTH), F32),
            jax.ShapeDtypeStruct((t, A_WIDTH), F32),
            jax.ShapeDtypeStruct((t, A_WIDTH), F32),
            jax.ShapeDtypeStruct((t, B_WIDTH), BF16),
        ],
        compiler_params=_cparams(("parallel",)),
        name="proj_even",
    )(x2d, g_mix.reshape(1, D_MODEL), w_in.astype(BF16), cq, s1q, s2q, ck, s1k, s2k,
      gmlp_norm.reshape(1, B_WIDTH), gmat, w_s.astype(BF16), bias)


_TQ = 128
_TK = _TQ + 2 * A_RADIUS


def _dilated_kernel(q_ref, k_ref, v_ref, o_ref, m_sc, l_sc, *, seq):
    half0 = lax.broadcasted_iota(I32, (_TQ, LANES), 1) < HEAD_DIM
    diff = lax.broadcasted_iota(I32, (_TQ, _TK), 1) - lax.broadcasted_iota(I32, (_TQ, _TK), 0)
    n_pat = len(A_DILATIONS)
    for pi, d in enumerate(A_DILATIONS):
        cls_len = seq // d
        tpc = cls_len // _TQ

        def body(j, carry, d=d, cls_len=cls_len, tpc=tpc, pi=pi):
            i = j // tpc
            n = j % tpc
            l0 = n * _TQ
            kst = jnp.clip(l0 - A_RADIUS, 0, cls_len - _TK)
            off = l0 - kst
            if d == 1:
                qrows = pl.ds(pl.multiple_of(l0, _TQ), _TQ)
                krows = pl.ds(pl.multiple_of(kst, A_RADIUS), _TK)
            else:
                qrows = pl.ds(l0 * d + i, _TQ, stride=d)
                krows = pl.ds(kst * d + i, _TK, stride=d)
            q = q_ref[qrows, :]
            kb = k_ref[krows, :].astype(BF16)
            vb = v_ref[krows, :].astype(BF16)
            ok = jnp.abs(diff - off) <= A_RADIUS
            parts = []
            for h in range(2):
                qh = jnp.where(half0 if h == 0 else jnp.logical_not(half0), q, 0.0).astype(BF16)
                s = lax.dot_general(qh, kb, _NT, preferred_element_type=F32)
                s = jnp.where(ok, s, NEG)
                mt = jnp.max(s, axis=-1, keepdims=True)
                p = jnp.exp2(s - mt)
                lt = jnp.sum(p, axis=-1, keepdims=True)
                ot = jnp.dot(p.astype(BF16), vb, preferred_element_type=F32)
                parts.append((mt, lt, ot))
            mt = jnp.where(half0, parts[0][0], parts[1][0])
            lt = jnp.where(half0, parts[0][1], parts[1][1])
            ot = jnp.where(half0, parts[0][2], parts[1][2])
            if pi > 0:
                mp = m_sc[qrows, :]
                mn = jnp.maximum(mp, mt)
                a = jnp.exp2(mp - mn)
                b = jnp.exp2(mt - mn)
                lt = a * l_sc[qrows, :] + b * lt
                ot = a * o_ref[qrows, :] + b * ot
                mt = mn
            if pi == n_pat - 1:
                o_ref[qrows, :] = ot / lt
            else:
                m_sc[qrows, :] = mt
                l_sc[qrows, :] = lt
                o_ref[qrows, :] = ot
            return carry

        lax.fori_loop(0, seq // _TQ, body, 0)


def _dilated(q, k, v):
    b, s, w = q.shape
    spec = pl.BlockSpec((None, s, LANES), lambda bi, hi: (bi, 0, hi))
    return pl.pallas_call(
        functools.partial(_dilated_kernel, seq=s),
        grid=(b, w // LANES),
        in_specs=[spec, spec, spec],
        out_specs=spec,
        out_shape=jax.ShapeDtypeStruct((b, s, w), F32),
        scratch_shapes=[pltpu.VMEM((s, LANES), F32), pltpu.VMEM((s, LANES), F32)],
        compiler_params=_cparams(("parallel", "parallel")),
        name="dilated",
    )(q, k, v)


def _outproj_kernel(x_ref, a_ref, b_ref, wa_ref, wb_ref, gf_ref, wr_ref, x1_ref, h2_ref, aff_ref):
    x1 = (x_ref[...]
          + jnp.dot(a_ref[...].astype(BF16), wa_ref[...], preferred_element_type=F32)
          + jnp.dot(b_ref[...].astype(BF16), wb_ref[...], preferred_element_type=F32))
    x1_ref[...] = x1
    h2 = x1 * _rms_scale(x1) * gf_ref[...]
    h2_ref[...] = h2
    w_hi = wr_ref[0]
    w_lo = wr_ref[1]
    hi = h2.astype(BF16)
    lo = (h2 - hi.astype(F32)).astype(BF16)
    lg = (jnp.dot(hi, w_hi, preferred_element_type=F32) + jnp.dot(lo, w_hi, preferred_element_type=F32)
          + jnp.dot(hi, w_lo, preferred_element_type=F32))
    valid = lax.broadcasted_iota(I32, lg.shape, 1) < N_EXPERTS
    lg = jnp.where(valid, lg, NEG)
    e = jnp.exp(lg - jnp.max(lg, axis=-1, keepdims=True))
    aff = e / jnp.sum(e, axis=-1, keepdims=True)
    aff_t = aff.T
    for j in range(aff.shape[0] // LANES):
        aff_ref[j] = aff_t[:N_EXPERTS, j * LANES:(j + 1) * LANES]


def _outproj(x2d, a, b, wa, wb, g_ffn, w_router, tm=512):
    t = x2d.shape[0]
    wr = jnp.pad(w_router, ((0, 0), (0, LANES - N_EXPERTS)))
    wr_hi = wr.astype(BF16)
    wr_lo = (wr - wr_hi.astype(F32)).astype(BF16)
    wr2 = jnp.stack([wr_hi, wr_lo])
    row = lambda i: (i, 0)
    full = lambda i: (0, 0)
    return pl.pallas_call(
        _outproj_kernel,
        grid=(t // tm,),
        in_specs=[
            pl.BlockSpec((tm, D_MODEL), row),
            pl.BlockSpec((tm, a.shape[1]), row),
            pl.BlockSpec((tm, b.shape[1]), row),
            pl.BlockSpec(wa.shape, full),
            pl.BlockSpec(wb.shape, full),
            pl.BlockSpec((1, D_MODEL), full),
            pl.BlockSpec((2, D_MODEL, LANES), lambda i: (0, 0, 0)),
        ],
        out_specs=[
            pl.BlockSpec((tm, D_MODEL), row),
            pl.BlockSpec((tm, D_MODEL), row),
            pl.BlockSpec((tm // LANES, N_EXPERTS, LANES), lambda i: (i, 0, 0)),
        ],
        out_shape=[
            jax.ShapeDtypeStruct((t, D_MODEL), F32),
            jax.ShapeDtypeStruct((t, D_MODEL), F32),
            jax.ShapeDtypeStruct((t // LANES, N_EXPERTS, LANES), F32),
        ],
        compiler_params=_cparams(("parallel",)),
        name="outproj",
    )(x2d, a, b, wa.astype(BF16), wb.astype(BF16), g_ffn.reshape(1, D_MODEL), wr2)


def _route_kernel(aff_ref, idx_ref, gate_ref, thr_sc, need_sc, *, cap):
    nblk = aff_ref.shape[0] // N_EXPERTS
    bits = pltpu.bitcast(aff_ref[...], I32).reshape(nblk, N_EXPERTS, LANES)

    def count(pred):
        return jnp.sum(jnp.sum(jnp.where(pred, 1.0, 0.0), axis=0), axis=1, keepdims=True)

    def search(it, thr):
        cand = thr | jnp.left_shift(jnp.int32(1), 30 - it)
        return jnp.where(count(bits >= cand[None]) >= cap, cand, thr)

    thr = lax.fori_loop(0, 31, search, jnp.zeros((N_EXPERTS, 1), I32))
    need = cap - count(bits > thr[None])
    thr_sc[...] = jnp.broadcast_to(thr, (N_EXPERTS, LANES))
    need_sc[...] = jnp.broadcast_to(need, (N_EXPERTS, LANES))

    ri = lax.broadcasted_iota(I32, (LANES, LANES), 0)
    ci = lax.broadcasted_iota(I32, (LANES, LANES), 1)
    upper = jnp.where(ri <= ci, 1.0, 0.0).astype(BF16)
    lower = jnp.where(ci <= ri, 1.0, 0.0).astype(BF16)
    eye = jnp.where(ri == ci, 1.0, 0.0).astype(BF16)
    ones = jnp.ones((LANES, LANES), BF16)
    bi = lax.broadcasted_iota(I32, (nblk, nblk), 0)
    bj = lax.broadcasted_iota(I32, (nblk, nblk), 1)
    strict = jnp.where(bj < bi, 1.0, 0.0).astype(BF16)
    c_row = lax.broadcasted_iota(I32, (1, cap), 1).astype(F32)
    blk_iota = lax.broadcasted_iota(I32, (nblk, cap), 0).astype(F32)
    t_iota = lax.broadcasted_iota(I32, (LANES, cap), 0).astype(F32)
    rep = cap // LANES

    def cums(mask_bf16):
        lp = jnp.dot(mask_bf16, upper, preferred_element_type=F32)
        bc = jnp.dot(mask_bf16, ones, preferred_element_type=F32)
        bst = jnp.dot(strict, bc.astype(BF16), preferred_element_type=F32)
        return lp, bc, bst

    def per_expert(e, carry):
        a = aff_ref[pl.ds(e, nblk, stride=N_EXPERTS), :]
        ab = pltpu.bitcast(a, I32)
        thr_e = thr_sc[pl.ds(e, 1), :]
        need_e = need_sc[pl.ds(e, 1), :]
        gt = ab > thr_e
        eq = ab == thr_e
        eqf = jnp.where(eq, 1.0, 0.0)
        lp_q, _, bst_q = cums(eqf.astype(BF16))
        sel = jnp.logical_or(gt, jnp.logical_and(eq, bst_q + lp_q - eqf < need_e))
        mb = jnp.where(sel, 1.0, 0.0).astype(BF16)
        _, bc, bst = cums(mb)
        bend_w = jnp.tile(bst + bc, (1, rep))
        bst_w = jnp.tile(bst, (1, rep))
        blk_c = jnp.sum(jnp.where(bend_w <= c_row, 1.0, 0.0), axis=0, keepdims=True)
        onehot = blk_iota == blk_c
        bst_c = jnp.sum(jnp.where(onehot, bst_w, 0.0), axis=0, keepdims=True)
        r_c = c_row - bst_c
        ohb = jnp.where(onehot, 1.0, 0.0).astype(BF16)
        lp_t = lax.dot_general(lower, mb, _NT, preferred_element_type=F32)
        lp_c = jnp.dot(lp_t.astype(BF16), ohb, preferred_element_type=F32)
        tl_c = jnp.sum(jnp.where(lp_c <= r_c, 1.0, 0.0), axis=0, keepdims=True)
        idx_ref[pl.ds(e, 1), :] = (blk_c * LANES + tl_c).astype(I32)
        a_hi = a.astype(BF16)
        a_lo = (a - a_hi.astype(F32)).astype(BF16)
        at_hi = lax.dot_general(eye, a_hi, _NT, preferred_element_type=F32).astype(BF16)
        at_lo = lax.dot_general(eye, a_lo, _NT, preferred_element_type=F32).astype(BF16)
        g_c = (jnp.dot(at_hi, ohb, preferred_element_type=F32)
               + jnp.dot(at_lo, ohb, preferred_element_type=F32))
        gate_ref[pl.ds(e, 1), :] = jnp.sum(jnp.where(t_iota == tl_c, g_c, 0.0), axis=0, keepdims=True)
        return carry

    lax.fori_loop(0, N_EXPERTS, per_expert, 0)


def _route(aff2d, batch, seq):
    cap = EC_FACTOR * seq // N_EXPERTS
    rows = (seq // LANES) * N_EXPERTS
    out_spec = pl.BlockSpec((None, N_EXPERTS, cap), lambda b: (b, 0, 0))
    return pl.pallas_call(
        functools.partial(_route_kernel, cap=cap),
        grid=(batch,),
        in_specs=[pl.BlockSpec((rows, LANES), lambda b: (b, 0))],
        out_specs=[out_spec, out_spec],
        out_shape=[jax.ShapeDtypeStruct((batch, N_EXPERTS, cap), I32),
                   jax.ShapeDtypeStruct((batch, N_EXPERTS, cap), F32)],
        scratch_shapes=[pltpu.VMEM((N_EXPERTS, LANES), I32), pltpu.VMEM((N_EXPERTS, LANES), F32)],
        compiler_params=_cparams(("parallel",)),
        name="route",
    )(aff2d)


def _ffn_kernel(idx_ref, gate_ref, h_hbm, wg_ref, wu_ref, wd_ref, y_ref, buf, sem, *, seq, tc):
    base = pl.program_id(0) * seq

    def row_copy(r):
        return pltpu.make_async_copy(h_hbm.at[pl.ds(base + idx_ref[0, 0, r], 1), :],
                                     buf.at[pl.ds(r, 1), :], sem)

    def issue(r, carry):
        row_copy(r).start()
        return carry

    def drain(r, carry):
        row_copy(r).wait()
        return carry

    lax.fori_loop(0, tc, issue, 0)
    lax.fori_loop(0, tc, drain, 0)
    xs = buf[...].astype(BF16)
    g = jnp.dot(xs, wg_ref[...], preferred_element_type=F32)
    u = jnp.dot(xs, wu_ref[...], preferred_element_type=F32)
    hm = (jax.nn.silu(g) * u).astype(BF16)
    y = jnp.dot(hm, wd_ref[...], preferred_element_type=F32)
    gr = jnp.broadcast_to(gate_ref[0], (tc, tc))
    dg = jnp.where(lax.broadcasted_iota(I32, (tc, tc), 0) == lax.broadcasted_iota(I32, (tc, tc), 1), gr, 0.0)
    gcol = _split_dot(dg, jnp.ones((tc, LANES), BF16))
    y_ref[...] = y * jnp.tile(gcol, (1, D_MODEL // LANES))


def _ffn(idx, gates, h2d, w_gate, w_up, w_down, seq, tc=256):
    b, ne, cap = idx.shape
    nct = cap // tc
    idx3 = idx.reshape(b * ne * nct, 1, tc)
    gate3 = gates.reshape(b * ne * nct, 1, tc)
    slot = lambda bi, ei, ci: ((bi * ne + ei) * nct + ci, 0, 0)
    return pl.pallas_call(
        functools.partial(_ffn_kernel, seq=seq, tc=tc),
        grid=(b, ne, nct),
        in_specs=[
            pl.BlockSpec((1, 1, tc), slot, memory_space=pltpu.SMEM),
            pl.BlockSpec((1, 1, tc), slot),
            pl.BlockSpec(memory_space=pl.ANY),
            pl.BlockSpec((None, D_MODEL, EXPERT_FF), lambda bi, ei, ci: (ei, 0, 0)),
            pl.BlockSpec((None, D_MODEL, EXPERT_FF), lambda bi, ei, ci: (ei, 0, 0)),
            pl.BlockSpec((None, EXPERT_FF, D_MODEL), lambda bi, ei, ci: (ei, 0, 0)),
        ],
        out_specs=pl.BlockSpec((None, None, tc, D_MODEL), lambda bi, ei, ci: (bi, ei, ci, 0)),
        out_shape=jax.ShapeDtypeStruct((b, ne, cap, D_MODEL), F32),
        scratch_shapes=[pltpu.VMEM((tc, D_MODEL), F32), pltpu.SemaphoreType.DMA],
        compiler_params=_cparams(("arbitrary", "arbitrary", "arbitrary")),
        name="ffn",
    )(idx3, gate3, h2d, w_gate.astype(BF16), w_up.astype(BF16), w_down.astype(BF16))


def _combine_kernel(x_ref, m_ref, g_ref, o_ref, *, final):
    x = x_ref[...] + m_ref[...]
    if final:
        x = x * _rms_scale(x) * g_ref[...]
    o_ref[...] = x


def _combine(x2d, moe2d, g_final, final, tm=1024):
    t = x2d.shape[0]
    row = lambda i: (i, 0)
    return pl.pallas_call(
        functools.partial(_combine_kernel, final=final),
        grid=(t // tm,),
        in_specs=[pl.BlockSpec((tm, D_MODEL), row), pl.BlockSpec((tm, D_MODEL), row),
                  pl.BlockSpec((1, D_MODEL), lambda i: (0, 0))],
        out_specs=pl.BlockSpec((tm, D_MODEL), row),
        out_shape=jax.ShapeDtypeStruct((t, D_MODEL), F32),
        compiler_params=_cparams(("parallel",)),
        name="combine",
    )(x2d, moe2d, g_final.reshape(1, D_MODEL))


_SLAB_Q0 = 512
_SLAB_K0 = _SLAB_Q0 + D_HEADS * LANES
_SLAB_V0 = _SLAB_K0 + D_KV_HEADS * LANES
_ODD_COLS = _SLAB_V0 + D_KV_HEADS * LANES


def _proj_odd_kernel(x_ref, g_ref, wm_ref, cqn_ref, wq_ref, ckvn_ref, wkv_ref, dqn_ref, dkn_ref,
                     ccq_ref, s1cq_ref, s2cq_ref, cck_ref, s1ck_ref, s2ck_ref,
                     cdq_ref, s1dq_ref, s2dq_ref, cdk_ref, s1dk_ref, s2dk_ref,
                     qc_ref, kc_ref, vc_ref, qd_ref, kd_ref, vd_ref):
    x = x_ref[...]
    y = (x * _rms_scale(x) * g_ref[...]).astype(BF16)
    pm = jnp.dot(y, wm_ref[...], preferred_element_type=F32)
    one64 = jnp.where(lax.broadcasted_iota(I32, (1, LANES), 1) == HEAD_DIM, 1.0, 0.0)
    half_rope = C_ROPE // 2

    cq = pm[:, :C_Q_RANK]
    cqn = (cq * _rms_scale(cq) * cqn_ref[...]).astype(BF16)
    qc = jnp.dot(cqn, wq_ref[...], preferred_element_type=F32)
    ckv = pm[:, C_Q_RANK:C_Q_RANK + C_KV_RANK]
    ckvn = (ckv * _rms_scale(ckv) * ckvn_ref[...]).astype(BF16)
    kv = jnp.dot(ckvn, wkv_ref[...], preferred_element_type=F32)
    kr = _rope3(pm[:, C_Q_RANK + C_KV_RANK:_SLAB_Q0], cck_ref, s1ck_ref, s2ck_ref, half_rope)
    for h in range(C_HEADS):
        sl = slice(h * LANES, (h + 1) * LANES)
        qc_ref[:, sl] = _rope3(qc[:, sl], ccq_ref, s1cq_ref, s2cq_ref, half_rope).astype(BF16)
        kc_ref[:, sl] = (kv[:, sl] + kr).astype(BF16)
        vc_ref[:, sl] = (kv[:, C_HEADS * LANES + h * LANES:C_HEADS * LANES + (h + 1) * LANES] + one64).astype(BF16)

    def head_norm(xg, gn_ref):
        ss = jnp.sum(xg * xg, axis=-1, keepdims=True) * (1.0 / HEAD_DIM)
        return xg * lax.rsqrt(ss + EPS) * gn_ref[...]

    for g in range(D_HEADS):
        xg = pm[:, _SLAB_Q0 + g * LANES:_SLAB_Q0 + (g + 1) * LANES]
        qd_ref[:, g * LANES:(g + 1) * LANES] = _rope3(head_norm(xg, dqn_ref), cdq_ref, s1dq_ref, s2dq_ref,
                                                     HEAD_DIM // 4).astype(BF16)
    for g in range(D_KV_HEADS):
        sl = slice(g * LANES, (g + 1) * LANES)
        xg = pm[:, _SLAB_K0 + g * LANES:_SLAB_K0 + (g + 1) * LANES]
        kd_ref[:, sl] = _rope3(head_norm(xg, dkn_ref), cdk_ref, s1dk_ref, s2dk_ref, HEAD_DIM // 4).astype(BF16)
        vd_ref[:, sl] = (pm[:, _SLAB_V0 + g * LANES:_SLAB_V0 + (g + 1) * LANES] + one64).astype(BF16)


def _slabs(w, n_heads, width, lane_off=0):
    k = w.shape[0]
    w3 = w.reshape(k, n_heads, width)
    w3 = jnp.pad(w3, ((0, 0), (0, 0), (lane_off, LANES - width - lane_off)))
    return w3.reshape(k, n_heads * LANES)


def _axial_tables(row, col, scale):
    half = HEAD_DIM // 2
    cr, s1r, s2r = _rope_tables(row, D_THETA, half, 0, LANES, scale)
    cc, s1c, s2c = _rope_tables(col, D_THETA, half, half, LANES, scale)
    lane = jnp.arange(LANES)[None, :]
    return jnp.where(lane < half, cr, cc), s1r + s1c, s2r + s2c


def _proj_odd(x2d, seq, g_mix, w_in, cq_norm, w_cq_up, ckv_norm, w_ckv_up, dq_norm, dk_norm, tm=512):
    t = x2d.shape[0]
    nblk = seq // tm
    o1 = C_Q_RANK
    o2 = o1 + C_KV_RANK
    o3 = o2 + C_ROPE
    o4 = o3 + D_HEADS * HEAD_DIM
    o5 = o4 + D_KV_HEADS * HEAD_DIM
    wm = jnp.concatenate([
        w_in[:, :o2],
        _slabs(w_in[:, o2:o3], 1, C_ROPE, C_NOPE),
        _slabs(w_in[:, o3:o4], D_HEADS, HEAD_DIM),
        _slabs(w_in[:, o4:o5], D_KV_HEADS, HEAD_DIM),
        _slabs(w_in[:, o5:], D_KV_HEADS, HEAD_DIM),
    ], axis=1).astype(BF16)
    assert wm.shape[1] == _ODD_COLS
    wq = _slabs(w_cq_up, C_HEADS, C_NOPE + C_ROPE).astype(BF16)
    kv3 = w_ckv_up.reshape(C_KV_RANK, C_HEADS, 2 * HEAD_DIM)
    wkv = jnp.concatenate([
        _slabs(kv3[:, :, :C_NOPE].reshape(C_KV_RANK, -1), C_HEADS, C_NOPE),
        _slabs(kv3[:, :, C_NOPE:].reshape(C_KV_RANK, -1), C_HEADS, HEAD_DIM),
    ], axis=1).astype(BF16)
    pad64 = lambda g: jnp.pad(g, (0, LANES - HEAD_DIM)).reshape(1, LANES)

    pos = jnp.arange(seq, dtype=I32)
    row_pos = pos // GRID_W
    col_pos = pos % GRID_W
    c_scale = (C_NOPE + C_ROPE) ** -0.5 * LOG2E
    d_scale = HEAD_DIM ** -0.5 * LOG2E
    tabs = (_rope_tables(pos, ROPE_THETA, C_ROPE, C_NOPE, LANES, c_scale)
            + _rope_tables(pos, ROPE_THETA, C_ROPE, C_NOPE, LANES, 1.0)
            + _axial_tables(row_pos, col_pos, d_scale)
            + _axial_tables(row_pos, col_pos, 1.0))

    row = lambda i: (i, 0)
    full = lambda i: (0, 0)
    tspec = pl.BlockSpec((tm, LANES), lambda i: (i % nblk, 0))
    wide = C_HEADS * LANES
    kvw = D_KV_HEADS * LANES
    return pl.pallas_call(
        _proj_odd_kernel,
        grid=(t // tm,),
        in_specs=[
            pl.BlockSpec((tm, D_MODEL), row),
            pl.BlockSpec((1, D_MODEL), full),
            pl.BlockSpec(wm.shape, full),
            pl.BlockSpec((1, C_Q_RANK), full),
            pl.BlockSpec(wq.shape, full),
            pl.BlockSpec((1, C_KV_RANK), full),
            pl.BlockSpec(wkv.shape, full),
            pl.BlockSpec((1, LANES), full),
            pl.BlockSpec((1, LANES), full),
        ] + [tspec] * 12,
        out_specs=[
            pl.BlockSpec((tm, wide), row), pl.BlockSpec((tm, wide), row), pl.BlockSpec((tm, wide), row),
            pl.BlockSpec((tm, wide), row), pl.BlockSpec((tm, kvw), row), pl.BlockSpec((tm, kvw), row),
        ],
        out_shape=[
            jax.ShapeDtypeStruct((t, wide), BF16), jax.ShapeDtypeStruct((t, wide), BF16),
            jax.ShapeDtypeStruct((t, wide), BF16), jax.ShapeDtypeStruct((t, wide), BF16),
            jax.ShapeDtypeStruct((t, kvw), BF16), jax.ShapeDtypeStruct((t, kvw), BF16),
        ],
        compiler_params=_cparams(("parallel",)),
        name="proj_odd",
    )(x2d, g_mix.reshape(1, D_MODEL), wm, cq_norm.reshape(1, -1), wq, ckv_norm.reshape(1, -1), wkv,
      pad64(dq_norm), pad64(dk_norm), *tabs)


def _flash_kernel(q_ref, k_ref, v_ref, o_ref, qs_sc, m_sc, acc_sc, *, group, tq, tk):
    ki = pl.program_id(3)

    @pl.when(ki == 0)
    def _():
        for g in range(group):
            qs_sc[g * tq:(g + 1) * tq, :] = q_ref[:, g * LANES:(g + 1) * LANES]
        m_sc[...] = jnp.full(m_sc.shape, NEG, F32)
        acc_sc[...] = jnp.zeros(acc_sc.shape, F32)

    s = lax.dot_general(qs_sc[...], k_ref[...], _NT, preferred_element_type=F32)
    m_prev = m_sc[...]
    m_new = jnp.maximum(m_prev, jnp.max(s, axis=1, keepdims=True))
    alpha = jnp.exp2(m_prev - m_new)
    p = jnp.exp2(s - jnp.tile(m_new, (1, tk // LANES)))
    acc_sc[...] = alpha * acc_sc[...] + jnp.dot(p.astype(BF16), v_ref[...], preferred_element_type=F32)
    m_sc[...] = m_new

    @pl.when(ki == pl.num_programs(3) - 1)
    def _():
        acc = acc_sc[...]
        o = acc / acc[:, HEAD_DIM:HEAD_DIM + 1]
        for g in range(group):
            o_ref[:, g * LANES:(g + 1) * LANES] = o[g * tq:(g + 1) * tq].astype(BF16)


def _flash(q, k, v, group, rows=1024, tk=512):
    b, s, qw = q.shape
    hk = k.shape[2] // LANES
    tq = rows // group
    qspec = pl.BlockSpec((None, tq, group * LANES), lambda bi, hi, qi, ki: (bi, qi, hi))
    kspec = pl.BlockSpec((None, tk, LANES), lambda bi, hi, qi, ki: (bi, ki, hi))
    return pl.pallas_call(
        functools.partial(_flash_kernel, group=group, tq=tq, tk=tk),
        grid=(b, hk, s // tq, s // tk),
        in_specs=[qspec, kspec, kspec],
        out_specs=qspec,
        out_shape=jax.ShapeDtypeStruct((b, s, qw), BF16),
        scratch_shapes=[pltpu.VMEM((rows, LANES), BF16), pltpu.VMEM((rows, LANES), F32),
                        pltpu.VMEM((rows, LANES), F32)],
        compiler_params=_cparams(("parallel", "parallel", "parallel", "arbitrary")),
        name="flash",
    )(q, k, v)


def _moe(x1, h2, aff, batch, seq, w_gate, w_up, w_down, g_final, final):
    idx, gates = _route(aff.reshape(-1, LANES), batch, seq)
    y = _ffn(idx, gates, h2, w_gate, w_up, w_down, seq)
    flat = (idx + (jnp.arange(batch, dtype=I32) * seq)[:, None, None]).reshape(-1)
    moe = jnp.zeros((batch * seq, D_MODEL), F32).at[flat].add(y.reshape(-1, D_MODEL))
    return _combine(x1, moe, g_final, final)


def kernel(x, norm_mix, norm_ffn, even_w_in, even_gmlp_norm, even_w_spatial, even_b_spatial, even_w_out,
           odd_w_in, odd_cq_norm, odd_w_cq_up, odd_ckv_norm, odd_w_ckv_up, odd_dq_norm, odd_dk_norm, odd_w_out,
           moe_w_router, moe_w_gate, moe_w_up, moe_w_down, final_norm):
    b, s, d = x.shape
    depth = norm_mix.shape[0]
    x2d = x.reshape(b * s, d)
    for i in range(depth):
        j = i // 2
        last = i == depth - 1
        if i % 2 == 0:
            q, k, v, go = _proj_even(x2d, s, norm_mix[i], even_w_in[j], even_gmlp_norm[j], even_w_spatial[j],
                                     even_b_spatial[j])
            a = _dilated(q.reshape(b, s, A_WIDTH), k.reshape(b, s, A_WIDTH), v.reshape(b, s, A_WIDTH))
            x1, h2, aff = _outproj(x2d, a.reshape(b * s, A_WIDTH), go, even_w_out[j][:A_WIDTH],
                                   even_w_out[j][A_WIDTH:], norm_ffn[i], moe_w_router[i])
        else:
            qc, kc, vc, qd, kd, vd = _proj_odd(x2d, s, norm_mix[i], odd_w_in[j], odd_cq_norm[j], odd_w_cq_up[j],
                                               odd_ckv_norm[j], odd_w_ckv_up[j], odd_dq_norm[j], odd_dk_norm[j])
            r3 = lambda z: z.reshape(b, s, -1)
            oc = _flash(r3(qc), r3(kc), r3(vc), 1)
            od = _flash(r3(qd), r3(kd), r3(vd), D_HEADS // D_KV_HEADS)
            cw = C_HEADS * HEAD_DIM
            x1, h2, aff = _outproj(x2d, oc.reshape(b * s, -1), od.reshape(b * s, -1),
                                   _slabs(odd_w_out[j][:cw].T, C_HEADS, HEAD_DIM).T,
                                   _slabs(odd_w_out[j][cw:].T, D_HEADS, HEAD_DIM).T,
                                   norm_ffn[i], moe_w_router[i])
        x2d = _moe(x1, h2, aff, b, s, moe_w_gate[i], moe_w_up[i], moe_w_down[i], final_norm, last)
    return x2d.reshape(b, s, d)
```

```python
import functools
import math

import jax
import jax.numpy as jnp
from jax import lax
from jax.experimental import pallas as pl
from jax.experimental.pallas import tpu as pltpu

F32 = jnp.float32
BF16 = jnp.bfloat16
I32 = jnp.int32

EPS = 1e-6
NEG = -1e30
LOG2E = 1.4426950408889634

D_MODEL = 1024
HEAD_DIM = 64
ROPE_THETA = 500000.0
ROT_DIM = 16
GRID_W = 64
A_HEADS = 12
A_WIDTH = 768
A_DILATIONS = (1, 4, 16)
A_RADIUS = 64
B_WIDTH = 256
B_GROUPS = 4
B_CHUNK = 128
C_HEADS = 8
C_Q_RANK = 256
C_KV_RANK = 128
C_NOPE = 64
C_ROPE = 32
D_HEADS = 8
D_KV_HEADS = 2
D_THETA = 10000.0
N_EXPERTS = 16
EC_FACTOR = 2
EXPERT_FF = 512

_NORM_MARGIN = 1.01
_MAX_SCORE_BOUND = 55.0

LANES = 128
VMEM_LIMIT = 48 * 1024 * 1024

_NT = (((1,), (1,)), ((), ()))


def _cparams(sem):
    return pltpu.CompilerParams(dimension_semantics=sem, vmem_limit_bytes=VMEM_LIMIT)


def _rms_scale(x):
    return lax.rsqrt(jnp.mean(x * x, axis=-1, keepdims=True) + EPS)


def _rope3(a, c_ref, s1_ref, s2_ref, shift):
    return (a * c_ref[...] + pltpu.roll(a, LANES - shift, 1) * s1_ref[...]
            + pltpu.roll(a, shift, 1) * s2_ref[...])


def _split_dot(x, w_bf16):
    hi = x.astype(BF16)
    lo = (x - hi.astype(F32)).astype(BF16)
    return (jnp.dot(hi, w_bf16, preferred_element_type=F32)
            + jnp.dot(lo, w_bf16, preferred_element_type=F32))


def _proj_even_kernel(x_ref, g_ref, w_ref, cq_ref, s1q_ref, s2q_ref, ck_ref, s1k_ref, s2k_ref,
                      gn_ref, gmat_ref, ws_ref, bs_ref,
                      q_ref, k_ref, v_ref, go_ref):
    x = x_ref[...]
    y = (x * _rms_scale(x) * g_ref[...]).astype(BF16)
    tm = x.shape[0]

    aq = jnp.dot(y, w_ref[:, 0:A_WIDTH], preferred_element_type=F32)
    for j in range(A_WIDTH // LANES):
        sl = slice(j * LANES, (j + 1) * LANES)
        q_ref[:, sl] = _rope3(aq[:, sl], cq_ref, s1q_ref, s2q_ref, ROT_DIM // 2)
    ak = jnp.dot(y, w_ref[:, A_WIDTH:2 * A_WIDTH], preferred_element_type=F32)
    for j in range(A_WIDTH // LANES):
        sl = slice(j * LANES, (j + 1) * LANES)
        k_ref[:, sl] = _rope3(ak[:, sl], ck_ref, s1k_ref, s2k_ref, ROT_DIM // 2)
    v_ref[...] = jnp.dot(y, w_ref[:, 2 * A_WIDTH:3 * A_WIDTH], preferred_element_type=F32)

    z = jnp.dot(y, w_ref[:, 3 * A_WIDTH:3 * A_WIDTH + 2 * B_WIDTH], preferred_element_type=F32)
    ge = jax.nn.gelu(z)
    u = ge[:, :B_WIDTH]
    vv = ge[:, B_WIDTH:]
    ss = _split_dot(vv * vv, gmat_ref[...])
    vn = (vv * lax.rsqrt(ss + EPS) * gn_ref[...]).astype(BF16)
    grp = lax.broadcasted_iota(I32, (B_CHUNK, B_WIDTH), 1) // (B_WIDTH // B_GROUPS)
    for c in range(tm // B_CHUNK):
        rows = slice(c * B_CHUNK, (c + 1) * B_CHUNK)
        vc = vn[rows]
        mg = [jnp.dot(ws_ref[g], vc, preferred_element_type=F32) for g in range(B_GROUPS)]
        mixed = jnp.where(grp == 0, mg[0], jnp.where(grp == 1, mg[1], jnp.where(grp == 2, mg[2], mg[3])))
        go_ref[rows, :] = (u[rows] * (mixed + bs_ref[...])).astype(BF16)


def _rope_tables(pos, theta, r, lane_off, period, scale):
    half = r // 2
    inv = jnp.power(jnp.float32(theta), -jnp.arange(half, dtype=F32) * (2.0 / r))
    ang = pos.astype(F32)[:, None] * inv[None, :]
    cos, sin = jnp.cos(ang), jnp.sin(ang)
    o = (jnp.arange(LANES) % period) - lane_off
    in_lo = (o >= 0) & (o < half)
    in_hi = (o >= half) & (o < r)
    idx = jnp.clip(jnp.where(in_hi, o - half, o), 0, half - 1)
    c = jnp.where((in_lo | in_hi)[None, :], cos[:, idx], 1.0)
    s1 = jnp.where(in_lo[None, :], -sin[:, idx], 0.0)
    s2 = jnp.where(in_hi[None, :], sin[:, idx], 0.0)
    return c * scale, s1 * scale, s2 * scale


def _proj_even(x2d, seq, g_mix, w_in, gmlp_norm, w_s, b_s, tm=512):
    t = x2d.shape[0]
    nblk = seq // tm
    pos = jnp.arange(seq, dtype=I32)
    qscale = HEAD_DIM ** -0.5 * LOG2E
    cq, s1q, s2q = _rope_tables(pos, ROPE_THETA, ROT_DIM, 0, HEAD_DIM, qscale)
    ck, s1k, s2k = _rope_tables(pos, ROPE_THETA, ROT_DIM, 0, HEAD_DIM, 1.0)
    gdim = B_WIDTH // B_GROUPS
    gid = jnp.arange(B_WIDTH) // gdim
    gmat = jnp.where(gid[:, None] == gid[None, :], 1.0 / gdim, 0.0).astype(BF16)
    bias = jnp.repeat(b_s.T, gdim, axis=1)
    row = lambda i: (i, 0)
    tab = lambda i: (i % nblk, 0)
    full = lambda i: (0, 0)
    tspec = pl.BlockSpec((tm, LANES), tab)
    return pl.pallas_call(
        _proj_even_kernel,
        grid=(t // tm,),
        in_specs=[
            pl.BlockSpec((tm, D_MODEL), row),
            pl.BlockSpec((1, D_MODEL), full),
            pl.BlockSpec(w_in.shape, full),
            tspec, tspec, tspec, tspec, tspec, tspec,
            pl.BlockSpec((1, B_WIDTH), full),
            pl.BlockSpec((B_WIDTH, B_WIDTH), full),
            pl.BlockSpec((B_GROUPS, B_CHUNK, B_CHUNK), lambda i: (0, 0, 0)),
            pl.BlockSpec((B_CHUNK, B_WIDTH), full),
        ],
        out_specs=[
            pl.BlockSpec((tm, A_WIDTH), row),
            pl.BlockSpec((tm, A_WIDTH), row),
            pl.BlockSpec((tm, A_WIDTH), row),
            pl.BlockSpec((tm, B_WIDTH), row),
        ],
        out_shape=[
            jax.ShapeDtypeStruct((t, A_WIDTH), F32),
            jax.ShapeDtypeStruct((t, A_WIDTH), F32),
            jax.ShapeDtypeStruct((t, A_WIDTH), F32),
            jax.ShapeDtypeStruct((t, B_WIDTH), BF16),
        ],
        compiler_params=_cparams(("parallel",)),
        name="proj_even",
    )(x2d, g_mix.reshape(1, D_MODEL), w_in.astype(BF16), cq, s1q, s2q, ck, s1k, s2k,
      gmlp_norm.reshape(1, B_WIDTH), gmat, w_s.astype(BF16), bias)


_TQ = 128
_TK = _TQ + 2 * A_RADIUS


def _dilated_kernel(q_ref, k_ref, v_ref, o_ref, m_sc, l_sc, *, seq):
    half0 = lax.broadcasted_iota(I32, (_TQ, LANES), 1) < HEAD_DIM
    diff = lax.broadcasted_iota(I32, (_TQ, _TK), 1) - lax.broadcasted_iota(I32, (_TQ, _TK), 0)
    n_pat = len(A_DILATIONS)
    for pi, d in enumerate(A_DILATIONS):
        cls_len = seq // d
        tpc = cls_len // _TQ

        def body(j, carry, d=d, cls_len=cls_len, tpc=tpc, pi=pi):
            i = j // tpc
            n = j % tpc
            l0 = n * _TQ
            kst = jnp.clip(l0 - A_RADIUS, 0, cls_len - _TK)
            off = l0 - kst
            if d == 1:
                qrows = pl.ds(pl.multiple_of(l0, _TQ), _TQ)
                krows = pl.ds(pl.multiple_of(kst, A_RADIUS), _TK)
            else:
                qrows = pl.ds(l0 * d + i, _TQ, stride=d)
                krows = pl.ds(kst * d + i, _TK, stride=d)
            q = q_ref[qrows, :]
            kb = k_ref[krows, :].astype(BF16)
            vb = v_ref[krows, :].astype(BF16)
            ok = jnp.abs(diff - off) <= A_RADIUS
            parts = []
            for h in range(2):
                qh = jnp.where(half0 if h == 0 else jnp.logical_not(half0), q, 0.0).astype(BF16)
                s = lax.dot_general(qh, kb, _NT, preferred_element_type=F32)
                s = jnp.where(ok, s, NEG)
                mt = jnp.max(s, axis=-1, keepdims=True)
                p = jnp.exp2(s - mt)
                lt = jnp.sum(p, axis=-1, keepdims=True)
                ot = jnp.dot(p.astype(BF16), vb, preferred_element_type=F32)
                parts.append((mt, lt, ot))
            mt = jnp.where(half0, parts[0][0], parts[1][0])
            lt = jnp.where(half0, parts[0][1], parts[1][1])
            ot = jnp.where(half0, parts[0][2], parts[1][2])
            if pi > 0:
                mp = m_sc[qrows, :]
                mn = jnp.maximum(mp, mt)
                a = jnp.exp2(mp - mn)
                b = jnp.exp2(mt - mn)
                lt = a * l_sc[qrows, :] + b * lt
                ot = a * o_ref[qrows, :] + b * ot
                mt = mn
            if pi == n_pat - 1:
                o_ref[qrows, :] = ot / lt
            else:
                m_sc[qrows, :] = mt
                l_sc[qrows, :] = lt
                o_ref[qrows, :] = ot
            return carry

        lax.fori_loop(0, seq // _TQ, body, 0)


def _dilated(q, k, v):
    b, s, w = q.shape
    spec = pl.BlockSpec((None, s, LANES), lambda bi, hi: (bi, 0, hi))
    return pl.pallas_call(
        functools.partial(_dilated_kernel, seq=s),
        grid=(b, w // LANES),
        in_specs=[spec, spec, spec],
        out_specs=spec,
        out_shape=jax.ShapeDtypeStruct((b, s, w), F32),
        scratch_shapes=[pltpu.VMEM((s, LANES), F32), pltpu.VMEM((s, LANES), F32)],
        compiler_params=_cparams(("parallel", "parallel")),
        name="dilated",
    )(q, k, v)


def _outproj_kernel(x_ref, a_ref, b_ref, wa_ref, wb_ref, gf_ref, wr_ref, x1_ref, h2_ref, aff_ref):
    x1 = (x_ref[...]
          + jnp.dot(a_ref[...].astype(BF16), wa_ref[...], preferred_element_type=F32)
          + jnp.dot(b_ref[...].astype(BF16), wb_ref[...], preferred_element_type=F32))
    x1_ref[...] = x1
    h2 = x1 * _rms_scale(x1) * gf_ref[...]
    h2_ref[...] = h2
    w_hi = wr_ref[0]
    w_lo = wr_ref[1]
    hi = h2.astype(BF16)
    lo = (h2 - hi.astype(F32)).astype(BF16)
    lg = (jnp.dot(hi, w_hi, preferred_element_type=F32) + jnp.dot(lo, w_hi, preferred_element_type=F32)
          + jnp.dot(hi, w_lo, preferred_element_type=F32))
    valid = lax.broadcasted_iota(I32, lg.shape, 1) < N_EXPERTS
    lg = jnp.where(valid, lg, NEG)
    e = jnp.exp(lg - jnp.max(lg, axis=-1, keepdims=True))
    aff = e / jnp.sum(e, axis=-1, keepdims=True)
    aff_t = aff.T
    for j in range(aff.shape[0] // LANES):
        aff_ref[j] = aff_t[:N_EXPERTS, j * LANES:(j + 1) * LANES]


def _outproj(x2d, a, b, wa, wb, g_ffn, w_router, tm=512):
    t = x2d.shape[0]
    wr = jnp.pad(w_router, ((0, 0), (0, LANES - N_EXPERTS)))
    wr_hi = wr.astype(BF16)
    wr_lo = (wr - wr_hi.astype(F32)).astype(BF16)
    wr2 = jnp.stack([wr_hi, wr_lo])
    row = lambda i: (i, 0)
    full = lambda i: (0, 0)
    return pl.pallas_call(
        _outproj_kernel,
        grid=(t // tm,),
        in_specs=[
            pl.BlockSpec((tm, D_MODEL), row),
            pl.BlockSpec((tm, a.shape[1]), row),
            pl.BlockSpec((tm, b.shape[1]), row),
            pl.BlockSpec(wa.shape, full),
            pl.BlockSpec(wb.shape, full),
            pl.BlockSpec((1, D_MODEL), full),
            pl.BlockSpec((2, D_MODEL, LANES), lambda i: (0, 0, 0)),
        ],
        out_specs=[
            pl.BlockSpec((tm, D_MODEL), row),
            pl.BlockSpec((tm, D_MODEL), row),
            pl.BlockSpec((tm // LANES, N_EXPERTS, LANES), lambda i: (i, 0, 0)),
        ],
        out_shape=[
            jax.ShapeDtypeStruct((t, D_MODEL), F32),
            jax.ShapeDtypeStruct((t, D_MODEL), F32),
            jax.ShapeDtypeStruct((t // LANES, N_EXPERTS, LANES), F32),
        ],
        compiler_params=_cparams(("parallel",)),
        name="outproj",
    )(x2d, a, b, wa.astype(BF16), wb.astype(BF16), g_ffn.reshape(1, D_MODEL), wr2)


def _route_kernel(aff_ref, idx_ref, gate_ref, spos_ref, cb_ref, thr_sc, need_sc, *, cap):
    nblk = aff_ref.shape[0] // N_EXPERTS
    bits = pltpu.bitcast(aff_ref[...], I32).reshape(nblk, N_EXPERTS, LANES)

    def count(pred):
        return jnp.sum(jnp.sum(jnp.where(pred, 1.0, 0.0), axis=0), axis=1, keepdims=True)

    def search(it, thr):
        cand = thr | jnp.left_shift(jnp.int32(1), 30 - it)
        return jnp.where(count(bits >= cand[None]) >= cap, cand, thr)

    thr = lax.fori_loop(0, 31, search, jnp.zeros((N_EXPERTS, 1), I32))
    need = cap - count(bits > thr[None])
    thr_sc[...] = jnp.broadcast_to(thr, (N_EXPERTS, LANES))
    need_sc[...] = jnp.broadcast_to(need, (N_EXPERTS, LANES))

    ri = lax.broadcasted_iota(I32, (LANES, LANES), 0)
    ci = lax.broadcasted_iota(I32, (LANES, LANES), 1)
    upper = jnp.where(ri <= ci, 1.0, 0.0).astype(BF16)
    lower = jnp.where(ci <= ri, 1.0, 0.0).astype(BF16)
    eye = jnp.where(ri == ci, 1.0, 0.0).astype(BF16)
    ones = jnp.ones((LANES, LANES), BF16)
    bi = lax.broadcasted_iota(I32, (nblk, nblk), 0)
    bj = lax.broadcasted_iota(I32, (nblk, nblk), 1)
    strict = jnp.where(bj < bi, 1.0, 0.0).astype(BF16)
    before = jnp.where(bi < bj, 1.0, 0.0).astype(BF16)
    mean_rows = jnp.full((8, LANES), 1.0 / LANES, BF16)
    c_row = lax.broadcasted_iota(I32, (1, cap), 1).astype(F32)
    blk_iota = lax.broadcasted_iota(I32, (nblk, cap), 0).astype(F32)
    t_iota = lax.broadcasted_iota(I32, (LANES, cap), 0).astype(F32)
    rep = cap // LANES

    def cums(mask_bf16):
        lp = jnp.dot(mask_bf16, upper, preferred_element_type=F32)
        bc = jnp.dot(mask_bf16, ones, preferred_element_type=F32)
        bst = jnp.dot(strict, bc.astype(BF16), preferred_element_type=F32)
        return lp, bc, bst

    def per_expert(e, carry):
        a = aff_ref[pl.ds(e, nblk, stride=N_EXPERTS), :]
        ab = pltpu.bitcast(a, I32)
        thr_e = thr_sc[pl.ds(e, 1), :]
        need_e = need_sc[pl.ds(e, 1), :]
        gt = ab > thr_e
        eq = ab == thr_e
        eqf = jnp.where(eq, 1.0, 0.0)
        lp_q, _, bst_q = cums(eqf.astype(BF16))
        sel = jnp.logical_or(gt, jnp.logical_and(eq, bst_q + lp_q - eqf < need_e))
        mb = jnp.where(sel, 1.0, 0.0).astype(BF16)
        lp, bc, bst = cums(mb)
        spos_ref[pl.ds(e, nblk, stride=N_EXPERTS), :] = jnp.where(sel, bst + lp - 1.0, -1.0)
        bc_row = lax.dot_general(mean_rows, bc.astype(BF16), _NT, preferred_element_type=F32)
        cb_ref[pl.ds(e, 1), :] = jnp.dot(bc_row.astype(BF16), before, preferred_element_type=F32)[:1].astype(I32)
        bend_w = jnp.tile(bst + bc, (1, rep))
        bst_w = jnp.tile(bst, (1, rep))
        blk_c = jnp.sum(jnp.where(bend_w <= c_row, 1.0, 0.0), axis=0, keepdims=True)
        onehot = blk_iota == blk_c
        bst_c = jnp.sum(jnp.where(onehot, bst_w, 0.0), axis=0, keepdims=True)
        r_c = c_row - bst_c
        ohb = jnp.where(onehot, 1.0, 0.0).astype(BF16)
        lp_t = lax.dot_general(lower, mb, _NT, preferred_element_type=F32)
        lp_c = jnp.dot(lp_t.astype(BF16), ohb, preferred_element_type=F32)
        tl_c = jnp.sum(jnp.where(lp_c <= r_c, 1.0, 0.0), axis=0, keepdims=True)
        idx_ref[pl.ds(e, 1), :] = (blk_c * LANES + tl_c).astype(I32)
        a_hi = a.astype(BF16)
        a_lo = (a - a_hi.astype(F32)).astype(BF16)
        at_hi = lax.dot_general(eye, a_hi, _NT, preferred_element_type=F32).astype(BF16)
        at_lo = lax.dot_general(eye, a_lo, _NT, preferred_element_type=F32).astype(BF16)
        g_c = (jnp.dot(at_hi, ohb, preferred_element_type=F32)
               + jnp.dot(at_lo, ohb, preferred_element_type=F32))
        gate_ref[pl.ds(e, 1), :] = jnp.sum(jnp.where(t_iota == tl_c, g_c, 0.0), axis=0, keepdims=True)
        return carry

    lax.fori_loop(0, N_EXPERTS, per_expert, 0)


def _route(aff2d, batch, seq):
    cap = EC_FACTOR * seq // N_EXPERTS
    nblk = seq // LANES
    rows = nblk * N_EXPERTS
    out_spec = pl.BlockSpec((None, N_EXPERTS, cap), lambda b: (b, 0, 0))
    return pl.pallas_call(
        functools.partial(_route_kernel, cap=cap),
        grid=(batch,),
        in_specs=[pl.BlockSpec((rows, LANES), lambda b: (b, 0))],
        out_specs=[out_spec, out_spec, pl.BlockSpec((rows, LANES), lambda b: (b, 0)),
                   pl.BlockSpec((None, N_EXPERTS, nblk), lambda b: (b, 0, 0))],
        out_shape=[jax.ShapeDtypeStruct((batch, N_EXPERTS, cap), I32),
                   jax.ShapeDtypeStruct((batch, N_EXPERTS, cap), F32),
                   jax.ShapeDtypeStruct((batch * rows, LANES), F32),
                   jax.ShapeDtypeStruct((batch, N_EXPERTS, nblk), I32)],
        scratch_shapes=[pltpu.VMEM((N_EXPERTS, LANES), I32), pltpu.VMEM((N_EXPERTS, LANES), F32)],
        compiler_params=_cparams(("parallel",)),
        name="route",
    )(aff2d)


def _ffn_kernel(idx_ref, gate_ref, h_hbm, wg_ref, wu_ref, wd_ref, y_ref, buf, sem, *, seq, tc, nsub):
    base = pl.program_id(0) * seq

    def row_copy(j, r, slot):
        return pltpu.make_async_copy(h_hbm.at[pl.ds(base + idx_ref[0, 0, j * tc + r], 1), :],
                                     buf.at[slot, pl.ds(r, 1), :], sem.at[slot])

    def issue(j, slot):
        for r in range(tc):
            row_copy(j, r, slot).start()

    diag = lax.broadcasted_iota(I32, (tc, tc), 0) == lax.broadcasted_iota(I32, (tc, tc), 1)
    ones = jnp.ones((tc, LANES), BF16)
    issue(0, 0)
    for j in range(nsub):
        slot = j % 2
        if j + 1 < nsub:
            issue(j + 1, 1 - slot)
        for r in range(tc):
            row_copy(j, r, slot).wait()
        xs = buf[slot].astype(BF16)
        g = jnp.dot(xs, wg_ref[...], preferred_element_type=F32)
        u = jnp.dot(xs, wu_ref[...], preferred_element_type=F32)
        hm = (jax.nn.silu(g) * u).astype(BF16)
        y = jnp.dot(hm, wd_ref[...], preferred_element_type=F32)
        gr = jnp.broadcast_to(gate_ref[0, :, j * tc:(j + 1) * tc], (tc, tc))
        gcol = _split_dot(jnp.where(diag, gr, 0.0), ones)
        y_ref[j * tc:(j + 1) * tc, :] = (y * jnp.tile(gcol, (1, D_MODEL // LANES))).astype(BF16)


def _ffn(idx, gates, h2d, w_gate, w_up, w_down, seq, tc=256):
    b, ne, cap = idx.shape
    tc = min(tc, cap)
    idx3 = idx.reshape(b * ne, 1, cap)
    gate3 = gates.reshape(b * ne, 1, cap)
    slot = lambda bi, ei: (bi * ne + ei, 0, 0)
    wspec = lambda shape: pl.BlockSpec((None,) + shape, lambda bi, ei: (ei, 0, 0))
    return pl.pallas_call(
        functools.partial(_ffn_kernel, seq=seq, tc=tc, nsub=cap // tc),
        grid=(b, ne),
        in_specs=[
            pl.BlockSpec((1, 1, cap), slot, memory_space=pltpu.SMEM),
            pl.BlockSpec((1, 1, cap), slot),
            pl.BlockSpec(memory_space=pl.ANY),
            wspec((D_MODEL, EXPERT_FF)), wspec((D_MODEL, EXPERT_FF)), wspec((EXPERT_FF, D_MODEL)),
        ],
        out_specs=pl.BlockSpec((None, None, cap, D_MODEL), lambda bi, ei: (bi, ei, 0, 0)),
        out_shape=jax.ShapeDtypeStruct((b, ne, cap, D_MODEL), BF16),
        scratch_shapes=[pltpu.VMEM((2, tc, D_MODEL), F32), pltpu.SemaphoreType.DMA((2,))],
        compiler_params=pltpu.CompilerParams(dimension_semantics=("arbitrary", "arbitrary"),
                                             vmem_limit_bytes=VMEM_LIMIT, disable_bounds_checks=True),
        name="ffn",
    )(idx3, gate3, h2d, w_gate.astype(BF16), w_up.astype(BF16), w_down.astype(BF16))


_CTM = 256
_CWIN = 128


def _combine_kernel(cb_ref, x_ref, sp_ref, y_hbm, g_ref, o_ref, ybuf, xbuf, sem, xsem, *,
                    final, tiles_per_seq, nblk, cap):
    i = pl.program_id(0)
    n_tiles = pl.num_programs(0)
    b = i // tiles_per_seq
    slot = i % 2

    def window(tile, e):
        tb = tile // tiles_per_seq
        off = (tb * N_EXPERTS + e) * (nblk + 1) + (tile % tiles_per_seq) * (_CTM // LANES)
        s0 = cb_ref[off]
        s1 = cb_ref[off + _CTM // LANES]
        start = jnp.minimum((s0 // 64) * 64, cap - _CWIN)
        return s1, pl.multiple_of(start, 64)

    def fetch(tile, e, start, buf_slot):
        return pltpu.make_async_copy(y_hbm.at[tile // tiles_per_seq, e, pl.ds(start, _CWIN), :],
                                     ybuf.at[buf_slot, pl.ds(e * _CWIN, _CWIN), :], sem.at[buf_slot])

    def fetch_all(tile, buf_slot):
        for e in range(N_EXPERTS):
            fetch(tile, e, window(tile, e)[1], buf_slot).start()

    @pl.when(i == 0)
    def _():
        fetch_all(i, slot)

    @pl.when(i + 1 < n_tiles)
    def _():
        fetch_all(i + 1, 1 - slot)

    wins = [window(i, e) for e in range(N_EXPERTS)]
    pad = jnp.full((LANES - N_EXPERTS, LANES), -1.0, F32)
    sp_t = jnp.concatenate([jnp.concatenate([sp_ref[hf], pad], axis=0).T for hf in range(_CTM // LANES)],
                           axis=0)
    lane = lax.broadcasted_iota(I32, (1, _CWIN), 1).astype(F32)
    hits = [jnp.where(sp_t[:, e:e + 1] - wins[e][1].astype(F32) == lane, 1.0, 0.0).astype(BF16)
            for e in range(N_EXPERTS)]
    for e in range(N_EXPERTS):
        fetch(i, e, wins[e][1], slot).wait()
    o_ref[...] = x_ref[...] + jnp.dot(jnp.concatenate(hits, axis=1), ybuf[slot], preferred_element_type=F32)
    for e in range(N_EXPERTS):
        s1, start = wins[e]
        col = sp_t[:, e:e + 1]

        def extra(k, carry, e=e, s1=s1, start=start, col=col):
            lo = start + (k + 1) * _CWIN
            st = pl.multiple_of(jnp.minimum(lo, cap - _CWIN), 64)
            cp = pltpu.make_async_copy(y_hbm.at[b, e, pl.ds(st, _CWIN), :], xbuf, xsem)
            cp.start()
            cp.wait()
            hit = jnp.where(jnp.logical_and(col - st.astype(F32) == lane, col >= lo.astype(F32)), 1.0, 0.0)
            o_ref[...] += jnp.dot(hit.astype(BF16), xbuf[...], preferred_element_type=F32)
            return carry

        n_extra = jnp.maximum(s1 - start - 1, 0) // _CWIN
        lax.fori_loop(0, n_extra, extra, 0)
    if final:
        x = o_ref[...]
        o_ref[...] = x * _rms_scale(x) * g_ref[...]


def _combine(x2d, spos, cb, y, g_final, final, seq):
    t = x2d.shape[0]
    batch, ne, cap, _ = y.shape
    nblk = seq // LANES
    cb_full = jnp.concatenate([cb, jnp.full((batch, ne, 1), cap, I32)], axis=-1).reshape(-1)
    spb = _CTM // LANES
    return pl.pallas_call(
        functools.partial(_combine_kernel, final=final, tiles_per_seq=seq // _CTM, nblk=nblk, cap=cap),
        grid_spec=pltpu.PrefetchScalarGridSpec(
            num_scalar_prefetch=1,
            grid=(t // _CTM,),
            in_specs=[
                pl.BlockSpec((_CTM, D_MODEL), lambda i, c: (i, 0)),
                pl.BlockSpec((spb, N_EXPERTS, LANES), lambda i, c: (i, 0, 0)),
                pl.BlockSpec(memory_space=pl.ANY),
                pl.BlockSpec((1, D_MODEL), lambda i, c: (0, 0)),
            ],
            out_specs=pl.BlockSpec((_CTM, D_MODEL), lambda i, c: (i, 0)),
            scratch_shapes=[pltpu.VMEM((2, N_EXPERTS * _CWIN, D_MODEL), BF16), pltpu.VMEM((_CWIN, D_MODEL), BF16),
                            pltpu.SemaphoreType.DMA((2,)), pltpu.SemaphoreType.DMA],
        ),
        out_shape=jax.ShapeDtypeStruct((t, D_MODEL), F32),
        compiler_params=_cparams(("arbitrary",)),
        name="combine",
    )(cb_full, x2d, spos, y, g_final.reshape(1, D_MODEL))


_SLAB_Q0 = 512
_SLAB_K0 = _SLAB_Q0 + D_HEADS * LANES
_SLAB_V0 = _SLAB_K0 + D_KV_HEADS * LANES
_ODD_COLS = _SLAB_V0 + D_KV_HEADS * LANES


def _proj_odd_kernel(x_ref, g_ref, wm_ref, cqn_ref, wq_ref, ckvn_ref, wkv_ref, dqn_ref, dkn_ref,
                     ccq_ref, s1cq_ref, s2cq_ref, cck_ref, s1ck_ref, s2ck_ref,
                     cdq_ref, s1dq_ref, s2dq_ref, cdk_ref, s1dk_ref, s2dk_ref,
                     qc_ref, kc_ref, vc_ref, qd_ref, kd_ref, vd_ref, stat_ref, *, steps_per_seq):
    x = x_ref[...]
    y = (x * _rms_scale(x) * g_ref[...]).astype(BF16)
    pm = jnp.dot(y, wm_ref[...], preferred_element_type=F32)
    lane = lax.broadcasted_iota(I32, (1, LANES), 1)
    one64 = jnp.where(lane == HEAD_DIM, 1.0, 0.0)
    last_lane = lane == LANES - 1
    half_rope = C_ROPE // 2
    stats = [jnp.zeros((1, LANES), F32), jnp.zeros((1, LANES), F32)]

    def with_norm(val, fill, row, col):
        vb = val.astype(BF16).astype(F32)
        nrm = jnp.sqrt(jnp.sum(vb * vb, axis=-1, keepdims=True))
        stats[row] = jnp.where(lane == col, jnp.max(nrm, axis=0, keepdims=True), stats[row])
        return jnp.where(last_lane, nrm * _NORM_MARGIN if fill is None else fill, val).astype(BF16)

    cq = pm[:, :C_Q_RANK]
    cqn = (cq * _rms_scale(cq) * cqn_ref[...]).astype(BF16)
    qc = jnp.dot(cqn, wq_ref[...], preferred_element_type=F32)
    ckv = pm[:, C_Q_RANK:C_Q_RANK + C_KV_RANK]
    ckvn = (ckv * _rms_scale(ckv) * ckvn_ref[...]).astype(BF16)
    kv = jnp.dot(ckvn, wkv_ref[...], preferred_element_type=F32)
    kr = _rope3(pm[:, C_Q_RANK + C_KV_RANK:_SLAB_Q0], cck_ref, s1ck_ref, s2ck_ref, half_rope)
    for h in range(C_HEADS):
        sl = slice(h * LANES, (h + 1) * LANES)
        qc_ref[:, sl] = with_norm(_rope3(qc[:, sl], ccq_ref, s1cq_ref, s2cq_ref, half_rope), None, 1, h)
        kc_ref[:, sl] = with_norm(kv[:, sl] + kr, -1.0, 0, h)
        vc_ref[:, sl] = (kv[:, C_HEADS * LANES + h * LANES:C_HEADS * LANES + (h + 1) * LANES] + one64).astype(BF16)

    def head_norm(xg, gn_ref):
        ss = jnp.sum(xg * xg, axis=-1, keepdims=True) * (1.0 / HEAD_DIM)
        return xg * lax.rsqrt(ss + EPS) * gn_ref[...]

    for g in range(D_HEADS):
        xg = pm[:, _SLAB_Q0 + g * LANES:_SLAB_Q0 + (g + 1) * LANES]
        qd_ref[:, g * LANES:(g + 1) * LANES] = with_norm(
            _rope3(head_norm(xg, dqn_ref), cdq_ref, s1dq_ref, s2dq_ref, HEAD_DIM // 4), None, 1, C_HEADS + g)
    for g in range(D_KV_HEADS):
        sl = slice(g * LANES, (g + 1) * LANES)
        xg = pm[:, _SLAB_K0 + g * LANES:_SLAB_K0 + (g + 1) * LANES]
        kd_ref[:, sl] = with_norm(_rope3(head_norm(xg, dkn_ref), cdk_ref, s1dk_ref, s2dk_ref, HEAD_DIM // 4),
                                  -1.0, 0, C_HEADS + g)
        vd_ref[:, sl] = (pm[:, _SLAB_V0 + g * LANES:_SLAB_V0 + (g + 1) * LANES] + one64).astype(BF16)

    new = jnp.concatenate(stats + [jnp.zeros((6, LANES), F32)], axis=0)

    @pl.when(pl.program_id(0) % steps_per_seq == 0)
    def _():
        stat_ref[...] = new

    @pl.when(pl.program_id(0) % steps_per_seq != 0)
    def _():
        stat_ref[...] = jnp.maximum(stat_ref[...], new)


def _slabs(w, n_heads, width, lane_off=0):
    k = w.shape[0]
    w3 = w.reshape(k, n_heads, width)
    w3 = jnp.pad(w3, ((0, 0), (0, 0), (lane_off, LANES - width - lane_off)))
    return w3.reshape(k, n_heads * LANES)


def _axial_tables(row, col, scale):
    half = HEAD_DIM // 2
    cr, s1r, s2r = _rope_tables(row, D_THETA, half, 0, LANES, scale)
    cc, s1c, s2c = _rope_tables(col, D_THETA, half, half, LANES, scale)
    lane = jnp.arange(LANES)[None, :]
    return jnp.where(lane < half, cr, cc), s1r + s1c, s2r + s2c


def _proj_odd(x2d, seq, g_mix, w_in, cq_norm, w_cq_up, ckv_norm, w_ckv_up, dq_norm, dk_norm, tm=512):
    t = x2d.shape[0]
    nblk = seq // tm
    o1 = C_Q_RANK
    o2 = o1 + C_KV_RANK
    o3 = o2 + C_ROPE
    o4 = o3 + D_HEADS * HEAD_DIM
    o5 = o4 + D_KV_HEADS * HEAD_DIM
    wm = jnp.concatenate([
        w_in[:, :o2],
        _slabs(w_in[:, o2:o3], 1, C_ROPE, C_NOPE),
        _slabs(w_in[:, o3:o4], D_HEADS, HEAD_DIM),
        _slabs(w_in[:, o4:o5], D_KV_HEADS, HEAD_DIM),
        _slabs(w_in[:, o5:], D_KV_HEADS, HEAD_DIM),
    ], axis=1).astype(BF16)
    assert wm.shape[1] == _ODD_COLS
    wq = _slabs(w_cq_up, C_HEADS, C_NOPE + C_ROPE).astype(BF16)
    kv3 = w_ckv_up.reshape(C_KV_RANK, C_HEADS, 2 * HEAD_DIM)
    wkv = jnp.concatenate([
        _slabs(kv3[:, :, :C_NOPE].reshape(C_KV_RANK, -1), C_HEADS, C_NOPE),
        _slabs(kv3[:, :, C_NOPE:].reshape(C_KV_RANK, -1), C_HEADS, HEAD_DIM),
    ], axis=1).astype(BF16)
    pad64 = lambda g: jnp.pad(g, (0, LANES - HEAD_DIM)).reshape(1, LANES)

    pos = jnp.arange(seq, dtype=I32)
    row_pos = pos // GRID_W
    col_pos = pos % GRID_W
    c_scale = (C_NOPE + C_ROPE) ** -0.5 * LOG2E
    d_scale = HEAD_DIM ** -0.5 * LOG2E
    tabs = (_rope_tables(pos, ROPE_THETA, C_ROPE, C_NOPE, LANES, c_scale)
            + _rope_tables(pos, ROPE_THETA, C_ROPE, C_NOPE, LANES, 1.0)
            + _axial_tables(row_pos, col_pos, d_scale)
            + _axial_tables(row_pos, col_pos, 1.0))

    row = lambda i: (i, 0)
    full = lambda i: (0, 0)
    tspec = pl.BlockSpec((tm, LANES), lambda i: (i % nblk, 0))
    wide = C_HEADS * LANES
    kvw = D_KV_HEADS * LANES
    return pl.pallas_call(
        functools.partial(_proj_odd_kernel, steps_per_seq=nblk),
        grid=(t // tm,),
        in_specs=[
            pl.BlockSpec((tm, D_MODEL), row),
            pl.BlockSpec((1, D_MODEL), full),
            pl.BlockSpec(wm.shape, full),
            pl.BlockSpec((1, C_Q_RANK), full),
            pl.BlockSpec(wq.shape, full),
            pl.BlockSpec((1, C_KV_RANK), full),
            pl.BlockSpec(wkv.shape, full),
            pl.BlockSpec((1, LANES), full),
            pl.BlockSpec((1, LANES), full),
        ] + [tspec] * 12,
        out_specs=[
            pl.BlockSpec((tm, wide), row), pl.BlockSpec((tm, wide), row), pl.BlockSpec((tm, wide), row),
            pl.BlockSpec((tm, wide), row), pl.BlockSpec((tm, kvw), row), pl.BlockSpec((tm, kvw), row),
            pl.BlockSpec((None, 8, LANES), lambda i: (i // nblk, 0, 0)),
        ],
        out_shape=[
            jax.ShapeDtypeStruct((t, wide), BF16), jax.ShapeDtypeStruct((t, wide), BF16),
            jax.ShapeDtypeStruct((t, wide), BF16), jax.ShapeDtypeStruct((t, wide), BF16),
            jax.ShapeDtypeStruct((t, kvw), BF16), jax.ShapeDtypeStruct((t, kvw), BF16),
            jax.ShapeDtypeStruct((t // seq, 8, LANES), F32),
        ],
        compiler_params=_cparams(("arbitrary",)),
        name="proj_odd",
    )(x2d, g_mix.reshape(1, D_MODEL), wm, cq_norm.reshape(1, -1), wq, ckv_norm.reshape(1, -1), wkv,
      pad64(dq_norm), pad64(dk_norm), *tabs)


def _flash_kernel(q_ref, k_ref, v_ref, o_ref, qs_sc, m_sc, acc_sc, *, group, tq, tk):
    ki = pl.program_id(3)

    @pl.when(ki == 0)
    def _():
        for g in range(group):
            qs_sc[g * tq:(g + 1) * tq, :] = q_ref[:, g * LANES:(g + 1) * LANES]
        m_sc[...] = jnp.full(m_sc.shape, NEG, F32)
        acc_sc[...] = jnp.zeros(acc_sc.shape, F32)

    s = lax.dot_general(qs_sc[...], k_ref[...], _NT, preferred_element_type=F32)
    m_prev = m_sc[...]
    m_new = jnp.maximum(m_prev, jnp.max(s, axis=1, keepdims=True))
    alpha = jnp.exp2(m_prev - m_new)
    p = jnp.exp2(s - jnp.tile(m_new, (1, tk // LANES)))
    acc_sc[...] = alpha * acc_sc[...] + jnp.dot(p.astype(BF16), v_ref[...], preferred_element_type=F32)
    m_sc[...] = m_new

    @pl.when(ki == pl.num_programs(3) - 1)
    def _():
        acc = acc_sc[...]
        o = acc / acc[:, HEAD_DIM:HEAD_DIM + 1]
        for g in range(group):
            o_ref[:, g * LANES:(g + 1) * LANES] = o[g * tq:(g + 1) * tq].astype(BF16)


def _flash_bounded_kernel(nk_ref, q_ref, k_ref, v_ref, o_ref, qs_sc, acc_sc, *, group, tq):
    ki = pl.program_id(3)
    head = pl.program_id(0) * pl.num_programs(1) + pl.program_id(1)

    @pl.when(ki == 0)
    def _():
        nk = nk_ref[head] * _NORM_MARGIN
        fix = jnp.where(lax.broadcasted_iota(I32, (1, LANES), 1) == LANES - 1, nk, 1.0)
        for g in range(group):
            qs_sc[g * tq:(g + 1) * tq, :] = (q_ref[:, g * LANES:(g + 1) * LANES].astype(F32) * fix).astype(BF16)
        acc_sc[...] = jnp.zeros(acc_sc.shape, F32)

    s = lax.dot_general(qs_sc[...], k_ref[...], _NT, preferred_element_type=F32)
    acc_sc[...] += jnp.dot(jnp.exp2(s).astype(BF16), v_ref[...], preferred_element_type=F32)

    @pl.when(ki == pl.num_programs(3) - 1)
    def _():
        acc = acc_sc[...]
        o = acc / acc[:, HEAD_DIM:HEAD_DIM + 1]
        for g in range(group):
            o_ref[:, g * LANES:(g + 1) * LANES] = o[g * tq:(g + 1) * tq].astype(BF16)


def _flash(q, k, v, group, nk, bounded, rows=1024, tk=512, tk_bounded=2048):
    b, s, qw = q.shape
    hk = k.shape[2] // LANES
    tq = rows // group
    tk = min(tk, s)
    tkb = min(tk_bounded, s)
    out_shape = jax.ShapeDtypeStruct((b, s, qw), BF16)
    sem = ("parallel", "parallel", "parallel", "arbitrary")

    def running_max(q, k, v, nk):
        qspec = pl.BlockSpec((None, tq, group * LANES), lambda bi, hi, qi, ki: (bi, qi, hi))
        kspec = pl.BlockSpec((None, tk, LANES), lambda bi, hi, qi, ki: (bi, ki, hi))
        return pl.pallas_call(
            functools.partial(_flash_kernel, group=group, tq=tq, tk=tk),
            grid=(b, hk, s // tq, s // tk),
            in_specs=[qspec, kspec, kspec],
            out_specs=qspec,
            out_shape=out_shape,
            scratch_shapes=[pltpu.VMEM((rows, LANES), BF16), pltpu.VMEM((rows, LANES), F32),
                            pltpu.VMEM((rows, LANES), F32)],
            compiler_params=_cparams(sem),
            name="flash",
        )(q, k, v)

    def bound(q, k, v, nk):
        qspec = pl.BlockSpec((None, tq, group * LANES), lambda bi, hi, qi, ki, nkr: (bi, qi, hi))
        kspec = pl.BlockSpec((None, tkb, LANES), lambda bi, hi, qi, ki, nkr: (bi, ki, hi))
        return pl.pallas_call(
            functools.partial(_flash_bounded_kernel, group=group, tq=tq),
            grid_spec=pltpu.PrefetchScalarGridSpec(
                num_scalar_prefetch=1,
                grid=(b, hk, s // tq, s // tkb),
                in_specs=[qspec, kspec, kspec],
                out_specs=qspec,
                scratch_shapes=[pltpu.VMEM((rows, LANES), BF16), pltpu.VMEM((rows, LANES), F32)],
            ),
            out_shape=out_shape,
            compiler_params=_cparams(sem),
            name="flash_bounded",
        )(nk.reshape(-1), q, k, v)

    return lax.cond(bounded, bound, running_max, q, k, v, nk)


def _moe(x1, h2, aff, batch, seq, w_gate, w_up, w_down, g_final, final):
    idx, gates, spos, cb = _route(aff.reshape(-1, LANES), batch, seq)
    y = _ffn(idx, gates, h2, w_gate, w_up, w_down, seq)
    return _combine(x1, spos.reshape(-1, N_EXPERTS, LANES), cb, y, g_final, final, seq)


def kernel(x, norm_mix, norm_ffn, even_w_in, even_gmlp_norm, even_w_spatial, even_b_spatial, even_w_out,
           odd_w_in, odd_cq_norm, odd_w_cq_up, odd_ckv_norm, odd_w_ckv_up, odd_dq_norm, odd_dk_norm, odd_w_out,
           moe_w_router, moe_w_gate, moe_w_up, moe_w_down, final_norm):
    b, s, d = x.shape
    depth = norm_mix.shape[0]
    x2d = x.reshape(b * s, d)
    for i in range(depth):
        j = i // 2
        last = i == depth - 1
        if i % 2 == 0:
            q, k, v, go = _proj_even(x2d, s, norm_mix[i], even_w_in[j], even_gmlp_norm[j], even_w_spatial[j],
                                     even_b_spatial[j])
            a = _dilated(q.reshape(b, s, A_WIDTH), k.reshape(b, s, A_WIDTH), v.reshape(b, s, A_WIDTH))
            x1, h2, aff = _outproj(x2d, a.reshape(b * s, A_WIDTH), go, even_w_out[j][:A_WIDTH],
                                   even_w_out[j][A_WIDTH:], norm_ffn[i], moe_w_router[i])
        else:
            qc, kc, vc, qd, kd, vd, stat = _proj_odd(x2d, s, norm_mix[i], odd_w_in[j], odd_cq_norm[j],
                                                     odd_w_cq_up[j], odd_ckv_norm[j], odd_w_ckv_up[j],
                                                     odd_dq_norm[j], odd_dk_norm[j])
            grp = D_HEADS // D_KV_HEADS
            kn_c, kn_d = stat[:, 0, :C_HEADS], stat[:, 0, C_HEADS:C_HEADS + D_KV_HEADS]
            qn_c = stat[:, 1, :C_HEADS]
            qn_d = stat[:, 1, C_HEADS:C_HEADS + D_HEADS].reshape(b, D_KV_HEADS, grp)
            worst = jnp.maximum(jnp.max(qn_c * kn_c), jnp.max(qn_d * kn_d[:, :, None])) * _NORM_MARGIN ** 2
            bounded = worst <= _MAX_SCORE_BOUND
            r3 = lambda z: z.reshape(b, s, -1)
            oc = _flash(r3(qc), r3(kc), r3(vc), 1, kn_c, bounded)
            od = _flash(r3(qd), r3(kd), r3(vd), grp, kn_d, bounded)
            cw = C_HEADS * HEAD_DIM
            x1, h2, aff = _outproj(x2d, oc.reshape(b * s, -1), od.reshape(b * s, -1),
                                   _slabs(odd_w_out[j][:cw].T, C_HEADS, HEAD_DIM).T,
                                   _slabs(odd_w_out[j][cw:].T, D_HEADS, HEAD_DIM).T,
                                   norm_ffn[i], moe_w_router[i])
        x2d = _moe(x1, h2, aff, b, s, moe_w_gate[i], moe_w_up[i], moe_w_down[i], final_norm, last)
    return x2d.reshape(b, s, d)
```

```python
import functools
import math

import jax
import jax.numpy as jnp
from jax import lax
from jax.experimental import pallas as pl
from jax.experimental.pallas import tpu as pltpu

F32 = jnp.float32
BF16 = jnp.bfloat16
I32 = jnp.int32

EPS = 1e-6
NEG = -1e30
LOG2E = 1.4426950408889634

D_MODEL = 1024
HEAD_DIM = 64
ROPE_THETA = 500000.0
ROT_DIM = 16
GRID_W = 64
A_HEADS = 12
A_WIDTH = 768
A_DILATIONS = (1, 4, 16)
A_RADIUS = 64
B_WIDTH = 256
B_GROUPS = 4
B_CHUNK = 128
C_HEADS = 8
C_Q_RANK = 256
C_KV_RANK = 128
C_NOPE = 64
C_ROPE = 32
D_HEADS = 8
D_KV_HEADS = 2
D_THETA = 10000.0
N_EXPERTS = 16
EC_FACTOR = 2
EXPERT_FF = 512

_NORM_MARGIN = 1.01
_MAX_SCORE_BOUND = 55.0

LANES = 128
VMEM_LIMIT = 48 * 1024 * 1024
_DILATED_VMEM_LIMIT = 56 * 1024 * 1024

_NT = (((1,), (1,)), ((), ()))


def _cparams(sem):
    return pltpu.CompilerParams(dimension_semantics=sem, vmem_limit_bytes=VMEM_LIMIT)


def _rms_scale(x):
    return lax.rsqrt(jnp.mean(x * x, axis=-1, keepdims=True) + EPS)


def _rope3(a, c_ref, s1_ref, s2_ref, shift):
    return (a * c_ref[...] + pltpu.roll(a, LANES - shift, 1) * s1_ref[...]
            + pltpu.roll(a, shift, 1) * s2_ref[...])


def _split_dot(x, w_bf16):
    hi = x.astype(BF16)
    lo = (x - hi.astype(F32)).astype(BF16)
    return (jnp.dot(hi, w_bf16, preferred_element_type=F32)
            + jnp.dot(lo, w_bf16, preferred_element_type=F32))


def _proj_even_kernel(x_ref, g_ref, w_ref, cq_ref, s1q_ref, s2q_ref, ck_ref, s1k_ref, s2k_ref,
                      gn_ref, gmat_ref, ws_ref, bs_ref,
                      q_ref, k_ref, v_ref, go_ref):
    x = x_ref[...]
    y = (x * _rms_scale(x) * g_ref[...]).astype(BF16)
    tm = x.shape[0]

    aq = jnp.dot(y, w_ref[:, 0:A_WIDTH], preferred_element_type=F32)
    for j in range(A_WIDTH // LANES):
        sl = slice(j * LANES, (j + 1) * LANES)
        q_ref[:, sl] = _rope3(aq[:, sl], cq_ref, s1q_ref, s2q_ref, ROT_DIM // 2)
    ak = jnp.dot(y, w_ref[:, A_WIDTH:2 * A_WIDTH], preferred_element_type=F32)
    for j in range(A_WIDTH // LANES):
        sl = slice(j * LANES, (j + 1) * LANES)
        k_ref[:, sl] = _rope3(ak[:, sl], ck_ref, s1k_ref, s2k_ref, ROT_DIM // 2)
    v_ref[...] = jnp.dot(y, w_ref[:, 2 * A_WIDTH:3 * A_WIDTH], preferred_element_type=F32)

    z = jnp.dot(y, w_ref[:, 3 * A_WIDTH:3 * A_WIDTH + 2 * B_WIDTH], preferred_element_type=F32)
    ge = jax.nn.gelu(z)
    u = ge[:, :B_WIDTH]
    vv = ge[:, B_WIDTH:]
    ss = _split_dot(vv * vv, gmat_ref[...])
    vn = (vv * lax.rsqrt(ss + EPS) * gn_ref[...]).astype(BF16)
    grp = lax.broadcasted_iota(I32, (B_CHUNK, B_WIDTH), 1) // (B_WIDTH // B_GROUPS)
    for c in range(tm // B_CHUNK):
        rows = slice(c * B_CHUNK, (c + 1) * B_CHUNK)
        vc = vn[rows]
        mg = [jnp.dot(ws_ref[g], vc, preferred_element_type=F32) for g in range(B_GROUPS)]
        mixed = jnp.where(grp == 0, mg[0], jnp.where(grp == 1, mg[1], jnp.where(grp == 2, mg[2], mg[3])))
        go_ref[rows, :] = (u[rows] * (mixed + bs_ref[...])).astype(BF16)


def _rope_tables(pos, theta, r, lane_off, period, scale):
    half = r // 2
    inv = jnp.power(jnp.float32(theta), -jnp.arange(half, dtype=F32) * (2.0 / r))
    ang = pos.astype(F32)[:, None] * inv[None, :]
    cos, sin = jnp.cos(ang), jnp.sin(ang)
    o = (jnp.arange(LANES) % period) - lane_off
    in_lo = (o >= 0) & (o < half)
    in_hi = (o >= half) & (o < r)
    idx = jnp.clip(jnp.where(in_hi, o - half, o), 0, half - 1)
    c = jnp.where((in_lo | in_hi)[None, :], cos[:, idx], 1.0)
    s1 = jnp.where(in_lo[None, :], -sin[:, idx], 0.0)
    s2 = jnp.where(in_hi[None, :], sin[:, idx], 0.0)
    return c * scale, s1 * scale, s2 * scale


def _proj_even(x2d, seq, g_mix, w_in, gmlp_norm, w_s, b_s, tm=512):
    t = x2d.shape[0]
    nblk = seq // tm
    pos = jnp.arange(seq, dtype=I32)
    qscale = HEAD_DIM ** -0.5 * LOG2E
    cq, s1q, s2q = _rope_tables(pos, ROPE_THETA, ROT_DIM, 0, HEAD_DIM, qscale)
    ck, s1k, s2k = _rope_tables(pos, ROPE_THETA, ROT_DIM, 0, HEAD_DIM, 1.0)
    gdim = B_WIDTH // B_GROUPS
    gid = jnp.arange(B_WIDTH) // gdim
    gmat = jnp.where(gid[:, None] == gid[None, :], 1.0 / gdim, 0.0).astype(BF16)
    bias = jnp.repeat(b_s.T, gdim, axis=1)
    row = lambda i: (i, 0)
    tab = lambda i: (i % nblk, 0)
    full = lambda i: (0, 0)
    tspec = pl.BlockSpec((tm, LANES), tab)
    return pl.pallas_call(
        _proj_even_kernel,
        grid=(t // tm,),
        in_specs=[
            pl.BlockSpec((tm, D_MODEL), row),
            pl.BlockSpec((1, D_MODEL), full),
            pl.BlockSpec(w_in.shape, full),
            tspec, tspec, tspec, tspec, tspec, tspec,
            pl.BlockSpec((1, B_WIDTH), full),
            pl.BlockSpec((B_WIDTH, B_WIDTH), full),
            pl.BlockSpec((B_GROUPS, B_CHUNK, B_CHUNK), lambda i: (0, 0, 0)),
            pl.BlockSpec((B_CHUNK, B_WIDTH), full),
        ],
        out_specs=[
            pl.BlockSpec((tm, A_WIDTH), row),
            pl.BlockSpec((tm, A_WIDTH), row),
            pl.BlockSpec((tm, A_WIDTH), row),
            pl.BlockSpec((tm, B_WIDTH), row),
        ],
        out_shape=[
            jax.ShapeDtypeStruct((t, A_WIDTH), F32),
            jax.ShapeDtypeStruct((t, A_WIDTH), F32),
            jax.ShapeDtypeStruct((t, A_WIDTH), F32),
            jax.ShapeDtypeStruct((t, B_WIDTH), BF16),
        ],
        compiler_params=_cparams(("parallel",)),
        name="proj_even",
    )(x2d, g_mix.reshape(1, D_MODEL), w_in.astype(BF16), cq, s1q, s2q, ck, s1k, s2k,
      gmlp_norm.reshape(1, B_WIDTH), gmat, w_s.astype(BF16), bias)


_TQ = 128
_TK = _TQ + 2 * A_RADIUS
_NORM_ROWS = 512


def _dilated_kernel(q_ref, k_ref, v_ref, o_ref, b_sc, l_sc, bias_sc, *, seq):
    half0 = lax.broadcasted_iota(I32, (1, LANES), 1) < HEAD_DIM
    row_head = lax.broadcasted_iota(I32, (LANES, LANES), 0) // HEAD_DIM
    pick = [jnp.where(row_head == h, 1.0, 0.0).astype(BF16) for h in range(2)]
    n_pat = len(A_DILATIONS)
    n_norm = seq // _NORM_ROWS
    zero_row = jnp.zeros((1, LANES), F32)

    def head_sq(ref, c):
        rows = pl.ds(pl.multiple_of(c * _NORM_ROWS, _NORM_ROWS), _NORM_ROWS)
        xb = ref[rows, :].astype(BF16).astype(F32)
        sq = xb * xb
        return rows, [_split_dot(sq, pick[h]) for h in range(2)]

    def k_pass(c, mx):
        k2 = head_sq(k_ref, c)[1]
        return tuple(jnp.maximum(mx[h], jnp.max(k2[h], axis=0, keepdims=True)) for h in range(2))

    max_k2 = lax.fori_loop(0, n_norm, k_pass, (zero_row, zero_row))

    def q_pass(c, mx):
        rows, q2 = head_sq(q_ref, c)
        for h in range(2):
            bound = jnp.sqrt(q2[h] * max_k2[h]) * (_NORM_MARGIN * _NORM_MARGIN)
            b_sc.at[h][rows, :] = bound
            mx = jnp.maximum(mx, jnp.max(bound, axis=0, keepdims=True))
        return mx

    worst = jnp.max(lax.fori_loop(0, n_norm, q_pass, zero_row))

    diff = lax.broadcasted_iota(I32, (_TQ, _TK), 1) - lax.broadcasted_iota(I32, (_TQ, _TK), 0)
    for case in range(3):
        bias_sc[case] = jnp.where(jnp.abs(diff - case * A_RADIUS) <= A_RADIUS, 0.0, NEG)
    one_bf16 = jnp.ones((), BF16)

    def run(bounded):
        for pi, d in enumerate(A_DILATIONS):
            cls_len = seq // d
            tpc = cls_len // _TQ

            def body(j, carry, d=d, cls_len=cls_len, tpc=tpc, pi=pi):
                i = j // tpc
                n = j % tpc
                l0 = n * _TQ
                kst = jnp.clip(l0 - A_RADIUS, 0, cls_len - _TK)
                bias = bias_sc[(l0 - kst) // A_RADIUS]
                if d == 1:
                    qrows = pl.ds(pl.multiple_of(l0, _TQ), _TQ)
                    krows = pl.ds(pl.multiple_of(kst, A_RADIUS), _TK)
                else:
                    qrows = pl.ds(l0 * d + i, _TQ, stride=d)
                    krows = pl.ds(kst * d + i, _TK, stride=d)
                q = q_ref[qrows, :]
                kb = k_ref[krows, :].astype(BF16)
                vb = v_ref[krows, :].astype(BF16)
                parts = []
                for h in range(2):
                    qh = jnp.where(half0 if h == 0 else jnp.logical_not(half0), q, 0.0).astype(BF16)
                    s = lax.dot_general(qh, kb, _NT, preferred_element_type=F32) + bias
                    if bounded:
                        mt = jnp.tile(b_sc.at[h][qrows, :], (1, _TK // LANES))
                    else:
                        mt = jnp.max(s, axis=-1, keepdims=True)
                    p = jnp.exp2(s - mt).astype(BF16)
                    vh = jnp.where(half0 if h == 0 else jnp.logical_not(half0), vb, one_bf16)
                    parts.append((mt, jnp.dot(p, vh, preferred_element_type=F32)))
                ot = jnp.where(half0, parts[0][1], parts[1][1])
                lt = pltpu.roll(jnp.where(half0, parts[1][1], parts[0][1]), HEAD_DIM, 1)
                if bounded:
                    if pi > 0:
                        lt = l_sc[qrows, :] + lt
                        ot = o_ref[qrows, :] + ot
                else:
                    mt = jnp.where(half0, parts[0][0], parts[1][0])
                    if pi > 0:
                        mp = b_sc.at[0][qrows, :]
                        mn = jnp.maximum(mp, mt)
                        a = jnp.exp2(mp - mn)
                        b = jnp.exp2(mt - mn)
                        lt = a * l_sc[qrows, :] + b * lt
                        ot = a * o_ref[qrows, :] + b * ot
                        mt = mn
                    if pi < n_pat - 1:
                        b_sc.at[0][qrows, :] = mt
                if pi == n_pat - 1:
                    o_ref[qrows, :] = ot / lt
                else:
                    l_sc[qrows, :] = lt
                    o_ref[qrows, :] = ot
                return carry

            lax.fori_loop(0, seq // _TQ, body, 0, unroll=4)

    @pl.when(worst <= _MAX_SCORE_BOUND)
    def _():
        run(True)

    @pl.when(jnp.logical_not(worst <= _MAX_SCORE_BOUND))
    def _():
        run(False)


def _dilated(q, k, v):
    b, s, w = q.shape
    spec = pl.BlockSpec((None, s, LANES), lambda bi, hi: (bi, 0, hi))
    return pl.pallas_call(
        functools.partial(_dilated_kernel, seq=s),
        grid=(b, w // LANES),
        in_specs=[spec, spec, spec],
        out_specs=spec,
        out_shape=jax.ShapeDtypeStruct((b, s, w), F32),
        scratch_shapes=[pltpu.VMEM((2, s, LANES), F32), pltpu.VMEM((s, LANES), F32),
                        pltpu.VMEM((3, _TQ, _TK), F32)],
        compiler_params=pltpu.CompilerParams(dimension_semantics=("parallel", "parallel"),
                                             vmem_limit_bytes=_DILATED_VMEM_LIMIT),
        name="dilated",
    )(q, k, v)


def _outproj_kernel(x_ref, a_ref, b_ref, wa_ref, wb_ref, gf_ref, wr_ref, x1_ref, h2_ref, aff_ref):
    x1 = (x_ref[...]
          + jnp.dot(a_ref[...].astype(BF16), wa_ref[...], preferred_element_type=F32)
          + jnp.dot(b_ref[...].astype(BF16), wb_ref[...], preferred_element_type=F32))
    x1_ref[...] = x1
    h2 = x1 * _rms_scale(x1) * gf_ref[...]
    h2_ref[...] = h2
    w_hi = wr_ref[0]
    w_lo = wr_ref[1]
    hi = h2.astype(BF16)
    lo = (h2 - hi.astype(F32)).astype(BF16)
    lg = (jnp.dot(hi, w_hi, preferred_element_type=F32) + jnp.dot(lo, w_hi, preferred_element_type=F32)
          + jnp.dot(hi, w_lo, preferred_element_type=F32))
    valid = lax.broadcasted_iota(I32, lg.shape, 1) < N_EXPERTS
    lg = jnp.where(valid, lg, NEG)
    e = jnp.exp(lg - jnp.max(lg, axis=-1, keepdims=True))
    aff = e / jnp.sum(e, axis=-1, keepdims=True)
    aff_t = aff.T
    for j in range(aff.shape[0] // LANES):
        aff_ref[j] = aff_t[:N_EXPERTS, j * LANES:(j + 1) * LANES]


def _outproj(x2d, a, b, wa, wb, g_ffn, w_router, tm=512):
    t = x2d.shape[0]
    wr = jnp.pad(w_router, ((0, 0), (0, LANES - N_EXPERTS)))
    wr_hi = wr.astype(BF16)
    wr_lo = (wr - wr_hi.astype(F32)).astype(BF16)
    wr2 = jnp.stack([wr_hi, wr_lo])
    row = lambda i: (i, 0)
    full = lambda i: (0, 0)
    return pl.pallas_call(
        _outproj_kernel,
        grid=(t // tm,),
        in_specs=[
            pl.BlockSpec((tm, D_MODEL), row),
            pl.BlockSpec((tm, a.shape[1]), row),
            pl.BlockSpec((tm, b.shape[1]), row),
            pl.BlockSpec(wa.shape, full),
            pl.BlockSpec(wb.shape, full),
            pl.BlockSpec((1, D_MODEL), full),
            pl.BlockSpec((2, D_MODEL, LANES), lambda i: (0, 0, 0)),
        ],
        out_specs=[
            pl.BlockSpec((tm, D_MODEL), row),
            pl.BlockSpec((tm, D_MODEL), row),
            pl.BlockSpec((tm // LANES, N_EXPERTS, LANES), lambda i: (i, 0, 0)),
        ],
        out_shape=[
            jax.ShapeDtypeStruct((t, D_MODEL), F32),
            jax.ShapeDtypeStruct((t, D_MODEL), F32),
            jax.ShapeDtypeStruct((t // LANES, N_EXPERTS, LANES), F32),
        ],
        compiler_params=_cparams(("parallel",)),
        name="outproj",
    )(x2d, a, b, wa.astype(BF16), wb.astype(BF16), g_ffn.reshape(1, D_MODEL), wr2)


def _route_kernel(aff_ref, idx_ref, gate_ref, spos_ref, cb_ref, thr_sc, need_sc, *, cap):
    nblk = aff_ref.shape[0] // N_EXPERTS
    bits = pltpu.bitcast(aff_ref[...], I32).reshape(nblk, N_EXPERTS, LANES)

    def count(pred):
        return jnp.sum(jnp.sum(jnp.where(pred, 1.0, 0.0), axis=0), axis=1, keepdims=True)

    def search(it, thr):
        cand = thr | jnp.left_shift(jnp.int32(1), 30 - it)
        return jnp.where(count(bits >= cand[None]) >= cap, cand, thr)

    thr = lax.fori_loop(0, 31, search, jnp.zeros((N_EXPERTS, 1), I32))
    need = cap - count(bits > thr[None])
    thr_sc[...] = jnp.broadcast_to(thr, (N_EXPERTS, LANES))
    need_sc[...] = jnp.broadcast_to(need, (N_EXPERTS, LANES))

    ri = lax.broadcasted_iota(I32, (LANES, LANES), 0)
    ci = lax.broadcasted_iota(I32, (LANES, LANES), 1)
    upper = jnp.where(ri <= ci, 1.0, 0.0).astype(BF16)
    lower = jnp.where(ci <= ri, 1.0, 0.0).astype(BF16)
    eye = jnp.where(ri == ci, 1.0, 0.0).astype(BF16)
    ones = jnp.ones((LANES, LANES), BF16)
    bi = lax.broadcasted_iota(I32, (nblk, nblk), 0)
    bj = lax.broadcasted_iota(I32, (nblk, nblk), 1)
    strict = jnp.where(bj < bi, 1.0, 0.0).astype(BF16)
    before = jnp.where(bi < bj, 1.0, 0.0).astype(BF16)
    mean_rows = jnp.full((8, LANES), 1.0 / LANES, BF16)
    c_row = lax.broadcasted_iota(I32, (1, cap), 1).astype(F32)
    blk_iota = lax.broadcasted_iota(I32, (nblk, cap), 0).astype(F32)
    t_iota = lax.broadcasted_iota(I32, (LANES, cap), 0).astype(F32)
    rep = cap // LANES

    def cums(mask_bf16):
        lp = jnp.dot(mask_bf16, upper, preferred_element_type=F32)
        bc = jnp.dot(mask_bf16, ones, preferred_element_type=F32)
        bst = jnp.dot(strict, bc.astype(BF16), preferred_element_type=F32)
        return lp, bc, bst

    def per_expert(e, carry):
        a = aff_ref[pl.ds(e, nblk, stride=N_EXPERTS), :]
        ab = pltpu.bitcast(a, I32)
        thr_e = thr_sc[pl.ds(e, 1), :]
        need_e = need_sc[pl.ds(e, 1), :]
        gt = ab > thr_e
        eq = ab == thr_e
        eqf = jnp.where(eq, 1.0, 0.0)
        lp_q, _, bst_q = cums(eqf.astype(BF16))
        sel = jnp.logical_or(gt, jnp.logical_and(eq, bst_q + lp_q - eqf < need_e))
        mb = jnp.where(sel, 1.0, 0.0).astype(BF16)
        lp, bc, bst = cums(mb)
        spos_ref[pl.ds(e, nblk, stride=N_EXPERTS), :] = jnp.where(sel, bst + lp - 1.0, -1.0)
        bc_row = lax.dot_general(mean_rows, bc.astype(BF16), _NT, preferred_element_type=F32)
        cb_ref[pl.ds(e, 1), :] = jnp.dot(bc_row.astype(BF16), before, preferred_element_type=F32)[:1].astype(I32)
        bend_w = jnp.tile(bst + bc, (1, rep))
        bst_w = jnp.tile(bst, (1, rep))
        blk_c = jnp.sum(jnp.where(bend_w <= c_row, 1.0, 0.0), axis=0, keepdims=True)
        onehot = blk_iota == blk_c
        bst_c = jnp.sum(jnp.where(onehot, bst_w, 0.0), axis=0, keepdims=True)
        r_c = c_row - bst_c
        ohb = jnp.where(onehot, 1.0, 0.0).astype(BF16)
        lp_t = lax.dot_general(lower, mb, _NT, preferred_element_type=F32)
        lp_c = jnp.dot(lp_t.astype(BF16), ohb, preferred_element_type=F32)
        tl_c = jnp.sum(jnp.where(lp_c <= r_c, 1.0, 0.0), axis=0, keepdims=True)
        idx_ref[pl.ds(e, 1), :] = (blk_c * LANES + tl_c).astype(I32)
        a_hi = a.astype(BF16)
        a_lo = (a - a_hi.astype(F32)).astype(BF16)
        at_hi = lax.dot_general(eye, a_hi, _NT, preferred_element_type=F32).astype(BF16)
        at_lo = lax.dot_general(eye, a_lo, _NT, preferred_element_type=F32).astype(BF16)
        g_c = (jnp.dot(at_hi, ohb, preferred_element_type=F32)
               + jnp.dot(at_lo, ohb, preferred_element_type=F32))
        gate_ref[pl.ds(e, 1), :] = jnp.sum(jnp.where(t_iota == tl_c, g_c, 0.0), axis=0, keepdims=True)
        return carry

    lax.fori_loop(0, N_EXPERTS, per_expert, 0)


def _route(aff2d, batch, seq):
    cap = EC_FACTOR * seq // N_EXPERTS
    nblk = seq // LANES
    rows = nblk * N_EXPERTS
    out_spec = pl.BlockSpec((None, N_EXPERTS, cap), lambda b: (b, 0, 0))
    return pl.pallas_call(
        functools.partial(_route_kernel, cap=cap),
        grid=(batch,),
        in_specs=[pl.BlockSpec((rows, LANES), lambda b: (b, 0))],
        out_specs=[out_spec, out_spec, pl.BlockSpec((rows, LANES), lambda b: (b, 0)),
                   pl.BlockSpec((None, N_EXPERTS, nblk), lambda b: (b, 0, 0))],
        out_shape=[jax.ShapeDtypeStruct((batch, N_EXPERTS, cap), I32),
                   jax.ShapeDtypeStruct((batch, N_EXPERTS, cap), F32),
                   jax.ShapeDtypeStruct((batch * rows, LANES), F32),
                   jax.ShapeDtypeStruct((batch, N_EXPERTS, nblk), I32)],
        scratch_shapes=[pltpu.VMEM((N_EXPERTS, LANES), I32), pltpu.VMEM((N_EXPERTS, LANES), F32)],
        compiler_params=_cparams(("parallel",)),
        name="route",
    )(aff2d)


def _ffn_kernel(idx_ref, gate_ref, h_hbm, wg_ref, wu_ref, wd_ref, y_ref, buf, sem, *, seq, tc, nsub):
    base = pl.program_id(0) * seq

    def row_copy(j, r, slot):
        return pltpu.make_async_copy(h_hbm.at[pl.ds(base + idx_ref[0, 0, j * tc + r], 1), :],
                                     buf.at[slot, pl.ds(r, 1), :], sem.at[slot])

    def issue(j, slot):
        for r in range(tc):
            row_copy(j, r, slot).start()

    diag = lax.broadcasted_iota(I32, (tc, tc), 0) == lax.broadcasted_iota(I32, (tc, tc), 1)
    ones = jnp.ones((tc, LANES), BF16)
    issue(0, 0)
    for j in range(nsub):
        slot = j % 2
        if j + 1 < nsub:
            issue(j + 1, 1 - slot)
        for r in range(tc):
            row_copy(j, r, slot).wait()
        xs = buf[slot].astype(BF16)
        g = jnp.dot(xs, wg_ref[...], preferred_element_type=F32)
        u = jnp.dot(xs, wu_ref[...], preferred_element_type=F32)
        hm = (jax.nn.silu(g) * u).astype(BF16)
        y = jnp.dot(hm, wd_ref[...], preferred_element_type=F32)
        gr = jnp.broadcast_to(gate_ref[0, :, j * tc:(j + 1) * tc], (tc, tc))
        gcol = _split_dot(jnp.where(diag, gr, 0.0), ones)
        y_ref[j * tc:(j + 1) * tc, :] = (y * jnp.tile(gcol, (1, D_MODEL // LANES))).astype(BF16)


def _ffn(idx, gates, h2d, w_gate, w_up, w_down, seq, tc=256):
    b, ne, cap = idx.shape
    tc = min(tc, cap)
    idx3 = idx.reshape(b * ne, 1, cap)
    gate3 = gates.reshape(b * ne, 1, cap)
    slot = lambda bi, ei: (bi * ne + ei, 0, 0)
    wspec = lambda shape: pl.BlockSpec((None,) + shape, lambda bi, ei: (ei, 0, 0))
    return pl.pallas_call(
        functools.partial(_ffn_kernel, seq=seq, tc=tc, nsub=cap // tc),
        grid=(b, ne),
        in_specs=[
            pl.BlockSpec((1, 1, cap), slot, memory_space=pltpu.SMEM),
            pl.BlockSpec((1, 1, cap), slot),
            pl.BlockSpec(memory_space=pl.ANY),
            wspec((D_MODEL, EXPERT_FF)), wspec((D_MODEL, EXPERT_FF)), wspec((EXPERT_FF, D_MODEL)),
        ],
        out_specs=pl.BlockSpec((None, None, cap, D_MODEL), lambda bi, ei: (bi, ei, 0, 0)),
        out_shape=jax.ShapeDtypeStruct((b, ne, cap, D_MODEL), BF16),
        scratch_shapes=[pltpu.VMEM((2, tc, D_MODEL), F32), pltpu.SemaphoreType.DMA((2,))],
        compiler_params=pltpu.CompilerParams(dimension_semantics=("arbitrary", "arbitrary"),
                                             vmem_limit_bytes=VMEM_LIMIT, disable_bounds_checks=True),
        name="ffn",
    )(idx3, gate3, h2d, w_gate.astype(BF16), w_up.astype(BF16), w_down.astype(BF16))


_CTM = 256
_CWIN = 128


def _combine_kernel(cb_ref, x_ref, sp_ref, y_hbm, g_ref, o_ref, ybuf, xbuf, sem, xsem, *,
                    final, tiles_per_seq, nblk, cap):
    i = pl.program_id(0)
    n_tiles = pl.num_programs(0)
    b = i // tiles_per_seq
    slot = i % 2

    def window(tile, e):
        tb = tile // tiles_per_seq
        off = (tb * N_EXPERTS + e) * (nblk + 1) + (tile % tiles_per_seq) * (_CTM // LANES)
        s0 = cb_ref[off]
        s1 = cb_ref[off + _CTM // LANES]
        start = jnp.minimum((s0 // 64) * 64, cap - _CWIN)
        return s1, pl.multiple_of(start, 64)

    def fetch(tile, e, start, buf_slot):
        return pltpu.make_async_copy(y_hbm.at[tile // tiles_per_seq, e, pl.ds(start, _CWIN), :],
                                     ybuf.at[buf_slot, pl.ds(e * _CWIN, _CWIN), :], sem.at[buf_slot])

    def fetch_all(tile, buf_slot):
        for e in range(N_EXPERTS):
            fetch(tile, e, window(tile, e)[1], buf_slot).start()

    @pl.when(i == 0)
    def _():
        fetch_all(i, slot)

    @pl.when(i + 1 < n_tiles)
    def _():
        fetch_all(i + 1, 1 - slot)

    wins = [window(i, e) for e in range(N_EXPERTS)]
    pad = jnp.full((LANES - N_EXPERTS, LANES), -1.0, F32)
    sp_t = jnp.concatenate([jnp.concatenate([sp_ref[hf], pad], axis=0).T for hf in range(_CTM // LANES)],
                           axis=0)
    lane = lax.broadcasted_iota(I32, (1, _CWIN), 1).astype(F32)
    hits = [jnp.where(sp_t[:, e:e + 1] - wins[e][1].astype(F32) == lane, 1.0, 0.0).astype(BF16)
            for e in range(N_EXPERTS)]
    for e in range(N_EXPERTS):
        fetch(i, e, wins[e][1], slot).wait()
    o_ref[...] = x_ref[...] + jnp.dot(jnp.concatenate(hits, axis=1), ybuf[slot], preferred_element_type=F32)
    for e in range(N_EXPERTS):
        s1, start = wins[e]
        col = sp_t[:, e:e + 1]

        def extra(k, carry, e=e, s1=s1, start=start, col=col):
            lo = start + (k + 1) * _CWIN
            st = pl.multiple_of(jnp.minimum(lo, cap - _CWIN), 64)
            cp = pltpu.make_async_copy(y_hbm.at[b, e, pl.ds(st, _CWIN), :], xbuf, xsem)
            cp.start()
            cp.wait()
            hit = jnp.where(jnp.logical_and(col - st.astype(F32) == lane, col >= lo.astype(F32)), 1.0, 0.0)
            o_ref[...] += jnp.dot(hit.astype(BF16), xbuf[...], preferred_element_type=F32)
            return carry

        n_extra = jnp.maximum(s1 - start - 1, 0) // _CWIN
        lax.fori_loop(0, n_extra, extra, 0)
    if final:
        x = o_ref[...]
        o_ref[...] = x * _rms_scale(x) * g_ref[...]


def _combine(x2d, spos, cb, y, g_final, final, seq):
    t = x2d.shape[0]
    batch, ne, cap, _ = y.shape
    nblk = seq // LANES
    cb_full = jnp.concatenate([cb, jnp.full((batch, ne, 1), cap, I32)], axis=-1).reshape(-1)
    spb = _CTM // LANES
    return pl.pallas_call(
        functools.partial(_combine_kernel, final=final, tiles_per_seq=seq // _CTM, nblk=nblk, cap=cap),
        grid_spec=pltpu.PrefetchScalarGridSpec(
            num_scalar_prefetch=1,
            grid=(t // _CTM,),
            in_specs=[
                pl.BlockSpec((_CTM, D_MODEL), lambda i, c: (i, 0)),
                pl.BlockSpec((spb, N_EXPERTS, LANES), lambda i, c: (i, 0, 0)),
                pl.BlockSpec(memory_space=pl.ANY),
                pl.BlockSpec((1, D_MODEL), lambda i, c: (0, 0)),
            ],
            out_specs=pl.BlockSpec((_CTM, D_MODEL), lambda i, c: (i, 0)),
            scratch_shapes=[pltpu.VMEM((2, N_EXPERTS * _CWIN, D_MODEL), BF16), pltpu.VMEM((_CWIN, D_MODEL), BF16),
                            pltpu.SemaphoreType.DMA((2,)), pltpu.SemaphoreType.DMA],
        ),
        out_shape=jax.ShapeDtypeStruct((t, D_MODEL), F32),
        compiler_params=_cparams(("arbitrary",)),
        name="combine",
    )(cb_full, x2d, spos, y, g_final.reshape(1, D_MODEL))


_SLAB_Q0 = 512
_SLAB_K0 = _SLAB_Q0 + D_HEADS * LANES
_SLAB_V0 = _SLAB_K0 + D_KV_HEADS * LANES
_ODD_COLS = _SLAB_V0 + D_KV_HEADS * LANES


def _proj_odd_kernel(x_ref, g_ref, wm_ref, cqn_ref, wq_ref, ckvn_ref, wkv_ref, dqn_ref, dkn_ref,
                     ccq_ref, s1cq_ref, s2cq_ref, cck_ref, s1ck_ref, s2ck_ref,
                     cdq_ref, s1dq_ref, s2dq_ref, cdk_ref, s1dk_ref, s2dk_ref,
                     qc_ref, kc_ref, vc_ref, qd_ref, kd_ref, vd_ref, stat_ref, *, steps_per_seq):
    x = x_ref[...]
    y = (x * _rms_scale(x) * g_ref[...]).astype(BF16)
    pm = jnp.dot(y, wm_ref[...], preferred_element_type=F32)
    lane = lax.broadcasted_iota(I32, (1, LANES), 1)
    one64 = jnp.where(lane == HEAD_DIM, 1.0, 0.0)
    last_lane = lane == LANES - 1
    half_rope = C_ROPE // 2
    stats = [jnp.zeros((1, LANES), F32), jnp.zeros((1, LANES), F32)]

    def with_norm(val, fill, row, col):
        vb = val.astype(BF16).astype(F32)
        nrm = jnp.sqrt(jnp.sum(vb * vb, axis=-1, keepdims=True))
        stats[row] = jnp.where(lane == col, jnp.max(nrm, axis=0, keepdims=True), stats[row])
        return jnp.where(last_lane, nrm * _NORM_MARGIN if fill is None else fill, val).astype(BF16)

    cq = pm[:, :C_Q_RANK]
    cqn = (cq * _rms_scale(cq) * cqn_ref[...]).astype(BF16)
    qc = jnp.dot(cqn, wq_ref[...], preferred_element_type=F32)
    ckv = pm[:, C_Q_RANK:C_Q_RANK + C_KV_RANK]
    ckvn = (ckv * _rms_scale(ckv) * ckvn_ref[...]).astype(BF16)
    kv = jnp.dot(ckvn, wkv_ref[...], preferred_element_type=F32)
    kr = _rope3(pm[:, C_Q_RANK + C_KV_RANK:_SLAB_Q0], cck_ref, s1ck_ref, s2ck_ref, half_rope)
    for h in range(C_HEADS):
        sl = slice(h * LANES, (h + 1) * LANES)
        qc_ref[:, sl] = with_norm(_rope3(qc[:, sl], ccq_ref, s1cq_ref, s2cq_ref, half_rope), None, 1, h)
        kc_ref[:, sl] = with_norm(kv[:, sl] + kr, -1.0, 0, h)
        vc_ref[:, sl] = (kv[:, C_HEADS * LANES + h * LANES:C_HEADS * LANES + (h + 1) * LANES] + one64).astype(BF16)

    def head_norm(xg, gn_ref):
        ss = jnp.sum(xg * xg, axis=-1, keepdims=True) * (1.0 / HEAD_DIM)
        return xg * lax.rsqrt(ss + EPS) * gn_ref[...]

    for g in range(D_HEADS):
        xg = pm[:, _SLAB_Q0 + g * LANES:_SLAB_Q0 + (g + 1) * LANES]
        qd_ref[:, g * LANES:(g + 1) * LANES] = with_norm(
            _rope3(head_norm(xg, dqn_ref), cdq_ref, s1dq_ref, s2dq_ref, HEAD_DIM // 4), None, 1, C_HEADS + g)
    for g in range(D_KV_HEADS):
        sl = slice(g * LANES, (g + 1) * LANES)
        xg = pm[:, _SLAB_K0 + g * LANES:_SLAB_K0 + (g + 1) * LANES]
        kd_ref[:, sl] = with_norm(_rope3(head_norm(xg, dkn_ref), cdk_ref, s1dk_ref, s2dk_ref, HEAD_DIM // 4),
                                  -1.0, 0, C_HEADS + g)
        vd_ref[:, sl] = (pm[:, _SLAB_V0 + g * LANES:_SLAB_V0 + (g + 1) * LANES] + one64).astype(BF16)

    new = jnp.concatenate(stats + [jnp.zeros((6, LANES), F32)], axis=0)

    @pl.when(pl.program_id(0) % steps_per_seq == 0)
    def _():
        stat_ref[...] = new

    @pl.when(pl.program_id(0) % steps_per_seq != 0)
    def _():
        stat_ref[...] = jnp.maximum(stat_ref[...], new)


def _slabs(w, n_heads, width, lane_off=0):
    k = w.shape[0]
    w3 = w.reshape(k, n_heads, width)
    w3 = jnp.pad(w3, ((0, 0), (0, 0), (lane_off, LANES - width - lane_off)))
    return w3.reshape(k, n_heads * LANES)


def _axial_tables(row, col, scale):
    half = HEAD_DIM // 2
    cr, s1r, s2r = _rope_tables(row, D_THETA, half, 0, LANES, scale)
    cc, s1c, s2c = _rope_tables(col, D_THETA, half, half, LANES, scale)
    lane = jnp.arange(LANES)[None, :]
    return jnp.where(lane < half, cr, cc), s1r + s1c, s2r + s2c


def _proj_odd(x2d, seq, g_mix, w_in, cq_norm, w_cq_up, ckv_norm, w_ckv_up, dq_norm, dk_norm, tm=512):
    t = x2d.shape[0]
    nblk = seq // tm
    o1 = C_Q_RANK
    o2 = o1 + C_KV_RANK
    o3 = o2 + C_ROPE
    o4 = o3 + D_HEADS * HEAD_DIM
    o5 = o4 + D_KV_HEADS * HEAD_DIM
    wm = jnp.concatenate([
        w_in[:, :o2],
        _slabs(w_in[:, o2:o3], 1, C_ROPE, C_NOPE),
        _slabs(w_in[:, o3:o4], D_HEADS, HEAD_DIM),
        _slabs(w_in[:, o4:o5], D_KV_HEADS, HEAD_DIM),
        _slabs(w_in[:, o5:], D_KV_HEADS, HEAD_DIM),
    ], axis=1).astype(BF16)
    assert wm.shape[1] == _ODD_COLS
    wq = _slabs(w_cq_up, C_HEADS, C_NOPE + C_ROPE).astype(BF16)
    kv3 = w_ckv_up.reshape(C_KV_RANK, C_HEADS, 2 * HEAD_DIM)
    wkv = jnp.concatenate([
        _slabs(kv3[:, :, :C_NOPE].reshape(C_KV_RANK, -1), C_HEADS, C_NOPE),
        _slabs(kv3[:, :, C_NOPE:].reshape(C_KV_RANK, -1), C_HEADS, HEAD_DIM),
    ], axis=1).astype(BF16)
    pad64 = lambda g: jnp.pad(g, (0, LANES - HEAD_DIM)).reshape(1, LANES)

    pos = jnp.arange(seq, dtype=I32)
    row_pos = pos // GRID_W
    col_pos = pos % GRID_W
    c_scale = (C_NOPE + C_ROPE) ** -0.5 * LOG2E
    d_scale = HEAD_DIM ** -0.5 * LOG2E
    tabs = (_rope_tables(pos, ROPE_THETA, C_ROPE, C_NOPE, LANES, c_scale)
            + _rope_tables(pos, ROPE_THETA, C_ROPE, C_NOPE, LANES, 1.0)
            + _axial_tables(row_pos, col_pos, d_scale)
            + _axial_tables(row_pos, col_pos, 1.0))

    row = lambda i: (i, 0)
    full = lambda i: (0, 0)
    tspec = pl.BlockSpec((tm, LANES), lambda i: (i % nblk, 0))
    wide = C_HEADS * LANES
    kvw = D_KV_HEADS * LANES
    return pl.pallas_call(
        functools.partial(_proj_odd_kernel, steps_per_seq=nblk),
        grid=(t // tm,),
        in_specs=[
            pl.BlockSpec((tm, D_MODEL), row),
            pl.BlockSpec((1, D_MODEL), full),
            pl.BlockSpec(wm.shape, full),
            pl.BlockSpec((1, C_Q_RANK), full),
            pl.BlockSpec(wq.shape, full),
            pl.BlockSpec((1, C_KV_RANK), full),
            pl.BlockSpec(wkv.shape, full),
            pl.BlockSpec((1, LANES), full),
            pl.BlockSpec((1, LANES), full),
        ] + [tspec] * 12,
        out_specs=[
            pl.BlockSpec((tm, wide), row), pl.BlockSpec((tm, wide), row), pl.BlockSpec((tm, wide), row),
            pl.BlockSpec((tm, wide), row), pl.BlockSpec((tm, kvw), row), pl.BlockSpec((tm, kvw), row),
            pl.BlockSpec((None, 8, LANES), lambda i: (i // nblk, 0, 0)),
        ],
        out_shape=[
            jax.ShapeDtypeStruct((t, wide), BF16), jax.ShapeDtypeStruct((t, wide), BF16),
            jax.ShapeDtypeStruct((t, wide), BF16), jax.ShapeDtypeStruct((t, wide), BF16),
            jax.ShapeDtypeStruct((t, kvw), BF16), jax.ShapeDtypeStruct((t, kvw), BF16),
            jax.ShapeDtypeStruct((t // seq, 8, LANES), F32),
        ],
        compiler_params=_cparams(("arbitrary",)),
        name="proj_odd",
    )(x2d, g_mix.reshape(1, D_MODEL), wm, cq_norm.reshape(1, -1), wq, ckv_norm.reshape(1, -1), wkv,
      pad64(dq_norm), pad64(dk_norm), *tabs)


def _flash_kernel(q_ref, k_ref, v_ref, o_ref, qs_sc, m_sc, acc_sc, *, group, tq, tk):
    ki = pl.program_id(3)

    @pl.when(ki == 0)
    def _():
        for g in range(group):
            qs_sc[g * tq:(g + 1) * tq, :] = q_ref[:, g * LANES:(g + 1) * LANES]
        m_sc[...] = jnp.full(m_sc.shape, NEG, F32)
        acc_sc[...] = jnp.zeros(acc_sc.shape, F32)

    s = lax.dot_general(qs_sc[...], k_ref[...], _NT, preferred_element_type=F32)
    m_prev = m_sc[...]
    m_new = jnp.maximum(m_prev, jnp.max(s, axis=1, keepdims=True))
    alpha = jnp.exp2(m_prev - m_new)
    p = jnp.exp2(s - jnp.tile(m_new, (1, tk // LANES)))
    acc_sc[...] = alpha * acc_sc[...] + jnp.dot(p.astype(BF16), v_ref[...], preferred_element_type=F32)
    m_sc[...] = m_new

    @pl.when(ki == pl.num_programs(3) - 1)
    def _():
        acc = acc_sc[...]
        o = acc / acc[:, HEAD_DIM:HEAD_DIM + 1]
        for g in range(group):
            o_ref[:, g * LANES:(g + 1) * LANES] = o[g * tq:(g + 1) * tq].astype(BF16)


def _flash_bounded_kernel(nk_ref, q_ref, k_ref, v_ref, o_ref, qs_sc, acc_sc, *, group, tq):
    ki = pl.program_id(3)
    head = pl.program_id(0) * pl.num_programs(1) + pl.program_id(1)

    @pl.when(ki == 0)
    def _():
        nk = nk_ref[head] * _NORM_MARGIN
        fix = jnp.where(lax.broadcasted_iota(I32, (1, LANES), 1) == LANES - 1, nk, 1.0)
        for g in range(group):
            qs_sc[g * tq:(g + 1) * tq, :] = (q_ref[:, g * LANES:(g + 1) * LANES].astype(F32) * fix).astype(BF16)
        acc_sc[...] = jnp.zeros(acc_sc.shape, F32)

    s = lax.dot_general(qs_sc[...], k_ref[...], _NT, preferred_element_type=F32)
    acc_sc[...] += jnp.dot(jnp.exp2(s).astype(BF16), v_ref[...], preferred_element_type=F32)

    @pl.when(ki == pl.num_programs(3) - 1)
    def _():
        acc = acc_sc[...]
        o = acc / acc[:, HEAD_DIM:HEAD_DIM + 1]
        for g in range(group):
            o_ref[:, g * LANES:(g + 1) * LANES] = o[g * tq:(g + 1) * tq].astype(BF16)


def _flash(q, k, v, group, nk, bounded, rows=1024, tk=512, tk_bounded=2048):
    b, s, qw = q.shape
    hk = k.shape[2] // LANES
    tq = rows // group
    tk = min(tk, s)
    tkb = min(tk_bounded, s)
    out_shape = jax.ShapeDtypeStruct((b, s, qw), BF16)
    sem = ("parallel", "parallel", "parallel", "arbitrary")

    def running_max(q, k, v, nk):
        qspec = pl.BlockSpec((None, tq, group * LANES), lambda bi, hi, qi, ki: (bi, qi, hi))
        kspec = pl.BlockSpec((None, tk, LANES), lambda bi, hi, qi, ki: (bi, ki, hi))
        return pl.pallas_call(
            functools.partial(_flash_kernel, group=group, tq=tq, tk=tk),
            grid=(b, hk, s // tq, s // tk),
            in_specs=[qspec, kspec, kspec],
            out_specs=qspec,
            out_shape=out_shape,
            scratch_shapes=[pltpu.VMEM((rows, LANES), BF16), pltpu.VMEM((rows, LANES), F32),
                            pltpu.VMEM((rows, LANES), F32)],
            compiler_params=_cparams(sem),
            name="flash",
        )(q, k, v)

    def bound(q, k, v, nk):
        qspec = pl.BlockSpec((None, tq, group * LANES), lambda bi, hi, qi, ki, nkr: (bi, qi, hi))
        kspec = pl.BlockSpec((None, tkb, LANES), lambda bi, hi, qi, ki, nkr: (bi, ki, hi))
        return pl.pallas_call(
            functools.partial(_flash_bounded_kernel, group=group, tq=tq),
            grid_spec=pltpu.PrefetchScalarGridSpec(
                num_scalar_prefetch=1,
                grid=(b, hk, s // tq, s // tkb),
                in_specs=[qspec, kspec, kspec],
                out_specs=qspec,
                scratch_shapes=[pltpu.VMEM((rows, LANES), BF16), pltpu.VMEM((rows, LANES), F32)],
            ),
            out_shape=out_shape,
            compiler_params=_cparams(sem),
            name="flash_bounded",
        )(nk.reshape(-1), q, k, v)

    return lax.cond(bounded, bound, running_max, q, k, v, nk)


def _moe(x1, h2, aff, batch, seq, w_gate, w_up, w_down, g_final, final):
    idx, gates, spos, cb = _route(aff.reshape(-1, LANES), batch, seq)
    y = _ffn(idx, gates, h2, w_gate, w_up, w_down, seq)
    return _combine(x1, spos.reshape(-1, N_EXPERTS, LANES), cb, y, g_final, final, seq)


def kernel(x, norm_mix, norm_ffn, even_w_in, even_gmlp_norm, even_w_spatial, even_b_spatial, even_w_out,
           odd_w_in, odd_cq_norm, odd_w_cq_up, odd_ckv_norm, odd_w_ckv_up, odd_dq_norm, odd_dk_norm, odd_w_out,
           moe_w_router, moe_w_gate, moe_w_up, moe_w_down, final_norm):
    b, s, d = x.shape
    depth = norm_mix.shape[0]
    x2d = x.reshape(b * s, d)
    for i in range(depth):
        j = i // 2
        last = i == depth - 1
        if i % 2 == 0:
            q, k, v, go = _proj_even(x2d, s, norm_mix[i], even_w_in[j], even_gmlp_norm[j], even_w_spatial[j],
                                     even_b_spatial[j])
            a = _dilated(q.reshape(b, s, A_WIDTH), k.reshape(b, s, A_WIDTH), v.reshape(b, s, A_WIDTH))
            x1, h2, aff = _outproj(x2d, a.reshape(b * s, A_WIDTH), go, even_w_out[j][:A_WIDTH],
                                   even_w_out[j][A_WIDTH:], norm_ffn[i], moe_w_router[i])
        else:
            qc, kc, vc, qd, kd, vd, stat = _proj_odd(x2d, s, norm_mix[i], odd_w_in[j], odd_cq_norm[j],
                                                     odd_w_cq_up[j], odd_ckv_norm[j], odd_w_ckv_up[j],
                                                     odd_dq_norm[j], odd_dk_norm[j])
            grp = D_HEADS // D_KV_HEADS
            kn_c, kn_d = stat[:, 0, :C_HEADS], stat[:, 0, C_HEADS:C_HEADS + D_KV_HEADS]
            qn_c = stat[:, 1, :C_HEADS]
            qn_d = stat[:, 1, C_HEADS:C_HEADS + D_HEADS].reshape(b, D_KV_HEADS, grp)
            worst = jnp.maximum(jnp.max(qn_c * kn_c), jnp.max(qn_d * kn_d[:, :, None])) * _NORM_MARGIN ** 2
            bounded = worst <= _MAX_SCORE_BOUND
            r3 = lambda z: z.reshape(b, s, -1)
            oc = _flash(r3(qc), r3(kc), r3(vc), 1, kn_c, bounded)
            od = _flash(r3(qd), r3(kd), r3(vd), grp, kn_d, bounded)
            cw = C_HEADS * HEAD_DIM
            x1, h2, aff = _outproj(x2d, oc.reshape(b * s, -1), od.reshape(b * s, -1),
                                   _slabs(odd_w_out[j][:cw].T, C_HEADS, HEAD_DIM).T,
                                   _slabs(odd_w_out[j][cw:].T, D_HEADS, HEAD_DIM).T,
                                   norm_ffn[i], moe_w_router[i])
        x2d = _moe(x1, h2, aff, b, s, moe_w_gate[i], moe_w_up[i], moe_w_down[i], final_norm, last)
    return x2d.reshape(b, s, d)
```

```python
import functools
import math

import jax
import jax.numpy as jnp
from jax import lax
from jax.experimental import pallas as pl
from jax.experimental.pallas import tpu as pltpu

F32 = jnp.float32
BF16 = jnp.bfloat16
I32 = jnp.int32

EPS = 1e-6
NEG = -1e30
LOG2E = 1.4426950408889634

D_MODEL = 1024
HEAD_DIM = 64
ROPE_THETA = 500000.0
ROT_DIM = 16
GRID_W = 64
A_HEADS = 12
A_WIDTH = 768
A_DILATIONS = (1, 4, 16)
A_RADIUS = 64
B_WIDTH = 256
B_GROUPS = 4
B_CHUNK = 128
C_HEADS = 8
C_Q_RANK = 256
C_KV_RANK = 128
C_NOPE = 64
C_ROPE = 32
D_HEADS = 8
D_KV_HEADS = 2
D_THETA = 10000.0
N_EXPERTS = 16
EC_FACTOR = 2
EXPERT_FF = 512

_NORM_MARGIN = 1.01
_MAX_SCORE_BOUND = 55.0

LANES = 128
VMEM_LIMIT = 48 * 1024 * 1024
_DILATED_VMEM_LIMIT = 56 * 1024 * 1024

_NT = (((1,), (1,)), ((), ()))


def _cparams(sem):
    return pltpu.CompilerParams(dimension_semantics=sem, vmem_limit_bytes=VMEM_LIMIT)


def _rms_scale(x):
    return lax.rsqrt(jnp.mean(x * x, axis=-1, keepdims=True) + EPS)


def _rope3(a, c, s1, s2, shift):
    return a * c + pltpu.roll(a, LANES - shift, 1) * s1 + pltpu.roll(a, shift, 1) * s2


def _split_dot(x, w_bf16):
    hi = x.astype(BF16)
    lo = (x - hi.astype(F32)).astype(BF16)
    return (jnp.dot(hi, w_bf16, preferred_element_type=F32)
            + jnp.dot(lo, w_bf16, preferred_element_type=F32))


def _proj_even_kernel(x_ref, g_ref, w_ref, cq_ref, s1q_ref, s2q_ref, ck_ref, s1k_ref, s2k_ref,
                      gn_ref, gmat_ref, ws_ref, bs_ref,
                      q_ref, k_ref, v_ref, go_ref):
    x = x_ref[...]
    y = (x * _rms_scale(x) * g_ref[...]).astype(BF16)
    tm = x.shape[0]

    aq = jnp.dot(y, w_ref[:, 0:A_WIDTH], preferred_element_type=F32)
    tq = (cq_ref[...], s1q_ref[...], s2q_ref[...])
    for j in range(A_WIDTH // LANES):
        sl = slice(j * LANES, (j + 1) * LANES)
        q_ref[:, sl] = _rope3(aq[:, sl], *tq, ROT_DIM // 2)
    ak = jnp.dot(y, w_ref[:, A_WIDTH:2 * A_WIDTH], preferred_element_type=F32)
    tk = (ck_ref[...], s1k_ref[...], s2k_ref[...])
    for j in range(A_WIDTH // LANES):
        sl = slice(j * LANES, (j + 1) * LANES)
        k_ref[:, sl] = _rope3(ak[:, sl], *tk, ROT_DIM // 2)
    v_ref[...] = jnp.dot(y, w_ref[:, 2 * A_WIDTH:3 * A_WIDTH], preferred_element_type=F32)

    z = jnp.dot(y, w_ref[:, 3 * A_WIDTH:3 * A_WIDTH + 2 * B_WIDTH], preferred_element_type=F32)
    ge = jax.nn.gelu(z)
    u = ge[:, :B_WIDTH]
    vv = ge[:, B_WIDTH:]
    ss = _split_dot(vv * vv, gmat_ref[...])
    vn = (vv * lax.rsqrt(ss + EPS) * gn_ref[...]).astype(BF16)
    grp = lax.broadcasted_iota(I32, (B_CHUNK, B_WIDTH), 1) // (B_WIDTH // B_GROUPS)
    for c in range(tm // B_CHUNK):
        rows = slice(c * B_CHUNK, (c + 1) * B_CHUNK)
        vc = vn[rows]
        mg = [jnp.dot(ws_ref[g], vc, preferred_element_type=F32) for g in range(B_GROUPS)]
        mixed = jnp.where(grp == 0, mg[0], jnp.where(grp == 1, mg[1], jnp.where(grp == 2, mg[2], mg[3])))
        go_ref[rows, :] = (u[rows] * (mixed + bs_ref[...])).astype(BF16)


def _rope_tables(pos, theta, r, lane_off, period, scale):
    half = r // 2
    inv = jnp.power(jnp.float32(theta), -jnp.arange(half, dtype=F32) * (2.0 / r))
    ang = pos.astype(F32)[:, None] * inv[None, :]
    cos, sin = jnp.cos(ang), jnp.sin(ang)
    o = (jnp.arange(LANES) % period) - lane_off
    in_lo = (o >= 0) & (o < half)
    in_hi = (o >= half) & (o < r)
    idx = jnp.clip(jnp.where(in_hi, o - half, o), 0, half - 1)
    c = jnp.where((in_lo | in_hi)[None, :], cos[:, idx], 1.0)
    s1 = jnp.where(in_lo[None, :], -sin[:, idx], 0.0)
    s2 = jnp.where(in_hi[None, :], sin[:, idx], 0.0)
    return c * scale, s1 * scale, s2 * scale


def _proj_even(x2d, seq, g_mix, w_in, gmlp_norm, w_s, b_s, tm=512):
    t = x2d.shape[0]
    nblk = seq // tm
    pos = jnp.arange(seq, dtype=I32)
    qscale = HEAD_DIM ** -0.5 * LOG2E
    cq, s1q, s2q = _rope_tables(pos, ROPE_THETA, ROT_DIM, 0, HEAD_DIM, qscale)
    ck, s1k, s2k = _rope_tables(pos, ROPE_THETA, ROT_DIM, 0, HEAD_DIM, 1.0)
    gdim = B_WIDTH // B_GROUPS
    gid = jnp.arange(B_WIDTH) // gdim
    gmat = jnp.where(gid[:, None] == gid[None, :], 1.0 / gdim, 0.0).astype(BF16)
    bias = jnp.repeat(b_s.T, gdim, axis=1)
    row = lambda i: (i, 0)
    tab = lambda i: (i % nblk, 0)
    full = lambda i: (0, 0)
    tspec = pl.BlockSpec((tm, LANES), tab)
    return pl.pallas_call(
        _proj_even_kernel,
        grid=(t // tm,),
        in_specs=[
            pl.BlockSpec((tm, D_MODEL), row),
            pl.BlockSpec((1, D_MODEL), full),
            pl.BlockSpec(w_in.shape, full),
            tspec, tspec, tspec, tspec, tspec, tspec,
            pl.BlockSpec((1, B_WIDTH), full),
            pl.BlockSpec((B_WIDTH, B_WIDTH), full),
            pl.BlockSpec((B_GROUPS, B_CHUNK, B_CHUNK), lambda i: (0, 0, 0)),
            pl.BlockSpec((B_CHUNK, B_WIDTH), full),
        ],
        out_specs=[
            pl.BlockSpec((tm, A_WIDTH), row),
            pl.BlockSpec((tm, A_WIDTH), row),
            pl.BlockSpec((tm, A_WIDTH), row),
            pl.BlockSpec((tm, B_WIDTH), row),
        ],
        out_shape=[
            jax.ShapeDtypeStruct((t, A_WIDTH), F32),
            jax.ShapeDtypeStruct((t, A_WIDTH), F32),
            jax.ShapeDtypeStruct((t, A_WIDTH), F32),
            jax.ShapeDtypeStruct((t, B_WIDTH), BF16),
        ],
        compiler_params=_cparams(("parallel",)),
        name="proj_even",
    )(x2d, g_mix.reshape(1, D_MODEL), w_in.astype(BF16), cq, s1q, s2q, ck, s1k, s2k,
      gmlp_norm.reshape(1, B_WIDTH), gmat, w_s.astype(BF16), bias)


_TQ = 128
_TK = _TQ + 2 * A_RADIUS
_NORM_ROWS = 512


def _dilated_kernel(q_ref, k_ref, v_ref, o_ref, b_sc, l_sc, bias_sc, *, seq):
    half0 = lax.broadcasted_iota(I32, (1, LANES), 1) < HEAD_DIM
    row_head = lax.broadcasted_iota(I32, (LANES, LANES), 0) // HEAD_DIM
    pick = [jnp.where(row_head == h, 1.0, 0.0).astype(BF16) for h in range(2)]
    n_pat = len(A_DILATIONS)
    n_norm = seq // _NORM_ROWS
    zero_row = jnp.zeros((1, LANES), F32)

    def head_sq(ref, c):
        rows = pl.ds(pl.multiple_of(c * _NORM_ROWS, _NORM_ROWS), _NORM_ROWS)
        xb = ref[rows, :].astype(BF16).astype(F32)
        sq = xb * xb
        return rows, [_split_dot(sq, pick[h]) for h in range(2)]

    def k_pass(c, mx):
        k2 = head_sq(k_ref, c)[1]
        return tuple(jnp.maximum(mx[h], jnp.max(k2[h], axis=0, keepdims=True)) for h in range(2))

    max_k2 = lax.fori_loop(0, n_norm, k_pass, (zero_row, zero_row))

    def q_pass(c, mx):
        rows, q2 = head_sq(q_ref, c)
        for h in range(2):
            bound = jnp.sqrt(q2[h] * max_k2[h]) * (_NORM_MARGIN * _NORM_MARGIN)
            b_sc.at[h][rows, :] = bound
            mx = jnp.maximum(mx, jnp.max(bound, axis=0, keepdims=True))
        return mx

    worst = jnp.max(lax.fori_loop(0, n_norm, q_pass, zero_row))

    diff = lax.broadcasted_iota(I32, (_TQ, _TK), 1) - lax.broadcasted_iota(I32, (_TQ, _TK), 0)
    for case in range(3):
        bias_sc[case] = jnp.where(jnp.abs(diff - case * A_RADIUS) <= A_RADIUS, 0.0, NEG)
    one_bf16 = jnp.ones((), BF16)

    def run(bounded):
        for pi, d in enumerate(A_DILATIONS):
            cls_len = seq // d
            tpc = cls_len // _TQ

            def body(j, carry, d=d, cls_len=cls_len, tpc=tpc, pi=pi):
                i = j // tpc
                n = j % tpc
                l0 = n * _TQ
                kst = jnp.clip(l0 - A_RADIUS, 0, cls_len - _TK)
                bias = bias_sc[(l0 - kst) // A_RADIUS]
                if d == 1:
                    qrows = pl.ds(pl.multiple_of(l0, _TQ), _TQ)
                    krows = pl.ds(pl.multiple_of(kst, A_RADIUS), _TK)
                else:
                    qrows = pl.ds(l0 * d + i, _TQ, stride=d)
                    krows = pl.ds(kst * d + i, _TK, stride=d)
                q = q_ref[qrows, :]
                kb = k_ref[krows, :].astype(BF16)
                vb = v_ref[krows, :].astype(BF16)
                parts = []
                for h in range(2):
                    qh = jnp.where(half0 if h == 0 else jnp.logical_not(half0), q, 0.0).astype(BF16)
                    s = lax.dot_general(qh, kb, _NT, preferred_element_type=F32) + bias
                    if bounded:
                        mt = jnp.tile(b_sc.at[h][qrows, :], (1, _TK // LANES))
                    else:
                        mt = jnp.max(s, axis=-1, keepdims=True)
                    p = jnp.exp2(s - mt).astype(BF16)
                    vh = jnp.where(half0 if h == 0 else jnp.logical_not(half0), vb, one_bf16)
                    parts.append((mt, jnp.dot(p, vh, preferred_element_type=F32)))
                ot = jnp.where(half0, parts[0][1], parts[1][1])
                lt = pltpu.roll(jnp.where(half0, parts[1][1], parts[0][1]), HEAD_DIM, 1)
                if bounded:
                    if pi > 0:
                        lt = l_sc[qrows, :] + lt
                        ot = o_ref[qrows, :] + ot
                else:
                    mt = jnp.where(half0, parts[0][0], parts[1][0])
                    if pi > 0:
                        mp = b_sc.at[0][qrows, :]
                        mn = jnp.maximum(mp, mt)
                        a = jnp.exp2(mp - mn)
                        b = jnp.exp2(mt - mn)
                        lt = a * l_sc[qrows, :] + b * lt
                        ot = a * o_ref[qrows, :] + b * ot
                        mt = mn
                    if pi < n_pat - 1:
                        b_sc.at[0][qrows, :] = mt
                if pi == n_pat - 1:
                    o_ref[qrows, :] = ot / lt
                else:
                    l_sc[qrows, :] = lt
                    o_ref[qrows, :] = ot
                return carry

            lax.fori_loop(0, seq // _TQ, body, 0, unroll=4)

    @pl.when(worst <= _MAX_SCORE_BOUND)
    def _():
        run(True)

    @pl.when(jnp.logical_not(worst <= _MAX_SCORE_BOUND))
    def _():
        run(False)


def _dilated(q, k, v):
    b, s, w = q.shape
    spec = pl.BlockSpec((None, s, LANES), lambda bi, hi: (bi, 0, hi))
    return pl.pallas_call(
        functools.partial(_dilated_kernel, seq=s),
        grid=(b, w // LANES),
        in_specs=[spec, spec, spec],
        out_specs=spec,
        out_shape=jax.ShapeDtypeStruct((b, s, w), F32),
        scratch_shapes=[pltpu.VMEM((2, s, LANES), F32), pltpu.VMEM((s, LANES), F32),
                        pltpu.VMEM((3, _TQ, _TK), F32)],
        compiler_params=pltpu.CompilerParams(dimension_semantics=("parallel", "parallel"),
                                             vmem_limit_bytes=_DILATED_VMEM_LIMIT),
        name="dilated",
    )(q, k, v)


def _outproj_kernel(x_ref, a_ref, b_ref, wa_ref, wb_ref, gf_ref, wr_ref, x1_ref, h2_ref, aff_ref):
    x1 = (x_ref[...]
          + jnp.dot(a_ref[...].astype(BF16), wa_ref[...], preferred_element_type=F32)
          + jnp.dot(b_ref[...].astype(BF16), wb_ref[...], preferred_element_type=F32))
    x1_ref[...] = x1
    h2 = x1 * _rms_scale(x1) * gf_ref[...]
    h2_ref[...] = h2
    w_hi = wr_ref[0]
    w_lo = wr_ref[1]
    hi = h2.astype(BF16)
    lo = (h2 - hi.astype(F32)).astype(BF16)
    lg = (jnp.dot(hi, w_hi, preferred_element_type=F32) + jnp.dot(lo, w_hi, preferred_element_type=F32)
          + jnp.dot(hi, w_lo, preferred_element_type=F32))
    valid = lax.broadcasted_iota(I32, lg.shape, 1) < N_EXPERTS
    lg = jnp.where(valid, lg, NEG)
    e = jnp.exp(lg - jnp.max(lg, axis=-1, keepdims=True))
    aff = e / jnp.sum(e, axis=-1, keepdims=True)
    aff_t = aff.T
    for j in range(aff.shape[0] // LANES):
        aff_ref[j] = aff_t[:N_EXPERTS, j * LANES:(j + 1) * LANES]


def _outproj(x2d, a, b, wa, wb, g_ffn, w_router, tm=512):
    t = x2d.shape[0]
    wr = jnp.pad(w_router, ((0, 0), (0, LANES - N_EXPERTS)))
    wr_hi = wr.astype(BF16)
    wr_lo = (wr - wr_hi.astype(F32)).astype(BF16)
    wr2 = jnp.stack([wr_hi, wr_lo])
    row = lambda i: (i, 0)
    full = lambda i: (0, 0)
    return pl.pallas_call(
        _outproj_kernel,
        grid=(t // tm,),
        in_specs=[
            pl.BlockSpec((tm, D_MODEL), row),
            pl.BlockSpec((tm, a.shape[1]), row),
            pl.BlockSpec((tm, b.shape[1]), row),
            pl.BlockSpec(wa.shape, full),
            pl.BlockSpec(wb.shape, full),
            pl.BlockSpec((1, D_MODEL), full),
            pl.BlockSpec((2, D_MODEL, LANES), lambda i: (0, 0, 0)),
        ],
        out_specs=[
            pl.BlockSpec((tm, D_MODEL), row),
            pl.BlockSpec((tm, D_MODEL), row),
            pl.BlockSpec((tm // LANES, N_EXPERTS, LANES), lambda i: (i, 0, 0)),
        ],
        out_shape=[
            jax.ShapeDtypeStruct((t, D_MODEL), F32),
            jax.ShapeDtypeStruct((t, D_MODEL), F32),
            jax.ShapeDtypeStruct((t // LANES, N_EXPERTS, LANES), F32),
        ],
        compiler_params=_cparams(("parallel",)),
        name="outproj",
    )(x2d, a, b, wa.astype(BF16), wb.astype(BF16), g_ffn.reshape(1, D_MODEL), wr2)


def _route_kernel(aff_ref, idx_ref, gate_ref, spos_ref, cb_ref, thr_sc, need_sc, *, cap):
    nblk = aff_ref.shape[0] // N_EXPERTS
    bits = pltpu.bitcast(aff_ref[...], I32).reshape(nblk, N_EXPERTS, LANES)

    def count(pred):
        return jnp.sum(jnp.sum(jnp.where(pred, 1.0, 0.0), axis=0), axis=1, keepdims=True)

    def search(it, thr):
        cand = thr | jnp.left_shift(jnp.int32(1), 30 - it)
        return jnp.where(count(bits >= cand[None]) >= cap, cand, thr)

    thr = lax.fori_loop(0, 31, search, jnp.zeros((N_EXPERTS, 1), I32))
    need = cap - count(bits > thr[None])
    thr_sc[...] = jnp.broadcast_to(thr, (N_EXPERTS, LANES))
    need_sc[...] = jnp.broadcast_to(need, (N_EXPERTS, LANES))

    ri = lax.broadcasted_iota(I32, (LANES, LANES), 0)
    ci = lax.broadcasted_iota(I32, (LANES, LANES), 1)
    upper = jnp.where(ri <= ci, 1.0, 0.0).astype(BF16)
    lower = jnp.where(ci <= ri, 1.0, 0.0).astype(BF16)
    eye = jnp.where(ri == ci, 1.0, 0.0).astype(BF16)
    ones = jnp.ones((LANES, LANES), BF16)
    bi = lax.broadcasted_iota(I32, (nblk, nblk), 0)
    bj = lax.broadcasted_iota(I32, (nblk, nblk), 1)
    strict = jnp.where(bj < bi, 1.0, 0.0).astype(BF16)
    before = jnp.where(bi < bj, 1.0, 0.0).astype(BF16)
    mean_rows = jnp.full((8, LANES), 1.0 / LANES, BF16)
    c_row = lax.broadcasted_iota(I32, (1, cap), 1).astype(F32)
    blk_iota = lax.broadcasted_iota(I32, (nblk, cap), 0).astype(F32)
    t_iota = lax.broadcasted_iota(I32, (LANES, cap), 0).astype(F32)
    rep = cap // LANES

    def cums(mask_bf16):
        lp = jnp.dot(mask_bf16, upper, preferred_element_type=F32)
        bc = jnp.dot(mask_bf16, ones, preferred_element_type=F32)
        bst = jnp.dot(strict, bc.astype(BF16), preferred_element_type=F32)
        return lp, bc, bst

    def per_expert(e, carry):
        a = aff_ref[pl.ds(e, nblk, stride=N_EXPERTS), :]
        ab = pltpu.bitcast(a, I32)
        thr_e = thr_sc[pl.ds(e, 1), :]
        need_e = need_sc[pl.ds(e, 1), :]
        gt = ab > thr_e
        eq = ab == thr_e
        eqf = jnp.where(eq, 1.0, 0.0)
        lp_q, _, bst_q = cums(eqf.astype(BF16))
        sel = jnp.logical_or(gt, jnp.logical_and(eq, bst_q + lp_q - eqf < need_e))
        mb = jnp.where(sel, 1.0, 0.0).astype(BF16)
        lp, bc, bst = cums(mb)
        spos_ref[pl.ds(e, nblk, stride=N_EXPERTS), :] = jnp.where(sel, bst + lp - 1.0, -1.0)
        bc_row = lax.dot_general(mean_rows, bc.astype(BF16), _NT, preferred_element_type=F32)
        cb_ref[pl.ds(e, 1), :] = jnp.dot(bc_row.astype(BF16), before, preferred_element_type=F32)[:1].astype(I32)
        bend_w = jnp.tile(bst + bc, (1, rep))
        bst_w = jnp.tile(bst, (1, rep))
        blk_c = jnp.sum(jnp.where(bend_w <= c_row, 1.0, 0.0), axis=0, keepdims=True)
        onehot = blk_iota == blk_c
        bst_c = jnp.sum(jnp.where(onehot, bst_w, 0.0), axis=0, keepdims=True)
        r_c = c_row - bst_c
        ohb = jnp.where(onehot, 1.0, 0.0).astype(BF16)
        lp_t = lax.dot_general(lower, mb, _NT, preferred_element_type=F32)
        lp_c = jnp.dot(lp_t.astype(BF16), ohb, preferred_element_type=F32)
        tl_c = jnp.sum(jnp.where(lp_c <= r_c, 1.0, 0.0), axis=0, keepdims=True)
        idx_ref[pl.ds(e, 1), :] = (blk_c * LANES + tl_c).astype(I32)
        a_hi = a.astype(BF16)
        a_lo = (a - a_hi.astype(F32)).astype(BF16)
        at_hi = lax.dot_general(eye, a_hi, _NT, preferred_element_type=F32).astype(BF16)
        at_lo = lax.dot_general(eye, a_lo, _NT, preferred_element_type=F32).astype(BF16)
        g_c = (jnp.dot(at_hi, ohb, preferred_element_type=F32)
               + jnp.dot(at_lo, ohb, preferred_element_type=F32))
        gate_ref[pl.ds(e, 1), :] = jnp.sum(jnp.where(t_iota == tl_c, g_c, 0.0), axis=0, keepdims=True)
        return carry

    lax.fori_loop(0, N_EXPERTS, per_expert, 0)


def _route(aff2d, batch, seq):
    cap = EC_FACTOR * seq // N_EXPERTS
    nblk = seq // LANES
    rows = nblk * N_EXPERTS
    out_spec = pl.BlockSpec((None, N_EXPERTS, cap), lambda b: (b, 0, 0))
    return pl.pallas_call(
        functools.partial(_route_kernel, cap=cap),
        grid=(batch,),
        in_specs=[pl.BlockSpec((rows, LANES), lambda b: (b, 0))],
        out_specs=[out_spec, out_spec, pl.BlockSpec((rows, LANES), lambda b: (b, 0)),
                   pl.BlockSpec((None, N_EXPERTS, nblk), lambda b: (b, 0, 0))],
        out_shape=[jax.ShapeDtypeStruct((batch, N_EXPERTS, cap), I32),
                   jax.ShapeDtypeStruct((batch, N_EXPERTS, cap), F32),
                   jax.ShapeDtypeStruct((batch * rows, LANES), F32),
                   jax.ShapeDtypeStruct((batch, N_EXPERTS, nblk), I32)],
        scratch_shapes=[pltpu.VMEM((N_EXPERTS, LANES), I32), pltpu.VMEM((N_EXPERTS, LANES), F32)],
        compiler_params=_cparams(("parallel",)),
        name="route",
    )(aff2d)


def _ffn_kernel(idx_ref, gate_ref, h_hbm, wg32_ref, wu32_ref, wd32_ref, y_ref, buf, sem, wg_ref, wu_ref, wd_ref,
                *, seq, tc, nsub):
    base = pl.program_id(1) * seq

    @pl.when(pl.program_id(1) == 0)
    def _():
        wg_ref[...] = wg32_ref[...].astype(BF16)
        wu_ref[...] = wu32_ref[...].astype(BF16)
        wd_ref[...] = wd32_ref[...].astype(BF16)

    def row_copy(j, r, slot):
        return pltpu.make_async_copy(h_hbm.at[pl.ds(base + idx_ref[0, 0, j * tc + r], 1), :],
                                     buf.at[slot, pl.ds(r, 1), :], sem.at[slot])

    def issue(j, slot):
        for r in range(tc):
            row_copy(j, r, slot).start()

    diag = lax.broadcasted_iota(I32, (tc, tc), 0) == lax.broadcasted_iota(I32, (tc, tc), 1)
    ones = jnp.ones((tc, LANES), BF16)
    issue(0, 0)
    for j in range(nsub):
        slot = j % 2
        if j + 1 < nsub:
            issue(j + 1, 1 - slot)
        for r in range(tc):
            row_copy(j, r, slot).wait()
        xs = buf[slot].astype(BF16)
        g = jnp.dot(xs, wg_ref[...], preferred_element_type=F32)
        u = jnp.dot(xs, wu_ref[...], preferred_element_type=F32)
        hm = (jax.nn.silu(g) * u).astype(BF16)
        y = jnp.dot(hm, wd_ref[...], preferred_element_type=F32)
        gr = jnp.broadcast_to(gate_ref[0, :, j * tc:(j + 1) * tc], (tc, tc))
        gcol = _split_dot(jnp.where(diag, gr, 0.0), ones)
        y_ref[j * tc:(j + 1) * tc, :] = (y * jnp.tile(gcol, (1, D_MODEL // LANES))).astype(BF16)


def _ffn(idx, gates, h2d, w_gate, w_up, w_down, layer, seq, tc=256):
    b, ne, cap = idx.shape
    tc = min(tc, cap)
    idx3 = idx.reshape(b * ne, 1, cap)
    gate3 = gates.reshape(b * ne, 1, cap)
    slot = lambda ei, bi: (bi * ne + ei, 0, 0)
    wspec = lambda shape: pl.BlockSpec((None, None) + shape, lambda ei, bi: (layer, ei, 0, 0))
    return pl.pallas_call(
        functools.partial(_ffn_kernel, seq=seq, tc=tc, nsub=cap // tc),
        grid=(ne, b),
        in_specs=[
            pl.BlockSpec((1, 1, cap), slot, memory_space=pltpu.SMEM),
            pl.BlockSpec((1, 1, cap), slot),
            pl.BlockSpec(memory_space=pl.ANY),
            wspec((D_MODEL, EXPERT_FF)), wspec((D_MODEL, EXPERT_FF)), wspec((EXPERT_FF, D_MODEL)),
        ],
        out_specs=pl.BlockSpec((None, None, cap, D_MODEL), lambda ei, bi: (bi, ei, 0, 0)),
        out_shape=jax.ShapeDtypeStruct((b, ne, cap, D_MODEL), BF16),
        scratch_shapes=[pltpu.VMEM((2, tc, D_MODEL), F32), pltpu.SemaphoreType.DMA((2,)),
                        pltpu.VMEM((D_MODEL, EXPERT_FF), BF16), pltpu.VMEM((D_MODEL, EXPERT_FF), BF16),
                        pltpu.VMEM((EXPERT_FF, D_MODEL), BF16)],
        compiler_params=pltpu.CompilerParams(dimension_semantics=("arbitrary", "arbitrary"),
                                             vmem_limit_bytes=VMEM_LIMIT, disable_bounds_checks=True),
        name="ffn",
    )(idx3, gate3, h2d, w_gate, w_up, w_down)


_CTM = 256
_CWIN = 128


def _combine_kernel(cb_ref, x_ref, sp_ref, y_hbm, g_ref, o_ref, ybuf, xbuf, sem, xsem, *,
                    final, tiles_per_seq, nblk, cap):
    i = pl.program_id(0)
    n_tiles = pl.num_programs(0)
    b = i // tiles_per_seq
    slot = i % 2

    def window(tile, e):
        tb = tile // tiles_per_seq
        off = (tb * N_EXPERTS + e) * (nblk + 1) + (tile % tiles_per_seq) * (_CTM // LANES)
        s0 = cb_ref[off]
        s1 = cb_ref[off + _CTM // LANES]
        start = jnp.minimum((s0 // 64) * 64, cap - _CWIN)
        return s1, pl.multiple_of(start, 64)

    def fetch(tile, e, start, buf_slot):
        return pltpu.make_async_copy(y_hbm.at[tile // tiles_per_seq, e, pl.ds(start, _CWIN), :],
                                     ybuf.at[buf_slot, pl.ds(e * _CWIN, _CWIN), :], sem.at[buf_slot])

    def fetch_all(tile, buf_slot):
        for e in range(N_EXPERTS):
            fetch(tile, e, window(tile, e)[1], buf_slot).start()

    @pl.when(i == 0)
    def _():
        fetch_all(i, slot)

    @pl.when(i + 1 < n_tiles)
    def _():
        fetch_all(i + 1, 1 - slot)

    wins = [window(i, e) for e in range(N_EXPERTS)]
    pad = jnp.full((LANES - N_EXPERTS, LANES), -1.0, F32)
    sp_t = jnp.concatenate([jnp.concatenate([sp_ref[hf], pad], axis=0).T for hf in range(_CTM // LANES)],
                           axis=0)
    lane = lax.broadcasted_iota(I32, (1, _CWIN), 1).astype(F32)
    hits = [jnp.where(sp_t[:, e:e + 1] - wins[e][1].astype(F32) == lane, 1.0, 0.0).astype(BF16)
            for e in range(N_EXPERTS)]
    for e in range(N_EXPERTS):
        fetch(i, e, wins[e][1], slot).wait()
    o_ref[...] = x_ref[...] + jnp.dot(jnp.concatenate(hits, axis=1), ybuf[slot], preferred_element_type=F32)
    for e in range(N_EXPERTS):
        s1, start = wins[e]
        col = sp_t[:, e:e + 1]

        def extra(k, carry, e=e, s1=s1, start=start, col=col):
            lo = start + (k + 1) * _CWIN
            st = pl.multiple_of(jnp.minimum(lo, cap - _CWIN), 64)
            cp = pltpu.make_async_copy(y_hbm.at[b, e, pl.ds(st, _CWIN), :], xbuf, xsem)
            cp.start()
            cp.wait()
            hit = jnp.where(jnp.logical_and(col - st.astype(F32) == lane, col >= lo.astype(F32)), 1.0, 0.0)
            o_ref[...] += jnp.dot(hit.astype(BF16), xbuf[...], preferred_element_type=F32)
            return carry

        n_extra = jnp.maximum(s1 - start - 1, 0) // _CWIN
        lax.fori_loop(0, n_extra, extra, 0)
    if final:
        x = o_ref[...]
        o_ref[...] = x * _rms_scale(x) * g_ref[...]


def _combine(x2d, spos, cb, y, g_final, final, seq):
    t = x2d.shape[0]
    batch, ne, cap, _ = y.shape
    nblk = seq // LANES
    cb_full = jnp.concatenate([cb, jnp.full((batch, ne, 1), cap, I32)], axis=-1).reshape(-1)
    spb = _CTM // LANES
    return pl.pallas_call(
        functools.partial(_combine_kernel, final=final, tiles_per_seq=seq // _CTM, nblk=nblk, cap=cap),
        grid_spec=pltpu.PrefetchScalarGridSpec(
            num_scalar_prefetch=1,
            grid=(t // _CTM,),
            in_specs=[
                pl.BlockSpec((_CTM, D_MODEL), lambda i, c: (i, 0)),
                pl.BlockSpec((spb, N_EXPERTS, LANES), lambda i, c: (i, 0, 0)),
                pl.BlockSpec(memory_space=pl.ANY),
                pl.BlockSpec((1, D_MODEL), lambda i, c: (0, 0)),
            ],
            out_specs=pl.BlockSpec((_CTM, D_MODEL), lambda i, c: (i, 0)),
            scratch_shapes=[pltpu.VMEM((2, N_EXPERTS * _CWIN, D_MODEL), BF16), pltpu.VMEM((_CWIN, D_MODEL), BF16),
                            pltpu.SemaphoreType.DMA((2,)), pltpu.SemaphoreType.DMA],
        ),
        out_shape=jax.ShapeDtypeStruct((t, D_MODEL), F32),
        compiler_params=_cparams(("arbitrary",)),
        name="combine",
    )(cb_full, x2d, spos, y, g_final.reshape(1, D_MODEL))


_SLAB_Q0 = 512
_SLAB_K0 = _SLAB_Q0 + D_HEADS * LANES
_SLAB_V0 = _SLAB_K0 + D_KV_HEADS * LANES
_ODD_COLS = _SLAB_V0 + D_KV_HEADS * LANES


def _proj_odd_kernel(x_ref, g_ref, wm_ref, cqn_ref, wq_ref, ckvn_ref, wkv_ref, dqn_ref, dkn_ref,
                     ccq_ref, s1cq_ref, s2cq_ref, cck_ref, s1ck_ref, s2ck_ref,
                     cdq_ref, s1dq_ref, s2dq_ref, cdk_ref, s1dk_ref, s2dk_ref,
                     qc_ref, kc_ref, vc_ref, qd_ref, kd_ref, vd_ref, stat_ref, *, steps_per_seq):
    lane = lax.broadcasted_iota(I32, (1, LANES), 1)
    one64 = jnp.where(lane == HEAD_DIM, 1.0, 0.0)
    last_lane = lane == LANES - 1
    half_rope = C_ROPE // 2
    stats = [jnp.zeros((1, LANES), F32), jnp.zeros((1, LANES), F32)]
    ones_mat = jnp.ones((LANES, LANES), BF16)

    def with_norm(val, fill, row, col):
        n2 = jnp.dot((val * val).astype(BF16), ones_mat, preferred_element_type=F32)
        stats[row] = jnp.where(lane == col, jnp.maximum(jnp.max(n2, axis=0, keepdims=True), stats[row]), stats[row])
        return jnp.where(last_lane, fill, val).astype(BF16)

    def head_norm(xg, gn_ref):
        ss = jnp.sum(xg * xg, axis=-1, keepdims=True) * (1.0 / HEAD_DIM)
        return xg * lax.rsqrt(ss + EPS) * gn_ref[...]

    n_chunks = 2
    rows_per = x_ref.shape[0] // n_chunks
    for c in range(n_chunks):
        rows = slice(c * rows_per, (c + 1) * rows_per)
        tab = lambda *refs: [r[rows, :] for r in refs]
        x = x_ref[rows, :]
        y = (x * _rms_scale(x) * g_ref[...]).astype(BF16)
        pm = jnp.dot(y, wm_ref[...], preferred_element_type=F32)
        cq = pm[:, :C_Q_RANK]
        cqn = (cq * _rms_scale(cq) * cqn_ref[...]).astype(BF16)
        qc = jnp.dot(cqn, wq_ref[...], preferred_element_type=F32)
        ckv = pm[:, C_Q_RANK:C_Q_RANK + C_KV_RANK]
        ckvn = (ckv * _rms_scale(ckv) * ckvn_ref[...]).astype(BF16)
        kv = jnp.dot(ckvn, wkv_ref[...], preferred_element_type=F32)
        kr = _rope3(pm[:, C_Q_RANK + C_KV_RANK:_SLAB_Q0], *tab(cck_ref, s1ck_ref, s2ck_ref), half_rope)
        t_cq = tab(ccq_ref, s1cq_ref, s2cq_ref)
        for h in range(C_HEADS):
            sl = slice(h * LANES, (h + 1) * LANES)
            qc_ref[rows, sl] = with_norm(_rope3(qc[:, sl], *t_cq, half_rope), 1.0, 1, h)
            kc_ref[rows, sl] = with_norm(kv[:, sl] + kr, -1.0, 0, h)
            vc_ref[rows, sl] = (kv[:, C_HEADS * LANES + h * LANES:C_HEADS * LANES + (h + 1) * LANES]
                                + one64).astype(BF16)
        t_dq = tab(cdq_ref, s1dq_ref, s2dq_ref)
        t_dk = tab(cdk_ref, s1dk_ref, s2dk_ref)
        for g in range(D_HEADS):
            xg = pm[:, _SLAB_Q0 + g * LANES:_SLAB_Q0 + (g + 1) * LANES]
            qd_ref[rows, g * LANES:(g + 1) * LANES] = with_norm(
                _rope3(head_norm(xg, dqn_ref), *t_dq, HEAD_DIM // 4), 1.0, 1, C_HEADS + g)
        for g in range(D_KV_HEADS):
            sl = slice(g * LANES, (g + 1) * LANES)
            xg = pm[:, _SLAB_K0 + g * LANES:_SLAB_K0 + (g + 1) * LANES]
            kd_ref[rows, sl] = with_norm(_rope3(head_norm(xg, dkn_ref), *t_dk, HEAD_DIM // 4), -1.0, 0, C_HEADS + g)
            vd_ref[rows, sl] = (pm[:, _SLAB_V0 + g * LANES:_SLAB_V0 + (g + 1) * LANES] + one64).astype(BF16)

    new = jnp.concatenate(stats + [jnp.zeros((6, LANES), F32)], axis=0)

    @pl.when(pl.program_id(0) % steps_per_seq == 0)
    def _():
        stat_ref[...] = new

    @pl.when(pl.program_id(0) % steps_per_seq != 0)
    def _():
        stat_ref[...] = jnp.maximum(stat_ref[...], new)


def _slabs(w, n_heads, width, lane_off=0):
    k = w.shape[0]
    w3 = w.reshape(k, n_heads, width)
    w3 = jnp.pad(w3, ((0, 0), (0, 0), (lane_off, LANES - width - lane_off)))
    return w3.reshape(k, n_heads * LANES)


def _axial_tables(row, col, scale):
    half = HEAD_DIM // 2
    cr, s1r, s2r = _rope_tables(row, D_THETA, half, 0, LANES, scale)
    cc, s1c, s2c = _rope_tables(col, D_THETA, half, half, LANES, scale)
    lane = jnp.arange(LANES)[None, :]
    return jnp.where(lane < half, cr, cc), s1r + s1c, s2r + s2c


def _proj_odd(x2d, seq, g_mix, w_in, cq_norm, w_cq_up, ckv_norm, w_ckv_up, dq_norm, dk_norm, tm=512):
    t = x2d.shape[0]
    nblk = seq // tm
    o1 = C_Q_RANK
    o2 = o1 + C_KV_RANK
    o3 = o2 + C_ROPE
    o4 = o3 + D_HEADS * HEAD_DIM
    o5 = o4 + D_KV_HEADS * HEAD_DIM
    wm = jnp.concatenate([
        w_in[:, :o2],
        _slabs(w_in[:, o2:o3], 1, C_ROPE, C_NOPE),
        _slabs(w_in[:, o3:o4], D_HEADS, HEAD_DIM),
        _slabs(w_in[:, o4:o5], D_KV_HEADS, HEAD_DIM),
        _slabs(w_in[:, o5:], D_KV_HEADS, HEAD_DIM),
    ], axis=1).astype(BF16)
    assert wm.shape[1] == _ODD_COLS
    wq = _slabs(w_cq_up, C_HEADS, C_NOPE + C_ROPE).astype(BF16)
    kv3 = w_ckv_up.reshape(C_KV_RANK, C_HEADS, 2 * HEAD_DIM)
    wkv = jnp.concatenate([
        _slabs(kv3[:, :, :C_NOPE].reshape(C_KV_RANK, -1), C_HEADS, C_NOPE),
        _slabs(kv3[:, :, C_NOPE:].reshape(C_KV_RANK, -1), C_HEADS, HEAD_DIM),
    ], axis=1).astype(BF16)
    pad64 = lambda g: jnp.pad(g, (0, LANES - HEAD_DIM)).reshape(1, LANES)

    pos = jnp.arange(seq, dtype=I32)
    row_pos = pos // GRID_W
    col_pos = pos % GRID_W
    c_scale = (C_NOPE + C_ROPE) ** -0.5 * LOG2E
    d_scale = HEAD_DIM ** -0.5 * LOG2E
    tabs = (_rope_tables(pos, ROPE_THETA, C_ROPE, C_NOPE, LANES, c_scale)
            + _rope_tables(pos, ROPE_THETA, C_ROPE, C_NOPE, LANES, 1.0)
            + _axial_tables(row_pos, col_pos, d_scale)
            + _axial_tables(row_pos, col_pos, 1.0))

    row = lambda i: (i, 0)
    full = lambda i: (0, 0)
    tspec = pl.BlockSpec((tm, LANES), lambda i: (i % nblk, 0))
    wide = C_HEADS * LANES
    kvw = D_KV_HEADS * LANES
    return pl.pallas_call(
        functools.partial(_proj_odd_kernel, steps_per_seq=nblk),
        grid=(t // tm,),
        in_specs=[
            pl.BlockSpec((tm, D_MODEL), row),
            pl.BlockSpec((1, D_MODEL), full),
            pl.BlockSpec(wm.shape, full),
            pl.BlockSpec((1, C_Q_RANK), full),
            pl.BlockSpec(wq.shape, full),
            pl.BlockSpec((1, C_KV_RANK), full),
            pl.BlockSpec(wkv.shape, full),
            pl.BlockSpec((1, LANES), full),
            pl.BlockSpec((1, LANES), full),
        ] + [tspec] * 12,
        out_specs=[
            pl.BlockSpec((tm, wide), row), pl.BlockSpec((tm, wide), row), pl.BlockSpec((tm, wide), row),
            pl.BlockSpec((tm, wide), row), pl.BlockSpec((tm, kvw), row), pl.BlockSpec((tm, kvw), row),
            pl.BlockSpec((None, 8, LANES), lambda i: (i // nblk, 0, 0)),
        ],
        out_shape=[
            jax.ShapeDtypeStruct((t, wide), BF16), jax.ShapeDtypeStruct((t, wide), BF16),
            jax.ShapeDtypeStruct((t, wide), BF16), jax.ShapeDtypeStruct((t, wide), BF16),
            jax.ShapeDtypeStruct((t, kvw), BF16), jax.ShapeDtypeStruct((t, kvw), BF16),
            jax.ShapeDtypeStruct((t // seq, 8, LANES), F32),
        ],
        compiler_params=_cparams(("arbitrary",)),
        name="proj_odd",
    )(x2d, g_mix.reshape(1, D_MODEL), wm, cq_norm.reshape(1, -1), wq, ckv_norm.reshape(1, -1), wkv,
      pad64(dq_norm), pad64(dk_norm), *tabs)


def _flash_kernel(q_ref, k_ref, v_ref, o_ref, qs_sc, m_sc, acc_sc, *, group, tq, tk):
    ki = pl.program_id(3)

    @pl.when(ki == 0)
    def _():
        for g in range(group):
            qs_sc[g * tq:(g + 1) * tq, :] = q_ref[:, g * LANES:(g + 1) * LANES]
        m_sc[...] = jnp.full(m_sc.shape, NEG, F32)
        acc_sc[...] = jnp.zeros(acc_sc.shape, F32)

    s = lax.dot_general(qs_sc[...], k_ref[...], _NT, preferred_element_type=F32)
    m_prev = m_sc[...]
    m_new = jnp.maximum(m_prev, jnp.max(s, axis=1, keepdims=True))
    alpha = jnp.exp2(m_prev - m_new)
    p = jnp.exp2(s - jnp.tile(m_new, (1, tk // LANES)))
    acc_sc[...] = alpha * acc_sc[...] + jnp.dot(p.astype(BF16), v_ref[...], preferred_element_type=F32)
    m_sc[...] = m_new

    @pl.when(ki == pl.num_programs(3) - 1)
    def _():
        acc = acc_sc[...]
        o = acc / acc[:, HEAD_DIM:HEAD_DIM + 1]
        for g in range(group):
            o_ref[:, g * LANES:(g + 1) * LANES] = o[g * tq:(g + 1) * tq].astype(BF16)


def _flash_bounded_kernel(bound_ref, q_ref, k_ref, v_ref, o_ref, qs_sc, acc_sc, *, group, tq):
    ki = pl.program_id(3)
    head = pl.program_id(0) * pl.num_programs(1) + pl.program_id(1)

    @pl.when(ki == 0)
    def _():
        fix = jnp.where(lax.broadcasted_iota(I32, (1, LANES), 1) == LANES - 1, bound_ref[head], 1.0)
        for g in range(group):
            qs_sc[g * tq:(g + 1) * tq, :] = (q_ref[:, g * LANES:(g + 1) * LANES].astype(F32) * fix).astype(BF16)
        acc_sc[...] = jnp.zeros(acc_sc.shape, F32)

    s_t = lax.dot_general(k_ref[...], qs_sc[...], _NT, preferred_element_type=F32)
    p_t = jnp.exp2(s_t).astype(BF16)
    acc_sc[...] += lax.dot_general(v_ref[...], p_t, (((0,), (0,)), ((), ())), preferred_element_type=F32)

    @pl.when(ki == pl.num_programs(3) - 1)
    def _():
        acc = acc_sc[...]
        o = (acc / acc[HEAD_DIM:HEAD_DIM + 1, :]).T
        for g in range(group):
            o_ref[:, g * LANES:(g + 1) * LANES] = o[g * tq:(g + 1) * tq].astype(BF16)


def _flash(q, k, v, group, nk, bounded, rows=1024, tk=512, tk_bounded=2048):
    b, s, qw = q.shape
    hk = k.shape[2] // LANES
    tq = rows // group
    tk = min(tk, s)
    tkb = min(tk_bounded, s)
    out_shape = jax.ShapeDtypeStruct((b, s, qw), BF16)
    sem = ("parallel", "parallel", "parallel", "arbitrary")

    def running_max(q, k, v, nk):
        qspec = pl.BlockSpec((None, tq, group * LANES), lambda bi, hi, qi, ki: (bi, qi, hi))
        kspec = pl.BlockSpec((None, tk, LANES), lambda bi, hi, qi, ki: (bi, ki, hi))
        return pl.pallas_call(
            functools.partial(_flash_kernel, group=group, tq=tq, tk=tk),
            grid=(b, hk, s // tq, s // tk),
            in_specs=[qspec, kspec, kspec],
            out_specs=qspec,
            out_shape=out_shape,
            scratch_shapes=[pltpu.VMEM((rows, LANES), BF16), pltpu.VMEM((rows, LANES), F32),
                            pltpu.VMEM((rows, LANES), F32)],
            compiler_params=_cparams(sem),
            name="flash",
        )(q, k, v)

    def bound(q, k, v, nk):
        qspec = pl.BlockSpec((None, tq, group * LANES), lambda bi, hi, qi, ki, nkr: (bi, qi, hi))
        kspec = pl.BlockSpec((None, tkb, LANES), lambda bi, hi, qi, ki, nkr: (bi, ki, hi))
        return pl.pallas_call(
            functools.partial(_flash_bounded_kernel, group=group, tq=tq),
            grid_spec=pltpu.PrefetchScalarGridSpec(
                num_scalar_prefetch=1,
                grid=(b, hk, s // tq, s // tkb),
                in_specs=[qspec, kspec, kspec],
                out_specs=qspec,
                scratch_shapes=[pltpu.VMEM((rows, LANES), BF16), pltpu.VMEM((LANES, rows), F32)],
            ),
            out_shape=out_shape,
            compiler_params=_cparams(sem),
            name="flash_bounded",
        )(nk.reshape(-1), q, k, v)

    return lax.cond(bounded, bound, running_max, q, k, v, nk)


def _moe(x1, h2, aff, batch, seq, w_gate, w_up, w_down, layer, g_final, final):
    idx, gates, spos, cb = _route(aff.reshape(-1, LANES), batch, seq)
    y = _ffn(idx, gates, h2, w_gate, w_up, w_down, layer, seq)
    return _combine(x1, spos.reshape(-1, N_EXPERTS, LANES), cb, y, g_final, final, seq)


def kernel(x, norm_mix, norm_ffn, even_w_in, even_gmlp_norm, even_w_spatial, even_b_spatial, even_w_out,
           odd_w_in, odd_cq_norm, odd_w_cq_up, odd_ckv_norm, odd_w_ckv_up, odd_dq_norm, odd_dk_norm, odd_w_out,
           moe_w_router, moe_w_gate, moe_w_up, moe_w_down, final_norm):
    b, s, d = x.shape
    depth = norm_mix.shape[0]
    x2d = x.reshape(b * s, d)
    for i in range(depth):
        j = i // 2
        last = i == depth - 1
        if i % 2 == 0:
            q, k, v, go = _proj_even(x2d, s, norm_mix[i], even_w_in[j], even_gmlp_norm[j], even_w_spatial[j],
                                     even_b_spatial[j])
            a = _dilated(q.reshape(b, s, A_WIDTH), k.reshape(b, s, A_WIDTH), v.reshape(b, s, A_WIDTH))
            x1, h2, aff = _outproj(x2d, a.reshape(b * s, A_WIDTH), go, even_w_out[j][:A_WIDTH],
                                   even_w_out[j][A_WIDTH:], norm_ffn[i], moe_w_router[i])
        else:
            qc, kc, vc, qd, kd, vd, stat = _proj_odd(x2d, s, norm_mix[i], odd_w_in[j], odd_cq_norm[j],
                                                     odd_w_cq_up[j], odd_ckv_norm[j], odd_w_ckv_up[j],
                                                     odd_dq_norm[j], odd_dk_norm[j])
            grp = D_HEADS // D_KV_HEADS
            k2_c, k2_d = stat[:, 0, :C_HEADS], stat[:, 0, C_HEADS:C_HEADS + D_KV_HEADS]
            q2_c = stat[:, 1, :C_HEADS]
            q2_d = jnp.max(stat[:, 1, C_HEADS:C_HEADS + D_HEADS].reshape(b, D_KV_HEADS, grp), axis=-1)
            bound_c = jnp.sqrt(q2_c * k2_c) * _NORM_MARGIN ** 2
            bound_d = jnp.sqrt(q2_d * k2_d) * _NORM_MARGIN ** 2
            bounded = jnp.maximum(jnp.max(bound_c), jnp.max(bound_d)) <= _MAX_SCORE_BOUND
            r3 = lambda z: z.reshape(b, s, -1)
            oc = _flash(r3(qc), r3(kc), r3(vc), 1, bound_c, bounded)
            od = _flash(r3(qd), r3(kd), r3(vd), grp, bound_d, bounded)
            cw = C_HEADS * HEAD_DIM
            x1, h2, aff = _outproj(x2d, oc.reshape(b * s, -1), od.reshape(b * s, -1),
                                   _slabs(odd_w_out[j][:cw].T, C_HEADS, HEAD_DIM).T,
                                   _slabs(odd_w_out[j][cw:].T, D_HEADS, HEAD_DIM).T,
                                   norm_ffn[i], moe_w_router[i])
        x2d = _moe(x1, h2, aff, b, s, moe_w_gate, moe_w_up, moe_w_down, i, final_norm, last)
    return x2d.reshape(b, s, d)
```

```python
import functools
import math

import jax
import jax.numpy as jnp
from jax import lax
from jax.experimental import pallas as pl
from jax.experimental.pallas import tpu as pltpu

F32 = jnp.float32
BF16 = jnp.bfloat16
I32 = jnp.int32

EPS = 1e-6
NEG = -1e30
LOG2E = 1.4426950408889634

D_MODEL = 1024
HEAD_DIM = 64
ROPE_THETA = 500000.0
ROT_DIM = 16
GRID_W = 64
A_HEADS = 12
A_WIDTH = 768
A_DILATIONS = (1, 4, 16)
A_RADIUS = 64
B_WIDTH = 256
B_GROUPS = 4
B_CHUNK = 128
C_HEADS = 8
C_Q_RANK = 256
C_KV_RANK = 128
C_NOPE = 64
C_ROPE = 32
D_HEADS = 8
D_KV_HEADS = 2
D_THETA = 10000.0
N_EXPERTS = 16
EC_FACTOR = 2
EXPERT_FF = 512

_NORM_MARGIN = 1.01
_MAX_SCORE_BOUND = 55.0

LANES = 128
VMEM_LIMIT = 48 * 1024 * 1024
_DILATED_VMEM_LIMIT = 56 * 1024 * 1024

_NT = (((1,), (1,)), ((), ()))


def _cparams(sem):
    return pltpu.CompilerParams(dimension_semantics=sem, vmem_limit_bytes=VMEM_LIMIT)


def _rms_scale(x):
    return lax.rsqrt(jnp.mean(x * x, axis=-1, keepdims=True) + EPS)


def _rope3(a, c, s1, s2, shift):
    return a * c + pltpu.roll(a, LANES - shift, 1) * s1 + pltpu.roll(a, shift, 1) * s2


def _split_dot(x, w_bf16):
    hi = x.astype(BF16)
    lo = (x - hi.astype(F32)).astype(BF16)
    return (jnp.dot(hi, w_bf16, preferred_element_type=F32)
            + jnp.dot(lo, w_bf16, preferred_element_type=F32))


def _proj_even_kernel(x_ref, g_ref, w_ref, cq_ref, s1q_ref, s2q_ref, ck_ref, s1k_ref, s2k_ref,
                      gn_ref, gmat_ref, ws_ref, bs_ref,
                      q_ref, k_ref, v_ref, go_ref):
    x = x_ref[...]
    y = (x * _rms_scale(x) * g_ref[...]).astype(BF16)
    tm = x.shape[0]

    aq = jnp.dot(y, w_ref[:, 0:A_WIDTH], preferred_element_type=F32)
    tq = (cq_ref[...], s1q_ref[...], s2q_ref[...])
    for j in range(A_WIDTH // LANES):
        sl = slice(j * LANES, (j + 1) * LANES)
        q_ref[:, sl] = _rope3(aq[:, sl], *tq, ROT_DIM // 2)
    ak = jnp.dot(y, w_ref[:, A_WIDTH:2 * A_WIDTH], preferred_element_type=F32)
    tk = (ck_ref[...], s1k_ref[...], s2k_ref[...])
    for j in range(A_WIDTH // LANES):
        sl = slice(j * LANES, (j + 1) * LANES)
        k_ref[:, sl] = _rope3(ak[:, sl], *tk, ROT_DIM // 2)
    v_ref[...] = jnp.dot(y, w_ref[:, 2 * A_WIDTH:3 * A_WIDTH], preferred_element_type=F32)

    z = jnp.dot(y, w_ref[:, 3 * A_WIDTH:3 * A_WIDTH + 2 * B_WIDTH], preferred_element_type=F32)
    ge = jax.nn.gelu(z)
    u = ge[:, :B_WIDTH]
    vv = ge[:, B_WIDTH:]
    ss = _split_dot(vv * vv, gmat_ref[...])
    vn = (vv * lax.rsqrt(ss + EPS) * gn_ref[...]).astype(BF16)
    grp = lax.broadcasted_iota(I32, (B_CHUNK, B_WIDTH), 1) // (B_WIDTH // B_GROUPS)
    for c in range(tm // B_CHUNK):
        rows = slice(c * B_CHUNK, (c + 1) * B_CHUNK)
        vc = vn[rows]
        mg = [jnp.dot(ws_ref[g], vc, preferred_element_type=F32) for g in range(B_GROUPS)]
        mixed = jnp.where(grp == 0, mg[0], jnp.where(grp == 1, mg[1], jnp.where(grp == 2, mg[2], mg[3])))
        go_ref[rows, :] = (u[rows] * (mixed + bs_ref[...])).astype(BF16)


def _rope_tables(pos, theta, r, lane_off, period, scale):
    half = r // 2
    inv = jnp.power(jnp.float32(theta), -jnp.arange(half, dtype=F32) * (2.0 / r))
    ang = pos.astype(F32)[:, None] * inv[None, :]
    cos, sin = jnp.cos(ang), jnp.sin(ang)
    o = (jnp.arange(LANES) % period) - lane_off
    in_lo = (o >= 0) & (o < half)
    in_hi = (o >= half) & (o < r)
    idx = jnp.clip(jnp.where(in_hi, o - half, o), 0, half - 1)
    c = jnp.where((in_lo | in_hi)[None, :], cos[:, idx], 1.0)
    s1 = jnp.where(in_lo[None, :], -sin[:, idx], 0.0)
    s2 = jnp.where(in_hi[None, :], sin[:, idx], 0.0)
    return c * scale, s1 * scale, s2 * scale


def _proj_even(x2d, seq, g_mix, w_in, gmlp_norm, w_s, b_s, tm=512):
    t = x2d.shape[0]
    nblk = seq // tm
    pos = jnp.arange(seq, dtype=I32)
    qscale = HEAD_DIM ** -0.5 * LOG2E
    cq, s1q, s2q = _rope_tables(pos, ROPE_THETA, ROT_DIM, 0, HEAD_DIM, qscale)
    ck, s1k, s2k = _rope_tables(pos, ROPE_THETA, ROT_DIM, 0, HEAD_DIM, 1.0)
    gdim = B_WIDTH // B_GROUPS
    gid = jnp.arange(B_WIDTH) // gdim
    gmat = jnp.where(gid[:, None] == gid[None, :], 1.0 / gdim, 0.0).astype(BF16)
    bias = jnp.repeat(b_s.T, gdim, axis=1)
    row = lambda i: (i, 0)
    tab = lambda i: (i % nblk, 0)
    full = lambda i: (0, 0)
    tspec = pl.BlockSpec((tm, LANES), tab)
    return pl.pallas_call(
        _proj_even_kernel,
        grid=(t // tm,),
        in_specs=[
            pl.BlockSpec((tm, D_MODEL), row),
            pl.BlockSpec((1, D_MODEL), full),
            pl.BlockSpec(w_in.shape, full),
            tspec, tspec, tspec, tspec, tspec, tspec,
            pl.BlockSpec((1, B_WIDTH), full),
            pl.BlockSpec((B_WIDTH, B_WIDTH), full),
            pl.BlockSpec((B_GROUPS, B_CHUNK, B_CHUNK), lambda i: (0, 0, 0)),
            pl.BlockSpec((B_CHUNK, B_WIDTH), full),
        ],
        out_specs=[
            pl.BlockSpec((tm, A_WIDTH), row),
            pl.BlockSpec((tm, A_WIDTH), row),
            pl.BlockSpec((tm, A_WIDTH), row),
            pl.BlockSpec((tm, B_WIDTH), row),
        ],
        out_shape=[
            jax.ShapeDtypeStruct((t, A_WIDTH), F32),
            jax.ShapeDtypeStruct((t, A_WIDTH), F32),
            jax.ShapeDtypeStruct((t, A_WIDTH), F32),
            jax.ShapeDtypeStruct((t, B_WIDTH), BF16),
        ],
        compiler_params=_cparams(("parallel",)),
        name="proj_even",
    )(x2d, g_mix.reshape(1, D_MODEL), w_in.astype(BF16), cq, s1q, s2q, ck, s1k, s2k,
      gmlp_norm.reshape(1, B_WIDTH), gmat, w_s.astype(BF16), bias)


_TQ = 128
_TK = _TQ + 2 * A_RADIUS
_NORM_ROWS = 512


def _dilated_kernel(q_ref, k_ref, v_ref, o_ref, m_sc, l_sc, bias_sc, *, seq):
    half0 = lax.broadcasted_iota(I32, (1, LANES), 1) < HEAD_DIM
    row_head = lax.broadcasted_iota(I32, (LANES, LANES), 0) // HEAD_DIM
    pick = [jnp.where(row_head == h, 1.0, 0.0).astype(BF16) for h in range(2)]
    n_pat = len(A_DILATIONS)
    zero_row = jnp.zeros((1, LANES), F32)

    def max_head_sq(ref):
        def body(c, mx):
            x = ref[pl.ds(pl.multiple_of(c * _NORM_ROWS, _NORM_ROWS), _NORM_ROWS), :]
            sq = (x * x).astype(BF16)
            return tuple(jnp.maximum(mx[h], jnp.max(jnp.dot(sq, pick[h], preferred_element_type=F32),
                                                    axis=0, keepdims=True)) for h in range(2))
        return lax.fori_loop(0, seq // _NORM_ROWS, body, (zero_row, zero_row))

    max_q2 = max_head_sq(q_ref)
    max_k2 = max_head_sq(k_ref)
    bound = [jnp.sqrt(max_q2[h] * max_k2[h]) * (_NORM_MARGIN * _NORM_MARGIN) for h in range(2)]
    worst = jnp.max(jnp.maximum(bound[0], bound[1]))

    diff = lax.broadcasted_iota(I32, (_TQ, _TK), 1) - lax.broadcasted_iota(I32, (_TQ, _TK), 0)
    for case in range(3):
        band = jnp.where(jnp.abs(diff - case * A_RADIUS) <= A_RADIUS, 0.0, NEG)
        bias_sc[6 + case] = band
        for h in range(2):
            bias_sc[3 * h + case] = band - jnp.tile(bound[h], (1, _TK // LANES))
    one_bf16 = jnp.ones((), BF16)

    def run(bounded):
        for pi, d in enumerate(A_DILATIONS):
            cls_len = seq // d
            tpc = cls_len // _TQ

            def body(j, carry, d=d, cls_len=cls_len, tpc=tpc, pi=pi):
                i = j // tpc
                n = j % tpc
                l0 = n * _TQ
                kst = jnp.clip(l0 - A_RADIUS, 0, cls_len - _TK)
                case = (l0 - kst) // A_RADIUS
                if d == 1:
                    qrows = pl.ds(pl.multiple_of(l0, _TQ), _TQ)
                    krows = pl.ds(pl.multiple_of(kst, A_RADIUS), _TK)
                else:
                    qrows = pl.ds(l0 * d + i, _TQ, stride=d)
                    krows = pl.ds(kst * d + i, _TK, stride=d)
                q = q_ref[qrows, :]
                kb = k_ref[krows, :].astype(BF16)
                vb = v_ref[krows, :].astype(BF16)
                parts = []
                for h in range(2):
                    qh = jnp.where(half0 if h == 0 else jnp.logical_not(half0), q, 0.0).astype(BF16)
                    s = lax.dot_general(qh, kb, _NT, preferred_element_type=F32)
                    if bounded:
                        mt = None
                        p = jnp.exp2(s + bias_sc[3 * h + case]).astype(BF16)
                    else:
                        s = s + bias_sc[6 + case]
                        mt = jnp.max(s, axis=-1, keepdims=True)
                        p = jnp.exp2(s - mt).astype(BF16)
                    vh = jnp.where(half0 if h == 0 else jnp.logical_not(half0), vb, one_bf16)
                    parts.append((mt, jnp.dot(p, vh, preferred_element_type=F32)))
                ot = jnp.where(half0, parts[0][1], parts[1][1])
                lt = pltpu.roll(jnp.where(half0, parts[1][1], parts[0][1]), HEAD_DIM, 1)
                if bounded:
                    if pi > 0:
                        lt = l_sc[qrows, :] + lt
                        ot = o_ref[qrows, :] + ot
                else:
                    mt = jnp.where(half0, parts[0][0], parts[1][0])
                    if pi > 0:
                        mp = m_sc[qrows, :]
                        mn = jnp.maximum(mp, mt)
                        a = jnp.exp2(mp - mn)
                        b = jnp.exp2(mt - mn)
                        lt = a * l_sc[qrows, :] + b * lt
                        ot = a * o_ref[qrows, :] + b * ot
                        mt = mn
                    if pi < n_pat - 1:
                        m_sc[qrows, :] = mt
                if pi == n_pat - 1:
                    o_ref[qrows, :] = ot / lt
                else:
                    l_sc[qrows, :] = lt
                    o_ref[qrows, :] = ot
                return carry

            lax.fori_loop(0, seq // _TQ, body, 0, unroll=8)

    @pl.when(worst <= _MAX_SCORE_BOUND)
    def _():
        run(True)

    @pl.when(jnp.logical_not(worst <= _MAX_SCORE_BOUND))
    def _():
        run(False)


def _dilated(q, k, v):
    b, s, w = q.shape
    spec = pl.BlockSpec((None, s, LANES), lambda bi, hi: (bi, 0, hi))
    return pl.pallas_call(
        functools.partial(_dilated_kernel, seq=s),
        grid=(b, w // LANES),
        in_specs=[spec, spec, spec],
        out_specs=spec,
        out_shape=jax.ShapeDtypeStruct((b, s, w), F32),
        scratch_shapes=[pltpu.VMEM((s, LANES), F32), pltpu.VMEM((s, LANES), F32),
                        pltpu.VMEM((9, _TQ, _TK), F32)],
        compiler_params=pltpu.CompilerParams(dimension_semantics=("parallel", "parallel"),
                                             vmem_limit_bytes=_DILATED_VMEM_LIMIT),
        name="dilated",
    )(q, k, v)


def _outproj_kernel(x_ref, a_ref, b_ref, wa_ref, wb_ref, gf_ref, wr_ref, x1_ref, h2_ref, aff_ref):
    x1 = (x_ref[...]
          + jnp.dot(a_ref[...].astype(BF16), wa_ref[...], preferred_element_type=F32)
          + jnp.dot(b_ref[...].astype(BF16), wb_ref[...], preferred_element_type=F32))
    x1_ref[...] = x1
    h2 = x1 * _rms_scale(x1) * gf_ref[...]
    h2_ref[...] = h2
    hi = h2.astype(BF16)
    lo = (h2 - hi.astype(F32)).astype(BF16)
    both = jnp.dot(hi, wr_ref[...], preferred_element_type=F32)
    lg = (both[:, :LANES] + both[:, LANES:]
          + jnp.dot(lo, wr_ref[:, :LANES], preferred_element_type=F32))
    valid = lax.broadcasted_iota(I32, lg.shape, 1) < N_EXPERTS
    lg = jnp.where(valid, lg, NEG)
    e = jnp.exp(lg - jnp.max(lg, axis=-1, keepdims=True))
    aff = e / jnp.sum(e, axis=-1, keepdims=True)
    aff_t = aff.T
    for j in range(aff.shape[0] // LANES):
        aff_ref[j] = aff_t[:N_EXPERTS, j * LANES:(j + 1) * LANES]


def _outproj(x2d, a, b, wa, wb, g_ffn, w_router, tm=512):
    t = x2d.shape[0]
    wr = jnp.pad(w_router, ((0, 0), (0, LANES - N_EXPERTS)))
    wr_hi = wr.astype(BF16)
    wr_lo = (wr - wr_hi.astype(F32)).astype(BF16)
    wr2 = jnp.concatenate([wr_hi, wr_lo], axis=1)
    row = lambda i: (i, 0)
    full = lambda i: (0, 0)
    return pl.pallas_call(
        _outproj_kernel,
        grid=(t // tm,),
        in_specs=[
            pl.BlockSpec((tm, D_MODEL), row),
            pl.BlockSpec((tm, a.shape[1]), row),
            pl.BlockSpec((tm, b.shape[1]), row),
            pl.BlockSpec(wa.shape, full),
            pl.BlockSpec(wb.shape, full),
            pl.BlockSpec((1, D_MODEL), full),
            pl.BlockSpec((D_MODEL, 2 * LANES), full),
        ],
        out_specs=[
            pl.BlockSpec((tm, D_MODEL), row),
            pl.BlockSpec((tm, D_MODEL), row),
            pl.BlockSpec((tm // LANES, N_EXPERTS, LANES), lambda i: (i, 0, 0)),
        ],
        out_shape=[
            jax.ShapeDtypeStruct((t, D_MODEL), F32),
            jax.ShapeDtypeStruct((t, D_MODEL), F32),
            jax.ShapeDtypeStruct((t // LANES, N_EXPERTS, LANES), F32),
        ],
        compiler_params=_cparams(("parallel",)),
        name="outproj",
    )(x2d, a, b, wa.astype(BF16), wb.astype(BF16), g_ffn.reshape(1, D_MODEL), wr2)


def _route_kernel(aff_ref, idx_ref, gate_ref, spos_ref, cb_ref, thr_sc, need_sc, *, cap):
    nblk = aff_ref.shape[0] // N_EXPERTS
    bits = pltpu.bitcast(aff_ref[...], I32).reshape(nblk, N_EXPERTS, LANES)

    def count(pred):
        return jnp.sum(jnp.sum(jnp.where(pred, 1.0, 0.0), axis=0), axis=1, keepdims=True)

    def search(it, thr):
        cand = thr | jnp.left_shift(jnp.int32(1), 30 - it)
        return jnp.where(count(bits >= cand[None]) >= cap, cand, thr)

    thr = lax.fori_loop(0, 31, search, jnp.zeros((N_EXPERTS, 1), I32))
    need = cap - count(bits > thr[None])
    thr_sc[...] = jnp.broadcast_to(thr, (N_EXPERTS, LANES))
    need_sc[...] = jnp.broadcast_to(need, (N_EXPERTS, LANES))

    ri = lax.broadcasted_iota(I32, (LANES, LANES), 0)
    ci = lax.broadcasted_iota(I32, (LANES, LANES), 1)
    upper = jnp.where(ri <= ci, 1.0, 0.0).astype(BF16)
    lower = jnp.where(ci <= ri, 1.0, 0.0).astype(BF16)
    eye = jnp.where(ri == ci, 1.0, 0.0).astype(BF16)
    ones = jnp.ones((LANES, LANES), BF16)
    bi = lax.broadcasted_iota(I32, (nblk, nblk), 0)
    bj = lax.broadcasted_iota(I32, (nblk, nblk), 1)
    strict = jnp.where(bj < bi, 1.0, 0.0).astype(BF16)
    before = jnp.where(bi < bj, 1.0, 0.0).astype(BF16)
    mean_rows = jnp.full((8, LANES), 1.0 / LANES, BF16)
    c_row = lax.broadcasted_iota(I32, (1, cap), 1).astype(F32)
    blk_iota = lax.broadcasted_iota(I32, (nblk, cap), 0).astype(F32)
    t_iota = lax.broadcasted_iota(I32, (LANES, cap), 0).astype(F32)
    rep = cap // LANES

    def cums(mask_bf16):
        lp = jnp.dot(mask_bf16, upper, preferred_element_type=F32)
        bc = jnp.dot(mask_bf16, ones, preferred_element_type=F32)
        bst = jnp.dot(strict, bc.astype(BF16), preferred_element_type=F32)
        return lp, bc, bst

    def per_expert(e, carry):
        a = aff_ref[pl.ds(e, nblk, stride=N_EXPERTS), :]
        ab = pltpu.bitcast(a, I32)
        thr_e = thr_sc[pl.ds(e, 1), :]
        need_e = need_sc[pl.ds(e, 1), :]
        gt = ab > thr_e
        eq = ab == thr_e
        eqf = jnp.where(eq, 1.0, 0.0)
        lp_q, _, bst_q = cums(eqf.astype(BF16))
        sel = jnp.logical_or(gt, jnp.logical_and(eq, bst_q + lp_q - eqf < need_e))
        mb = jnp.where(sel, 1.0, 0.0).astype(BF16)
        lp, bc, bst = cums(mb)
        spos_ref[pl.ds(e, nblk, stride=N_EXPERTS), :] = jnp.where(sel, bst + lp - 1.0, -1.0)
        bc_row = lax.dot_general(mean_rows, bc.astype(BF16), _NT, preferred_element_type=F32)
        cb_ref[pl.ds(e, 1), :] = jnp.dot(bc_row.astype(BF16), before, preferred_element_type=F32)[:1].astype(I32)
        bend_w = jnp.tile(bst + bc, (1, rep))
        bst_w = jnp.tile(bst, (1, rep))
        blk_c = jnp.sum(jnp.where(bend_w <= c_row, 1.0, 0.0), axis=0, keepdims=True)
        onehot = blk_iota == blk_c
        bst_c = jnp.sum(jnp.where(onehot, bst_w, 0.0), axis=0, keepdims=True)
        r_c = c_row - bst_c
        ohb = jnp.where(onehot, 1.0, 0.0).astype(BF16)
        lp_t = lax.dot_general(lower, mb, _NT, preferred_element_type=F32)
        lp_c = jnp.dot(lp_t.astype(BF16), ohb, preferred_element_type=F32)
        tl_c = jnp.sum(jnp.where(lp_c <= r_c, 1.0, 0.0), axis=0, keepdims=True)
        idx_ref[pl.ds(e, 1), :] = (blk_c * LANES + tl_c).astype(I32)
        a_hi = a.astype(BF16)
        a_lo = (a - a_hi.astype(F32)).astype(BF16)
        at_hi = lax.dot_general(eye, a_hi, _NT, preferred_element_type=F32).astype(BF16)
        at_lo = lax.dot_general(eye, a_lo, _NT, preferred_element_type=F32).astype(BF16)
        g_c = (jnp.dot(at_hi, ohb, preferred_element_type=F32)
               + jnp.dot(at_lo, ohb, preferred_element_type=F32))
        gate_ref[pl.ds(e, 1), :] = jnp.sum(jnp.where(t_iota == tl_c, g_c, 0.0), axis=0, keepdims=True)
        return carry

    lax.fori_loop(0, N_EXPERTS, per_expert, 0)


def _route(aff2d, batch, seq):
    cap = EC_FACTOR * seq // N_EXPERTS
    nblk = seq // LANES
    rows = nblk * N_EXPERTS
    out_spec = pl.BlockSpec((None, N_EXPERTS, cap), lambda b: (b, 0, 0))
    return pl.pallas_call(
        functools.partial(_route_kernel, cap=cap),
        grid=(batch,),
        in_specs=[pl.BlockSpec((rows, LANES), lambda b: (b, 0))],
        out_specs=[out_spec, out_spec, pl.BlockSpec((rows, LANES), lambda b: (b, 0)),
                   pl.BlockSpec((None, N_EXPERTS, nblk), lambda b: (b, 0, 0))],
        out_shape=[jax.ShapeDtypeStruct((batch, N_EXPERTS, cap), I32),
                   jax.ShapeDtypeStruct((batch, N_EXPERTS, cap), F32),
                   jax.ShapeDtypeStruct((batch * rows, LANES), F32),
                   jax.ShapeDtypeStruct((batch, N_EXPERTS, nblk), I32)],
        scratch_shapes=[pltpu.VMEM((N_EXPERTS, LANES), I32), pltpu.VMEM((N_EXPERTS, LANES), F32)],
        compiler_params=_cparams(("parallel",)),
        name="route",
    )(aff2d)


def _ffn_kernel(idx_ref, nxt_ref, gate_ref, h_hbm, wg32_ref, wu32_ref, wd32_ref, y_ref, buf, sem,
                wg_ref, wu_ref, wd_ref, *, seq, tc, nsub):
    seq_id = pl.program_id(1)
    n_seq = pl.num_programs(1)
    step = pl.program_id(0) * n_seq + seq_id
    last_step = pl.num_programs(0) * n_seq - 1
    base = seq_id * seq
    next_base = jnp.where(seq_id + 1 < n_seq, seq_id + 1, 0) * seq

    @pl.when(seq_id == 0)
    def _():
        wg_ref[...] = wg32_ref[...].astype(BF16)
        wu_ref[...] = wu32_ref[...].astype(BF16)
        wd_ref[...] = wd32_ref[...].astype(BF16)

    def row_copy(ids, row0, j, r, slot):
        return pltpu.make_async_copy(h_hbm.at[pl.ds(row0 + ids[0, 0, j * tc + r], 1), :],
                                     buf.at[slot, pl.ds(r, 1), :], sem.at[slot])

    def issue(ids, row0, j, slot):
        for r in range(tc):
            row_copy(ids, row0, j, r, slot).start()

    diag = lax.broadcasted_iota(I32, (tc, tc), 0) == lax.broadcasted_iota(I32, (tc, tc), 1)
    ones = jnp.ones((tc, LANES), BF16)

    @pl.when(step == 0)
    def _():
        issue(idx_ref, base, 0, 0)

    for j in range(nsub):
        slot = j % 2
        if j + 1 < nsub:
            issue(idx_ref, base, j + 1, 1 - slot)
        else:
            @pl.when(step < last_step)
            def _():
                issue(nxt_ref, next_base, 0, nsub % 2)
        for r in range(tc):
            row_copy(idx_ref, base, j, r, slot).wait()
        xs = buf[slot].astype(BF16)
        g = jnp.dot(xs, wg_ref[...], preferred_element_type=F32)
        u = jnp.dot(xs, wu_ref[...], preferred_element_type=F32)
        hm = (jax.nn.silu(g) * u).astype(BF16)
        y = jnp.dot(hm, wd_ref[...], preferred_element_type=F32)
        gr = jnp.broadcast_to(gate_ref[0, :, j * tc:(j + 1) * tc], (tc, tc))
        gcol = _split_dot(jnp.where(diag, gr, 0.0), ones)
        y_ref[j * tc:(j + 1) * tc, :] = (y * jnp.tile(gcol, (1, D_MODEL // LANES))).astype(BF16)


def _ffn(idx, gates, h2d, w_gate, w_up, w_down, layer, seq, tc=256):
    b, ne, cap = idx.shape
    tc = min(tc, cap // 2)
    nsub = cap // tc
    assert nsub % 2 == 0
    idx3 = idx.reshape(b * ne, 1, cap)
    gate3 = gates.reshape(b * ne, 1, cap)
    slot = lambda ei, bi: (bi * ne + ei, 0, 0)
    next_slot = lambda ei, bi: (jnp.where(bi + 1 < b, (bi + 1) * ne + ei, jnp.minimum(ei + 1, ne - 1)), 0, 0)
    wspec = lambda shape: pl.BlockSpec((None, None) + shape, lambda ei, bi: (layer, ei, 0, 0))
    return pl.pallas_call(
        functools.partial(_ffn_kernel, seq=seq, tc=tc, nsub=nsub),
        grid=(ne, b),
        in_specs=[
            pl.BlockSpec((1, 1, cap), slot, memory_space=pltpu.SMEM),
            pl.BlockSpec((1, 1, cap), next_slot, memory_space=pltpu.SMEM),
            pl.BlockSpec((1, 1, cap), slot),
            pl.BlockSpec(memory_space=pl.ANY),
            wspec((D_MODEL, EXPERT_FF)), wspec((D_MODEL, EXPERT_FF)), wspec((EXPERT_FF, D_MODEL)),
        ],
        out_specs=pl.BlockSpec((None, None, cap, D_MODEL), lambda ei, bi: (bi, ei, 0, 0)),
        out_shape=jax.ShapeDtypeStruct((b, ne, cap, D_MODEL), BF16),
        scratch_shapes=[pltpu.VMEM((2, tc, D_MODEL), F32), pltpu.SemaphoreType.DMA((2,)),
                        pltpu.VMEM((D_MODEL, EXPERT_FF), BF16), pltpu.VMEM((D_MODEL, EXPERT_FF), BF16),
                        pltpu.VMEM((EXPERT_FF, D_MODEL), BF16)],
        compiler_params=pltpu.CompilerParams(dimension_semantics=("arbitrary", "arbitrary"),
                                             vmem_limit_bytes=VMEM_LIMIT, disable_bounds_checks=True),
        name="ffn",
    )(idx3, idx3, gate3, h2d, w_gate, w_up, w_down)


_CTM = 256
_CWIN = 128


def _combine_kernel(cb_ref, x_ref, sp_ref, y_hbm, g_ref, o_ref, ybuf, xbuf, sem, xsem, *,
                    final, tiles_per_seq, nblk, cap):
    i = pl.program_id(0)
    n_tiles = pl.num_programs(0)
    b = i // tiles_per_seq
    slot = i % 2

    def window(tile, e):
        tb = tile // tiles_per_seq
        off = (tb * N_EXPERTS + e) * (nblk + 1) + (tile % tiles_per_seq) * (_CTM // LANES)
        s0 = cb_ref[off]
        s1 = cb_ref[off + _CTM // LANES]
        start = jnp.minimum((s0 // 64) * 64, cap - _CWIN)
        return s1, pl.multiple_of(start, 64)

    def fetch(tile, e, start, buf_slot):
        return pltpu.make_async_copy(y_hbm.at[tile // tiles_per_seq, e, pl.ds(start, _CWIN), :],
                                     ybuf.at[buf_slot, pl.ds(e * _CWIN, _CWIN), :], sem.at[buf_slot])

    def fetch_all(tile, buf_slot):
        for e in range(N_EXPERTS):
            fetch(tile, e, window(tile, e)[1], buf_slot).start()

    @pl.when(i == 0)
    def _():
        fetch_all(i, slot)

    @pl.when(i + 1 < n_tiles)
    def _():
        fetch_all(i + 1, 1 - slot)

    wins = [window(i, e) for e in range(N_EXPERTS)]
    pad = jnp.full((LANES - N_EXPERTS, LANES), -1.0, F32)
    sp_t = jnp.concatenate([jnp.concatenate([sp_ref[hf], pad], axis=0).T for hf in range(_CTM // LANES)],
                           axis=0)
    lane = lax.broadcasted_iota(I32, (1, _CWIN), 1).astype(F32)
    hits = [jnp.where(sp_t[:, e:e + 1] - wins[e][1].astype(F32) == lane, 1.0, 0.0).astype(BF16)
            for e in range(N_EXPERTS)]
    for e in range(N_EXPERTS):
        fetch(i, e, wins[e][1], slot).wait()
    o_ref[...] = x_ref[...] + jnp.dot(jnp.concatenate(hits, axis=1), ybuf[slot], preferred_element_type=F32)
    for e in range(N_EXPERTS):
        s1, start = wins[e]
        col = sp_t[:, e:e + 1]

        def extra(k, carry, e=e, s1=s1, start=start, col=col):
            lo = start + (k + 1) * _CWIN
            st = pl.multiple_of(jnp.minimum(lo, cap - _CWIN), 64)
            cp = pltpu.make_async_copy(y_hbm.at[b, e, pl.ds(st, _CWIN), :], xbuf, xsem)
            cp.start()
            cp.wait()
            hit = jnp.where(jnp.logical_and(col - st.astype(F32) == lane, col >= lo.astype(F32)), 1.0, 0.0)
            o_ref[...] += jnp.dot(hit.astype(BF16), xbuf[...], preferred_element_type=F32)
            return carry

        n_extra = jnp.maximum(s1 - start - 1, 0) // _CWIN
        lax.fori_loop(0, n_extra, extra, 0)
    if final:
        x = o_ref[...]
        o_ref[...] = x * _rms_scale(x) * g_ref[...]


def _combine(x2d, spos, cb, y, g_final, final, seq):
    t = x2d.shape[0]
    batch, ne, cap, _ = y.shape
    nblk = seq // LANES
    cb_full = jnp.concatenate([cb, jnp.full((batch, ne, 1), cap, I32)], axis=-1).reshape(-1)
    spb = _CTM // LANES
    return pl.pallas_call(
        functools.partial(_combine_kernel, final=final, tiles_per_seq=seq // _CTM, nblk=nblk, cap=cap),
        grid_spec=pltpu.PrefetchScalarGridSpec(
            num_scalar_prefetch=1,
            grid=(t // _CTM,),
            in_specs=[
                pl.BlockSpec((_CTM, D_MODEL), lambda i, c: (i, 0)),
                pl.BlockSpec((spb, N_EXPERTS, LANES), lambda i, c: (i, 0, 0)),
                pl.BlockSpec(memory_space=pl.ANY),
                pl.BlockSpec((1, D_MODEL), lambda i, c: (0, 0)),
            ],
            out_specs=pl.BlockSpec((_CTM, D_MODEL), lambda i, c: (i, 0)),
            scratch_shapes=[pltpu.VMEM((2, N_EXPERTS * _CWIN, D_MODEL), BF16), pltpu.VMEM((_CWIN, D_MODEL), BF16),
                            pltpu.SemaphoreType.DMA((2,)), pltpu.SemaphoreType.DMA],
        ),
        out_shape=jax.ShapeDtypeStruct((t, D_MODEL), F32),
        compiler_params=_cparams(("arbitrary",)),
        name="combine",
    )(cb_full, x2d, spos, y, g_final.reshape(1, D_MODEL))


_SLAB_Q0 = 512
_SLAB_K0 = _SLAB_Q0 + D_HEADS * LANES
_SLAB_V0 = _SLAB_K0 + D_KV_HEADS * LANES
_ODD_COLS = _SLAB_V0 + D_KV_HEADS * LANES


def _proj_odd_kernel(x_ref, g_ref, wm_ref, cqn_ref, wq_ref, ckvn_ref, wkv_ref, dqn_ref, dkn_ref,
                     ccq_ref, s1cq_ref, s2cq_ref, cck_ref, s1ck_ref, s2ck_ref,
                     cdq_ref, s1dq_ref, s2dq_ref, cdk_ref, s1dk_ref, s2dk_ref,
                     qc_ref, kc_ref, vc_ref, qd_ref, kd_ref, vd_ref, stat_ref, *, steps_per_seq):
    lane = lax.broadcasted_iota(I32, (1, LANES), 1)
    one64 = jnp.where(lane == HEAD_DIM, 1.0, 0.0)
    last_lane = lane == LANES - 1
    half_rope = C_ROPE // 2
    stats = [jnp.zeros((1, LANES), F32), jnp.zeros((1, LANES), F32)]
    ones_mat = jnp.ones((LANES, LANES), BF16)

    def with_norm(val, fill, row, col):
        n2 = jnp.dot((val * val).astype(BF16), ones_mat, preferred_element_type=F32)
        stats[row] = jnp.where(lane == col, jnp.maximum(jnp.max(n2, axis=0, keepdims=True), stats[row]), stats[row])
        return jnp.where(last_lane, fill, val).astype(BF16)

    def head_norm(xg, gn_ref):
        ss = jnp.sum(xg * xg, axis=-1, keepdims=True) * (1.0 / HEAD_DIM)
        return xg * lax.rsqrt(ss + EPS) * gn_ref[...]

    n_chunks = 2
    rows_per = x_ref.shape[0] // n_chunks
    for c in range(n_chunks):
        rows = slice(c * rows_per, (c + 1) * rows_per)
        tab = lambda *refs: [r[rows, :] for r in refs]
        x = x_ref[rows, :]
        y = (x * _rms_scale(x) * g_ref[...]).astype(BF16)
        pm = jnp.dot(y, wm_ref[...], preferred_element_type=F32)
        cq = pm[:, :C_Q_RANK]
        cqn = (cq * _rms_scale(cq) * cqn_ref[...]).astype(BF16)
        qc = jnp.dot(cqn, wq_ref[...], preferred_element_type=F32)
        ckv = pm[:, C_Q_RANK:C_Q_RANK + C_KV_RANK]
        ckvn = (ckv * _rms_scale(ckv) * ckvn_ref[...]).astype(BF16)
        kv = jnp.dot(ckvn, wkv_ref[...], preferred_element_type=F32)
        kr = _rope3(pm[:, C_Q_RANK + C_KV_RANK:_SLAB_Q0], *tab(cck_ref, s1ck_ref, s2ck_ref), half_rope)
        t_cq = tab(ccq_ref, s1cq_ref, s2cq_ref)
        for h in range(C_HEADS):
            sl = slice(h * LANES, (h + 1) * LANES)
            qc_ref[rows, sl] = with_norm(_rope3(qc[:, sl], *t_cq, half_rope), 1.0, 1, h)
            kc_ref[rows, sl] = with_norm(kv[:, sl] + kr, -1.0, 0, h)
            vc_ref[rows, sl] = (kv[:, C_HEADS * LANES + h * LANES:C_HEADS * LANES + (h + 1) * LANES]
                                + one64).astype(BF16)
        t_dq = tab(cdq_ref, s1dq_ref, s2dq_ref)
        t_dk = tab(cdk_ref, s1dk_ref, s2dk_ref)
        for g in range(D_HEADS):
            xg = pm[:, _SLAB_Q0 + g * LANES:_SLAB_Q0 + (g + 1) * LANES]
            qd_ref[rows, g * LANES:(g + 1) * LANES] = with_norm(
                _rope3(head_norm(xg, dqn_ref), *t_dq, HEAD_DIM // 4), 1.0, 1, C_HEADS + g)
        for g in range(D_KV_HEADS):
            sl = slice(g * LANES, (g + 1) * LANES)
            xg = pm[:, _SLAB_K0 + g * LANES:_SLAB_K0 + (g + 1) * LANES]
            kd_ref[rows, sl] = with_norm(_rope3(head_norm(xg, dkn_ref), *t_dk, HEAD_DIM // 4), -1.0, 0, C_HEADS + g)
            vd_ref[rows, sl] = (pm[:, _SLAB_V0 + g * LANES:_SLAB_V0 + (g + 1) * LANES] + one64).astype(BF16)

    new = jnp.concatenate(stats + [jnp.zeros((6, LANES), F32)], axis=0)

    @pl.when(pl.program_id(0) % steps_per_seq == 0)
    def _():
        stat_ref[...] = new

    @pl.when(pl.program_id(0) % steps_per_seq != 0)
    def _():
        stat_ref[...] = jnp.maximum(stat_ref[...], new)


def _slabs(w, n_heads, width, lane_off=0):
    k = w.shape[0]
    w3 = w.reshape(k, n_heads, width)
    w3 = jnp.pad(w3, ((0, 0), (0, 0), (lane_off, LANES - width - lane_off)))
    return w3.reshape(k, n_heads * LANES)


def _axial_tables(row, col, scale):
    half = HEAD_DIM // 2
    cr, s1r, s2r = _rope_tables(row, D_THETA, half, 0, LANES, scale)
    cc, s1c, s2c = _rope_tables(col, D_THETA, half, half, LANES, scale)
    lane = jnp.arange(LANES)[None, :]
    return jnp.where(lane < half, cr, cc), s1r + s1c, s2r + s2c


def _proj_odd(x2d, seq, g_mix, w_in, cq_norm, w_cq_up, ckv_norm, w_ckv_up, dq_norm, dk_norm, tm=512):
    t = x2d.shape[0]
    nblk = seq // tm
    o1 = C_Q_RANK
    o2 = o1 + C_KV_RANK
    o3 = o2 + C_ROPE
    o4 = o3 + D_HEADS * HEAD_DIM
    o5 = o4 + D_KV_HEADS * HEAD_DIM
    wm = jnp.concatenate([
        w_in[:, :o2],
        _slabs(w_in[:, o2:o3], 1, C_ROPE, C_NOPE),
        _slabs(w_in[:, o3:o4], D_HEADS, HEAD_DIM),
        _slabs(w_in[:, o4:o5], D_KV_HEADS, HEAD_DIM),
        _slabs(w_in[:, o5:], D_KV_HEADS, HEAD_DIM),
    ], axis=1).astype(BF16)
    assert wm.shape[1] == _ODD_COLS
    wq = _slabs(w_cq_up, C_HEADS, C_NOPE + C_ROPE).astype(BF16)
    kv3 = w_ckv_up.reshape(C_KV_RANK, C_HEADS, 2 * HEAD_DIM)
    wkv = jnp.concatenate([
        _slabs(kv3[:, :, :C_NOPE].reshape(C_KV_RANK, -1), C_HEADS, C_NOPE),
        _slabs(kv3[:, :, C_NOPE:].reshape(C_KV_RANK, -1), C_HEADS, HEAD_DIM),
    ], axis=1).astype(BF16)
    pad64 = lambda g: jnp.pad(g, (0, LANES - HEAD_DIM)).reshape(1, LANES)

    pos = jnp.arange(seq, dtype=I32)
    row_pos = pos // GRID_W
    col_pos = pos % GRID_W
    c_scale = (C_NOPE + C_ROPE) ** -0.5 * LOG2E
    d_scale = HEAD_DIM ** -0.5 * LOG2E
    tabs = (_rope_tables(pos, ROPE_THETA, C_ROPE, C_NOPE, LANES, c_scale)
            + _rope_tables(pos, ROPE_THETA, C_ROPE, C_NOPE, LANES, 1.0)
            + _axial_tables(row_pos, col_pos, d_scale)
            + _axial_tables(row_pos, col_pos, 1.0))

    row = lambda i: (i, 0)
    full = lambda i: (0, 0)
    tspec = pl.BlockSpec((tm, LANES), lambda i: (i % nblk, 0))
    wide = C_HEADS * LANES
    kvw = D_KV_HEADS * LANES
    return pl.pallas_call(
        functools.partial(_proj_odd_kernel, steps_per_seq=nblk),
        grid=(t // tm,),
        in_specs=[
            pl.BlockSpec((tm, D_MODEL), row),
            pl.BlockSpec((1, D_MODEL), full),
            pl.BlockSpec(wm.shape, full),
            pl.BlockSpec((1, C_Q_RANK), full),
            pl.BlockSpec(wq.shape, full),
            pl.BlockSpec((1, C_KV_RANK), full),
            pl.BlockSpec(wkv.shape, full),
            pl.BlockSpec((1, LANES), full),
            pl.BlockSpec((1, LANES), full),
        ] + [tspec] * 12,
        out_specs=[
            pl.BlockSpec((tm, wide), row), pl.BlockSpec((tm, wide), row), pl.BlockSpec((tm, wide), row),
            pl.BlockSpec((tm, wide), row), pl.BlockSpec((tm, kvw), row), pl.BlockSpec((tm, kvw), row),
            pl.BlockSpec((None, 8, LANES), lambda i: (i // nblk, 0, 0)),
        ],
        out_shape=[
            jax.ShapeDtypeStruct((t, wide), BF16), jax.ShapeDtypeStruct((t, wide), BF16),
            jax.ShapeDtypeStruct((t, wide), BF16), jax.ShapeDtypeStruct((t, wide), BF16),
            jax.ShapeDtypeStruct((t, kvw), BF16), jax.ShapeDtypeStruct((t, kvw), BF16),
            jax.ShapeDtypeStruct((t // seq, 8, LANES), F32),
        ],
        compiler_params=_cparams(("arbitrary",)),
        name="proj_odd",
    )(x2d, g_mix.reshape(1, D_MODEL), wm, cq_norm.reshape(1, -1), wq, ckv_norm.reshape(1, -1), wkv,
      pad64(dq_norm), pad64(dk_norm), *tabs)


def _flash_kernel(q_ref, k_ref, v_ref, o_ref, qs_sc, m_sc, acc_sc, *, group, tq, tk):
    ki = pl.program_id(3)

    @pl.when(ki == 0)
    def _():
        for g in range(group):
            qs_sc[g * tq:(g + 1) * tq, :] = q_ref[:, g * LANES:(g + 1) * LANES]
        m_sc[...] = jnp.full(m_sc.shape, NEG, F32)
        acc_sc[...] = jnp.zeros(acc_sc.shape, F32)

    s = lax.dot_general(qs_sc[...], k_ref[...], _NT, preferred_element_type=F32)
    m_prev = m_sc[...]
    m_new = jnp.maximum(m_prev, jnp.max(s, axis=1, keepdims=True))
    alpha = jnp.exp2(m_prev - m_new)
    p = jnp.exp2(s - jnp.tile(m_new, (1, tk // LANES)))
    acc_sc[...] = alpha * acc_sc[...] + jnp.dot(p.astype(BF16), v_ref[...], preferred_element_type=F32)
    m_sc[...] = m_new

    @pl.when(ki == pl.num_programs(3) - 1)
    def _():
        acc = acc_sc[...]
        o = acc / acc[:, HEAD_DIM:HEAD_DIM + 1]
        for g in range(group):
            o_ref[:, g * LANES:(g + 1) * LANES] = o[g * tq:(g + 1) * tq].astype(BF16)


def _flash_bounded_kernel(bound_ref, q_ref, k_ref, v_ref, o_ref, qs_sc, acc_sc, *, group, tq):
    ki = pl.program_id(3)
    head = pl.program_id(0) * pl.num_programs(1) + pl.program_id(1)

    @pl.when(ki == 0)
    def _():
        fix = jnp.where(lax.broadcasted_iota(I32, (1, LANES), 1) == LANES - 1, bound_ref[head], 1.0)
        for g in range(group):
            qs_sc[g * tq:(g + 1) * tq, :] = (q_ref[:, g * LANES:(g + 1) * LANES].astype(F32) * fix).astype(BF16)
        acc_sc[...] = jnp.zeros(acc_sc.shape, F32)

    s_t = lax.dot_general(k_ref[...], qs_sc[...], _NT, preferred_element_type=F32)
    p_t = jnp.exp2(s_t).astype(BF16)
    acc_sc[...] += lax.dot_general(v_ref[...], p_t, (((0,), (0,)), ((), ())), preferred_element_type=F32)

    @pl.when(ki == pl.num_programs(3) - 1)
    def _():
        acc = acc_sc[...]
        o = (acc / acc[HEAD_DIM:HEAD_DIM + 1, :]).T
        for g in range(group):
            o_ref[:, g * LANES:(g + 1) * LANES] = o[g * tq:(g + 1) * tq].astype(BF16)


def _flash(q, k, v, group, nk, bounded, rows=1024, tk=512, tk_bounded=2048):
    b, s, qw = q.shape
    hk = k.shape[2] // LANES
    tq = rows // group
    tk = min(tk, s)
    tkb = min(tk_bounded, s)
    out_shape = jax.ShapeDtypeStruct((b, s, qw), BF16)
    sem = ("parallel", "parallel", "parallel", "arbitrary")

    def running_max(q, k, v, nk):
        qspec = pl.BlockSpec((None, tq, group * LANES), lambda bi, hi, qi, ki: (bi, qi, hi))
        kspec = pl.BlockSpec((None, tk, LANES), lambda bi, hi, qi, ki: (bi, ki, hi))
        return pl.pallas_call(
            functools.partial(_flash_kernel, group=group, tq=tq, tk=tk),
            grid=(b, hk, s // tq, s // tk),
            in_specs=[qspec, kspec, kspec],
            out_specs=qspec,
            out_shape=out_shape,
            scratch_shapes=[pltpu.VMEM((rows, LANES), BF16), pltpu.VMEM((rows, LANES), F32),
                            pltpu.VMEM((rows, LANES), F32)],
            compiler_params=_cparams(sem),
            name="flash",
        )(q, k, v)

    def bound(q, k, v, nk):
        qspec = pl.BlockSpec((None, tq, group * LANES), lambda bi, hi, qi, ki, nkr: (bi, qi, hi))
        kspec = pl.BlockSpec((None, tkb, LANES), lambda bi, hi, qi, ki, nkr: (bi, ki, hi))
        return pl.pallas_call(
            functools.partial(_flash_bounded_kernel, group=group, tq=tq),
            grid_spec=pltpu.PrefetchScalarGridSpec(
                num_scalar_prefetch=1,
                grid=(b, hk, s // tq, s // tkb),
                in_specs=[qspec, kspec, kspec],
                out_specs=qspec,
                scratch_shapes=[pltpu.VMEM((rows, LANES), BF16), pltpu.VMEM((LANES, rows), F32)],
            ),
            out_shape=out_shape,
            compiler_params=_cparams(sem),
            name="flash_bounded",
        )(nk.reshape(-1), q, k, v)

    return lax.cond(bounded, bound, running_max, q, k, v, nk)


def _moe(x1, h2, aff, batch, seq, w_gate, w_up, w_down, layer, g_final, final):
    idx, gates, spos, cb = _route(aff.reshape(-1, LANES), batch, seq)
    y = _ffn(idx, gates, h2, w_gate, w_up, w_down, layer, seq)
    return _combine(x1, spos.reshape(-1, N_EXPERTS, LANES), cb, y, g_final, final, seq)


def kernel(x, norm_mix, norm_ffn, even_w_in, even_gmlp_norm, even_w_spatial, even_b_spatial, even_w_out,
           odd_w_in, odd_cq_norm, odd_w_cq_up, odd_ckv_norm, odd_w_ckv_up, odd_dq_norm, odd_dk_norm, odd_w_out,
           moe_w_router, moe_w_gate, moe_w_up, moe_w_down, final_norm):
    b, s, d = x.shape
    depth = norm_mix.shape[0]
    x2d = x.reshape(b * s, d)
    for i in range(depth):
        j = i // 2
        last = i == depth - 1
        if i % 2 == 0:
            q, k, v, go = _proj_even(x2d, s, norm_mix[i], even_w_in[j], even_gmlp_norm[j], even_w_spatial[j],
                                     even_b_spatial[j])
            a = _dilated(q.reshape(b, s, A_WIDTH), k.reshape(b, s, A_WIDTH), v.reshape(b, s, A_WIDTH))
            x1, h2, aff = _outproj(x2d, a.reshape(b * s, A_WIDTH), go, even_w_out[j][:A_WIDTH],
                                   even_w_out[j][A_WIDTH:], norm_ffn[i], moe_w_router[i])
        else:
            qc, kc, vc, qd, kd, vd, stat = _proj_odd(x2d, s, norm_mix[i], odd_w_in[j], odd_cq_norm[j],
                                                     odd_w_cq_up[j], odd_ckv_norm[j], odd_w_ckv_up[j],
                                                     odd_dq_norm[j], odd_dk_norm[j])
            grp = D_HEADS // D_KV_HEADS
            k2_c, k2_d = stat[:, 0, :C_HEADS], stat[:, 0, C_HEADS:C_HEADS + D_KV_HEADS]
            q2_c = stat[:, 1, :C_HEADS]
            q2_d = jnp.max(stat[:, 1, C_HEADS:C_HEADS + D_HEADS].reshape(b, D_KV_HEADS, grp), axis=-1)
            bound_c = jnp.sqrt(q2_c * k2_c) * _NORM_MARGIN ** 2
            bound_d = jnp.sqrt(q2_d * k2_d) * _NORM_MARGIN ** 2
            bounded = jnp.maximum(jnp.max(bound_c), jnp.max(bound_d)) <= _MAX_SCORE_BOUND
            r3 = lambda z: z.reshape(b, s, -1)
            oc = _flash(r3(qc), r3(kc), r3(vc), 1, bound_c, bounded)
            od = _flash(r3(qd), r3(kd), r3(vd), grp, bound_d, bounded)
            cw = C_HEADS * HEAD_DIM
            x1, h2, aff = _outproj(x2d, oc.reshape(b * s, -1), od.reshape(b * s, -1),
                                   _slabs(odd_w_out[j][:cw].T, C_HEADS, HEAD_DIM).T,
                                   _slabs(odd_w_out[j][cw:].T, D_HEADS, HEAD_DIM).T,
                                   norm_ffn[i], moe_w_router[i])
        x2d = _moe(x1, h2, aff, b, s, moe_w_gate, moe_w_up, moe_w_down, i, final_norm, last)
    return x2d.reshape(b, s, d)
```

```python
import functools
import math

import jax
import jax.numpy as jnp
from jax import lax
from jax.experimental import pallas as pl
from jax.experimental.pallas import tpu as pltpu

F32 = jnp.float32
BF16 = jnp.bfloat16
I32 = jnp.int32

EPS = 1e-6
NEG = -1e30
LOG2E = 1.4426950408889634

D_MODEL = 1024
HEAD_DIM = 64
ROPE_THETA = 500000.0
ROT_DIM = 16
GRID_W = 64
A_HEADS = 12
A_WIDTH = 768
A_DILATIONS = (1, 4, 16)
A_RADIUS = 64
B_WIDTH = 256
B_GROUPS = 4
B_CHUNK = 128
C_HEADS = 8
C_Q_RANK = 256
C_KV_RANK = 128
C_NOPE = 64
C_ROPE = 32
D_HEADS = 8
D_KV_HEADS = 2
D_THETA = 10000.0
N_EXPERTS = 16
EC_FACTOR = 2
EXPERT_FF = 512

_NORM_MARGIN = 1.01
_MAX_SCORE_BOUND = 55.0

LANES = 128
VMEM_LIMIT = 48 * 1024 * 1024
_DILATED_VMEM_LIMIT = 56 * 1024 * 1024

_NT = (((1,), (1,)), ((), ()))


def _cparams(sem):
    return pltpu.CompilerParams(dimension_semantics=sem, vmem_limit_bytes=VMEM_LIMIT)


def _rms_scale(x):
    return lax.rsqrt(jnp.mean(x * x, axis=-1, keepdims=True) + EPS)


def _rope3(a, c, s1, s2, shift):
    return a * c + pltpu.roll(a, LANES - shift, 1) * s1 + pltpu.roll(a, shift, 1) * s2


def _split_dot(x, w_bf16):
    hi = x.astype(BF16)
    lo = (x - hi.astype(F32)).astype(BF16)
    return (jnp.dot(hi, w_bf16, preferred_element_type=F32)
            + jnp.dot(lo, w_bf16, preferred_element_type=F32))


def _proj_even_kernel(x_ref, g_ref, w_ref, cq_ref, s1q_ref, s2q_ref, ck_ref, s1k_ref, s2k_ref,
                      gn_ref, gmat_ref, ws_ref, bs_ref,
                      q_ref, k_ref, v_ref, go_ref):
    x = x_ref[...]
    y = (x * _rms_scale(x) * g_ref[...]).astype(BF16)
    tm = x.shape[0]

    aq = jnp.dot(y, w_ref[:, 0:A_WIDTH], preferred_element_type=F32)
    tq = (cq_ref[...], s1q_ref[...], s2q_ref[...])
    for j in range(A_WIDTH // LANES):
        sl = slice(j * LANES, (j + 1) * LANES)
        q_ref[:, sl] = _rope3(aq[:, sl], *tq, ROT_DIM // 2)
    ak = jnp.dot(y, w_ref[:, A_WIDTH:2 * A_WIDTH], preferred_element_type=F32)
    tk = (ck_ref[...], s1k_ref[...], s2k_ref[...])
    for j in range(A_WIDTH // LANES):
        sl = slice(j * LANES, (j + 1) * LANES)
        k_ref[:, sl] = _rope3(ak[:, sl], *tk, ROT_DIM // 2)
    v_ref[...] = jnp.dot(y, w_ref[:, 2 * A_WIDTH:3 * A_WIDTH], preferred_element_type=F32)

    z = jnp.dot(y, w_ref[:, 3 * A_WIDTH:3 * A_WIDTH + 2 * B_WIDTH], preferred_element_type=F32)
    ge = jax.nn.gelu(z)
    u = ge[:, :B_WIDTH]
    vv = ge[:, B_WIDTH:]
    ss = _split_dot(vv * vv, gmat_ref[...])
    vn = (vv * lax.rsqrt(ss + EPS) * gn_ref[...]).astype(BF16)
    grp = lax.broadcasted_iota(I32, (B_CHUNK, B_WIDTH), 1) // (B_WIDTH // B_GROUPS)
    for c in range(tm // B_CHUNK):
        rows = slice(c * B_CHUNK, (c + 1) * B_CHUNK)
        vc = vn[rows]
        mg = [jnp.dot(ws_ref[g], vc, preferred_element_type=F32) for g in range(B_GROUPS)]
        mixed = jnp.where(grp == 0, mg[0], jnp.where(grp == 1, mg[1], jnp.where(grp == 2, mg[2], mg[3])))
        go_ref[rows, :] = (u[rows] * (mixed + bs_ref[...])).astype(BF16)


def _rope_tables(pos, theta, r, lane_off, period, scale):
    half = r // 2
    inv = jnp.power(jnp.float32(theta), -jnp.arange(half, dtype=F32) * (2.0 / r))
    ang = pos.astype(F32)[:, None] * inv[None, :]
    cos, sin = jnp.cos(ang), jnp.sin(ang)
    o = (jnp.arange(LANES) % period) - lane_off
    in_lo = (o >= 0) & (o < half)
    in_hi = (o >= half) & (o < r)
    idx = jnp.clip(jnp.where(in_hi, o - half, o), 0, half - 1)
    c = jnp.where((in_lo | in_hi)[None, :], cos[:, idx], 1.0)
    s1 = jnp.where(in_lo[None, :], -sin[:, idx], 0.0)
    s2 = jnp.where(in_hi[None, :], sin[:, idx], 0.0)
    return c * scale, s1 * scale, s2 * scale


def _proj_even(x2d, seq, g_mix, w_in, gmlp_norm, w_s, b_s, tm=512):
    t = x2d.shape[0]
    nblk = seq // tm
    pos = jnp.arange(seq, dtype=I32)
    qscale = HEAD_DIM ** -0.5 * LOG2E
    cq, s1q, s2q = _rope_tables(pos, ROPE_THETA, ROT_DIM, 0, HEAD_DIM, qscale)
    ck, s1k, s2k = _rope_tables(pos, ROPE_THETA, ROT_DIM, 0, HEAD_DIM, 1.0)
    gdim = B_WIDTH // B_GROUPS
    gid = jnp.arange(B_WIDTH) // gdim
    gmat = jnp.where(gid[:, None] == gid[None, :], 1.0 / gdim, 0.0).astype(BF16)
    bias = jnp.repeat(b_s.T, gdim, axis=1)
    row = lambda i: (i, 0)
    tab = lambda i: (i % nblk, 0)
    full = lambda i: (0, 0)
    tspec = pl.BlockSpec((tm, LANES), tab)
    return pl.pallas_call(
        _proj_even_kernel,
        grid=(t // tm,),
        in_specs=[
            pl.BlockSpec((tm, D_MODEL), row),
            pl.BlockSpec((1, D_MODEL), full),
            pl.BlockSpec(w_in.shape, full),
            tspec, tspec, tspec, tspec, tspec, tspec,
            pl.BlockSpec((1, B_WIDTH), full),
            pl.BlockSpec((B_WIDTH, B_WIDTH), full),
            pl.BlockSpec((B_GROUPS, B_CHUNK, B_CHUNK), lambda i: (0, 0, 0)),
            pl.BlockSpec((B_CHUNK, B_WIDTH), full),
        ],
        out_specs=[
            pl.BlockSpec((tm, A_WIDTH), row),
            pl.BlockSpec((tm, A_WIDTH), row),
            pl.BlockSpec((tm, A_WIDTH), row),
            pl.BlockSpec((tm, B_WIDTH), row),
        ],
        out_shape=[
            jax.ShapeDtypeStruct((t, A_WIDTH), F32),
            jax.ShapeDtypeStruct((t, A_WIDTH), F32),
            jax.ShapeDtypeStruct((t, A_WIDTH), F32),
            jax.ShapeDtypeStruct((t, B_WIDTH), BF16),
        ],
        compiler_params=_cparams(("parallel",)),
        name="proj_even",
    )(x2d, g_mix.reshape(1, D_MODEL), w_in.astype(BF16), cq, s1q, s2q, ck, s1k, s2k,
      gmlp_norm.reshape(1, B_WIDTH), gmat, w_s.astype(BF16), bias)


_TQ = 128
_TK = _TQ + 2 * A_RADIUS
_NORM_ROWS = 512


def _dilated_kernel(q_ref, k_ref, v_ref, o_ref, m_sc, l_sc, bias_sc, *, seq):
    half0 = lax.broadcasted_iota(I32, (1, LANES), 1) < HEAD_DIM
    row_head = lax.broadcasted_iota(I32, (LANES, LANES), 0) // HEAD_DIM
    pick = [jnp.where(row_head == h, 1.0, 0.0).astype(BF16) for h in range(2)]
    n_pat = len(A_DILATIONS)
    zero_row = jnp.zeros((1, LANES), F32)

    def max_head_sq(ref):
        def body(c, mx):
            x = ref[pl.ds(pl.multiple_of(c * _NORM_ROWS, _NORM_ROWS), _NORM_ROWS), :]
            sq = (x * x).astype(BF16)
            return tuple(jnp.maximum(mx[h], jnp.max(jnp.dot(sq, pick[h], preferred_element_type=F32),
                                                    axis=0, keepdims=True)) for h in range(2))
        return lax.fori_loop(0, seq // _NORM_ROWS, body, (zero_row, zero_row))

    max_q2 = max_head_sq(q_ref)
    max_k2 = max_head_sq(k_ref)
    bound = [jnp.sqrt(max_q2[h] * max_k2[h]) * (_NORM_MARGIN * _NORM_MARGIN) for h in range(2)]
    worst = jnp.max(jnp.maximum(bound[0], bound[1]))

    diff = lax.broadcasted_iota(I32, (_TQ, _TK), 1) - lax.broadcasted_iota(I32, (_TQ, _TK), 0)
    for case in range(3):
        band = jnp.where(jnp.abs(diff - case * A_RADIUS) <= A_RADIUS, 0.0, NEG)
        bias_sc[6 + case] = band
        for h in range(2):
            bias_sc[3 * h + case] = band - jnp.tile(bound[h], (1, _TK // LANES))
    one_bf16 = jnp.ones((), BF16)

    def run(bounded):
        for pi, d in enumerate(A_DILATIONS):
            cls_len = seq // d
            tpc = cls_len // _TQ

            def tile(qrows, kb, vb, case, pi=pi):
                q = q_ref[qrows, :]
                parts = []
                for h in range(2):
                    qh = jnp.where(half0 if h == 0 else jnp.logical_not(half0), q, 0.0).astype(BF16)
                    s = lax.dot_general(qh, kb, _NT, preferred_element_type=F32)
                    if bounded:
                        mt = None
                        p = jnp.exp2(s + bias_sc[3 * h + case]).astype(BF16)
                    else:
                        s = s + bias_sc[6 + case]
                        mt = jnp.max(s, axis=-1, keepdims=True)
                        p = jnp.exp2(s - mt).astype(BF16)
                    vh = jnp.where(half0 if h == 0 else jnp.logical_not(half0), vb, one_bf16)
                    parts.append((mt, jnp.dot(p, vh, preferred_element_type=F32)))
                ot = jnp.where(half0, parts[0][1], parts[1][1])
                lt = pltpu.roll(jnp.where(half0, parts[1][1], parts[0][1]), HEAD_DIM, 1)
                if bounded:
                    if pi > 0:
                        lt = l_sc[qrows, :] + lt
                        ot = o_ref[qrows, :] + ot
                else:
                    mt = jnp.where(half0, parts[0][0], parts[1][0])
                    if pi > 0:
                        mp = m_sc[qrows, :]
                        mn = jnp.maximum(mp, mt)
                        a = jnp.exp2(mp - mn)
                        b = jnp.exp2(mt - mn)
                        lt = a * l_sc[qrows, :] + b * lt
                        ot = a * o_ref[qrows, :] + b * ot
                        mt = mn
                    if pi < n_pat - 1:
                        m_sc[qrows, :] = mt
                if pi == n_pat - 1:
                    o_ref[qrows, :] = ot / lt
                else:
                    l_sc[qrows, :] = lt
                    o_ref[qrows, :] = ot

            def window(l0, cls_len=cls_len):
                if isinstance(l0, int):
                    kst = min(max(l0 - A_RADIUS, 0), cls_len - _TK)
                else:
                    kst = jnp.clip(l0 - A_RADIUS, 0, cls_len - _TK)
                return kst, (l0 - kst) // A_RADIUS

            if d < 8:

                def body(j, carry, d=d, tpc=tpc):
                    i = j // tpc
                    l0 = (j % tpc) * _TQ
                    kst, case = window(l0)
                    if d == 1:
                        qrows = pl.ds(pl.multiple_of(l0, _TQ), _TQ)
                        krows = pl.ds(pl.multiple_of(kst, A_RADIUS), _TK)
                    else:
                        qrows = pl.ds(l0 * d + i, _TQ, stride=d)
                        krows = pl.ds(kst * d + i, _TK, stride=d)
                    tile(qrows, k_ref[krows, :].astype(BF16), v_ref[krows, :].astype(BF16), case)
                    return carry

                lax.fori_loop(0, seq // _TQ, body, 0, unroll=8)
            else:

                def body(i, carry, d=d, cls_len=cls_len, tpc=tpc):
                    cls = pl.ds(i, cls_len, stride=d)
                    kc = k_ref[cls, :].astype(BF16)
                    vc = v_ref[cls, :].astype(BF16)
                    for n in range(tpc):
                        kst, case = window(n * _TQ)
                        tile(pl.ds(n * _TQ * d + i, _TQ, stride=d), kc[kst:kst + _TK], vc[kst:kst + _TK], case)
                    return carry

                lax.fori_loop(0, d, body, 0, unroll=2)

    @pl.when(worst <= _MAX_SCORE_BOUND)
    def _():
        run(True)

    @pl.when(jnp.logical_not(worst <= _MAX_SCORE_BOUND))
    def _():
        run(False)


def _dilated(q, k, v):
    b, s, w = q.shape
    spec = pl.BlockSpec((None, s, LANES), lambda bi, hi: (bi, 0, hi))
    return pl.pallas_call(
        functools.partial(_dilated_kernel, seq=s),
        grid=(b, w // LANES),
        in_specs=[spec, spec, spec],
        out_specs=spec,
        out_shape=jax.ShapeDtypeStruct((b, s, w), F32),
        scratch_shapes=[pltpu.VMEM((s, LANES), F32), pltpu.VMEM((s, LANES), F32),
                        pltpu.VMEM((9, _TQ, _TK), F32)],
        compiler_params=pltpu.CompilerParams(dimension_semantics=("parallel", "parallel"),
                                             vmem_limit_bytes=_DILATED_VMEM_LIMIT),
        name="dilated",
    )(q, k, v)


def _outproj_kernel(x_ref, a_ref, b_ref, wa_ref, wb_ref, gf_ref, wr_ref, x1_ref, h2_ref, aff_ref):
    x1 = (x_ref[...]
          + jnp.dot(a_ref[...].astype(BF16), wa_ref[...], preferred_element_type=F32)
          + jnp.dot(b_ref[...].astype(BF16), wb_ref[...], preferred_element_type=F32))
    x1_ref[...] = x1
    h2 = x1 * _rms_scale(x1) * gf_ref[...]
    n_tiles = D_MODEL // LANES
    for j in range(n_tiles):
        h2_ref[pl.ds(j, h2.shape[0], stride=n_tiles), :] = h2[:, j * LANES:(j + 1) * LANES]
    hi = h2.astype(BF16)
    lo = (h2 - hi.astype(F32)).astype(BF16)
    both = jnp.dot(hi, wr_ref[...], preferred_element_type=F32)
    lg = (both[:, :LANES] + both[:, LANES:]
          + jnp.dot(lo, wr_ref[:, :LANES], preferred_element_type=F32))
    valid = lax.broadcasted_iota(I32, lg.shape, 1) < N_EXPERTS
    lg = jnp.where(valid, lg, NEG)
    e = jnp.exp(lg - jnp.max(lg, axis=-1, keepdims=True))
    aff = e / jnp.sum(e, axis=-1, keepdims=True)
    aff_t = aff.T
    for j in range(aff.shape[0] // LANES):
        aff_ref[j] = aff_t[:N_EXPERTS, j * LANES:(j + 1) * LANES]


def _outproj(x2d, a, b, wa, wb, g_ffn, w_router, tm=512):
    t = x2d.shape[0]
    wr = jnp.pad(w_router, ((0, 0), (0, LANES - N_EXPERTS)))
    wr_hi = wr.astype(BF16)
    wr_lo = (wr - wr_hi.astype(F32)).astype(BF16)
    wr2 = jnp.concatenate([wr_hi, wr_lo], axis=1)
    row = lambda i: (i, 0)
    full = lambda i: (0, 0)
    return pl.pallas_call(
        _outproj_kernel,
        grid=(t // tm,),
        in_specs=[
            pl.BlockSpec((tm, D_MODEL), row),
            pl.BlockSpec((tm, a.shape[1]), row),
            pl.BlockSpec((tm, b.shape[1]), row),
            pl.BlockSpec(wa.shape, full),
            pl.BlockSpec(wb.shape, full),
            pl.BlockSpec((1, D_MODEL), full),
            pl.BlockSpec((D_MODEL, 2 * LANES), full),
        ],
        out_specs=[
            pl.BlockSpec((tm, D_MODEL), row),
            pl.BlockSpec((tm * (D_MODEL // LANES), LANES), row),
            pl.BlockSpec((tm // LANES, N_EXPERTS, LANES), lambda i: (i, 0, 0)),
        ],
        out_shape=[
            jax.ShapeDtypeStruct((t, D_MODEL), F32),
            jax.ShapeDtypeStruct((t * (D_MODEL // LANES), LANES), F32),
            jax.ShapeDtypeStruct((t // LANES, N_EXPERTS, LANES), F32),
        ],
        compiler_params=_cparams(("parallel",)),
        name="outproj",
    )(x2d, a, b, wa.astype(BF16), wb.astype(BF16), g_ffn.reshape(1, D_MODEL), wr2)


def _route_kernel(aff_ref, idx_ref, gate_ref, spos_ref, cb_ref, thr_sc, need_sc, *, cap):
    nblk = aff_ref.shape[0] // N_EXPERTS
    bits = pltpu.bitcast(aff_ref[...], I32).reshape(nblk, N_EXPERTS, LANES)

    def count(pred):
        return jnp.sum(jnp.sum(jnp.where(pred, 1.0, 0.0), axis=0), axis=1, keepdims=True)

    def search(it, thr):
        cand = thr | jnp.left_shift(jnp.int32(1), 30 - it)
        return jnp.where(count(bits >= cand[None]) >= cap, cand, thr)

    thr = lax.fori_loop(0, 31, search, jnp.zeros((N_EXPERTS, 1), I32))
    need = cap - count(bits > thr[None])
    thr_sc[...] = jnp.broadcast_to(thr, (N_EXPERTS, LANES))
    need_sc[...] = jnp.broadcast_to(need, (N_EXPERTS, LANES))

    ri = lax.broadcasted_iota(I32, (LANES, LANES), 0)
    ci = lax.broadcasted_iota(I32, (LANES, LANES), 1)
    upper = jnp.where(ri <= ci, 1.0, 0.0).astype(BF16)
    lower = jnp.where(ci <= ri, 1.0, 0.0).astype(BF16)
    eye = jnp.where(ri == ci, 1.0, 0.0).astype(BF16)
    ones = jnp.ones((LANES, LANES), BF16)
    bi = lax.broadcasted_iota(I32, (nblk, nblk), 0)
    bj = lax.broadcasted_iota(I32, (nblk, nblk), 1)
    strict = jnp.where(bj < bi, 1.0, 0.0).astype(BF16)
    before = jnp.where(bi < bj, 1.0, 0.0).astype(BF16)
    mean_rows = jnp.full((8, LANES), 1.0 / LANES, BF16)
    c_row = lax.broadcasted_iota(I32, (1, cap), 1).astype(F32)
    blk_iota = lax.broadcasted_iota(I32, (nblk, cap), 0).astype(F32)
    t_iota = lax.broadcasted_iota(I32, (LANES, cap), 0).astype(F32)
    rep = cap // LANES

    def cums(mask_bf16):
        lp = jnp.dot(mask_bf16, upper, preferred_element_type=F32)
        bc = jnp.dot(mask_bf16, ones, preferred_element_type=F32)
        bst = jnp.dot(strict, bc.astype(BF16), preferred_element_type=F32)
        return lp, bc, bst

    def per_expert(e, carry):
        a = aff_ref[pl.ds(e, nblk, stride=N_EXPERTS), :]
        ab = pltpu.bitcast(a, I32)
        thr_e = thr_sc[pl.ds(e, 1), :]
        need_e = need_sc[pl.ds(e, 1), :]
        gt = ab > thr_e
        eq = ab == thr_e
        eqf = jnp.where(eq, 1.0, 0.0)
        lp_q, _, bst_q = cums(eqf.astype(BF16))
        sel = jnp.logical_or(gt, jnp.logical_and(eq, bst_q + lp_q - eqf < need_e))
        mb = jnp.where(sel, 1.0, 0.0).astype(BF16)
        lp, bc, bst = cums(mb)
        spos_ref[pl.ds(e, nblk, stride=N_EXPERTS), :] = jnp.where(sel, bst + lp - 1.0, -1.0)
        bc_row = lax.dot_general(mean_rows, bc.astype(BF16), _NT, preferred_element_type=F32)
        cb_ref[pl.ds(e, 1), :] = jnp.dot(bc_row.astype(BF16), before, preferred_element_type=F32)[:1].astype(I32)
        bend_w = jnp.tile(bst + bc, (1, rep))
        bst_w = jnp.tile(bst, (1, rep))
        blk_c = jnp.sum(jnp.where(bend_w <= c_row, 1.0, 0.0), axis=0, keepdims=True)
        onehot = blk_iota == blk_c
        bst_c = jnp.sum(jnp.where(onehot, bst_w, 0.0), axis=0, keepdims=True)
        r_c = c_row - bst_c
        ohb = jnp.where(onehot, 1.0, 0.0).astype(BF16)
        lp_t = lax.dot_general(lower, mb, _NT, preferred_element_type=F32)
        lp_c = jnp.dot(lp_t.astype(BF16), ohb, preferred_element_type=F32)
        tl_c = jnp.sum(jnp.where(lp_c <= r_c, 1.0, 0.0), axis=0, keepdims=True)
        idx_ref[pl.ds(e, 1), :] = (blk_c * LANES + tl_c).astype(I32)
        a_hi = a.astype(BF16)
        a_lo = (a - a_hi.astype(F32)).astype(BF16)
        at_hi = lax.dot_general(eye, a_hi, _NT, preferred_element_type=F32).astype(BF16)
        at_lo = lax.dot_general(eye, a_lo, _NT, preferred_element_type=F32).astype(BF16)
        g_c = (jnp.dot(at_hi, ohb, preferred_element_type=F32)
               + jnp.dot(at_lo, ohb, preferred_element_type=F32))
        gate_ref[pl.ds(e, 1), :] = jnp.sum(jnp.where(t_iota == tl_c, g_c, 0.0), axis=0, keepdims=True)
        return carry

    lax.fori_loop(0, N_EXPERTS, per_expert, 0)


def _route(aff2d, batch, seq):
    cap = EC_FACTOR * seq // N_EXPERTS
    nblk = seq // LANES
    rows = nblk * N_EXPERTS
    out_spec = pl.BlockSpec((None, N_EXPERTS, cap), lambda b: (b, 0, 0))
    return pl.pallas_call(
        functools.partial(_route_kernel, cap=cap),
        grid=(batch,),
        in_specs=[pl.BlockSpec((rows, LANES), lambda b: (b, 0))],
        out_specs=[out_spec, out_spec, pl.BlockSpec((rows, LANES), lambda b: (b, 0)),
                   pl.BlockSpec((None, N_EXPERTS, nblk), lambda b: (b, 0, 0))],
        out_shape=[jax.ShapeDtypeStruct((batch, N_EXPERTS, cap), I32),
                   jax.ShapeDtypeStruct((batch, N_EXPERTS, cap), F32),
                   jax.ShapeDtypeStruct((batch * rows, LANES), F32),
                   jax.ShapeDtypeStruct((batch, N_EXPERTS, nblk), I32)],
        scratch_shapes=[pltpu.VMEM((N_EXPERTS, LANES), I32), pltpu.VMEM((N_EXPERTS, LANES), F32)],
        compiler_params=_cparams(("parallel",)),
        name="route",
    )(aff2d)


def _ffn_kernel(idx_ref, nxt_ref, gate_ref, h_hbm, wg32_ref, wu32_ref, wd32_ref, y_ref, buf, sem,
                wg_ref, wu_ref, wd_ref, *, seq, tc, nsub):
    seq_id = pl.program_id(1)
    n_seq = pl.num_programs(1)
    step = pl.program_id(0) * n_seq + seq_id
    last_step = pl.num_programs(0) * n_seq - 1
    base = seq_id * seq
    next_base = jnp.where(seq_id + 1 < n_seq, seq_id + 1, 0) * seq

    @pl.when(seq_id == 0)
    def _():
        wg_ref[...] = wg32_ref[...].astype(BF16)
        wu_ref[...] = wu32_ref[...].astype(BF16)
        wd_ref[...] = wd32_ref[...].astype(BF16)

    n_tiles = D_MODEL // LANES

    def row_copy(ids, row0, j, r, slot):
        tok = pl.multiple_of((row0 + ids[0, 0, j * tc + r]) * n_tiles, n_tiles)
        return pltpu.make_async_copy(h_hbm.at[pl.ds(tok, n_tiles), :],
                                     buf.at[slot, pl.ds(r * n_tiles, n_tiles), :], sem.at[slot])

    def issue(ids, row0, j, slot):
        for r in range(tc):
            row_copy(ids, row0, j, r, slot).start()

    diag = lax.broadcasted_iota(I32, (tc, tc), 0) == lax.broadcasted_iota(I32, (tc, tc), 1)
    ones = jnp.ones((tc, LANES), BF16)

    ahead = 2

    @pl.when(step == 0)
    def _():
        for j in range(ahead):
            issue(idx_ref, base, j, j)

    for j in range(nsub):
        slot = j
        for r in range(tc):
            row_copy(idx_ref, base, j, r, slot).wait()
        xs = jnp.concatenate([buf.at[slot][pl.ds(c, tc, stride=n_tiles), :].astype(BF16) for c in range(n_tiles)],
                             axis=1)
        g = jnp.dot(xs, wg_ref[...], preferred_element_type=F32)
        u = jnp.dot(xs, wu_ref[...], preferred_element_type=F32)
        hm = (jax.nn.silu(g) * u).astype(BF16)
        y = jnp.dot(hm, wd_ref[...], preferred_element_type=F32)
        gr = jnp.broadcast_to(gate_ref[0, :, j * tc:(j + 1) * tc], (tc, tc))
        gcol = _split_dot(jnp.where(diag, gr, 0.0), ones)
        y_ref[j * tc:(j + 1) * tc, :] = (y * jnp.tile(gcol, (1, D_MODEL // LANES))).astype(BF16)
        if j + ahead < nsub:
            issue(idx_ref, base, j + ahead, j + ahead)
        else:
            issue(nxt_ref, next_base, j + ahead - nsub, j + ahead - nsub)

    @pl.when(step == last_step)
    def _():
        for j in range(ahead):
            for r in range(tc):
                row_copy(nxt_ref, next_base, j, r, j).wait()


def _ffn(idx, gates, h2d, w_gate, w_up, w_down, layer, seq, tc=256):
    b, ne, cap = idx.shape
    tc = min(tc, cap // 4)
    nsub = cap // tc
    assert nsub > 2
    idx3 = idx.reshape(b * ne, 1, cap)
    gate3 = gates.reshape(b * ne, 1, cap)
    slot = lambda ei, bi: (bi * ne + ei, 0, 0)
    next_slot = lambda ei, bi: (jnp.where(bi + 1 < b, (bi + 1) * ne + ei, jnp.minimum(ei + 1, ne - 1)), 0, 0)
    wspec = lambda shape: pl.BlockSpec((None, None) + shape, lambda ei, bi: (layer, ei, 0, 0))
    return pl.pallas_call(
        functools.partial(_ffn_kernel, seq=seq, tc=tc, nsub=nsub),
        grid=(ne, b),
        in_specs=[
            pl.BlockSpec((1, 1, cap), slot, memory_space=pltpu.SMEM),
            pl.BlockSpec((1, 1, cap), next_slot, memory_space=pltpu.SMEM),
            pl.BlockSpec((1, 1, cap), slot),
            pl.BlockSpec(memory_space=pl.ANY),
            wspec((D_MODEL, EXPERT_FF)), wspec((D_MODEL, EXPERT_FF)), wspec((EXPERT_FF, D_MODEL)),
        ],
        out_specs=pl.BlockSpec((None, None, cap, D_MODEL), lambda ei, bi: (bi, ei, 0, 0)),
        out_shape=jax.ShapeDtypeStruct((b, ne, cap, D_MODEL), BF16),
        scratch_shapes=[pltpu.VMEM((nsub, tc * (D_MODEL // LANES), LANES), F32), pltpu.SemaphoreType.DMA((nsub,)),
                        pltpu.VMEM((D_MODEL, EXPERT_FF), BF16), pltpu.VMEM((D_MODEL, EXPERT_FF), BF16),
                        pltpu.VMEM((EXPERT_FF, D_MODEL), BF16)],
        compiler_params=pltpu.CompilerParams(dimension_semantics=("arbitrary", "arbitrary"),
                                             vmem_limit_bytes=VMEM_LIMIT, disable_bounds_checks=True),
        name="ffn",
    )(idx3, idx3, gate3, h2d, w_gate, w_up, w_down)


_CTM = 256
_CWIN = 64
_CALIGN = 16


def _combine_kernel(cb_ref, x_ref, sp_ref, y_hbm, g_ref, o_ref, ybuf, xbuf, sem, xsem, *,
                    final, tiles_per_seq, nblk, cap):
    i = pl.program_id(0)
    n_tiles = pl.num_programs(0)
    b = i // tiles_per_seq
    slot = i % 2

    def window(tile, e):
        tb = tile // tiles_per_seq
        off = (tb * N_EXPERTS + e) * (nblk + 1) + (tile % tiles_per_seq) * (_CTM // LANES)
        s0 = cb_ref[off]
        s1 = cb_ref[off + _CTM // LANES]
        start = jnp.minimum((s0 // _CALIGN) * _CALIGN, cap - _CWIN)
        return s1, pl.multiple_of(start, _CALIGN)

    def fetch(tile, e, start, buf_slot):
        return pltpu.make_async_copy(y_hbm.at[tile // tiles_per_seq, e, pl.ds(start, _CWIN), :],
                                     ybuf.at[buf_slot, pl.ds(e * _CWIN, _CWIN), :], sem.at[buf_slot])

    def fetch_all(tile, buf_slot):
        for e in range(N_EXPERTS):
            fetch(tile, e, window(tile, e)[1], buf_slot).start()

    @pl.when(i == 0)
    def _():
        fetch_all(i, slot)

    @pl.when(i + 1 < n_tiles)
    def _():
        fetch_all(i + 1, 1 - slot)

    wins = [window(i, e) for e in range(N_EXPERTS)]
    pad = jnp.full((LANES - N_EXPERTS, LANES), -1.0, F32)
    sp_t = jnp.concatenate([jnp.concatenate([sp_ref[hf], pad], axis=0).T for hf in range(_CTM // LANES)],
                           axis=0)
    lane = lax.broadcasted_iota(I32, (1, _CWIN), 1).astype(F32)
    lane2 = lax.broadcasted_iota(I32, (1, LANES), 1)
    per_tile = LANES // _CWIN
    hits = []
    for e0 in range(0, N_EXPERTS, per_tile):
        rel = sp_t[:, e0:e0 + 1] - wins[e0][1].astype(F32)
        for k in range(1, per_tile):
            rel = jnp.where(lane2 < k * _CWIN, rel,
                            sp_t[:, e0 + k:e0 + k + 1] - (wins[e0 + k][1] - k * _CWIN).astype(F32))
        hits.append(jnp.where(rel == lane2.astype(F32), 1.0, 0.0).astype(BF16))
    for e in range(N_EXPERTS):
        fetch(i, e, wins[e][1], slot).wait()
    o_ref[...] = x_ref[...] + jnp.dot(jnp.concatenate(hits, axis=1), ybuf[slot], preferred_element_type=F32)
    for e in range(N_EXPERTS):
        s1, start = wins[e]
        col = sp_t[:, e:e + 1]

        def extra(k, carry, e=e, s1=s1, start=start, col=col):
            lo = start + (k + 1) * _CWIN
            st = pl.multiple_of(jnp.minimum(lo, cap - _CWIN), _CALIGN)
            cp = pltpu.make_async_copy(y_hbm.at[b, e, pl.ds(st, _CWIN), :], xbuf, xsem)
            cp.start()
            cp.wait()
            hit = jnp.where(jnp.logical_and(col - st.astype(F32) == lane, col >= lo.astype(F32)), 1.0, 0.0)
            o_ref[...] += jnp.dot(hit.astype(BF16), xbuf[...], preferred_element_type=F32)
            return carry

        n_extra = jnp.maximum(s1 - start - 1, 0) // _CWIN
        lax.fori_loop(0, n_extra, extra, 0)
    if final:
        x = o_ref[...]
        o_ref[...] = x * _rms_scale(x) * g_ref[...]


def _combine(x2d, spos, cb, y, g_final, final, seq):
    t = x2d.shape[0]
    batch, ne, cap, _ = y.shape
    nblk = seq // LANES
    cb_full = jnp.concatenate([cb, jnp.full((batch, ne, 1), cap, I32)], axis=-1).reshape(-1)
    spb = _CTM // LANES
    return pl.pallas_call(
        functools.partial(_combine_kernel, final=final, tiles_per_seq=seq // _CTM, nblk=nblk, cap=cap),
        grid_spec=pltpu.PrefetchScalarGridSpec(
            num_scalar_prefetch=1,
            grid=(t // _CTM,),
            in_specs=[
                pl.BlockSpec((_CTM, D_MODEL), lambda i, c: (i, 0)),
                pl.BlockSpec((spb, N_EXPERTS, LANES), lambda i, c: (i, 0, 0)),
                pl.BlockSpec(memory_space=pl.ANY),
                pl.BlockSpec((1, D_MODEL), lambda i, c: (0, 0)),
            ],
            out_specs=pl.BlockSpec((_CTM, D_MODEL), lambda i, c: (i, 0)),
            scratch_shapes=[pltpu.VMEM((2, N_EXPERTS * _CWIN, D_MODEL), BF16), pltpu.VMEM((_CWIN, D_MODEL), BF16),
                            pltpu.SemaphoreType.DMA((2,)), pltpu.SemaphoreType.DMA],
        ),
        out_shape=jax.ShapeDtypeStruct((t, D_MODEL), F32),
        compiler_params=_cparams(("arbitrary",)),
        name="combine",
    )(cb_full, x2d, spos, y, g_final.reshape(1, D_MODEL))


_SLAB_Q0 = 512
_SLAB_K0 = _SLAB_Q0 + D_HEADS * LANES
_SLAB_V0 = _SLAB_K0 + D_KV_HEADS * LANES
_ODD_COLS = _SLAB_V0 + D_KV_HEADS * LANES


def _proj_odd_kernel(x_ref, g_ref, wm_ref, cqn_ref, wq_ref, ckvn_ref, wkv_ref, dqn_ref, dkn_ref,
                     ccq_ref, s1cq_ref, s2cq_ref, cck_ref, s1ck_ref, s2ck_ref,
                     cdq_ref, s1dq_ref, s2dq_ref, cdk_ref, s1dk_ref, s2dk_ref,
                     qc_ref, kc_ref, vc_ref, qd_ref, kd_ref, vd_ref, stat_ref, *, steps_per_seq):
    lane = lax.broadcasted_iota(I32, (1, LANES), 1)
    one64 = jnp.where(lane == HEAD_DIM, 1.0, 0.0)
    last_lane = lane == LANES - 1
    half_rope = C_ROPE // 2
    stats = [jnp.zeros((1, LANES), F32), jnp.zeros((1, LANES), F32)]
    ones_mat = jnp.ones((LANES, LANES), BF16)

    def with_norm(val, fill, row, col):
        n2 = jnp.dot((val * val).astype(BF16), ones_mat, preferred_element_type=F32)
        stats[row] = jnp.where(lane == col, jnp.maximum(jnp.max(n2, axis=0, keepdims=True), stats[row]), stats[row])
        return jnp.where(last_lane, fill, val).astype(BF16)

    def head_norm(xg, gn_ref):
        ss = jnp.sum(xg * xg, axis=-1, keepdims=True) * (1.0 / HEAD_DIM)
        return xg * lax.rsqrt(ss + EPS) * gn_ref[...]

    n_chunks = 2
    rows_per = x_ref.shape[0] // n_chunks
    for c in range(n_chunks):
        rows = slice(c * rows_per, (c + 1) * rows_per)
        tab = lambda *refs: [r[rows, :] for r in refs]
        x = x_ref[rows, :]
        y = (x * _rms_scale(x) * g_ref[...]).astype(BF16)
        pm = jnp.dot(y, wm_ref[...], preferred_element_type=F32)
        cq = pm[:, :C_Q_RANK]
        cqn = (cq * _rms_scale(cq) * cqn_ref[...]).astype(BF16)
        qc = jnp.dot(cqn, wq_ref[...], preferred_element_type=F32)
        ckv = pm[:, C_Q_RANK:C_Q_RANK + C_KV_RANK]
        ckvn = (ckv * _rms_scale(ckv) * ckvn_ref[...]).astype(BF16)
        kv = jnp.dot(ckvn, wkv_ref[...], preferred_element_type=F32)
        kr = _rope3(pm[:, C_Q_RANK + C_KV_RANK:_SLAB_Q0], *tab(cck_ref, s1ck_ref, s2ck_ref), half_rope)
        t_cq = tab(ccq_ref, s1cq_ref, s2cq_ref)
        for h in range(C_HEADS):
            sl = slice(h * LANES, (h + 1) * LANES)
            qc_ref[rows, sl] = with_norm(_rope3(qc[:, sl], *t_cq, half_rope), 1.0, 1, h)
            kc_ref[rows, sl] = with_norm(kv[:, sl] + kr, -1.0, 0, h)
            vc_ref[rows, sl] = (kv[:, C_HEADS * LANES + h * LANES:C_HEADS * LANES + (h + 1) * LANES]
                                + one64).astype(BF16)
        t_dq = tab(cdq_ref, s1dq_ref, s2dq_ref)
        t_dk = tab(cdk_ref, s1dk_ref, s2dk_ref)
        for g in range(D_HEADS):
            xg = pm[:, _SLAB_Q0 + g * LANES:_SLAB_Q0 + (g + 1) * LANES]
            qd_ref[rows, g * LANES:(g + 1) * LANES] = with_norm(
                _rope3(head_norm(xg, dqn_ref), *t_dq, HEAD_DIM // 4), 1.0, 1, C_HEADS + g)
        for g in range(D_KV_HEADS):
            sl = slice(g * LANES, (g + 1) * LANES)
            xg = pm[:, _SLAB_K0 + g * LANES:_SLAB_K0 + (g + 1) * LANES]
            kd_ref[rows, sl] = with_norm(_rope3(head_norm(xg, dkn_ref), *t_dk, HEAD_DIM // 4), -1.0, 0, C_HEADS + g)
            vd_ref[rows, sl] = (pm[:, _SLAB_V0 + g * LANES:_SLAB_V0 + (g + 1) * LANES] + one64).astype(BF16)

    new = jnp.concatenate(stats + [jnp.zeros((6, LANES), F32)], axis=0)

    @pl.when(pl.program_id(0) % steps_per_seq == 0)
    def _():
        stat_ref[...] = new

    @pl.when(pl.program_id(0) % steps_per_seq != 0)
    def _():
        stat_ref[...] = jnp.maximum(stat_ref[...], new)


def _slabs(w, n_heads, width, lane_off=0):
    k = w.shape[0]
    w3 = w.reshape(k, n_heads, width)
    w3 = jnp.pad(w3, ((0, 0), (0, 0), (lane_off, LANES - width - lane_off)))
    return w3.reshape(k, n_heads * LANES)


def _axial_tables(row, col, scale):
    half = HEAD_DIM // 2
    cr, s1r, s2r = _rope_tables(row, D_THETA, half, 0, LANES, scale)
    cc, s1c, s2c = _rope_tables(col, D_THETA, half, half, LANES, scale)
    lane = jnp.arange(LANES)[None, :]
    return jnp.where(lane < half, cr, cc), s1r + s1c, s2r + s2c


def _proj_odd(x2d, seq, g_mix, w_in, cq_norm, w_cq_up, ckv_norm, w_ckv_up, dq_norm, dk_norm, tm=512):
    t = x2d.shape[0]
    nblk = seq // tm
    o1 = C_Q_RANK
    o2 = o1 + C_KV_RANK
    o3 = o2 + C_ROPE
    o4 = o3 + D_HEADS * HEAD_DIM
    o5 = o4 + D_KV_HEADS * HEAD_DIM
    wm = jnp.concatenate([
        w_in[:, :o2],
        _slabs(w_in[:, o2:o3], 1, C_ROPE, C_NOPE),
        _slabs(w_in[:, o3:o4], D_HEADS, HEAD_DIM),
        _slabs(w_in[:, o4:o5], D_KV_HEADS, HEAD_DIM),
        _slabs(w_in[:, o5:], D_KV_HEADS, HEAD_DIM),
    ], axis=1).astype(BF16)
    assert wm.shape[1] == _ODD_COLS
    wq = _slabs(w_cq_up, C_HEADS, C_NOPE + C_ROPE).astype(BF16)
    kv3 = w_ckv_up.reshape(C_KV_RANK, C_HEADS, 2 * HEAD_DIM)
    wkv = jnp.concatenate([
        _slabs(kv3[:, :, :C_NOPE].reshape(C_KV_RANK, -1), C_HEADS, C_NOPE),
        _slabs(kv3[:, :, C_NOPE:].reshape(C_KV_RANK, -1), C_HEADS, HEAD_DIM),
    ], axis=1).astype(BF16)
    pad64 = lambda g: jnp.pad(g, (0, LANES - HEAD_DIM)).reshape(1, LANES)

    pos = jnp.arange(seq, dtype=I32)
    row_pos = pos // GRID_W
    col_pos = pos % GRID_W
    c_scale = (C_NOPE + C_ROPE) ** -0.5 * LOG2E
    d_scale = HEAD_DIM ** -0.5 * LOG2E
    tabs = (_rope_tables(pos, ROPE_THETA, C_ROPE, C_NOPE, LANES, c_scale)
            + _rope_tables(pos, ROPE_THETA, C_ROPE, C_NOPE, LANES, 1.0)
            + _axial_tables(row_pos, col_pos, d_scale)
            + _axial_tables(row_pos, col_pos, 1.0))

    row = lambda i: (i, 0)
    full = lambda i: (0, 0)
    tspec = pl.BlockSpec((tm, LANES), lambda i: (i % nblk, 0))
    wide = C_HEADS * LANES
    kvw = D_KV_HEADS * LANES
    return pl.pallas_call(
        functools.partial(_proj_odd_kernel, steps_per_seq=nblk),
        grid=(t // tm,),
        in_specs=[
            pl.BlockSpec((tm, D_MODEL), row),
            pl.BlockSpec((1, D_MODEL), full),
            pl.BlockSpec(wm.shape, full),
            pl.BlockSpec((1, C_Q_RANK), full),
            pl.BlockSpec(wq.shape, full),
            pl.BlockSpec((1, C_KV_RANK), full),
            pl.BlockSpec(wkv.shape, full),
            pl.BlockSpec((1, LANES), full),
            pl.BlockSpec((1, LANES), full),
        ] + [tspec] * 12,
        out_specs=[
            pl.BlockSpec((tm, wide), row), pl.BlockSpec((tm, wide), row), pl.BlockSpec((tm, wide), row),
            pl.BlockSpec((tm, wide), row), pl.BlockSpec((tm, kvw), row), pl.BlockSpec((tm, kvw), row),
            pl.BlockSpec((None, 8, LANES), lambda i: (i // nblk, 0, 0)),
        ],
        out_shape=[
            jax.ShapeDtypeStruct((t, wide), BF16), jax.ShapeDtypeStruct((t, wide), BF16),
            jax.ShapeDtypeStruct((t, wide), BF16), jax.ShapeDtypeStruct((t, wide), BF16),
            jax.ShapeDtypeStruct((t, kvw), BF16), jax.ShapeDtypeStruct((t, kvw), BF16),
            jax.ShapeDtypeStruct((t // seq, 8, LANES), F32),
        ],
        compiler_params=_cparams(("arbitrary",)),
        name="proj_odd",
    )(x2d, g_mix.reshape(1, D_MODEL), wm, cq_norm.reshape(1, -1), wq, ckv_norm.reshape(1, -1), wkv,
      pad64(dq_norm), pad64(dk_norm), *tabs)


def _flash_kernel(q_ref, k_ref, v_ref, o_ref, qs_sc, m_sc, acc_sc, *, group, tq, tk):
    ki = pl.program_id(3)

    @pl.when(ki == 0)
    def _():
        for g in range(group):
            qs_sc[g * tq:(g + 1) * tq, :] = q_ref[:, g * LANES:(g + 1) * LANES]
        m_sc[...] = jnp.full(m_sc.shape, NEG, F32)
        acc_sc[...] = jnp.zeros(acc_sc.shape, F32)

    s = lax.dot_general(qs_sc[...], k_ref[...], _NT, preferred_element_type=F32)
    m_prev = m_sc[...]
    m_new = jnp.maximum(m_prev, jnp.max(s, axis=1, keepdims=True))
    alpha = jnp.exp2(m_prev - m_new)
    p = jnp.exp2(s - jnp.tile(m_new, (1, tk // LANES)))
    acc_sc[...] = alpha * acc_sc[...] + jnp.dot(p.astype(BF16), v_ref[...], preferred_element_type=F32)
    m_sc[...] = m_new

    @pl.when(ki == pl.num_programs(3) - 1)
    def _():
        acc = acc_sc[...]
        o = acc / acc[:, HEAD_DIM:HEAD_DIM + 1]
        for g in range(group):
            o_ref[:, g * LANES:(g + 1) * LANES] = o[g * tq:(g + 1) * tq].astype(BF16)


def _flash_bounded_kernel(bound_ref, q_ref, k_ref, v_ref, o_ref, qs_sc, acc_sc, *, group, tq):
    ki = pl.program_id(3)
    head = pl.program_id(0) * pl.num_programs(1) + pl.program_id(1)

    @pl.when(ki == 0)
    def _():
        fix = jnp.where(lax.broadcasted_iota(I32, (1, LANES), 1) == LANES - 1, bound_ref[head], 1.0)
        for g in range(group):
            qs_sc[g * tq:(g + 1) * tq, :] = (q_ref[:, g * LANES:(g + 1) * LANES].astype(F32) * fix).astype(BF16)
        acc_sc[...] = jnp.zeros(acc_sc.shape, F32)

    s_t = lax.dot_general(k_ref[...], qs_sc[...], _NT, preferred_element_type=F32)
    p_t = jnp.exp2(s_t).astype(BF16)
    acc_sc[...] += lax.dot_general(v_ref[...], p_t, (((0,), (0,)), ((), ())), preferred_element_type=F32)

    @pl.when(ki == pl.num_programs(3) - 1)
    def _():
        acc = acc_sc[...]
        o = (acc / acc[HEAD_DIM:HEAD_DIM + 1, :]).T
        for g in range(group):
            o_ref[:, g * LANES:(g + 1) * LANES] = o[g * tq:(g + 1) * tq].astype(BF16)


def _flash(q, k, v, group, nk, bounded, rows=1024, tk=512, tk_bounded=2048):
    b, s, qw = q.shape
    hk = k.shape[2] // LANES
    tq = rows // group
    tk = min(tk, s)
    tkb = min(tk_bounded, s)
    out_shape = jax.ShapeDtypeStruct((b, s, qw), BF16)
    sem = ("parallel", "parallel", "parallel", "arbitrary")

    def running_max(q, k, v, nk):
        qspec = pl.BlockSpec((None, tq, group * LANES), lambda bi, hi, qi, ki: (bi, qi, hi))
        kspec = pl.BlockSpec((None, tk, LANES), lambda bi, hi, qi, ki: (bi, ki, hi))
        return pl.pallas_call(
            functools.partial(_flash_kernel, group=group, tq=tq, tk=tk),
            grid=(b, hk, s // tq, s // tk),
            in_specs=[qspec, kspec, kspec],
            out_specs=qspec,
            out_shape=out_shape,
            scratch_shapes=[pltpu.VMEM((rows, LANES), BF16), pltpu.VMEM((rows, LANES), F32),
                            pltpu.VMEM((rows, LANES), F32)],
            compiler_params=_cparams(sem),
            name="flash",
        )(q, k, v)

    def bound(q, k, v, nk):
        qspec = pl.BlockSpec((None, tq, group * LANES), lambda bi, hi, qi, ki, nkr: (bi, qi, hi))
        kspec = pl.BlockSpec((None, tkb, LANES), lambda bi, hi, qi, ki, nkr: (bi, ki, hi))
        return pl.pallas_call(
            functools.partial(_flash_bounded_kernel, group=group, tq=tq),
            grid_spec=pltpu.PrefetchScalarGridSpec(
                num_scalar_prefetch=1,
                grid=(b, hk, s // tq, s // tkb),
                in_specs=[qspec, kspec, kspec],
                out_specs=qspec,
                scratch_shapes=[pltpu.VMEM((rows, LANES), BF16), pltpu.VMEM((LANES, rows), F32)],
            ),
            out_shape=out_shape,
            compiler_params=_cparams(sem),
            name="flash_bounded",
        )(nk.reshape(-1), q, k, v)

    return lax.cond(bounded, bound, running_max, q, k, v, nk)


def _moe(x1, h2, aff, batch, seq, w_gate, w_up, w_down, layer, g_final, final):
    idx, gates, spos, cb = _route(aff.reshape(-1, LANES), batch, seq)
    y = _ffn(idx, gates, h2, w_gate, w_up, w_down, layer, seq)
    return _combine(x1, spos.reshape(-1, N_EXPERTS, LANES), cb, y, g_final, final, seq)


def kernel(x, norm_mix, norm_ffn, even_w_in, even_gmlp_norm, even_w_spatial, even_b_spatial, even_w_out,
           odd_w_in, odd_cq_norm, odd_w_cq_up, odd_ckv_norm, odd_w_ckv_up, odd_dq_norm, odd_dk_norm, odd_w_out,
           moe_w_router, moe_w_gate, moe_w_up, moe_w_down, final_norm):
    b, s, d = x.shape
    depth = norm_mix.shape[0]
    x2d = x.reshape(b * s, d)
    for i in range(depth):
        j = i // 2
        last = i == depth - 1
        if i % 2 == 0:
            q, k, v, go = _proj_even(x2d, s, norm_mix[i], even_w_in[j], even_gmlp_norm[j], even_w_spatial[j],
                                     even_b_spatial[j])
            a = _dilated(q.reshape(b, s, A_WIDTH), k.reshape(b, s, A_WIDTH), v.reshape(b, s, A_WIDTH))
            x1, h2, aff = _outproj(x2d, a.reshape(b * s, A_WIDTH), go, even_w_out[j][:A_WIDTH],
                                   even_w_out[j][A_WIDTH:], norm_ffn[i], moe_w_router[i])
        else:
            qc, kc, vc, qd, kd, vd, stat = _proj_odd(x2d, s, norm_mix[i], odd_w_in[j], odd_cq_norm[j],
                                                     odd_w_cq_up[j], odd_ckv_norm[j], odd_w_ckv_up[j],
                                                     odd_dq_norm[j], odd_dk_norm[j])
            grp = D_HEADS // D_KV_HEADS
            k2_c, k2_d = stat[:, 0, :C_HEADS], stat[:, 0, C_HEADS:C_HEADS + D_KV_HEADS]
            q2_c = stat[:, 1, :C_HEADS]
            q2_d = jnp.max(stat[:, 1, C_HEADS:C_HEADS + D_HEADS].reshape(b, D_KV_HEADS, grp), axis=-1)
            bound_c = jnp.sqrt(q2_c * k2_c) * _NORM_MARGIN ** 2
            bound_d = jnp.sqrt(q2_d * k2_d) * _NORM_MARGIN ** 2
            bounded = jnp.maximum(jnp.max(bound_c), jnp.max(bound_d)) <= _MAX_SCORE_BOUND
            r3 = lambda z: z.reshape(b, s, -1)
            oc = _flash(r3(qc), r3(kc), r3(vc), 1, bound_c, bounded)
            od = _flash(r3(qd), r3(kd), r3(vd), grp, bound_d, bounded)
            cw = C_HEADS * HEAD_DIM
            x1, h2, aff = _outproj(x2d, oc.reshape(b * s, -1), od.reshape(b * s, -1),
                                   _slabs(odd_w_out[j][:cw].T, C_HEADS, HEAD_DIM).T,
                                   _slabs(odd_w_out[j][cw:].T, D_HEADS, HEAD_DIM).T,
                                   norm_ffn[i], moe_w_router[i])
        x2d = _moe(x1, h2, aff, b, s, moe_w_gate, moe_w_up, moe_w_down, i, final_norm, last)
    return x2d.reshape(b, s, d)
```

```python
import functools
import math

import jax
import jax.numpy as jnp
from jax import lax
from jax.experimental import pallas as pl
from jax.experimental.pallas import tpu as pltpu

F32 = jnp.float32
BF16 = jnp.bfloat16
I32 = jnp.int32

EPS = 1e-6
NEG = -1e30
LOG2E = 1.4426950408889634

D_MODEL = 1024
HEAD_DIM = 64
ROPE_THETA = 500000.0
ROT_DIM = 16
GRID_W = 64
A_HEADS = 12
A_WIDTH = 768
A_DILATIONS = (1, 4, 16)
A_RADIUS = 64
B_WIDTH = 256
B_GROUPS = 4
B_CHUNK = 128
C_HEADS = 8
C_Q_RANK = 256
C_KV_RANK = 128
C_NOPE = 64
C_ROPE = 32
D_HEADS = 8
D_KV_HEADS = 2
D_THETA = 10000.0
N_EXPERTS = 16
EC_FACTOR = 2
EXPERT_FF = 512

_NORM_MARGIN = 1.01
_MAX_SCORE_BOUND = 55.0

LANES = 128
VMEM_LIMIT = 48 * 1024 * 1024
_DILATED_VMEM_LIMIT = 56 * 1024 * 1024

_NT = (((1,), (1,)), ((), ()))


def _cparams(sem):
    return pltpu.CompilerParams(dimension_semantics=sem, vmem_limit_bytes=VMEM_LIMIT)


def _rms_scale(x):
    return lax.rsqrt(jnp.mean(x * x, axis=-1, keepdims=True) + EPS)


def _rope3(a, c, s1, s2, shift):
    return a * c + pltpu.roll(a, LANES - shift, 1) * s1 + pltpu.roll(a, shift, 1) * s2


def _split_dot(x, w_bf16):
    hi = x.astype(BF16)
    lo = (x - hi.astype(F32)).astype(BF16)
    return (jnp.dot(hi, w_bf16, preferred_element_type=F32)
            + jnp.dot(lo, w_bf16, preferred_element_type=F32))


def _proj_even_kernel(x_ref, g_ref, w_ref, cq_ref, s1q_ref, s2q_ref, ck_ref, s1k_ref, s2k_ref,
                      gn_ref, gmat_ref, ws_ref, bs_ref,
                      q_ref, k_ref, v_ref, go_ref):
    x = x_ref[...]
    y = (x * _rms_scale(x) * g_ref[...]).astype(BF16)
    tm = x.shape[0]

    aq = jnp.dot(y, w_ref[:, 0:A_WIDTH], preferred_element_type=F32)
    tq = (cq_ref[...], s1q_ref[...], s2q_ref[...])
    for j in range(A_WIDTH // LANES):
        sl = slice(j * LANES, (j + 1) * LANES)
        q_ref[:, sl] = _rope3(aq[:, sl], *tq, ROT_DIM // 2)
    ak = jnp.dot(y, w_ref[:, A_WIDTH:2 * A_WIDTH], preferred_element_type=F32)
    tk = (ck_ref[...], s1k_ref[...], s2k_ref[...])
    for j in range(A_WIDTH // LANES):
        sl = slice(j * LANES, (j + 1) * LANES)
        k_ref[:, sl] = _rope3(ak[:, sl], *tk, ROT_DIM // 2)
    v_ref[...] = jnp.dot(y, w_ref[:, 2 * A_WIDTH:3 * A_WIDTH], preferred_element_type=F32)

    z = jnp.dot(y, w_ref[:, 3 * A_WIDTH:3 * A_WIDTH + 2 * B_WIDTH], preferred_element_type=F32)
    ge = jax.nn.gelu(z)
    u = ge[:, :B_WIDTH]
    vv = ge[:, B_WIDTH:]
    ss = _split_dot(vv * vv, gmat_ref[...])
    vn = (vv * lax.rsqrt(ss + EPS) * gn_ref[...]).astype(BF16)
    grp = lax.broadcasted_iota(I32, (B_CHUNK, B_WIDTH), 1) // (B_WIDTH // B_GROUPS)
    for c in range(tm // B_CHUNK):
        rows = slice(c * B_CHUNK, (c + 1) * B_CHUNK)
        vc = vn[rows]
        mg = [jnp.dot(ws_ref[g], vc, preferred_element_type=F32) for g in range(B_GROUPS)]
        mixed = jnp.where(grp == 0, mg[0], jnp.where(grp == 1, mg[1], jnp.where(grp == 2, mg[2], mg[3])))
        go_ref[rows, :] = (u[rows] * (mixed + bs_ref[...])).astype(BF16)


def _rope_tables(pos, theta, r, lane_off, period, scale):
    half = r // 2
    inv = jnp.power(jnp.float32(theta), -jnp.arange(half, dtype=F32) * (2.0 / r))
    ang = pos.astype(F32)[:, None] * inv[None, :]
    cos, sin = jnp.cos(ang), jnp.sin(ang)
    o = (jnp.arange(LANES) % period) - lane_off
    in_lo = (o >= 0) & (o < half)
    in_hi = (o >= half) & (o < r)
    idx = jnp.clip(jnp.where(in_hi, o - half, o), 0, half - 1)
    c = jnp.where((in_lo | in_hi)[None, :], cos[:, idx], 1.0)
    s1 = jnp.where(in_lo[None, :], -sin[:, idx], 0.0)
    s2 = jnp.where(in_hi[None, :], sin[:, idx], 0.0)
    return c * scale, s1 * scale, s2 * scale


def _proj_even(x2d, seq, g_mix, w_in, gmlp_norm, w_s, b_s, tm=512):
    t = x2d.shape[0]
    nblk = seq // tm
    pos = jnp.arange(seq, dtype=I32)
    qscale = HEAD_DIM ** -0.5 * LOG2E
    cq, s1q, s2q = _rope_tables(pos, ROPE_THETA, ROT_DIM, 0, HEAD_DIM, qscale)
    ck, s1k, s2k = _rope_tables(pos, ROPE_THETA, ROT_DIM, 0, HEAD_DIM, 1.0)
    gdim = B_WIDTH // B_GROUPS
    gid = jnp.arange(B_WIDTH) // gdim
    gmat = jnp.where(gid[:, None] == gid[None, :], 1.0 / gdim, 0.0).astype(BF16)
    bias = jnp.repeat(b_s.T, gdim, axis=1)
    row = lambda i: (i, 0)
    tab = lambda i: (i % nblk, 0)
    full = lambda i: (0, 0)
    tspec = pl.BlockSpec((tm, LANES), tab)
    return pl.pallas_call(
        _proj_even_kernel,
        grid=(t // tm,),
        in_specs=[
            pl.BlockSpec((tm, D_MODEL), row),
            pl.BlockSpec((1, D_MODEL), full),
            pl.BlockSpec(w_in.shape, full),
            tspec, tspec, tspec, tspec, tspec, tspec,
            pl.BlockSpec((1, B_WIDTH), full),
            pl.BlockSpec((B_WIDTH, B_WIDTH), full),
            pl.BlockSpec((B_GROUPS, B_CHUNK, B_CHUNK), lambda i: (0, 0, 0)),
            pl.BlockSpec((B_CHUNK, B_WIDTH), full),
        ],
        out_specs=[
            pl.BlockSpec((tm, A_WIDTH), row),
            pl.BlockSpec((tm, A_WIDTH), row),
            pl.BlockSpec((tm, A_WIDTH), row),
            pl.BlockSpec((tm, B_WIDTH), row),
        ],
        out_shape=[
            jax.ShapeDtypeStruct((t, A_WIDTH), F32),
            jax.ShapeDtypeStruct((t, A_WIDTH), F32),
            jax.ShapeDtypeStruct((t, A_WIDTH), F32),
            jax.ShapeDtypeStruct((t, B_WIDTH), BF16),
        ],
        compiler_params=_cparams(("parallel",)),
        name="proj_even",
    )(x2d, g_mix.reshape(1, D_MODEL), w_in.astype(BF16), cq, s1q, s2q, ck, s1k, s2k,
      gmlp_norm.reshape(1, B_WIDTH), gmat, w_s.astype(BF16), bias)


_TQ = 128
_TK = _TQ + 2 * A_RADIUS
_NORM_ROWS = 512


def _dilated_kernel(q_ref, k_ref, v_ref, o_ref, m_sc, l_sc, bias_sc, *, seq):
    half0 = lax.broadcasted_iota(I32, (1, LANES), 1) < HEAD_DIM
    row_head = lax.broadcasted_iota(I32, (LANES, LANES), 0) // HEAD_DIM
    pick = [jnp.where(row_head == h, 1.0, 0.0).astype(BF16) for h in range(2)]
    n_pat = len(A_DILATIONS)
    zero_row = jnp.zeros((1, LANES), F32)

    def max_head_sq(ref):
        def body(c, mx):
            x = ref[pl.ds(pl.multiple_of(c * _NORM_ROWS, _NORM_ROWS), _NORM_ROWS), :]
            sq = (x * x).astype(BF16)
            return tuple(jnp.maximum(mx[h], jnp.max(jnp.dot(sq, pick[h], preferred_element_type=F32),
                                                    axis=0, keepdims=True)) for h in range(2))
        return lax.fori_loop(0, seq // _NORM_ROWS, body, (zero_row, zero_row))

    max_q2 = max_head_sq(q_ref)
    max_k2 = max_head_sq(k_ref)
    bound = [jnp.sqrt(max_q2[h] * max_k2[h]) * (_NORM_MARGIN * _NORM_MARGIN) for h in range(2)]
    worst = jnp.max(jnp.maximum(bound[0], bound[1]))

    diff = lax.broadcasted_iota(I32, (_TQ, _TK), 1) - lax.broadcasted_iota(I32, (_TQ, _TK), 0)
    for case in range(3):
        band = jnp.where(jnp.abs(diff - case * A_RADIUS) <= A_RADIUS, 0.0, NEG)
        bias_sc[6 + case] = band
        for h in range(2):
            bias_sc[3 * h + case] = band - jnp.tile(bound[h], (1, _TK // LANES))
    one_bf16 = jnp.ones((), BF16)

    def run(bounded):
        for pi, d in enumerate(A_DILATIONS):
            cls_len = seq // d
            tpc = cls_len // _TQ

            def tile(qrows, kb, vb, case, pi=pi):
                q = q_ref[qrows, :]
                parts = []
                for h in range(2):
                    qh = jnp.where(half0 if h == 0 else jnp.logical_not(half0), q, 0.0).astype(BF16)
                    s = lax.dot_general(qh, kb, _NT, preferred_element_type=F32)
                    if bounded:
                        mt = None
                        p = jnp.exp2(s + bias_sc[3 * h + case]).astype(BF16)
                    else:
                        s = s + bias_sc[6 + case]
                        mt = jnp.max(s, axis=-1, keepdims=True)
                        p = jnp.exp2(s - mt).astype(BF16)
                    vh = jnp.where(half0 if h == 0 else jnp.logical_not(half0), vb, one_bf16)
                    parts.append((mt, jnp.dot(p, vh, preferred_element_type=F32)))
                ot = jnp.where(half0, parts[0][1], parts[1][1])
                lt = pltpu.roll(jnp.where(half0, parts[1][1], parts[0][1]), HEAD_DIM, 1)
                if bounded:
                    if pi > 0:
                        lt = l_sc[qrows, :] + lt
                        ot = o_ref[qrows, :] + ot
                else:
                    mt = jnp.where(half0, parts[0][0], parts[1][0])
                    if pi > 0:
                        mp = m_sc[qrows, :]
                        mn = jnp.maximum(mp, mt)
                        a = jnp.exp2(mp - mn)
                        b = jnp.exp2(mt - mn)
                        lt = a * l_sc[qrows, :] + b * lt
                        ot = a * o_ref[qrows, :] + b * ot
                        mt = mn
                    if pi < n_pat - 1:
                        m_sc[qrows, :] = mt
                if pi == n_pat - 1:
                    o_ref[qrows, :] = ot / lt
                else:
                    l_sc[qrows, :] = lt
                    o_ref[qrows, :] = ot

            def window(l0, cls_len=cls_len):
                if isinstance(l0, int):
                    kst = min(max(l0 - A_RADIUS, 0), cls_len - _TK)
                else:
                    kst = jnp.clip(l0 - A_RADIUS, 0, cls_len - _TK)
                return kst, (l0 - kst) // A_RADIUS

            if d < 8:

                def body(j, carry, d=d, tpc=tpc):
                    i = j // tpc
                    l0 = (j % tpc) * _TQ
                    kst, case = window(l0)
                    if d == 1:
                        qrows = pl.ds(pl.multiple_of(l0, _TQ), _TQ)
                        krows = pl.ds(pl.multiple_of(kst, A_RADIUS), _TK)
                    else:
                        qrows = pl.ds(l0 * d + i, _TQ, stride=d)
                        krows = pl.ds(kst * d + i, _TK, stride=d)
                    tile(qrows, k_ref[krows, :].astype(BF16), v_ref[krows, :].astype(BF16), case)
                    return carry

                lax.fori_loop(0, seq // _TQ, body, 0, unroll=8)
            else:

                def body(i, carry, d=d, cls_len=cls_len, tpc=tpc):
                    cls = pl.ds(i, cls_len, stride=d)
                    kc = k_ref[cls, :].astype(BF16)
                    vc = v_ref[cls, :].astype(BF16)
                    for n in range(tpc):
                        kst, case = window(n * _TQ)
                        tile(pl.ds(n * _TQ * d + i, _TQ, stride=d), kc[kst:kst + _TK], vc[kst:kst + _TK], case)
                    return carry

                lax.fori_loop(0, d, body, 0, unroll=2)

    @pl.when(worst <= _MAX_SCORE_BOUND)
    def _():
        run(True)

    @pl.when(jnp.logical_not(worst <= _MAX_SCORE_BOUND))
    def _():
        run(False)


def _dilated(q, k, v):
    b, s, w = q.shape
    spec = pl.BlockSpec((None, s, LANES), lambda bi, hi: (bi, 0, hi))
    return pl.pallas_call(
        functools.partial(_dilated_kernel, seq=s),
        grid=(b, w // LANES),
        in_specs=[spec, spec, spec],
        out_specs=spec,
        out_shape=jax.ShapeDtypeStruct((b, s, w), F32),
        scratch_shapes=[pltpu.VMEM((s, LANES), F32), pltpu.VMEM((s, LANES), F32),
                        pltpu.VMEM((9, _TQ, _TK), F32)],
        compiler_params=pltpu.CompilerParams(dimension_semantics=("parallel", "parallel"),
                                             vmem_limit_bytes=_DILATED_VMEM_LIMIT),
        name="dilated",
    )(q, k, v)


def _outproj_kernel(x_ref, a_ref, b_ref, wa_ref, wb_ref, gf_ref, wr_ref, x1_ref, h2_ref, aff_ref):
    x1 = (x_ref[...]
          + jnp.dot(a_ref[...].astype(BF16), wa_ref[...], preferred_element_type=F32)
          + jnp.dot(b_ref[...].astype(BF16), wb_ref[...], preferred_element_type=F32))
    x1_ref[...] = x1
    h2 = x1 * _rms_scale(x1) * gf_ref[...]
    n_tiles = D_MODEL // LANES
    for j in range(n_tiles):
        h2_ref[pl.ds(j, h2.shape[0], stride=n_tiles), :] = h2[:, j * LANES:(j + 1) * LANES]
    hi = h2.astype(BF16)
    lo = (h2 - hi.astype(F32)).astype(BF16)
    both = jnp.dot(hi, wr_ref[...], preferred_element_type=F32)
    lg = (both[:, :LANES] + both[:, LANES:]
          + jnp.dot(lo, wr_ref[:, :LANES], preferred_element_type=F32))
    valid = lax.broadcasted_iota(I32, lg.shape, 1) < N_EXPERTS
    lg = jnp.where(valid, lg, NEG)
    e = jnp.exp(lg - jnp.max(lg, axis=-1, keepdims=True))
    aff = e / jnp.sum(e, axis=-1, keepdims=True)
    aff_t = aff.T
    for j in range(aff.shape[0] // LANES):
        aff_ref[j] = aff_t[:N_EXPERTS, j * LANES:(j + 1) * LANES]


def _outproj(x2d, a, b, wa, wb, g_ffn, w_router, tm=512):
    t = x2d.shape[0]
    wr = jnp.pad(w_router, ((0, 0), (0, LANES - N_EXPERTS)))
    wr_hi = wr.astype(BF16)
    wr_lo = (wr - wr_hi.astype(F32)).astype(BF16)
    wr2 = jnp.concatenate([wr_hi, wr_lo], axis=1)
    row = lambda i: (i, 0)
    full = lambda i: (0, 0)
    return pl.pallas_call(
        _outproj_kernel,
        grid=(t // tm,),
        in_specs=[
            pl.BlockSpec((tm, D_MODEL), row),
            pl.BlockSpec((tm, a.shape[1]), row),
            pl.BlockSpec((tm, b.shape[1]), row),
            pl.BlockSpec(wa.shape, full),
            pl.BlockSpec(wb.shape, full),
            pl.BlockSpec((1, D_MODEL), full),
            pl.BlockSpec((D_MODEL, 2 * LANES), full),
        ],
        out_specs=[
            pl.BlockSpec((tm, D_MODEL), row),
            pl.BlockSpec((tm * (D_MODEL // LANES), LANES), row),
            pl.BlockSpec((tm // LANES, N_EXPERTS, LANES), lambda i: (i, 0, 0)),
        ],
        out_shape=[
            jax.ShapeDtypeStruct((t, D_MODEL), F32),
            jax.ShapeDtypeStruct((t * (D_MODEL // LANES), LANES), F32),
            jax.ShapeDtypeStruct((t // LANES, N_EXPERTS, LANES), F32),
        ],
        compiler_params=_cparams(("parallel",)),
        name="outproj",
    )(x2d, a, b, wa.astype(BF16), wb.astype(BF16), g_ffn.reshape(1, D_MODEL), wr2)


def _route_kernel(aff_ref, idx_ref, gate_ref, spos_ref, cb_ref, thr_sc, need_sc, *, cap):
    nblk = aff_ref.shape[0] // N_EXPERTS
    bits = pltpu.bitcast(aff_ref[...], I32).reshape(nblk, N_EXPERTS, LANES)

    def count(pred):
        return jnp.sum(jnp.sum(jnp.where(pred, 1.0, 0.0), axis=0), axis=1, keepdims=True)

    def search(it, thr):
        cand = thr | jnp.left_shift(jnp.int32(1), 30 - it)
        return jnp.where(count(bits >= cand[None]) >= cap, cand, thr)

    thr = lax.fori_loop(0, 31, search, jnp.zeros((N_EXPERTS, 1), I32))
    need = cap - count(bits > thr[None])
    thr_sc[...] = jnp.broadcast_to(thr, (N_EXPERTS, LANES))
    need_sc[...] = jnp.broadcast_to(need, (N_EXPERTS, LANES))

    ri = lax.broadcasted_iota(I32, (LANES, LANES), 0)
    ci = lax.broadcasted_iota(I32, (LANES, LANES), 1)
    upper = jnp.where(ri <= ci, 1.0, 0.0).astype(BF16)
    lower = jnp.where(ci <= ri, 1.0, 0.0).astype(BF16)
    eye = jnp.where(ri == ci, 1.0, 0.0).astype(BF16)
    ones = jnp.ones((LANES, LANES), BF16)
    bi = lax.broadcasted_iota(I32, (nblk, nblk), 0)
    bj = lax.broadcasted_iota(I32, (nblk, nblk), 1)
    strict = jnp.where(bj < bi, 1.0, 0.0).astype(BF16)
    before = jnp.where(bi < bj, 1.0, 0.0).astype(BF16)
    mean_rows = jnp.full((8, LANES), 1.0 / LANES, BF16)
    c_row = lax.broadcasted_iota(I32, (1, cap), 1).astype(F32)
    blk_iota = lax.broadcasted_iota(I32, (nblk, cap), 0).astype(F32)
    t_iota = lax.broadcasted_iota(I32, (LANES, cap), 0).astype(F32)
    rep = cap // LANES

    def cums(mask_bf16):
        lp = jnp.dot(mask_bf16, upper, preferred_element_type=F32)
        bc = jnp.dot(mask_bf16, ones, preferred_element_type=F32)
        bst = jnp.dot(strict, bc.astype(BF16), preferred_element_type=F32)
        return lp, bc, bst

    def per_expert(e, carry):
        a = aff_ref[pl.ds(e, nblk, stride=N_EXPERTS), :]
        ab = pltpu.bitcast(a, I32)
        thr_e = thr_sc[pl.ds(e, 1), :]
        need_e = need_sc[pl.ds(e, 1), :]
        gt = ab > thr_e
        eq = ab == thr_e
        eqf = jnp.where(eq, 1.0, 0.0)
        lp_q, _, bst_q = cums(eqf.astype(BF16))
        sel = jnp.logical_or(gt, jnp.logical_and(eq, bst_q + lp_q - eqf < need_e))
        mb = jnp.where(sel, 1.0, 0.0).astype(BF16)
        lp, bc, bst = cums(mb)
        spos_ref[pl.ds(e, nblk, stride=N_EXPERTS), :] = jnp.where(sel, bst + lp - 1.0, -1.0)
        bc_row = lax.dot_general(mean_rows, bc.astype(BF16), _NT, preferred_element_type=F32)
        cb_ref[pl.ds(e, 1), :] = jnp.dot(bc_row.astype(BF16), before, preferred_element_type=F32)[:1].astype(I32)
        bend_w = jnp.tile(bst + bc, (1, rep))
        bst_w = jnp.tile(bst, (1, rep))
        blk_c = jnp.sum(jnp.where(bend_w <= c_row, 1.0, 0.0), axis=0, keepdims=True)
        onehot = blk_iota == blk_c
        bst_c = jnp.sum(jnp.where(onehot, bst_w, 0.0), axis=0, keepdims=True)
        r_c = c_row - bst_c
        ohb = jnp.where(onehot, 1.0, 0.0).astype(BF16)
        lp_t = lax.dot_general(lower, mb, _NT, preferred_element_type=F32)
        lp_c = jnp.dot(lp_t.astype(BF16), ohb, preferred_element_type=F32)
        tl_c = jnp.sum(jnp.where(lp_c <= r_c, 1.0, 0.0), axis=0, keepdims=True)
        idx_ref[pl.ds(e, 1), :] = (blk_c * LANES + tl_c).astype(I32)
        a_hi = a.astype(BF16)
        a_lo = (a - a_hi.astype(F32)).astype(BF16)
        at_hi = lax.dot_general(eye, a_hi, _NT, preferred_element_type=F32).astype(BF16)
        at_lo = lax.dot_general(eye, a_lo, _NT, preferred_element_type=F32).astype(BF16)
        g_c = (jnp.dot(at_hi, ohb, preferred_element_type=F32)
               + jnp.dot(at_lo, ohb, preferred_element_type=F32))
        gate_ref[pl.ds(e, 1), :] = jnp.sum(jnp.where(t_iota == tl_c, g_c, 0.0), axis=0, keepdims=True)
        return carry

    lax.fori_loop(0, N_EXPERTS, per_expert, 0, unroll=2)


def _route(aff2d, batch, seq):
    cap = EC_FACTOR * seq // N_EXPERTS
    nblk = seq // LANES
    rows = nblk * N_EXPERTS
    out_spec = pl.BlockSpec((None, N_EXPERTS, cap), lambda b: (b, 0, 0))
    return pl.pallas_call(
        functools.partial(_route_kernel, cap=cap),
        grid=(batch,),
        in_specs=[pl.BlockSpec((rows, LANES), lambda b: (b, 0))],
        out_specs=[out_spec, out_spec, pl.BlockSpec((rows, LANES), lambda b: (b, 0)),
                   pl.BlockSpec((None, N_EXPERTS, nblk), lambda b: (b, 0, 0))],
        out_shape=[jax.ShapeDtypeStruct((batch, N_EXPERTS, cap), I32),
                   jax.ShapeDtypeStruct((batch, N_EXPERTS, cap), F32),
                   jax.ShapeDtypeStruct((batch * rows, LANES), F32),
                   jax.ShapeDtypeStruct((batch, N_EXPERTS, nblk), I32)],
        scratch_shapes=[pltpu.VMEM((N_EXPERTS, LANES), I32), pltpu.VMEM((N_EXPERTS, LANES), F32)],
        compiler_params=_cparams(("parallel",)),
        name="route",
    )(aff2d)


def _ffn_kernel(idx_ref, nxt_ref, gate_ref, h_hbm, wg32_ref, wu32_ref, wd32_ref, y_ref, buf, sem,
                wg_ref, wu_ref, wd_ref, *, seq, tc, nsub):
    seq_id = pl.program_id(1)
    n_seq = pl.num_programs(1)
    step = pl.program_id(0) * n_seq + seq_id
    last_step = pl.num_programs(0) * n_seq - 1
    base = seq_id * seq
    next_base = jnp.where(seq_id + 1 < n_seq, seq_id + 1, 0) * seq

    @pl.when(seq_id == 0)
    def _():
        wg_ref[...] = wg32_ref[...].astype(BF16)
        wu_ref[...] = wu32_ref[...].astype(BF16)
        wd_ref[...] = wd32_ref[...].astype(BF16)

    n_tiles = D_MODEL // LANES

    def row_copy(ids, row0, j, r, slot):
        tok = pl.multiple_of((row0 + ids[0, 0, j * tc + r]) * n_tiles, n_tiles)
        return pltpu.make_async_copy(h_hbm.at[pl.ds(tok, n_tiles), :],
                                     buf.at[slot, pl.ds(r * n_tiles, n_tiles), :], sem.at[slot])

    def issue(ids, row0, j, slot):
        for r in range(tc):
            row_copy(ids, row0, j, r, slot).start()

    diag = lax.broadcasted_iota(I32, (tc, tc), 0) == lax.broadcasted_iota(I32, (tc, tc), 1)
    ones = jnp.ones((tc, LANES), BF16)

    ahead = 2

    @pl.when(step == 0)
    def _():
        for j in range(ahead):
            issue(idx_ref, base, j, j)

    for j in range(nsub):
        slot = j
        for r in range(tc):
            row_copy(idx_ref, base, j, r, slot).wait()
        xs = jnp.concatenate([buf.at[slot][pl.ds(c, tc, stride=n_tiles), :].astype(BF16) for c in range(n_tiles)],
                             axis=1)
        g = jnp.dot(xs, wg_ref[...], preferred_element_type=F32)
        u = jnp.dot(xs, wu_ref[...], preferred_element_type=F32)
        hm = (jax.nn.silu(g) * u).astype(BF16)
        y = jnp.dot(hm, wd_ref[...], preferred_element_type=F32)
        gr = jnp.broadcast_to(gate_ref[0, :, j * tc:(j + 1) * tc], (tc, tc))
        gcol = _split_dot(jnp.where(diag, gr, 0.0), ones)
        y_ref[j * tc:(j + 1) * tc, :] = (y * jnp.tile(gcol, (1, D_MODEL // LANES))).astype(BF16)
        if j + ahead < nsub:
            issue(idx_ref, base, j + ahead, j + ahead)
        else:
            issue(nxt_ref, next_base, j + ahead - nsub, j + ahead - nsub)

    @pl.when(step == last_step)
    def _():
        for j in range(ahead):
            for r in range(tc):
                row_copy(nxt_ref, next_base, j, r, j).wait()


def _ffn(idx, gates, h2d, w_gate, w_up, w_down, layer, seq, tc=256):
    b, ne, cap = idx.shape
    tc = min(tc, cap // 4)
    nsub = cap // tc
    assert nsub > 2
    idx3 = idx.reshape(b * ne, 1, cap)
    gate3 = gates.reshape(b * ne, 1, cap)
    slot = lambda ei, bi: (bi * ne + ei, 0, 0)
    next_slot = lambda ei, bi: (jnp.where(bi + 1 < b, (bi + 1) * ne + ei, jnp.minimum(ei + 1, ne - 1)), 0, 0)
    wspec = lambda shape: pl.BlockSpec((None, None) + shape, lambda ei, bi: (layer, ei, 0, 0))
    return pl.pallas_call(
        functools.partial(_ffn_kernel, seq=seq, tc=tc, nsub=nsub),
        grid=(ne, b),
        in_specs=[
            pl.BlockSpec((1, 1, cap), slot, memory_space=pltpu.SMEM),
            pl.BlockSpec((1, 1, cap), next_slot, memory_space=pltpu.SMEM),
            pl.BlockSpec((1, 1, cap), slot),
            pl.BlockSpec(memory_space=pl.ANY),
            wspec((D_MODEL, EXPERT_FF)), wspec((D_MODEL, EXPERT_FF)), wspec((EXPERT_FF, D_MODEL)),
        ],
        out_specs=pl.BlockSpec((None, None, cap, D_MODEL), lambda ei, bi: (bi, ei, 0, 0)),
        out_shape=jax.ShapeDtypeStruct((b, ne, cap, D_MODEL), BF16),
        scratch_shapes=[pltpu.VMEM((nsub, tc * (D_MODEL // LANES), LANES), F32), pltpu.SemaphoreType.DMA((nsub,)),
                        pltpu.VMEM((D_MODEL, EXPERT_FF), BF16), pltpu.VMEM((D_MODEL, EXPERT_FF), BF16),
                        pltpu.VMEM((EXPERT_FF, D_MODEL), BF16)],
        compiler_params=pltpu.CompilerParams(dimension_semantics=("arbitrary", "arbitrary"),
                                             vmem_limit_bytes=VMEM_LIMIT, disable_bounds_checks=True),
        name="ffn",
    )(idx3, idx3, gate3, h2d, w_gate, w_up, w_down)


_CTM = 256
_CWIN = 64
_CALIGN = 16


def _combine_kernel(cb_ref, x_ref, sp_ref, y_hbm, g_ref, o_ref, ybuf, xbuf, sem, xsem, *,
                    final, tiles_per_seq, nblk, cap):
    i = pl.program_id(0)
    n_tiles = pl.num_programs(0)
    b = i // tiles_per_seq
    slot = i % 2

    def window(tile, e):
        tb = tile // tiles_per_seq
        off = (tb * N_EXPERTS + e) * (nblk + 1) + (tile % tiles_per_seq) * (_CTM // LANES)
        s0 = cb_ref[off]
        s1 = cb_ref[off + _CTM // LANES]
        start = jnp.minimum((s0 // _CALIGN) * _CALIGN, cap - _CWIN)
        return s1, pl.multiple_of(start, _CALIGN)

    def fetch(tile, e, start, buf_slot):
        return pltpu.make_async_copy(y_hbm.at[tile // tiles_per_seq, e, pl.ds(start, _CWIN), :],
                                     ybuf.at[buf_slot, pl.ds(e * _CWIN, _CWIN), :], sem.at[buf_slot])

    def fetch_all(tile, buf_slot):
        for e in range(N_EXPERTS):
            fetch(tile, e, window(tile, e)[1], buf_slot).start()

    @pl.when(i == 0)
    def _():
        fetch_all(i, slot)

    @pl.when(i + 1 < n_tiles)
    def _():
        fetch_all(i + 1, 1 - slot)

    wins = [window(i, e) for e in range(N_EXPERTS)]
    pad = jnp.full((LANES - N_EXPERTS, LANES), -1.0, F32)
    sp_t = jnp.concatenate([jnp.concatenate([sp_ref[hf], pad], axis=0).T for hf in range(_CTM // LANES)],
                           axis=0)
    lane = lax.broadcasted_iota(I32, (1, _CWIN), 1).astype(F32)
    lane2 = lax.broadcasted_iota(I32, (1, LANES), 1)
    per_tile = LANES // _CWIN
    hits = []
    for e0 in range(0, N_EXPERTS, per_tile):
        rel = sp_t[:, e0:e0 + 1] - wins[e0][1].astype(F32)
        for k in range(1, per_tile):
            rel = jnp.where(lane2 < k * _CWIN, rel,
                            sp_t[:, e0 + k:e0 + k + 1] - (wins[e0 + k][1] - k * _CWIN).astype(F32))
        hits.append(jnp.where(rel == lane2.astype(F32), 1.0, 0.0).astype(BF16))
    for e in range(N_EXPERTS):
        fetch(i, e, wins[e][1], slot).wait()
    o_ref[...] = x_ref[...] + jnp.dot(jnp.concatenate(hits, axis=1), ybuf[slot], preferred_element_type=F32)
    for e in range(N_EXPERTS):
        s1, start = wins[e]
        col = sp_t[:, e:e + 1]

        def extra(k, carry, e=e, s1=s1, start=start, col=col):
            lo = start + (k + 1) * _CWIN
            st = pl.multiple_of(jnp.minimum(lo, cap - _CWIN), _CALIGN)
            cp = pltpu.make_async_copy(y_hbm.at[b, e, pl.ds(st, _CWIN), :], xbuf, xsem)
            cp.start()
            cp.wait()
            hit = jnp.where(jnp.logical_and(col - st.astype(F32) == lane, col >= lo.astype(F32)), 1.0, 0.0)
            o_ref[...] += jnp.dot(hit.astype(BF16), xbuf[...], preferred_element_type=F32)
            return carry

        n_extra = jnp.maximum(s1 - start - 1, 0) // _CWIN
        lax.fori_loop(0, n_extra, extra, 0)
    if final:
        x = o_ref[...]
        o_ref[...] = x * _rms_scale(x) * g_ref[...]


def _combine(x2d, spos, cb, y, g_final, final, seq):
    t = x2d.shape[0]
    batch, ne, cap, _ = y.shape
    nblk = seq // LANES
    cb_full = jnp.concatenate([cb, jnp.full((batch, ne, 1), cap, I32)], axis=-1).reshape(-1)
    spb = _CTM // LANES
    return pl.pallas_call(
        functools.partial(_combine_kernel, final=final, tiles_per_seq=seq // _CTM, nblk=nblk, cap=cap),
        grid_spec=pltpu.PrefetchScalarGridSpec(
            num_scalar_prefetch=1,
            grid=(t // _CTM,),
            in_specs=[
                pl.BlockSpec((_CTM, D_MODEL), lambda i, c: (i, 0)),
                pl.BlockSpec((spb, N_EXPERTS, LANES), lambda i, c: (i, 0, 0)),
                pl.BlockSpec(memory_space=pl.ANY),
                pl.BlockSpec((1, D_MODEL), lambda i, c: (0, 0)),
            ],
            out_specs=pl.BlockSpec((_CTM, D_MODEL), lambda i, c: (i, 0)),
            scratch_shapes=[pltpu.VMEM((2, N_EXPERTS * _CWIN, D_MODEL), BF16), pltpu.VMEM((_CWIN, D_MODEL), BF16),
                            pltpu.SemaphoreType.DMA((2,)), pltpu.SemaphoreType.DMA],
        ),
        out_shape=jax.ShapeDtypeStruct((t, D_MODEL), F32),
        compiler_params=_cparams(("arbitrary",)),
        name="combine",
    )(cb_full, x2d, spos, y, g_final.reshape(1, D_MODEL))


_SLAB_Q0 = 512
_SLAB_K0 = _SLAB_Q0 + D_HEADS * LANES
_SLAB_V0 = _SLAB_K0 + D_KV_HEADS * LANES
_ODD_COLS = _SLAB_V0 + D_KV_HEADS * LANES


def _proj_odd_kernel(x_ref, g_ref, wm_ref, cqn_ref, wq_ref, ckvn_ref, wkv_ref, dqn_ref, dkn_ref,
                     ccq_ref, s1cq_ref, s2cq_ref, cck_ref, s1ck_ref, s2ck_ref,
                     cdq_ref, s1dq_ref, s2dq_ref, cdk_ref, s1dk_ref, s2dk_ref,
                     qc_ref, kc_ref, vc_ref, qd_ref, kd_ref, vd_ref, stat_ref, *, steps_per_seq):
    lane = lax.broadcasted_iota(I32, (1, LANES), 1)
    one64 = jnp.where(lane == HEAD_DIM, 1.0, 0.0)
    last_lane = lane == LANES - 1
    half_rope = C_ROPE // 2
    stats = [jnp.zeros((1, LANES), F32), jnp.zeros((1, LANES), F32)]
    ones_mat = jnp.ones((LANES, LANES), BF16)

    def with_norm(val, fill, row, col):
        n2 = jnp.dot((val * val).astype(BF16), ones_mat, preferred_element_type=F32)
        stats[row] = jnp.where(lane == col, jnp.maximum(jnp.max(n2, axis=0, keepdims=True), stats[row]), stats[row])
        return jnp.where(last_lane, fill, val).astype(BF16)

    def head_norm(xg, gn_ref):
        ss = jnp.sum(xg * xg, axis=-1, keepdims=True) * (1.0 / HEAD_DIM)
        return xg * lax.rsqrt(ss + EPS) * gn_ref[...]

    n_chunks = 2
    rows_per = x_ref.shape[0] // n_chunks
    for c in range(n_chunks):
        rows = slice(c * rows_per, (c + 1) * rows_per)
        tab = lambda *refs: [r[rows, :] for r in refs]
        x = x_ref[rows, :]
        y = (x * _rms_scale(x) * g_ref[...]).astype(BF16)
        pm = jnp.dot(y, wm_ref[...], preferred_element_type=F32)
        cq = pm[:, :C_Q_RANK]
        cqn = (cq * _rms_scale(cq) * cqn_ref[...]).astype(BF16)
        qc = jnp.dot(cqn, wq_ref[...], preferred_element_type=F32)
        ckv = pm[:, C_Q_RANK:C_Q_RANK + C_KV_RANK]
        ckvn = (ckv * _rms_scale(ckv) * ckvn_ref[...]).astype(BF16)
        kv = jnp.dot(ckvn, wkv_ref[...], preferred_element_type=F32)
        kr = _rope3(pm[:, C_Q_RANK + C_KV_RANK:_SLAB_Q0], *tab(cck_ref, s1ck_ref, s2ck_ref), half_rope)
        t_cq = tab(ccq_ref, s1cq_ref, s2cq_ref)
        for h in range(C_HEADS):
            sl = slice(h * LANES, (h + 1) * LANES)
            qc_ref[rows, sl] = with_norm(_rope3(qc[:, sl], *t_cq, half_rope), 1.0, 1, h)
            kc_ref[rows, sl] = with_norm(kv[:, sl] + kr, -1.0, 0, h)
            vc_ref[rows, sl] = (kv[:, C_HEADS * LANES + h * LANES:C_HEADS * LANES + (h + 1) * LANES]
                                + one64).astype(BF16)
        t_dq = tab(cdq_ref, s1dq_ref, s2dq_ref)
        t_dk = tab(cdk_ref, s1dk_ref, s2dk_ref)
        for g in range(D_HEADS):
            xg = pm[:, _SLAB_Q0 + g * LANES:_SLAB_Q0 + (g + 1) * LANES]
            qd_ref[rows, g * LANES:(g + 1) * LANES] = with_norm(
                _rope3(head_norm(xg, dqn_ref), *t_dq, HEAD_DIM // 4), 1.0, 1, C_HEADS + g)
        for g in range(D_KV_HEADS):
            sl = slice(g * LANES, (g + 1) * LANES)
            xg = pm[:, _SLAB_K0 + g * LANES:_SLAB_K0 + (g + 1) * LANES]
            kd_ref[rows, sl] = with_norm(_rope3(head_norm(xg, dkn_ref), *t_dk, HEAD_DIM // 4), -1.0, 0, C_HEADS + g)
            vd_ref[rows, sl] = (pm[:, _SLAB_V0 + g * LANES:_SLAB_V0 + (g + 1) * LANES] + one64).astype(BF16)

    new = jnp.concatenate(stats + [jnp.zeros((6, LANES), F32)], axis=0)

    @pl.when(pl.program_id(0) % steps_per_seq == 0)
    def _():
        stat_ref[...] = new

    @pl.when(pl.program_id(0) % steps_per_seq != 0)
    def _():
        stat_ref[...] = jnp.maximum(stat_ref[...], new)


def _slabs(w, n_heads, width, lane_off=0):
    k = w.shape[0]
    w3 = w.reshape(k, n_heads, width)
    w3 = jnp.pad(w3, ((0, 0), (0, 0), (lane_off, LANES - width - lane_off)))
    return w3.reshape(k, n_heads * LANES)


def _axial_tables(row, col, scale):
    half = HEAD_DIM // 2
    cr, s1r, s2r = _rope_tables(row, D_THETA, half, 0, LANES, scale)
    cc, s1c, s2c = _rope_tables(col, D_THETA, half, half, LANES, scale)
    lane = jnp.arange(LANES)[None, :]
    return jnp.where(lane < half, cr, cc), s1r + s1c, s2r + s2c


def _proj_odd(x2d, seq, g_mix, w_in, cq_norm, w_cq_up, ckv_norm, w_ckv_up, dq_norm, dk_norm, tm=512):
    t = x2d.shape[0]
    nblk = seq // tm
    o1 = C_Q_RANK
    o2 = o1 + C_KV_RANK
    o3 = o2 + C_ROPE
    o4 = o3 + D_HEADS * HEAD_DIM
    o5 = o4 + D_KV_HEADS * HEAD_DIM
    wm = jnp.concatenate([
        w_in[:, :o2],
        _slabs(w_in[:, o2:o3], 1, C_ROPE, C_NOPE),
        _slabs(w_in[:, o3:o4], D_HEADS, HEAD_DIM),
        _slabs(w_in[:, o4:o5], D_KV_HEADS, HEAD_DIM),
        _slabs(w_in[:, o5:], D_KV_HEADS, HEAD_DIM),
    ], axis=1).astype(BF16)
    assert wm.shape[1] == _ODD_COLS
    wq = _slabs(w_cq_up, C_HEADS, C_NOPE + C_ROPE).astype(BF16)
    kv3 = w_ckv_up.reshape(C_KV_RANK, C_HEADS, 2 * HEAD_DIM)
    wkv = jnp.concatenate([
        _slabs(kv3[:, :, :C_NOPE].reshape(C_KV_RANK, -1), C_HEADS, C_NOPE),
        _slabs(kv3[:, :, C_NOPE:].reshape(C_KV_RANK, -1), C_HEADS, HEAD_DIM),
    ], axis=1).astype(BF16)
    pad64 = lambda g: jnp.pad(g, (0, LANES - HEAD_DIM)).reshape(1, LANES)

    pos = jnp.arange(seq, dtype=I32)
    row_pos = pos // GRID_W
    col_pos = pos % GRID_W
    c_scale = (C_NOPE + C_ROPE) ** -0.5 * LOG2E
    d_scale = HEAD_DIM ** -0.5 * LOG2E
    tabs = (_rope_tables(pos, ROPE_THETA, C_ROPE, C_NOPE, LANES, c_scale)
            + _rope_tables(pos, ROPE_THETA, C_ROPE, C_NOPE, LANES, 1.0)
            + _axial_tables(row_pos, col_pos, d_scale)
            + _axial_tables(row_pos, col_pos, 1.0))

    row = lambda i: (i, 0)
    full = lambda i: (0, 0)
    tspec = pl.BlockSpec((tm, LANES), lambda i: (i % nblk, 0))
    wide = C_HEADS * LANES
    kvw = D_KV_HEADS * LANES
    return pl.pallas_call(
        functools.partial(_proj_odd_kernel, steps_per_seq=nblk),
        grid=(t // tm,),
        in_specs=[
            pl.BlockSpec((tm, D_MODEL), row),
            pl.BlockSpec((1, D_MODEL), full),
            pl.BlockSpec(wm.shape, full),
            pl.BlockSpec((1, C_Q_RANK), full),
            pl.BlockSpec(wq.shape, full),
            pl.BlockSpec((1, C_KV_RANK), full),
            pl.BlockSpec(wkv.shape, full),
            pl.BlockSpec((1, LANES), full),
            pl.BlockSpec((1, LANES), full),
        ] + [tspec] * 12,
        out_specs=[
            pl.BlockSpec((tm, wide), row), pl.BlockSpec((tm, wide), row), pl.BlockSpec((tm, wide), row),
            pl.BlockSpec((tm, wide), row), pl.BlockSpec((tm, kvw), row), pl.BlockSpec((tm, kvw), row),
            pl.BlockSpec((None, 8, LANES), lambda i: (i // nblk, 0, 0)),
        ],
        out_shape=[
            jax.ShapeDtypeStruct((t, wide), BF16), jax.ShapeDtypeStruct((t, wide), BF16),
            jax.ShapeDtypeStruct((t, wide), BF16), jax.ShapeDtypeStruct((t, wide), BF16),
            jax.ShapeDtypeStruct((t, kvw), BF16), jax.ShapeDtypeStruct((t, kvw), BF16),
            jax.ShapeDtypeStruct((t // seq, 8, LANES), F32),
        ],
        compiler_params=_cparams(("arbitrary",)),
        name="proj_odd",
    )(x2d, g_mix.reshape(1, D_MODEL), wm, cq_norm.reshape(1, -1), wq, ckv_norm.reshape(1, -1), wkv,
      pad64(dq_norm), pad64(dk_norm), *tabs)


def _flash_kernel(q_ref, k_ref, v_ref, o_ref, qs_sc, m_sc, acc_sc, *, group, tq, tk):
    ki = pl.program_id(3)

    @pl.when(ki == 0)
    def _():
        for g in range(group):
            qs_sc[g * tq:(g + 1) * tq, :] = q_ref[:, g * LANES:(g + 1) * LANES]
        m_sc[...] = jnp.full(m_sc.shape, NEG, F32)
        acc_sc[...] = jnp.zeros(acc_sc.shape, F32)

    s = lax.dot_general(qs_sc[...], k_ref[...], _NT, preferred_element_type=F32)
    m_prev = m_sc[...]
    m_new = jnp.maximum(m_prev, jnp.max(s, axis=1, keepdims=True))
    alpha = jnp.exp2(m_prev - m_new)
    p = jnp.exp2(s - jnp.tile(m_new, (1, tk // LANES)))
    acc_sc[...] = alpha * acc_sc[...] + jnp.dot(p.astype(BF16), v_ref[...], preferred_element_type=F32)
    m_sc[...] = m_new

    @pl.when(ki == pl.num_programs(3) - 1)
    def _():
        acc = acc_sc[...]
        o = acc / acc[:, HEAD_DIM:HEAD_DIM + 1]
        for g in range(group):
            o_ref[:, g * LANES:(g + 1) * LANES] = o[g * tq:(g + 1) * tq].astype(BF16)


_FLASH_CHUNK = 2048
_V_ROWS = 80


def _flash_bounded_kernel(bound_ref, q_ref, k_ref, v_ref, o_ref, qs_sc, acc_sc, *, group, tq):
    ki = pl.program_id(3)
    head = pl.program_id(0) * pl.num_programs(1) + pl.program_id(1)

    @pl.when(ki == 0)
    def _():
        fix = jnp.where(lax.broadcasted_iota(I32, (1, LANES), 1) == LANES - 1, bound_ref[head], 1.0)
        for g in range(group):
            qs_sc[g * tq:(g + 1) * tq, :] = (q_ref[:, g * LANES:(g + 1) * LANES].astype(F32) * fix).astype(BF16)
        acc_sc[...] = jnp.zeros(acc_sc.shape, F32)

    chunk = min(_FLASH_CHUNK, k_ref.shape[0])
    n_chunks = k_ref.shape[0] // chunk
    qs = qs_sc[...]

    def scores(c):
        return lax.dot_general(k_ref[c * chunk:(c + 1) * chunk, :], qs, _NT, preferred_element_type=F32)

    def values(c, s_t):
        return lax.dot_general(v_ref[c * chunk:(c + 1) * chunk, :_V_ROWS], jnp.exp2(s_t).astype(BF16),
                               (((0,), (0,)), ((), ())), preferred_element_type=F32)

    acc = acc_sc[...]
    s_prev = scores(0)
    for c in range(1, n_chunks):
        s_next = scores(c)
        acc = acc + values(c - 1, s_prev)
        s_prev = s_next
    acc_sc[...] = acc + values(n_chunks - 1, s_prev)

    @pl.when(ki == pl.num_programs(3) - 1)
    def _():
        acc = acc_sc[...]
        o_t = acc / acc[HEAD_DIM:HEAD_DIM + 1, :]
        o = jnp.concatenate([o_t, jnp.zeros((LANES - _V_ROWS, o_t.shape[1]), F32)], axis=0).T
        for g in range(group):
            o_ref[:, g * LANES:(g + 1) * LANES] = o[g * tq:(g + 1) * tq].astype(BF16)


def _flash(q, k, v, group, nk, bounded, rows=1024, tk=512, tk_bounded=8192):
    b, s, qw = q.shape
    hk = k.shape[2] // LANES
    tq = rows // group
    tk = min(tk, s)
    tkb = min(tk_bounded, s)
    out_shape = jax.ShapeDtypeStruct((b, s, qw), BF16)
    sem = ("parallel", "parallel", "parallel", "arbitrary")

    def running_max(q, k, v, nk):
        qspec = pl.BlockSpec((None, tq, group * LANES), lambda bi, hi, qi, ki: (bi, qi, hi))
        kspec = pl.BlockSpec((None, tk, LANES), lambda bi, hi, qi, ki: (bi, ki, hi))
        return pl.pallas_call(
            functools.partial(_flash_kernel, group=group, tq=tq, tk=tk),
            grid=(b, hk, s // tq, s // tk),
            in_specs=[qspec, kspec, kspec],
            out_specs=qspec,
            out_shape=out_shape,
            scratch_shapes=[pltpu.VMEM((rows, LANES), BF16), pltpu.VMEM((rows, LANES), F32),
                            pltpu.VMEM((rows, LANES), F32)],
            compiler_params=_cparams(sem),
            name="flash",
        )(q, k, v)

    def bound(q, k, v, nk):
        qspec = pl.BlockSpec((None, tq, group * LANES), lambda bi, hi, qi, ki, nkr: (bi, qi, hi))
        kspec = pl.BlockSpec((None, tkb, LANES), lambda bi, hi, qi, ki, nkr: (bi, ki, hi))
        return pl.pallas_call(
            functools.partial(_flash_bounded_kernel, group=group, tq=tq),
            grid_spec=pltpu.PrefetchScalarGridSpec(
                num_scalar_prefetch=1,
                grid=(b, hk, s // tq, s // tkb),
                in_specs=[qspec, kspec, kspec],
                out_specs=qspec,
                scratch_shapes=[pltpu.VMEM((rows, LANES), BF16), pltpu.VMEM((_V_ROWS, rows), F32)],
            ),
            out_shape=out_shape,
            compiler_params=_cparams(sem),
            name="flash_bounded",
        )(nk.reshape(-1), q, k, v)

    return lax.cond(bounded, bound, running_max, q, k, v, nk)


def _moe(x1, h2, aff, batch, seq, w_gate, w_up, w_down, layer, g_final, final):
    idx, gates, spos, cb = _route(aff.reshape(-1, LANES), batch, seq)
    y = _ffn(idx, gates, h2, w_gate, w_up, w_down, layer, seq)
    return _combine(x1, spos.reshape(-1, N_EXPERTS, LANES), cb, y, g_final, final, seq)


def kernel(x, norm_mix, norm_ffn, even_w_in, even_gmlp_norm, even_w_spatial, even_b_spatial, even_w_out,
           odd_w_in, odd_cq_norm, odd_w_cq_up, odd_ckv_norm, odd_w_ckv_up, odd_dq_norm, odd_dk_norm, odd_w_out,
           moe_w_router, moe_w_gate, moe_w_up, moe_w_down, final_norm):
    b, s, d = x.shape
    depth = norm_mix.shape[0]
    x2d = x.reshape(b * s, d)
    for i in range(depth):
        j = i // 2
        last = i == depth - 1
        if i % 2 == 0:
            q, k, v, go = _proj_even(x2d, s, norm_mix[i], even_w_in[j], even_gmlp_norm[j], even_w_spatial[j],
                                     even_b_spatial[j])
            a = _dilated(q.reshape(b, s, A_WIDTH), k.reshape(b, s, A_WIDTH), v.reshape(b, s, A_WIDTH))
            x1, h2, aff = _outproj(x2d, a.reshape(b * s, A_WIDTH), go, even_w_out[j][:A_WIDTH],
                                   even_w_out[j][A_WIDTH:], norm_ffn[i], moe_w_router[i])
        else:
            qc, kc, vc, qd, kd, vd, stat = _proj_odd(x2d, s, norm_mix[i], odd_w_in[j], odd_cq_norm[j],
                                                     odd_w_cq_up[j], odd_ckv_norm[j], odd_w_ckv_up[j],
                                                     odd_dq_norm[j], odd_dk_norm[j])
            grp = D_HEADS // D_KV_HEADS
            k2_c, k2_d = stat[:, 0, :C_HEADS], stat[:, 0, C_HEADS:C_HEADS + D_KV_HEADS]
            q2_c = stat[:, 1, :C_HEADS]
            q2_d = jnp.max(stat[:, 1, C_HEADS:C_HEADS + D_HEADS].reshape(b, D_KV_HEADS, grp), axis=-1)
            bound_c = jnp.sqrt(q2_c * k2_c) * _NORM_MARGIN ** 2
            bound_d = jnp.sqrt(q2_d * k2_d) * _NORM_MARGIN ** 2
            bounded = jnp.maximum(jnp.max(bound_c), jnp.max(bound_d)) <= _MAX_SCORE_BOUND
            r3 = lambda z: z.reshape(b, s, -1)
            oc = _flash(r3(qc), r3(kc), r3(vc), 1, bound_c, bounded)
            od = _flash(r3(qd), r3(kd), r3(vd), grp, bound_d, bounded)
            cw = C_HEADS * HEAD_DIM
            x1, h2, aff = _outproj(x2d, oc.reshape(b * s, -1), od.reshape(b * s, -1),
                                   _slabs(odd_w_out[j][:cw].T, C_HEADS, HEAD_DIM).T,
                                   _slabs(odd_w_out[j][cw:].T, D_HEADS, HEAD_DIM).T,
                                   norm_ffn[i], moe_w_router[i])
        x2d = _moe(x1, h2, aff, b, s, moe_w_gate, moe_w_up, moe_w_down, i, final_norm, last)
    return x2d.reshape(b, s, d)
```

```python
import functools
import math

import jax
import jax.numpy as jnp
from jax import lax
from jax.experimental import pallas as pl
from jax.experimental.pallas import tpu as pltpu

F32 = jnp.float32
BF16 = jnp.bfloat16
I32 = jnp.int32

EPS = 1e-6
NEG = -1e30
LOG2E = 1.4426950408889634

D_MODEL = 1024
HEAD_DIM = 64
ROPE_THETA = 500000.0
ROT_DIM = 16
GRID_W = 64
A_HEADS = 12
A_WIDTH = 768
A_DILATIONS = (1, 4, 16)
A_RADIUS = 64
B_WIDTH = 256
B_GROUPS = 4
B_CHUNK = 128
C_HEADS = 8
C_Q_RANK = 256
C_KV_RANK = 128
C_NOPE = 64
C_ROPE = 32
D_HEADS = 8
D_KV_HEADS = 2
D_THETA = 10000.0
N_EXPERTS = 16
EC_FACTOR = 2
EXPERT_FF = 512

_NORM_MARGIN = 1.01
_MAX_SCORE_BOUND = 55.0

LANES = 128
VMEM_LIMIT = 48 * 1024 * 1024
_DILATED_VMEM_LIMIT = 56 * 1024 * 1024

_NT = (((1,), (1,)), ((), ()))


def _cparams(sem):
    return pltpu.CompilerParams(dimension_semantics=sem, vmem_limit_bytes=VMEM_LIMIT)


def _rms_scale(x):
    return lax.rsqrt(jnp.mean(x * x, axis=-1, keepdims=True) + EPS)


def _rope3(a, c, s1, s2, shift):
    return a * c + pltpu.roll(a, LANES - shift, 1) * s1 + pltpu.roll(a, shift, 1) * s2


def _split_dot(x, w_bf16):
    hi = x.astype(BF16)
    lo = (x - hi.astype(F32)).astype(BF16)
    return (jnp.dot(hi, w_bf16, preferred_element_type=F32)
            + jnp.dot(lo, w_bf16, preferred_element_type=F32))


def _proj_even_kernel(x_ref, g_ref, w_ref, cq_ref, s1q_ref, s2q_ref, ck_ref, s1k_ref, s2k_ref,
                      gn_ref, gmat_ref, ws_ref, bs_ref,
                      q_ref, k_ref, v_ref, go_ref):
    x = x_ref[...]
    y = (x * _rms_scale(x) * g_ref[...]).astype(BF16)
    tm = x.shape[0]

    aq = jnp.dot(y, w_ref[:, 0:A_WIDTH], preferred_element_type=F32)
    tq = (cq_ref[...], s1q_ref[...], s2q_ref[...])
    for j in range(A_WIDTH // LANES):
        sl = slice(j * LANES, (j + 1) * LANES)
        q_ref[:, sl] = _rope3(aq[:, sl], *tq, ROT_DIM // 2)
    ak = jnp.dot(y, w_ref[:, A_WIDTH:2 * A_WIDTH], preferred_element_type=F32)
    tk = (ck_ref[...], s1k_ref[...], s2k_ref[...])
    for j in range(A_WIDTH // LANES):
        sl = slice(j * LANES, (j + 1) * LANES)
        k_ref[:, sl] = _rope3(ak[:, sl], *tk, ROT_DIM // 2)
    v_ref[...] = jnp.dot(y, w_ref[:, 2 * A_WIDTH:3 * A_WIDTH], preferred_element_type=F32)

    z = jnp.dot(y, w_ref[:, 3 * A_WIDTH:3 * A_WIDTH + 2 * B_WIDTH], preferred_element_type=F32)
    ge = jax.nn.gelu(z)
    u = ge[:, :B_WIDTH]
    vv = ge[:, B_WIDTH:]
    ss = _split_dot(vv * vv, gmat_ref[...])
    vn = (vv * lax.rsqrt(ss + EPS) * gn_ref[...]).astype(BF16)
    grp = lax.broadcasted_iota(I32, (B_CHUNK, B_WIDTH), 1) // (B_WIDTH // B_GROUPS)
    for c in range(tm // B_CHUNK):
        rows = slice(c * B_CHUNK, (c + 1) * B_CHUNK)
        vc = vn[rows]
        mg = [jnp.dot(ws_ref[g], vc, preferred_element_type=F32) for g in range(B_GROUPS)]
        mixed = jnp.where(grp == 0, mg[0], jnp.where(grp == 1, mg[1], jnp.where(grp == 2, mg[2], mg[3])))
        go_ref[rows, :] = (u[rows] * (mixed + bs_ref[...])).astype(BF16)


def _rope_tables(pos, theta, r, lane_off, period, scale):
    half = r // 2
    inv = jnp.power(jnp.float32(theta), -jnp.arange(half, dtype=F32) * (2.0 / r))
    ang = pos.astype(F32)[:, None] * inv[None, :]
    cos, sin = jnp.cos(ang) * scale, jnp.sin(ang) * scale
    n = pos.shape[0]
    zero = jnp.zeros((n, half), F32)

    def place(lo, hi, fill):
        parts = [jnp.full((n, lane_off), fill, F32), lo, hi, jnp.full((n, period - lane_off - r), fill, F32)]
        return jnp.tile(jnp.concatenate([p for p in parts if p.shape[1]], axis=1), (1, LANES // period))

    return place(cos, cos, scale), place(-sin, zero, 0.0), place(zero, sin, 0.0)


def _proj_even(x2d, seq, g_mix, w_in, gmlp_norm, w_s, b_s, tm=512):
    t = x2d.shape[0]
    nblk = seq // tm
    pos = jnp.arange(seq, dtype=I32)
    qscale = HEAD_DIM ** -0.5 * LOG2E
    cq, s1q, s2q = _rope_tables(pos, ROPE_THETA, ROT_DIM, 0, HEAD_DIM, qscale)
    ck, s1k, s2k = _rope_tables(pos, ROPE_THETA, ROT_DIM, 0, HEAD_DIM, 1.0)
    gdim = B_WIDTH // B_GROUPS
    gid = jnp.arange(B_WIDTH) // gdim
    gmat = jnp.where(gid[:, None] == gid[None, :], 1.0 / gdim, 0.0).astype(BF16)
    bias = jnp.repeat(b_s.T, gdim, axis=1)
    row = lambda i: (i, 0)
    tab = lambda i: (i % nblk, 0)
    full = lambda i: (0, 0)
    tspec = pl.BlockSpec((tm, LANES), tab)
    return pl.pallas_call(
        _proj_even_kernel,
        grid=(t // tm,),
        in_specs=[
            pl.BlockSpec((tm, D_MODEL), row),
            pl.BlockSpec((1, D_MODEL), full),
            pl.BlockSpec(w_in.shape, full),
            tspec, tspec, tspec, tspec, tspec, tspec,
            pl.BlockSpec((1, B_WIDTH), full),
            pl.BlockSpec((B_WIDTH, B_WIDTH), full),
            pl.BlockSpec((B_GROUPS, B_CHUNK, B_CHUNK), lambda i: (0, 0, 0)),
            pl.BlockSpec((B_CHUNK, B_WIDTH), full),
        ],
        out_specs=[
            pl.BlockSpec((tm, A_WIDTH), row),
            pl.BlockSpec((tm, A_WIDTH), row),
            pl.BlockSpec((tm, A_WIDTH), row),
            pl.BlockSpec((tm, B_WIDTH), row),
        ],
        out_shape=[
            jax.ShapeDtypeStruct((t, A_WIDTH), F32),
            jax.ShapeDtypeStruct((t, A_WIDTH), F32),
            jax.ShapeDtypeStruct((t, A_WIDTH), F32),
            jax.ShapeDtypeStruct((t, B_WIDTH), BF16),
        ],
        compiler_params=_cparams(("parallel",)),
        name="proj_even",
    )(x2d, g_mix.reshape(1, D_MODEL), w_in.astype(BF16), cq, s1q, s2q, ck, s1k, s2k,
      gmlp_norm.reshape(1, B_WIDTH), gmat, w_s.astype(BF16), bias)


_TQ = 128
_TK = _TQ + 2 * A_RADIUS
_NORM_ROWS = 512


def _dilated_kernel(q_ref, k_ref, v_ref, o_ref, m_sc, l_sc, bias_sc, *, seq):
    half0 = lax.broadcasted_iota(I32, (1, LANES), 1) < HEAD_DIM
    row_head = lax.broadcasted_iota(I32, (LANES, LANES), 0) // HEAD_DIM
    pick = [jnp.where(row_head == h, 1.0, 0.0).astype(BF16) for h in range(2)]
    n_pat = len(A_DILATIONS)
    zero_row = jnp.zeros((1, LANES), F32)

    def max_head_sq(ref):
        def body(c, mx):
            x = ref[pl.ds(pl.multiple_of(c * _NORM_ROWS, _NORM_ROWS), _NORM_ROWS), :]
            sq = (x * x).astype(BF16)
            return tuple(jnp.maximum(mx[h], jnp.max(jnp.dot(sq, pick[h], preferred_element_type=F32),
                                                    axis=0, keepdims=True)) for h in range(2))
        return lax.fori_loop(0, seq // _NORM_ROWS, body, (zero_row, zero_row))

    max_q2 = max_head_sq(q_ref)
    max_k2 = max_head_sq(k_ref)
    bound = [jnp.sqrt(max_q2[h] * max_k2[h]) * (_NORM_MARGIN * _NORM_MARGIN) for h in range(2)]
    worst = jnp.max(jnp.maximum(bound[0], bound[1]))

    diff = lax.broadcasted_iota(I32, (_TQ, _TK), 1) - lax.broadcasted_iota(I32, (_TQ, _TK), 0)
    for case in range(3):
        band = jnp.where(jnp.abs(diff - case * A_RADIUS) <= A_RADIUS, 0.0, NEG)
        bias_sc[6 + case] = band
        for h in range(2):
            bias_sc[3 * h + case] = band - jnp.tile(bound[h], (1, _TK // LANES))
    one_bf16 = jnp.ones((), BF16)

    def run(bounded):
        for pi, d in enumerate(A_DILATIONS):
            cls_len = seq // d
            tpc = cls_len // _TQ

            def tile(qrows, kb, vb, case, pi=pi):
                q = q_ref[qrows, :]
                parts = []
                for h in range(2):
                    qh = jnp.where(half0 if h == 0 else jnp.logical_not(half0), q, 0.0).astype(BF16)
                    s = lax.dot_general(qh, kb, _NT, preferred_element_type=F32)
                    if bounded:
                        mt = None
                        p = jnp.exp2(s + bias_sc[3 * h + case]).astype(BF16)
                    else:
                        s = s + bias_sc[6 + case]
                        mt = jnp.max(s, axis=-1, keepdims=True)
                        p = jnp.exp2(s - mt).astype(BF16)
                    vh = jnp.where(half0 if h == 0 else jnp.logical_not(half0), vb, one_bf16)
                    parts.append((mt, jnp.dot(p, vh, preferred_element_type=F32)))
                ot = jnp.where(half0, parts[0][1], parts[1][1])
                lt = pltpu.roll(jnp.where(half0, parts[1][1], parts[0][1]), HEAD_DIM, 1)
                if bounded:
                    if pi > 0:
                        lt = l_sc[qrows, :] + lt
                        ot = o_ref[qrows, :] + ot
                else:
                    mt = jnp.where(half0, parts[0][0], parts[1][0])
                    if pi > 0:
                        mp = m_sc[qrows, :]
                        mn = jnp.maximum(mp, mt)
                        a = jnp.exp2(mp - mn)
                        b = jnp.exp2(mt - mn)
                        lt = a * l_sc[qrows, :] + b * lt
                        ot = a * o_ref[qrows, :] + b * ot
                        mt = mn
                    if pi < n_pat - 1:
                        m_sc[qrows, :] = mt
                if pi == n_pat - 1:
                    o_ref[qrows, :] = ot / lt
                else:
                    l_sc[qrows, :] = lt
                    o_ref[qrows, :] = ot

            def window(l0, cls_len=cls_len):
                if isinstance(l0, int):
                    kst = min(max(l0 - A_RADIUS, 0), cls_len - _TK)
                else:
                    kst = jnp.clip(l0 - A_RADIUS, 0, cls_len - _TK)
                return kst, (l0 - kst) // A_RADIUS

            if d < 8:

                def body(j, carry, d=d, tpc=tpc):
                    i = j // tpc
                    l0 = (j % tpc) * _TQ
                    kst, case = window(l0)
                    if d == 1:
                        qrows = pl.ds(pl.multiple_of(l0, _TQ), _TQ)
                        krows = pl.ds(pl.multiple_of(kst, A_RADIUS), _TK)
                    else:
                        qrows = pl.ds(l0 * d + i, _TQ, stride=d)
                        krows = pl.ds(kst * d + i, _TK, stride=d)
                    tile(qrows, k_ref[krows, :].astype(BF16), v_ref[krows, :].astype(BF16), case)
                    return carry

                lax.fori_loop(0, seq // _TQ, body, 0, unroll=16)
            else:

                def body(i, carry, d=d, cls_len=cls_len, tpc=tpc):
                    cls = pl.ds(i, cls_len, stride=d)
                    kc = k_ref[cls, :].astype(BF16)
                    vc = v_ref[cls, :].astype(BF16)
                    for n in range(tpc):
                        kst, case = window(n * _TQ)
                        tile(pl.ds(n * _TQ * d + i, _TQ, stride=d), kc[kst:kst + _TK], vc[kst:kst + _TK], case)
                    return carry

                lax.fori_loop(0, d, body, 0, unroll=4)

    @pl.when(worst <= _MAX_SCORE_BOUND)
    def _():
        run(True)

    @pl.when(jnp.logical_not(worst <= _MAX_SCORE_BOUND))
    def _():
        run(False)


def _dilated(q, k, v):
    b, s, w = q.shape
    spec = pl.BlockSpec((None, s, LANES), lambda bi, hi: (bi, 0, hi))
    return pl.pallas_call(
        functools.partial(_dilated_kernel, seq=s),
        grid=(b, w // LANES),
        in_specs=[spec, spec, spec],
        out_specs=spec,
        out_shape=jax.ShapeDtypeStruct((b, s, w), F32),
        scratch_shapes=[pltpu.VMEM((s, LANES), F32), pltpu.VMEM((s, LANES), F32),
                        pltpu.VMEM((9, _TQ, _TK), F32)],
        compiler_params=pltpu.CompilerParams(dimension_semantics=("parallel", "parallel"),
                                             vmem_limit_bytes=_DILATED_VMEM_LIMIT),
        name="dilated",
    )(q, k, v)


def _outproj_kernel(x_ref, a_ref, b_ref, wa_ref, wb_ref, gf_ref, wr_ref, x1_ref, h2_ref, aff_ref):
    x1 = (x_ref[...]
          + jnp.dot(a_ref[...].astype(BF16), wa_ref[...], preferred_element_type=F32)
          + jnp.dot(b_ref[...].astype(BF16), wb_ref[...], preferred_element_type=F32))
    x1_ref[...] = x1
    h2 = x1 * _rms_scale(x1) * gf_ref[...]
    n_tiles = D_MODEL // LANES
    for j in range(n_tiles):
        h2_ref[pl.ds(j, h2.shape[0], stride=n_tiles), :] = h2[:, j * LANES:(j + 1) * LANES]
    hi = h2.astype(BF16)
    lo = (h2 - hi.astype(F32)).astype(BF16)
    both = jnp.dot(hi, wr_ref[...], preferred_element_type=F32)
    lg = (both[:, :LANES] + both[:, LANES:]
          + jnp.dot(lo, wr_ref[:, :LANES], preferred_element_type=F32))
    valid = lax.broadcasted_iota(I32, lg.shape, 1) < N_EXPERTS
    lg = jnp.where(valid, lg, NEG)
    e = jnp.exp(lg - jnp.max(lg, axis=-1, keepdims=True))
    aff = e / jnp.sum(e, axis=-1, keepdims=True)
    aff_t = aff.T
    for j in range(aff.shape[0] // LANES):
        aff_ref[j] = aff_t[:N_EXPERTS, j * LANES:(j + 1) * LANES]


def _outproj(x2d, a, b, wa, wb, g_ffn, w_router, tm=512):
    t = x2d.shape[0]
    wr = jnp.pad(w_router, ((0, 0), (0, LANES - N_EXPERTS)))
    wr_hi = wr.astype(BF16)
    wr_lo = (wr - wr_hi.astype(F32)).astype(BF16)
    wr2 = jnp.concatenate([wr_hi, wr_lo], axis=1)
    row = lambda i: (i, 0)
    full = lambda i: (0, 0)
    return pl.pallas_call(
        _outproj_kernel,
        grid=(t // tm,),
        in_specs=[
            pl.BlockSpec((tm, D_MODEL), row),
            pl.BlockSpec((tm, a.shape[1]), row),
            pl.BlockSpec((tm, b.shape[1]), row),
            pl.BlockSpec(wa.shape, full),
            pl.BlockSpec(wb.shape, full),
            pl.BlockSpec((1, D_MODEL), full),
            pl.BlockSpec((D_MODEL, 2 * LANES), full),
        ],
        out_specs=[
            pl.BlockSpec((tm, D_MODEL), row),
            pl.BlockSpec((tm * (D_MODEL // LANES), LANES), row),
            pl.BlockSpec((tm // LANES, N_EXPERTS, LANES), lambda i: (i, 0, 0)),
        ],
        out_shape=[
            jax.ShapeDtypeStruct((t, D_MODEL), F32),
            jax.ShapeDtypeStruct((t * (D_MODEL // LANES), LANES), F32),
            jax.ShapeDtypeStruct((t // LANES, N_EXPERTS, LANES), F32),
        ],
        compiler_params=_cparams(("parallel",)),
        name="outproj",
    )(x2d, a, b, wa.astype(BF16), wb.astype(BF16), g_ffn.reshape(1, D_MODEL), wr2)


def _route_kernel(aff_ref, idx_ref, gate_ref, spos_ref, cb_ref, thr_sc, need_sc, *, cap):
    nblk = aff_ref.shape[0] // N_EXPERTS
    bits = pltpu.bitcast(aff_ref[...], I32).reshape(nblk, N_EXPERTS, LANES)

    def count(pred):
        return jnp.sum(jnp.sum(jnp.where(pred, 1.0, 0.0), axis=0), axis=1, keepdims=True)

    def search(it, thr):
        cand = thr | jnp.left_shift(jnp.int32(1), 30 - it)
        return jnp.where(count(bits >= cand[None]) >= cap, cand, thr)

    thr = lax.fori_loop(0, 31, search, jnp.zeros((N_EXPERTS, 1), I32))
    need = cap - count(bits > thr[None])
    thr_sc[...] = jnp.broadcast_to(thr, (N_EXPERTS, LANES))
    need_sc[...] = jnp.broadcast_to(need, (N_EXPERTS, LANES))

    ri = lax.broadcasted_iota(I32, (LANES, LANES), 0)
    ci = lax.broadcasted_iota(I32, (LANES, LANES), 1)
    upper = jnp.where(ri <= ci, 1.0, 0.0).astype(BF16)
    lower = jnp.where(ci <= ri, 1.0, 0.0).astype(BF16)
    eye = jnp.where(ri == ci, 1.0, 0.0).astype(BF16)
    ones = jnp.ones((LANES, LANES), BF16)
    bi = lax.broadcasted_iota(I32, (nblk, nblk), 0)
    bj = lax.broadcasted_iota(I32, (nblk, nblk), 1)
    strict = jnp.where(bj < bi, 1.0, 0.0).astype(BF16)
    before = jnp.where(bi < bj, 1.0, 0.0).astype(BF16)
    mean_rows = jnp.full((8, LANES), 1.0 / LANES, BF16)
    c_row = lax.broadcasted_iota(I32, (1, cap), 1).astype(F32)
    blk_iota = lax.broadcasted_iota(I32, (nblk, cap), 0).astype(F32)
    t_iota = lax.broadcasted_iota(I32, (LANES, cap), 0).astype(F32)
    rep = cap // LANES

    def cums(mask_bf16):
        lp = jnp.dot(mask_bf16, upper, preferred_element_type=F32)
        bc = jnp.dot(mask_bf16, ones, preferred_element_type=F32)
        bst = jnp.dot(strict, bc.astype(BF16), preferred_element_type=F32)
        return lp, bc, bst

    def per_expert(e, carry):
        a = aff_ref[pl.ds(e, nblk, stride=N_EXPERTS), :]
        ab = pltpu.bitcast(a, I32)
        thr_e = thr_sc[pl.ds(e, 1), :]
        need_e = need_sc[pl.ds(e, 1), :]
        gt = ab > thr_e
        eq = ab == thr_e
        eqf = jnp.where(eq, 1.0, 0.0)
        lp_q, _, bst_q = cums(eqf.astype(BF16))
        sel = jnp.logical_or(gt, jnp.logical_and(eq, bst_q + lp_q - eqf < need_e))
        mb = jnp.where(sel, 1.0, 0.0).astype(BF16)
        lp, bc, bst = cums(mb)
        spos_ref[pl.ds(e, nblk, stride=N_EXPERTS), :] = jnp.where(sel, bst + lp - 1.0, -1.0)
        bc_row = lax.dot_general(mean_rows, bc.astype(BF16), _NT, preferred_element_type=F32)
        cb_ref[pl.ds(e, 1), :] = jnp.dot(bc_row.astype(BF16), before, preferred_element_type=F32)[:1].astype(I32)
        bend_w = jnp.tile(bst + bc, (1, rep))
        bst_w = jnp.tile(bst, (1, rep))
        blk_c = jnp.sum(jnp.where(bend_w <= c_row, 1.0, 0.0), axis=0, keepdims=True)
        onehot = blk_iota == blk_c
        bst_c = jnp.sum(jnp.where(onehot, bst_w, 0.0), axis=0, keepdims=True)
        r_c = c_row - bst_c
        ohb = jnp.where(onehot, 1.0, 0.0).astype(BF16)
        lp_t = lax.dot_general(lower, mb, _NT, preferred_element_type=F32)
        lp_c = jnp.dot(lp_t.astype(BF16), ohb, preferred_element_type=F32)
        tl_c = jnp.sum(jnp.where(lp_c <= r_c, 1.0, 0.0), axis=0, keepdims=True)
        idx_ref[pl.ds(e, 1), :] = (blk_c * LANES + tl_c).astype(I32)
        a_hi = a.astype(BF16)
        a_lo = (a - a_hi.astype(F32)).astype(BF16)
        at_hi = lax.dot_general(eye, a_hi, _NT, preferred_element_type=F32).astype(BF16)
        at_lo = lax.dot_general(eye, a_lo, _NT, preferred_element_type=F32).astype(BF16)
        g_c = (jnp.dot(at_hi, ohb, preferred_element_type=F32)
               + jnp.dot(at_lo, ohb, preferred_element_type=F32))
        gate_ref[pl.ds(e, 1), :] = jnp.sum(jnp.where(t_iota == tl_c, g_c, 0.0), axis=0, keepdims=True)
        return carry

    lax.fori_loop(0, N_EXPERTS, per_expert, 0, unroll=2)


def _route(aff2d, batch, seq):
    cap = EC_FACTOR * seq // N_EXPERTS
    nblk = seq // LANES
    rows = nblk * N_EXPERTS
    out_spec = pl.BlockSpec((None, N_EXPERTS, cap), lambda b: (b, 0, 0))
    return pl.pallas_call(
        functools.partial(_route_kernel, cap=cap),
        grid=(batch,),
        in_specs=[pl.BlockSpec((rows, LANES), lambda b: (b, 0))],
        out_specs=[out_spec, out_spec, pl.BlockSpec((rows, LANES), lambda b: (b, 0)),
                   pl.BlockSpec((None, N_EXPERTS, nblk), lambda b: (b, 0, 0))],
        out_shape=[jax.ShapeDtypeStruct((batch, N_EXPERTS, cap), I32),
                   jax.ShapeDtypeStruct((batch, N_EXPERTS, cap), F32),
                   jax.ShapeDtypeStruct((batch * rows, LANES), F32),
                   jax.ShapeDtypeStruct((batch, N_EXPERTS, nblk), I32)],
        scratch_shapes=[pltpu.VMEM((N_EXPERTS, LANES), I32), pltpu.VMEM((N_EXPERTS, LANES), F32)],
        compiler_params=_cparams(("parallel",)),
        name="route",
    )(aff2d)


def _ffn_kernel(idx_ref, nxt_ref, gate_ref, h_hbm, wg32_ref, wu32_ref, wd32_ref, y_ref, buf, sem,
                wg_ref, wu_ref, wd_ref, *, seq, tc, nsub):
    seq_id = pl.program_id(1)
    n_seq = pl.num_programs(1)
    step = pl.program_id(0) * n_seq + seq_id
    last_step = pl.num_programs(0) * n_seq - 1
    base = seq_id * seq
    next_base = jnp.where(seq_id + 1 < n_seq, seq_id + 1, 0) * seq

    @pl.when(seq_id == 0)
    def _():
        wg_ref[...] = wg32_ref[...].astype(BF16)
        wu_ref[...] = wu32_ref[...].astype(BF16)
        wd_ref[...] = wd32_ref[...].astype(BF16)

    n_tiles = D_MODEL // LANES

    def row_copy(ids, row0, j, r, slot):
        tok = pl.multiple_of((row0 + ids[0, 0, j * tc + r]) * n_tiles, n_tiles)
        return pltpu.make_async_copy(h_hbm.at[pl.ds(tok, n_tiles), :],
                                     buf.at[slot, pl.ds(r * n_tiles, n_tiles), :], sem.at[slot])

    def issue(ids, row0, j, slot):
        for r in range(tc):
            row_copy(ids, row0, j, r, slot).start()

    diag = lax.broadcasted_iota(I32, (tc, tc), 0) == lax.broadcasted_iota(I32, (tc, tc), 1)
    ones = jnp.ones((tc, LANES), BF16)

    ahead = 2

    @pl.when(step == 0)
    def _():
        for j in range(ahead):
            issue(idx_ref, base, j, j)

    for j in range(nsub):
        slot = j
        for r in range(tc):
            row_copy(idx_ref, base, j, r, slot).wait()
        xs = jnp.concatenate([buf.at[slot][pl.ds(c, tc, stride=n_tiles), :].astype(BF16) for c in range(n_tiles)],
                             axis=1)
        g = jnp.dot(xs, wg_ref[...], preferred_element_type=F32)
        u = jnp.dot(xs, wu_ref[...], preferred_element_type=F32)
        hm = (jax.nn.silu(g) * u).astype(BF16)
        y = jnp.dot(hm, wd_ref[...], preferred_element_type=F32)
        gr = jnp.broadcast_to(gate_ref[0, :, j * tc:(j + 1) * tc], (tc, tc))
        gcol = _split_dot(jnp.where(diag, gr, 0.0), ones)
        y_ref[j * tc:(j + 1) * tc, :] = (y * jnp.tile(gcol, (1, D_MODEL // LANES))).astype(BF16)
        if j + ahead < nsub:
            issue(idx_ref, base, j + ahead, j + ahead)
        else:
            issue(nxt_ref, next_base, j + ahead - nsub, j + ahead - nsub)

    @pl.when(step == last_step)
    def _():
        for j in range(ahead):
            for r in range(tc):
                row_copy(nxt_ref, next_base, j, r, j).wait()


def _ffn(idx, gates, h2d, w_gate, w_up, w_down, layer, seq, tc=256):
    b, ne, cap = idx.shape
    tc = min(tc, cap // 4)
    nsub = cap // tc
    assert nsub > 2
    idx3 = idx.reshape(b * ne, 1, cap)
    gate3 = gates.reshape(b * ne, 1, cap)
    slot = lambda ei, bi: (bi * ne + ei, 0, 0)
    next_slot = lambda ei, bi: (jnp.where(bi + 1 < b, (bi + 1) * ne + ei, jnp.minimum(ei + 1, ne - 1)), 0, 0)
    wspec = lambda shape: pl.BlockSpec((None, None) + shape, lambda ei, bi: (layer, ei, 0, 0))
    return pl.pallas_call(
        functools.partial(_ffn_kernel, seq=seq, tc=tc, nsub=nsub),
        grid=(ne, b),
        in_specs=[
            pl.BlockSpec((1, 1, cap), slot, memory_space=pltpu.SMEM),
            pl.BlockSpec((1, 1, cap), next_slot, memory_space=pltpu.SMEM),
            pl.BlockSpec((1, 1, cap), slot),
            pl.BlockSpec(memory_space=pl.ANY),
            wspec((D_MODEL, EXPERT_FF)), wspec((D_MODEL, EXPERT_FF)), wspec((EXPERT_FF, D_MODEL)),
        ],
        out_specs=pl.BlockSpec((None, None, cap, D_MODEL), lambda ei, bi: (bi, ei, 0, 0)),
        out_shape=jax.ShapeDtypeStruct((b, ne, cap, D_MODEL), BF16),
        scratch_shapes=[pltpu.VMEM((nsub, tc * (D_MODEL // LANES), LANES), F32), pltpu.SemaphoreType.DMA((nsub,)),
                        pltpu.VMEM((D_MODEL, EXPERT_FF), BF16), pltpu.VMEM((D_MODEL, EXPERT_FF), BF16),
                        pltpu.VMEM((EXPERT_FF, D_MODEL), BF16)],
        compiler_params=pltpu.CompilerParams(dimension_semantics=("arbitrary", "arbitrary"),
                                             vmem_limit_bytes=VMEM_LIMIT, disable_bounds_checks=True),
        name="ffn",
    )(idx3, idx3, gate3, h2d, w_gate, w_up, w_down)


_CTM = 256
_CWIN = 64
_CALIGN = 16


def _combine_kernel(cb_ref, x_ref, sp_ref, y_hbm, g_ref, o_ref, ybuf, xbuf, sem, xsem, *,
                    final, tiles_per_seq, nblk, cap):
    i = pl.program_id(0)
    n_tiles = pl.num_programs(0)
    b = i // tiles_per_seq
    slot = i % 2

    def window(tile, e):
        tb = tile // tiles_per_seq
        off = (tb * N_EXPERTS + e) * (nblk + 1) + (tile % tiles_per_seq) * (_CTM // LANES)
        s0 = cb_ref[off]
        s1 = cb_ref[off + _CTM // LANES]
        start = jnp.minimum((s0 // _CALIGN) * _CALIGN, cap - _CWIN)
        return s1, pl.multiple_of(start, _CALIGN)

    def fetch(tile, e, start, buf_slot):
        return pltpu.make_async_copy(y_hbm.at[tile // tiles_per_seq, e, pl.ds(start, _CWIN), :],
                                     ybuf.at[buf_slot, pl.ds(e * _CWIN, _CWIN), :], sem.at[buf_slot])

    def fetch_all(tile, buf_slot):
        for e in range(N_EXPERTS):
            fetch(tile, e, window(tile, e)[1], buf_slot).start()

    @pl.when(i == 0)
    def _():
        fetch_all(i, slot)

    @pl.when(i + 1 < n_tiles)
    def _():
        fetch_all(i + 1, 1 - slot)

    wins = [window(i, e) for e in range(N_EXPERTS)]
    pad = jnp.full((LANES - N_EXPERTS, LANES), -1.0, F32)
    sp_t = jnp.concatenate([jnp.concatenate([sp_ref[hf], pad], axis=0).T for hf in range(_CTM // LANES)],
                           axis=0)
    lane = lax.broadcasted_iota(I32, (1, _CWIN), 1).astype(F32)
    lane2 = lax.broadcasted_iota(I32, (1, LANES), 1)
    per_tile = LANES // _CWIN
    hits = []
    for e0 in range(0, N_EXPERTS, per_tile):
        rel = sp_t[:, e0:e0 + 1] - wins[e0][1].astype(F32)
        for k in range(1, per_tile):
            rel = jnp.where(lane2 < k * _CWIN, rel,
                            sp_t[:, e0 + k:e0 + k + 1] - (wins[e0 + k][1] - k * _CWIN).astype(F32))
        hits.append(jnp.where(rel == lane2.astype(F32), 1.0, 0.0).astype(BF16))
    for e in range(N_EXPERTS):
        fetch(i, e, wins[e][1], slot).wait()
    o_ref[...] = x_ref[...] + jnp.dot(jnp.concatenate(hits, axis=1), ybuf[slot], preferred_element_type=F32)
    for e in range(N_EXPERTS):
        s1, start = wins[e]
        col = sp_t[:, e:e + 1]

        def extra(k, carry, e=e, s1=s1, start=start, col=col):
            lo = start + (k + 1) * _CWIN
            st = pl.multiple_of(jnp.minimum(lo, cap - _CWIN), _CALIGN)
            cp = pltpu.make_async_copy(y_hbm.at[b, e, pl.ds(st, _CWIN), :], xbuf, xsem)
            cp.start()
            cp.wait()
            hit = jnp.where(jnp.logical_and(col - st.astype(F32) == lane, col >= lo.astype(F32)), 1.0, 0.0)
            o_ref[...] += jnp.dot(hit.astype(BF16), xbuf[...], preferred_element_type=F32)
            return carry

        n_extra = jnp.maximum(s1 - start - 1, 0) // _CWIN
        lax.fori_loop(0, n_extra, extra, 0)
    if final:
        x = o_ref[...]
        o_ref[...] = x * _rms_scale(x) * g_ref[...]


def _combine(x2d, spos, cb, y, g_final, final, seq):
    t = x2d.shape[0]
    batch, ne, cap, _ = y.shape
    nblk = seq // LANES
    cb_full = jnp.concatenate([cb, jnp.full((batch, ne, 1), cap, I32)], axis=-1).reshape(-1)
    spb = _CTM // LANES
    return pl.pallas_call(
        functools.partial(_combine_kernel, final=final, tiles_per_seq=seq // _CTM, nblk=nblk, cap=cap),
        grid_spec=pltpu.PrefetchScalarGridSpec(
            num_scalar_prefetch=1,
            grid=(t // _CTM,),
            in_specs=[
                pl.BlockSpec((_CTM, D_MODEL), lambda i, c: (i, 0)),
                pl.BlockSpec((spb, N_EXPERTS, LANES), lambda i, c: (i, 0, 0)),
                pl.BlockSpec(memory_space=pl.ANY),
                pl.BlockSpec((1, D_MODEL), lambda i, c: (0, 0)),
            ],
            out_specs=pl.BlockSpec((_CTM, D_MODEL), lambda i, c: (i, 0)),
            scratch_shapes=[pltpu.VMEM((2, N_EXPERTS * _CWIN, D_MODEL), BF16), pltpu.VMEM((_CWIN, D_MODEL), BF16),
                            pltpu.SemaphoreType.DMA((2,)), pltpu.SemaphoreType.DMA],
        ),
        out_shape=jax.ShapeDtypeStruct((t, D_MODEL), F32),
        compiler_params=_cparams(("arbitrary",)),
        name="combine",
    )(cb_full, x2d, spos, y, g_final.reshape(1, D_MODEL))


_SLAB_Q0 = 512
_SLAB_K0 = _SLAB_Q0 + D_HEADS * LANES
_SLAB_V0 = _SLAB_K0 + D_KV_HEADS * LANES
_ODD_COLS = _SLAB_V0 + D_KV_HEADS * LANES


def _proj_odd_kernel(x_ref, g_ref, wm_ref, cqn_ref, wq_ref, ckvn_ref, wkv_ref, dqn_ref, dkn_ref,
                     ccq_ref, s1cq_ref, s2cq_ref, cck_ref, s1ck_ref, s2ck_ref,
                     cdq_ref, s1dq_ref, s2dq_ref, cdk_ref, s1dk_ref, s2dk_ref,
                     qc_ref, kc_ref, vc_ref, qd_ref, kd_ref, vd_ref, stat_ref, *, steps_per_seq):
    lane = lax.broadcasted_iota(I32, (1, LANES), 1)
    one64 = jnp.where(lane == HEAD_DIM, 1.0, 0.0)
    last_lane = lane == LANES - 1
    half_rope = C_ROPE // 2
    stats = [jnp.zeros((1, LANES), F32), jnp.zeros((1, LANES), F32)]
    ones_mat = jnp.ones((LANES, LANES), BF16)

    def with_norm(val, fill, row, col):
        n2 = jnp.dot((val * val).astype(BF16), ones_mat, preferred_element_type=F32)
        stats[row] = jnp.where(lane == col, jnp.maximum(jnp.max(n2, axis=0, keepdims=True), stats[row]), stats[row])
        return jnp.where(last_lane, fill, val).astype(BF16)

    def head_norm(xg, gn_ref):
        ss = jnp.sum(xg * xg, axis=-1, keepdims=True) * (1.0 / HEAD_DIM)
        return xg * lax.rsqrt(ss + EPS) * gn_ref[...]

    n_chunks = 2
    rows_per = x_ref.shape[0] // n_chunks
    for c in range(n_chunks):
        rows = slice(c * rows_per, (c + 1) * rows_per)
        tab = lambda *refs: [r[rows, :] for r in refs]
        x = x_ref[rows, :]
        y = (x * _rms_scale(x) * g_ref[...]).astype(BF16)
        pm = jnp.dot(y, wm_ref[...], preferred_element_type=F32)
        cq = pm[:, :C_Q_RANK]
        cqn = (cq * _rms_scale(cq) * cqn_ref[...]).astype(BF16)
        qc = jnp.dot(cqn, wq_ref[...], preferred_element_type=F32)
        ckv = pm[:, C_Q_RANK:C_Q_RANK + C_KV_RANK]
        ckvn = (ckv * _rms_scale(ckv) * ckvn_ref[...]).astype(BF16)
        kv = jnp.dot(ckvn, wkv_ref[...], preferred_element_type=F32)
        kr = _rope3(pm[:, C_Q_RANK + C_KV_RANK:_SLAB_Q0], *tab(cck_ref, s1ck_ref, s2ck_ref), half_rope)
        t_cq = tab(ccq_ref, s1cq_ref, s2cq_ref)
        for h in range(C_HEADS):
            sl = slice(h * LANES, (h + 1) * LANES)
            qc_ref[rows, sl] = with_norm(_rope3(qc[:, sl], *t_cq, half_rope), 1.0, 1, h)
            kc_ref[rows, sl] = with_norm(kv[:, sl] + kr, -1.0, 0, h)
            vc_ref[rows, sl] = (kv[:, C_HEADS * LANES + h * LANES:C_HEADS * LANES + (h + 1) * LANES]
                                + one64).astype(BF16)
        t_dq = tab(cdq_ref, s1dq_ref, s2dq_ref)
        t_dk = tab(cdk_ref, s1dk_ref, s2dk_ref)
        for g in range(D_HEADS):
            xg = pm[:, _SLAB_Q0 + g * LANES:_SLAB_Q0 + (g + 1) * LANES]
            qd_ref[rows, g * LANES:(g + 1) * LANES] = with_norm(
                _rope3(head_norm(xg, dqn_ref), *t_dq, HEAD_DIM // 4), 1.0, 1, C_HEADS + g)
        for g in range(D_KV_HEADS):
            sl = slice(g * LANES, (g + 1) * LANES)
            xg = pm[:, _SLAB_K0 + g * LANES:_SLAB_K0 + (g + 1) * LANES]
            kd_ref[rows, sl] = with_norm(_rope3(head_norm(xg, dkn_ref), *t_dk, HEAD_DIM // 4), -1.0, 0, C_HEADS + g)
            vd_ref[rows, sl] = (pm[:, _SLAB_V0 + g * LANES:_SLAB_V0 + (g + 1) * LANES] + one64).astype(BF16)

    new = jnp.concatenate(stats + [jnp.zeros((6, LANES), F32)], axis=0)

    @pl.when(pl.program_id(0) % steps_per_seq == 0)
    def _():
        stat_ref[...] = new

    @pl.when(pl.program_id(0) % steps_per_seq != 0)
    def _():
        stat_ref[...] = jnp.maximum(stat_ref[...], new)


def _slabs(w, n_heads, width, lane_off=0):
    k = w.shape[0]
    w3 = w.reshape(k, n_heads, width)
    w3 = jnp.pad(w3, ((0, 0), (0, 0), (lane_off, LANES - width - lane_off)))
    return w3.reshape(k, n_heads * LANES)


def _axial_tables(row, col, scale):
    half = HEAD_DIM // 2
    cr, s1r, s2r = _rope_tables(row, D_THETA, half, 0, LANES, scale)
    cc, s1c, s2c = _rope_tables(col, D_THETA, half, half, LANES, scale)
    lane = jnp.arange(LANES)[None, :]
    return jnp.where(lane < half, cr, cc), s1r + s1c, s2r + s2c


def _proj_odd(x2d, seq, g_mix, w_in, cq_norm, w_cq_up, ckv_norm, w_ckv_up, dq_norm, dk_norm, tm=512):
    t = x2d.shape[0]
    nblk = seq // tm
    o1 = C_Q_RANK
    o2 = o1 + C_KV_RANK
    o3 = o2 + C_ROPE
    o4 = o3 + D_HEADS * HEAD_DIM
    o5 = o4 + D_KV_HEADS * HEAD_DIM
    wm = jnp.concatenate([
        w_in[:, :o2],
        _slabs(w_in[:, o2:o3], 1, C_ROPE, C_NOPE),
        _slabs(w_in[:, o3:o4], D_HEADS, HEAD_DIM),
        _slabs(w_in[:, o4:o5], D_KV_HEADS, HEAD_DIM),
        _slabs(w_in[:, o5:], D_KV_HEADS, HEAD_DIM),
    ], axis=1).astype(BF16)
    assert wm.shape[1] == _ODD_COLS
    wq = _slabs(w_cq_up, C_HEADS, C_NOPE + C_ROPE).astype(BF16)
    kv3 = w_ckv_up.reshape(C_KV_RANK, C_HEADS, 2 * HEAD_DIM)
    wkv = jnp.concatenate([
        _slabs(kv3[:, :, :C_NOPE].reshape(C_KV_RANK, -1), C_HEADS, C_NOPE),
        _slabs(kv3[:, :, C_NOPE:].reshape(C_KV_RANK, -1), C_HEADS, HEAD_DIM),
    ], axis=1).astype(BF16)
    pad64 = lambda g: jnp.pad(g, (0, LANES - HEAD_DIM)).reshape(1, LANES)

    pos = jnp.arange(seq, dtype=I32)
    row_pos = pos // GRID_W
    col_pos = pos % GRID_W
    c_scale = (C_NOPE + C_ROPE) ** -0.5 * LOG2E
    d_scale = HEAD_DIM ** -0.5 * LOG2E
    tabs = (_rope_tables(pos, ROPE_THETA, C_ROPE, C_NOPE, LANES, c_scale)
            + _rope_tables(pos, ROPE_THETA, C_ROPE, C_NOPE, LANES, 1.0)
            + _axial_tables(row_pos, col_pos, d_scale)
            + _axial_tables(row_pos, col_pos, 1.0))

    row = lambda i: (i, 0)
    full = lambda i: (0, 0)
    tspec = pl.BlockSpec((tm, LANES), lambda i: (i % nblk, 0))
    wide = C_HEADS * LANES
    kvw = D_KV_HEADS * LANES
    return pl.pallas_call(
        functools.partial(_proj_odd_kernel, steps_per_seq=nblk),
        grid=(t // tm,),
        in_specs=[
            pl.BlockSpec((tm, D_MODEL), row),
            pl.BlockSpec((1, D_MODEL), full),
            pl.BlockSpec(wm.shape, full),
            pl.BlockSpec((1, C_Q_RANK), full),
            pl.BlockSpec(wq.shape, full),
            pl.BlockSpec((1, C_KV_RANK), full),
            pl.BlockSpec(wkv.shape, full),
            pl.BlockSpec((1, LANES), full),
            pl.BlockSpec((1, LANES), full),
        ] + [tspec] * 12,
        out_specs=[
            pl.BlockSpec((tm, wide), row), pl.BlockSpec((tm, wide), row), pl.BlockSpec((tm, wide), row),
            pl.BlockSpec((tm, wide), row), pl.BlockSpec((tm, kvw), row), pl.BlockSpec((tm, kvw), row),
            pl.BlockSpec((None, 8, LANES), lambda i: (i // nblk, 0, 0)),
        ],
        out_shape=[
            jax.ShapeDtypeStruct((t, wide), BF16), jax.ShapeDtypeStruct((t, wide), BF16),
            jax.ShapeDtypeStruct((t, wide), BF16), jax.ShapeDtypeStruct((t, wide), BF16),
            jax.ShapeDtypeStruct((t, kvw), BF16), jax.ShapeDtypeStruct((t, kvw), BF16),
            jax.ShapeDtypeStruct((t // seq, 8, LANES), F32),
        ],
        compiler_params=_cparams(("arbitrary",)),
        name="proj_odd",
    )(x2d, g_mix.reshape(1, D_MODEL), wm, cq_norm.reshape(1, -1), wq, ckv_norm.reshape(1, -1), wkv,
      pad64(dq_norm), pad64(dk_norm), *tabs)


def _flash_kernel(q_ref, k_ref, v_ref, o_ref, qs_sc, m_sc, acc_sc, *, group, tq, tk):
    ki = pl.program_id(3)

    @pl.when(ki == 0)
    def _():
        for g in range(group):
            qs_sc[g * tq:(g + 1) * tq, :] = q_ref[:, g * LANES:(g + 1) * LANES]
        m_sc[...] = jnp.full(m_sc.shape, NEG, F32)
        acc_sc[...] = jnp.zeros(acc_sc.shape, F32)

    s = lax.dot_general(qs_sc[...], k_ref[...], _NT, preferred_element_type=F32)
    m_prev = m_sc[...]
    m_new = jnp.maximum(m_prev, jnp.max(s, axis=1, keepdims=True))
    alpha = jnp.exp2(m_prev - m_new)
    p = jnp.exp2(s - jnp.tile(m_new, (1, tk // LANES)))
    acc_sc[...] = alpha * acc_sc[...] + jnp.dot(p.astype(BF16), v_ref[...], preferred_element_type=F32)
    m_sc[...] = m_new

    @pl.when(ki == pl.num_programs(3) - 1)
    def _():
        acc = acc_sc[...]
        o = acc / acc[:, HEAD_DIM:HEAD_DIM + 1]
        for g in range(group):
            o_ref[:, g * LANES:(g + 1) * LANES] = o[g * tq:(g + 1) * tq].astype(BF16)


_FLASH_CHUNK = 2048
_V_ROWS = 80


def _flash_bounded_kernel(bound_ref, q_ref, k_ref, v_ref, o_ref, qs_sc, acc_sc, *, group, tq):
    ki = pl.program_id(3)
    head = pl.program_id(0) * pl.num_programs(1) + pl.program_id(1)

    @pl.when(ki == 0)
    def _():
        fix = jnp.where(lax.broadcasted_iota(I32, (1, LANES), 1) == LANES - 1, bound_ref[head], 1.0)
        for g in range(group):
            qs_sc[g * tq:(g + 1) * tq, :] = (q_ref[:, g * LANES:(g + 1) * LANES].astype(F32) * fix).astype(BF16)
        acc_sc[...] = jnp.zeros(acc_sc.shape, F32)

    chunk = min(_FLASH_CHUNK, k_ref.shape[0])
    n_chunks = k_ref.shape[0] // chunk
    qs = qs_sc[...]

    def scores(c):
        return lax.dot_general(k_ref[c * chunk:(c + 1) * chunk, :], qs, _NT, preferred_element_type=F32)

    def values(c, s_t):
        return lax.dot_general(v_ref[c * chunk:(c + 1) * chunk, :_V_ROWS], jnp.exp2(s_t).astype(BF16),
                               (((0,), (0,)), ((), ())), preferred_element_type=F32)

    acc = acc_sc[...]
    s_prev = scores(0)
    for c in range(1, n_chunks):
        s_next = scores(c)
        acc = acc + values(c - 1, s_prev)
        s_prev = s_next
    acc_sc[...] = acc + values(n_chunks - 1, s_prev)

    @pl.when(ki == pl.num_programs(3) - 1)
    def _():
        acc = acc_sc[...]
        o_t = acc / acc[HEAD_DIM:HEAD_DIM + 1, :]
        o = jnp.concatenate([o_t, jnp.zeros((LANES - _V_ROWS, o_t.shape[1]), F32)], axis=0).T
        for g in range(group):
            o_ref[:, g * LANES:(g + 1) * LANES] = o[g * tq:(g + 1) * tq].astype(BF16)


def _flash(q, k, v, group, nk, bounded, rows=1024, tk=512, tk_bounded=8192):
    b, s, qw = q.shape
    hk = k.shape[2] // LANES
    tq = rows // group
    tk = min(tk, s)
    tkb = min(tk_bounded, s)
    out_shape = jax.ShapeDtypeStruct((b, s, qw), BF16)
    sem = ("parallel", "parallel", "parallel", "arbitrary")

    def running_max(q, k, v, nk):
        qspec = pl.BlockSpec((None, tq, group * LANES), lambda bi, hi, qi, ki: (bi, qi, hi))
        kspec = pl.BlockSpec((None, tk, LANES), lambda bi, hi, qi, ki: (bi, ki, hi))
        return pl.pallas_call(
            functools.partial(_flash_kernel, group=group, tq=tq, tk=tk),
            grid=(b, hk, s // tq, s // tk),
            in_specs=[qspec, kspec, kspec],
            out_specs=qspec,
            out_shape=out_shape,
            scratch_shapes=[pltpu.VMEM((rows, LANES), BF16), pltpu.VMEM((rows, LANES), F32),
                            pltpu.VMEM((rows, LANES), F32)],
            compiler_params=_cparams(sem),
            name="flash",
        )(q, k, v)

    def bound(q, k, v, nk):
        qspec = pl.BlockSpec((None, tq, group * LANES), lambda bi, hi, qi, ki, nkr: (bi, qi, hi))
        kspec = pl.BlockSpec((None, tkb, LANES), lambda bi, hi, qi, ki, nkr: (bi, ki, hi))
        return pl.pallas_call(
            functools.partial(_flash_bounded_kernel, group=group, tq=tq),
            grid_spec=pltpu.PrefetchScalarGridSpec(
                num_scalar_prefetch=1,
                grid=(b, hk, s // tq, s // tkb),
                in_specs=[qspec, kspec, kspec],
                out_specs=qspec,
                scratch_shapes=[pltpu.VMEM((rows, LANES), BF16), pltpu.VMEM((_V_ROWS, rows), F32)],
            ),
            out_shape=out_shape,
            compiler_params=_cparams(sem),
            name="flash_bounded",
        )(nk.reshape(-1), q, k, v)

    return lax.cond(bounded, bound, running_max, q, k, v, nk)


def _moe(x1, h2, aff, batch, seq, w_gate, w_up, w_down, layer, g_final, final):
    idx, gates, spos, cb = _route(aff.reshape(-1, LANES), batch, seq)
    y = _ffn(idx, gates, h2, w_gate, w_up, w_down, layer, seq)
    return _combine(x1, spos.reshape(-1, N_EXPERTS, LANES), cb, y, g_final, final, seq)


def kernel(x, norm_mix, norm_ffn, even_w_in, even_gmlp_norm, even_w_spatial, even_b_spatial, even_w_out,
           odd_w_in, odd_cq_norm, odd_w_cq_up, odd_ckv_norm, odd_w_ckv_up, odd_dq_norm, odd_dk_norm, odd_w_out,
           moe_w_router, moe_w_gate, moe_w_up, moe_w_down, final_norm):
    b, s, d = x.shape
    depth = norm_mix.shape[0]
    x2d = x.reshape(b * s, d)
    for i in range(depth):
        j = i // 2
        last = i == depth - 1
        if i % 2 == 0:
            q, k, v, go = _proj_even(x2d, s, norm_mix[i], even_w_in[j], even_gmlp_norm[j], even_w_spatial[j],
                                     even_b_spatial[j])
            a = _dilated(q.reshape(b, s, A_WIDTH), k.reshape(b, s, A_WIDTH), v.reshape(b, s, A_WIDTH))
            x1, h2, aff = _outproj(x2d, a.reshape(b * s, A_WIDTH), go, even_w_out[j][:A_WIDTH],
                                   even_w_out[j][A_WIDTH:], norm_ffn[i], moe_w_router[i])
        else:
            qc, kc, vc, qd, kd, vd, stat = _proj_odd(x2d, s, norm_mix[i], odd_w_in[j], odd_cq_norm[j],
                                                     odd_w_cq_up[j], odd_ckv_norm[j], odd_w_ckv_up[j],
                                                     odd_dq_norm[j], odd_dk_norm[j])
            grp = D_HEADS // D_KV_HEADS
            k2_c, k2_d = stat[:, 0, :C_HEADS], stat[:, 0, C_HEADS:C_HEADS + D_KV_HEADS]
            q2_c = stat[:, 1, :C_HEADS]
            q2_d = jnp.max(stat[:, 1, C_HEADS:C_HEADS + D_HEADS].reshape(b, D_KV_HEADS, grp), axis=-1)
            bound_c = jnp.sqrt(q2_c * k2_c) * _NORM_MARGIN ** 2
            bound_d = jnp.sqrt(q2_d * k2_d) * _NORM_MARGIN ** 2
            bounded = jnp.maximum(jnp.max(bound_c), jnp.max(bound_d)) <= _MAX_SCORE_BOUND
            r3 = lambda z: z.reshape(b, s, -1)
            oc = _flash(r3(qc), r3(kc), r3(vc), 1, bound_c, bounded)
            od = _flash(r3(qd), r3(kd), r3(vd), grp, bound_d, bounded)
            cw = C_HEADS * HEAD_DIM
            x1, h2, aff = _outproj(x2d, oc.reshape(b * s, -1), od.reshape(b * s, -1),
                                   _slabs(odd_w_out[j][:cw].T, C_HEADS, HEAD_DIM).T,
                                   _slabs(odd_w_out[j][cw:].T, D_HEADS, HEAD_DIM).T,
                                   norm_ffn[i], moe_w_router[i])
        x2d = _moe(x1, h2, aff, b, s, moe_w_gate, moe_w_up, moe_w_down, i, final_norm, last)
    return x2d.reshape(b, s, d)
```

```python
import functools
import math

import jax
import jax.numpy as jnp
from jax import lax
from jax.experimental import pallas as pl
from jax.experimental.pallas import tpu as pltpu

F32 = jnp.float32
BF16 = jnp.bfloat16
I32 = jnp.int32

EPS = 1e-6
NEG = -1e30
LOG2E = 1.4426950408889634

D_MODEL = 1024
HEAD_DIM = 64
ROPE_THETA = 500000.0
ROT_DIM = 16
GRID_W = 64
A_HEADS = 12
A_WIDTH = 768
A_DILATIONS = (1, 4, 16)
A_RADIUS = 64
B_WIDTH = 256
B_GROUPS = 4
B_CHUNK = 128
C_HEADS = 8
C_Q_RANK = 256
C_KV_RANK = 128
C_NOPE = 64
C_ROPE = 32
D_HEADS = 8
D_KV_HEADS = 2
D_THETA = 10000.0
N_EXPERTS = 16
EC_FACTOR = 2
EXPERT_FF = 512

_NORM_MARGIN = 1.01
_MAX_SCORE_BOUND = 55.0

LANES = 128
VMEM_LIMIT = 48 * 1024 * 1024
_DILATED_VMEM_LIMIT = 56 * 1024 * 1024

_NT = (((1,), (1,)), ((), ()))


def _cparams(sem):
    return pltpu.CompilerParams(dimension_semantics=sem, vmem_limit_bytes=VMEM_LIMIT)


def _rms_scale(x):
    return lax.rsqrt(jnp.mean(x * x, axis=-1, keepdims=True) + EPS)


def _rope3(a, c, s1, s2, shift):
    return a * c + pltpu.roll(a, LANES - shift, 1) * s1 + pltpu.roll(a, shift, 1) * s2


def _split_dot(x, w_bf16):
    hi = x.astype(BF16)
    lo = (x - hi.astype(F32)).astype(BF16)
    return (jnp.dot(hi, w_bf16, preferred_element_type=F32)
            + jnp.dot(lo, w_bf16, preferred_element_type=F32))


def _proj_even_kernel(x_ref, g_ref, w_ref, cq_ref, s1q_ref, s2q_ref, ck_ref, s1k_ref, s2k_ref,
                      gn_ref, gmat_ref, ws_ref, bs_ref,
                      q_ref, k_ref, v_ref, go_ref):
    x = x_ref[...]
    y = (x * _rms_scale(x) * g_ref[...]).astype(BF16)
    tm = x.shape[0]

    aq = jnp.dot(y, w_ref[:, 0:A_WIDTH], preferred_element_type=F32)
    tq = (cq_ref[...], s1q_ref[...], s2q_ref[...])
    for j in range(A_WIDTH // LANES):
        sl = slice(j * LANES, (j + 1) * LANES)
        q_ref[:, sl] = _rope3(aq[:, sl], *tq, ROT_DIM // 2)
    ak = jnp.dot(y, w_ref[:, A_WIDTH:2 * A_WIDTH], preferred_element_type=F32)
    tk = (ck_ref[...], s1k_ref[...], s2k_ref[...])
    for j in range(A_WIDTH // LANES):
        sl = slice(j * LANES, (j + 1) * LANES)
        k_ref[:, sl] = _rope3(ak[:, sl], *tk, ROT_DIM // 2)
    v_ref[...] = jnp.dot(y, w_ref[:, 2 * A_WIDTH:3 * A_WIDTH], preferred_element_type=F32)

    z = jnp.dot(y, w_ref[:, 3 * A_WIDTH:3 * A_WIDTH + 2 * B_WIDTH], preferred_element_type=F32)
    ge = jax.nn.gelu(z)
    u = ge[:, :B_WIDTH]
    vv = ge[:, B_WIDTH:]
    ss = _split_dot(vv * vv, gmat_ref[...])
    vn = (vv * lax.rsqrt(ss + EPS) * gn_ref[...]).astype(BF16)
    grp = lax.broadcasted_iota(I32, (B_CHUNK, B_WIDTH), 1) // (B_WIDTH // B_GROUPS)
    for c in range(tm // B_CHUNK):
        rows = slice(c * B_CHUNK, (c + 1) * B_CHUNK)
        vc = vn[rows]
        mg = [jnp.dot(ws_ref[g], vc, preferred_element_type=F32) for g in range(B_GROUPS)]
        mixed = jnp.where(grp == 0, mg[0], jnp.where(grp == 1, mg[1], jnp.where(grp == 2, mg[2], mg[3])))
        go_ref[rows, :] = (u[rows] * (mixed + bs_ref[...])).astype(BF16)


def _rope_tables(pos, theta, r, lane_off, period, scale):
    half = r // 2
    inv = jnp.power(jnp.float32(theta), -jnp.arange(half, dtype=F32) * (2.0 / r))
    ang = pos.astype(F32)[:, None] * inv[None, :]
    cos, sin = jnp.cos(ang), jnp.sin(ang)
    o = (jnp.arange(LANES) % period) - lane_off
    in_lo = (o >= 0) & (o < half)
    in_hi = (o >= half) & (o < r)
    idx = jnp.clip(jnp.where(in_hi, o - half, o), 0, half - 1)
    c = jnp.where((in_lo | in_hi)[None, :], cos[:, idx], 1.0)
    s1 = jnp.where(in_lo[None, :], -sin[:, idx], 0.0)
    s2 = jnp.where(in_hi[None, :], sin[:, idx], 0.0)
    return c * scale, s1 * scale, s2 * scale


def _proj_even(x2d, seq, g_mix, w_in, gmlp_norm, w_s, b_s, tm=512):
    t = x2d.shape[0]
    nblk = seq // tm
    pos = jnp.arange(seq, dtype=I32)
    qscale = HEAD_DIM ** -0.5 * LOG2E
    cq, s1q, s2q = _rope_tables(pos, ROPE_THETA, ROT_DIM, 0, HEAD_DIM, qscale)
    ck, s1k, s2k = _rope_tables(pos, ROPE_THETA, ROT_DIM, 0, HEAD_DIM, 1.0)
    gdim = B_WIDTH // B_GROUPS
    gid = jnp.arange(B_WIDTH) // gdim
    gmat = jnp.where(gid[:, None] == gid[None, :], 1.0 / gdim, 0.0).astype(BF16)
    bias = jnp.repeat(b_s.T, gdim, axis=1)
    row = lambda i: (i, 0)
    tab = lambda i: (i % nblk, 0)
    full = lambda i: (0, 0)
    tspec = pl.BlockSpec((tm, LANES), tab)
    return pl.pallas_call(
        _proj_even_kernel,
        grid=(t // tm,),
        in_specs=[
            pl.BlockSpec((tm, D_MODEL), row),
            pl.BlockSpec((1, D_MODEL), full),
            pl.BlockSpec(w_in.shape, full),
            tspec, tspec, tspec, tspec, tspec, tspec,
            pl.BlockSpec((1, B_WIDTH), full),
            pl.BlockSpec((B_WIDTH, B_WIDTH), full),
            pl.BlockSpec((B_GROUPS, B_CHUNK, B_CHUNK), lambda i: (0, 0, 0)),
            pl.BlockSpec((B_CHUNK, B_WIDTH), full),
        ],
        out_specs=[
            pl.BlockSpec((tm, A_WIDTH), row),
            pl.BlockSpec((tm, A_WIDTH), row),
            pl.BlockSpec((tm, A_WIDTH), row),
            pl.BlockSpec((tm, B_WIDTH), row),
        ],
        out_shape=[
            jax.ShapeDtypeStruct((t, A_WIDTH), F32),
            jax.ShapeDtypeStruct((t, A_WIDTH), F32),
            jax.ShapeDtypeStruct((t, A_WIDTH), F32),
            jax.ShapeDtypeStruct((t, B_WIDTH), BF16),
        ],
        compiler_params=_cparams(("parallel",)),
        name="proj_even",
    )(x2d, g_mix.reshape(1, D_MODEL), w_in.astype(BF16), cq, s1q, s2q, ck, s1k, s2k,
      gmlp_norm.reshape(1, B_WIDTH), gmat, w_s.astype(BF16), bias)


_TQ = 128
_TK = _TQ + 2 * A_RADIUS
_NORM_ROWS = 1024


def _dilated_kernel(q_ref, k_ref, v_ref, o_ref, m_sc, l_sc, bias_sc, *, seq):
    half0 = lax.broadcasted_iota(I32, (1, LANES), 1) < HEAD_DIM
    same_head = jnp.where(lax.broadcasted_iota(I32, (LANES, LANES), 0) // HEAD_DIM
                          == lax.broadcasted_iota(I32, (LANES, LANES), 1) // HEAD_DIM,
                          1.0, 0.0).astype(BF16)
    n_pat = len(A_DILATIONS)

    def max_head_sq(ref):
        def body(c, mx):
            x = ref[pl.ds(pl.multiple_of(c * _NORM_ROWS, _NORM_ROWS), _NORM_ROWS), :]
            n2 = jnp.dot((x * x).astype(BF16), same_head, preferred_element_type=F32)
            return jnp.maximum(mx, jnp.max(n2, axis=0, keepdims=True))
        return lax.fori_loop(0, seq // _NORM_ROWS, body, jnp.zeros((1, LANES), F32))

    both = jnp.sqrt(max_head_sq(q_ref) * max_head_sq(k_ref)) * (_NORM_MARGIN * _NORM_MARGIN)
    both = jnp.broadcast_to(both, (8, LANES))
    other = pltpu.roll(both, HEAD_DIM, 1)
    bound = [jnp.where(half0, both, other)[:1], jnp.where(half0, other, both)[:1]]
    worst = jnp.max(both)

    diff = lax.broadcasted_iota(I32, (_TQ, _TK), 1) - lax.broadcasted_iota(I32, (_TQ, _TK), 0)
    for case in range(3):
        band = jnp.where(jnp.abs(diff - case * A_RADIUS) <= A_RADIUS, 0.0, NEG)
        bias_sc[6 + case] = band
        for h in range(2):
            bias_sc[3 * h + case] = band - jnp.tile(bound[h], (1, _TK // LANES))
    one_bf16 = jnp.ones((), BF16)

    def run(bounded):
        for pi, d in enumerate(A_DILATIONS):
            cls_len = seq // d
            tpc = cls_len // _TQ

            def tile(qrows, kb, vb, case, pi=pi):
                q = q_ref[qrows, :]
                parts = []
                for h in range(2):
                    qh = jnp.where(half0 if h == 0 else jnp.logical_not(half0), q, 0.0).astype(BF16)
                    s = lax.dot_general(qh, kb, _NT, preferred_element_type=F32)
                    if bounded:
                        mt = None
                        p = jnp.exp2(s + bias_sc[3 * h + case]).astype(BF16)
                    else:
                        s = s + bias_sc[6 + case]
                        mt = jnp.max(s, axis=-1, keepdims=True)
                        p = jnp.exp2(s - mt).astype(BF16)
                    vh = jnp.where(half0 if h == 0 else jnp.logical_not(half0), vb, one_bf16)
                    parts.append((mt, jnp.dot(p, vh, preferred_element_type=F32)))
                ot = jnp.where(half0, parts[0][1], parts[1][1])
                lt = pltpu.roll(jnp.where(half0, parts[1][1], parts[0][1]), HEAD_DIM, 1)
                if bounded:
                    if pi > 0:
                        lt = l_sc[qrows, :] + lt
                        ot = o_ref[qrows, :] + ot
                else:
                    mt = jnp.where(half0, parts[0][0], parts[1][0])
                    if pi > 0:
                        mp = m_sc[qrows, :]
                        mn = jnp.maximum(mp, mt)
                        a = jnp.exp2(mp - mn)
                        b = jnp.exp2(mt - mn)
                        lt = a * l_sc[qrows, :] + b * lt
                        ot = a * o_ref[qrows, :] + b * ot
                        mt = mn
                    if pi < n_pat - 1:
                        m_sc[qrows, :] = mt
                if pi == n_pat - 1:
                    o_ref[qrows, :] = ot / lt
                else:
                    l_sc[qrows, :] = lt
                    o_ref[qrows, :] = ot

            def window(l0, cls_len=cls_len):
                if isinstance(l0, int):
                    kst = min(max(l0 - A_RADIUS, 0), cls_len - _TK)
                else:
                    kst = jnp.clip(l0 - A_RADIUS, 0, cls_len - _TK)
                return kst, (l0 - kst) // A_RADIUS

            if d < 8:

                def body(j, carry, d=d, tpc=tpc):
                    i = j // tpc
                    l0 = (j % tpc) * _TQ
                    kst, case = window(l0)
                    if d == 1:
                        qrows = pl.ds(pl.multiple_of(l0, _TQ), _TQ)
                        krows = pl.ds(pl.multiple_of(kst, A_RADIUS), _TK)
                    else:
                        qrows = pl.ds(l0 * d + i, _TQ, stride=d)
                        krows = pl.ds(kst * d + i, _TK, stride=d)
                    tile(qrows, k_ref[krows, :].astype(BF16), v_ref[krows, :].astype(BF16), case)
                    return carry

                lax.fori_loop(0, seq // _TQ, body, 0, unroll=16)
            else:

                def body(i, carry, d=d, cls_len=cls_len, tpc=tpc):
                    cls = pl.ds(i, cls_len, stride=d)
                    kc = k_ref[cls, :].astype(BF16)
                    vc = v_ref[cls, :].astype(BF16)
                    for n in range(tpc):
                        kst, case = window(n * _TQ)
                        tile(pl.ds(n * _TQ * d + i, _TQ, stride=d), kc[kst:kst + _TK], vc[kst:kst + _TK], case)
                    return carry

                lax.fori_loop(0, d, body, 0, unroll=4)

    @pl.when(worst <= _MAX_SCORE_BOUND)
    def _():
        run(True)

    @pl.when(jnp.logical_not(worst <= _MAX_SCORE_BOUND))
    def _():
        run(False)


def _dilated(q, k, v):
    b, s, w = q.shape
    spec = pl.BlockSpec((None, s, LANES), lambda bi, hi: (bi, 0, hi))
    return pl.pallas_call(
        functools.partial(_dilated_kernel, seq=s),
        grid=(b, w // LANES),
        in_specs=[spec, spec, spec],
        out_specs=spec,
        out_shape=jax.ShapeDtypeStruct((b, s, w), F32),
        scratch_shapes=[pltpu.VMEM((s, LANES), F32), pltpu.VMEM((s, LANES), F32),
                        pltpu.VMEM((9, _TQ, _TK), F32)],
        compiler_params=pltpu.CompilerParams(dimension_semantics=("parallel", "parallel"),
                                             vmem_limit_bytes=_DILATED_VMEM_LIMIT),
        name="dilated",
    )(q, k, v)


def _outproj_kernel(x_ref, a_ref, b_ref, wa_ref, wb_ref, gf_ref, wr_ref, x1_ref, h2_ref, aff_ref):
    x1 = (x_ref[...]
          + jnp.dot(a_ref[...].astype(BF16), wa_ref[...], preferred_element_type=F32)
          + jnp.dot(b_ref[...].astype(BF16), wb_ref[...], preferred_element_type=F32))
    x1_ref[...] = x1
    h2 = x1 * _rms_scale(x1) * gf_ref[...]
    n_tiles = D_MODEL // LANES
    for j in range(n_tiles):
        h2_ref[pl.ds(j, h2.shape[0], stride=n_tiles), :] = h2[:, j * LANES:(j + 1) * LANES]
    hi = h2.astype(BF16)
    lo = (h2 - hi.astype(F32)).astype(BF16)
    both = jnp.dot(hi, wr_ref[...], preferred_element_type=F32)
    lg = (both[:, :LANES] + both[:, LANES:]
          + jnp.dot(lo, wr_ref[:, :LANES], preferred_element_type=F32))
    valid = lax.broadcasted_iota(I32, lg.shape, 1) < N_EXPERTS
    lg = jnp.where(valid, lg, NEG)
    e = jnp.exp(lg - jnp.max(lg, axis=-1, keepdims=True))
    aff = e / jnp.sum(e, axis=-1, keepdims=True)
    aff_t = aff.T
    for j in range(aff.shape[0] // LANES):
        aff_ref[j] = aff_t[:N_EXPERTS, j * LANES:(j + 1) * LANES]


def _outproj(x2d, a, b, wa, wb, g_ffn, w_router, tm=512):
    t = x2d.shape[0]
    wr = jnp.pad(w_router, ((0, 0), (0, LANES - N_EXPERTS)))
    wr_hi = wr.astype(BF16)
    wr_lo = (wr - wr_hi.astype(F32)).astype(BF16)
    wr2 = jnp.concatenate([wr_hi, wr_lo], axis=1)
    row = lambda i: (i, 0)
    full = lambda i: (0, 0)
    return pl.pallas_call(
        _outproj_kernel,
        grid=(t // tm,),
        in_specs=[
            pl.BlockSpec((tm, D_MODEL), row),
            pl.BlockSpec((tm, a.shape[1]), row),
            pl.BlockSpec((tm, b.shape[1]), row),
            pl.BlockSpec(wa.shape, full),
            pl.BlockSpec(wb.shape, full),
            pl.BlockSpec((1, D_MODEL), full),
            pl.BlockSpec((D_MODEL, 2 * LANES), full),
        ],
        out_specs=[
            pl.BlockSpec((tm, D_MODEL), row),
            pl.BlockSpec((tm * (D_MODEL // LANES), LANES), row),
            pl.BlockSpec((tm // LANES, N_EXPERTS, LANES), lambda i: (i, 0, 0)),
        ],
        out_shape=[
            jax.ShapeDtypeStruct((t, D_MODEL), F32),
            jax.ShapeDtypeStruct((t * (D_MODEL // LANES), LANES), F32),
            jax.ShapeDtypeStruct((t // LANES, N_EXPERTS, LANES), F32),
        ],
        compiler_params=_cparams(("parallel",)),
        name="outproj",
    )(x2d, a, b, wa.astype(BF16), wb.astype(BF16), g_ffn.reshape(1, D_MODEL), wr2)


def _route_kernel(aff_ref, idx_ref, gate_ref, spos_ref, cb_ref, thr_sc, need_sc, *, cap):
    nblk = aff_ref.shape[0] // N_EXPERTS
    bits = pltpu.bitcast(aff_ref[...], I32).reshape(nblk, N_EXPERTS, LANES)

    def count(pred):
        return jnp.sum(jnp.sum(jnp.where(pred, 1.0, 0.0), axis=0), axis=1, keepdims=True)

    def search(it, thr):
        cand = thr | jnp.left_shift(jnp.int32(1), 30 - it)
        return jnp.where(count(bits >= cand[None]) >= cap, cand, thr)

    thr = lax.fori_loop(0, 31, search, jnp.zeros((N_EXPERTS, 1), I32))
    need = cap - count(bits > thr[None])
    thr_sc[...] = jnp.broadcast_to(thr, (N_EXPERTS, LANES))
    need_sc[...] = jnp.broadcast_to(need, (N_EXPERTS, LANES))

    ri = lax.broadcasted_iota(I32, (LANES, LANES), 0)
    ci = lax.broadcasted_iota(I32, (LANES, LANES), 1)
    upper = jnp.where(ri <= ci, 1.0, 0.0).astype(BF16)
    lower = jnp.where(ci <= ri, 1.0, 0.0).astype(BF16)
    eye = jnp.where(ri == ci, 1.0, 0.0).astype(BF16)
    ones = jnp.ones((LANES, LANES), BF16)
    bi = lax.broadcasted_iota(I32, (nblk, nblk), 0)
    bj = lax.broadcasted_iota(I32, (nblk, nblk), 1)
    strict = jnp.where(bj < bi, 1.0, 0.0).astype(BF16)
    before = jnp.where(bi < bj, 1.0, 0.0).astype(BF16)
    mean_rows = jnp.full((8, LANES), 1.0 / LANES, BF16)
    c_row = lax.broadcasted_iota(I32, (1, cap), 1).astype(F32)
    blk_iota = lax.broadcasted_iota(I32, (nblk, cap), 0).astype(F32)
    t_iota = lax.broadcasted_iota(I32, (LANES, cap), 0).astype(F32)
    rep = cap // LANES

    def cums(mask_bf16):
        lp = jnp.dot(mask_bf16, upper, preferred_element_type=F32)
        bc = jnp.dot(mask_bf16, ones, preferred_element_type=F32)
        bst = jnp.dot(strict, bc.astype(BF16), preferred_element_type=F32)
        return lp, bc, bst

    def per_expert(e, carry):
        a = aff_ref[pl.ds(e, nblk, stride=N_EXPERTS), :]
        ab = pltpu.bitcast(a, I32)
        thr_e = thr_sc[pl.ds(e, 1), :]
        need_e = need_sc[pl.ds(e, 1), :]
        gt = ab > thr_e
        eq = ab == thr_e
        eqf = jnp.where(eq, 1.0, 0.0)
        lp_q, _, bst_q = cums(eqf.astype(BF16))
        sel = jnp.logical_or(gt, jnp.logical_and(eq, bst_q + lp_q - eqf < need_e))
        mb = jnp.where(sel, 1.0, 0.0).astype(BF16)
        lp, bc, bst = cums(mb)
        spos_ref[pl.ds(e, nblk, stride=N_EXPERTS), :] = jnp.where(sel, bst + lp - 1.0, -1.0)
        bc_row = lax.dot_general(mean_rows, bc.astype(BF16), _NT, preferred_element_type=F32)
        cb_ref[pl.ds(e, 1), :] = jnp.dot(bc_row.astype(BF16), before, preferred_element_type=F32)[:1].astype(I32)
        bend_w = jnp.tile(bst + bc, (1, rep))
        bst_w = jnp.tile(bst, (1, rep))
        blk_c = jnp.sum(jnp.where(bend_w <= c_row, 1.0, 0.0), axis=0, keepdims=True)
        onehot = blk_iota == blk_c
        bst_c = jnp.sum(jnp.where(onehot, bst_w, 0.0), axis=0, keepdims=True)
        r_c = c_row - bst_c
        ohb = jnp.where(onehot, 1.0, 0.0).astype(BF16)
        lp_t = lax.dot_general(lower, mb, _NT, preferred_element_type=F32)
        lp_c = jnp.dot(lp_t.astype(BF16), ohb, preferred_element_type=F32)
        tl_c = jnp.sum(jnp.where(lp_c <= r_c, 1.0, 0.0), axis=0, keepdims=True)
        idx_ref[pl.ds(e, 1), :] = (blk_c * LANES + tl_c).astype(I32)
        a_hi = a.astype(BF16)
        a_lo = (a - a_hi.astype(F32)).astype(BF16)
        at_hi = lax.dot_general(eye, a_hi, _NT, preferred_element_type=F32).astype(BF16)
        at_lo = lax.dot_general(eye, a_lo, _NT, preferred_element_type=F32).astype(BF16)
        g_c = (jnp.dot(at_hi, ohb, preferred_element_type=F32)
               + jnp.dot(at_lo, ohb, preferred_element_type=F32))
        gate_ref[pl.ds(e, 1), :] = jnp.sum(jnp.where(t_iota == tl_c, g_c, 0.0), axis=0, keepdims=True)
        return carry

    lax.fori_loop(0, N_EXPERTS, per_expert, 0, unroll=2)


def _route(aff2d, batch, seq):
    cap = EC_FACTOR * seq // N_EXPERTS
    nblk = seq // LANES
    rows = nblk * N_EXPERTS
    out_spec = pl.BlockSpec((None, N_EXPERTS, cap), lambda b: (b, 0, 0))
    return pl.pallas_call(
        functools.partial(_route_kernel, cap=cap),
        grid=(batch,),
        in_specs=[pl.BlockSpec((rows, LANES), lambda b: (b, 0))],
        out_specs=[out_spec, out_spec, pl.BlockSpec((rows, LANES), lambda b: (b, 0)),
                   pl.BlockSpec((None, N_EXPERTS, nblk), lambda b: (b, 0, 0))],
        out_shape=[jax.ShapeDtypeStruct((batch, N_EXPERTS, cap), I32),
                   jax.ShapeDtypeStruct((batch, N_EXPERTS, cap), F32),
                   jax.ShapeDtypeStruct((batch * rows, LANES), F32),
                   jax.ShapeDtypeStruct((batch, N_EXPERTS, nblk), I32)],
        scratch_shapes=[pltpu.VMEM((N_EXPERTS, LANES), I32), pltpu.VMEM((N_EXPERTS, LANES), F32)],
        compiler_params=_cparams(("parallel",)),
        name="route",
    )(aff2d)


def _ffn_kernel(idx_ref, nxt_ref, gate_ref, h_hbm, wg32_ref, wu32_ref, wd32_ref, y_ref, buf, sem,
                wg_ref, wu_ref, wd_ref, *, seq, tc, nsub):
    seq_id = pl.program_id(1)
    n_seq = pl.num_programs(1)
    step = pl.program_id(0) * n_seq + seq_id
    last_step = pl.num_programs(0) * n_seq - 1
    base = seq_id * seq
    next_base = jnp.where(seq_id + 1 < n_seq, seq_id + 1, 0) * seq

    @pl.when(seq_id == 0)
    def _():
        wg_ref[...] = wg32_ref[...].astype(BF16)
        wu_ref[...] = wu32_ref[...].astype(BF16)
        wd_ref[...] = wd32_ref[...].astype(BF16)

    n_tiles = D_MODEL // LANES

    def row_copy(ids, row0, j, r, slot):
        tok = pl.multiple_of((row0 + ids[0, 0, j * tc + r]) * n_tiles, n_tiles)
        return pltpu.make_async_copy(h_hbm.at[pl.ds(tok, n_tiles), :],
                                     buf.at[slot, pl.ds(r * n_tiles, n_tiles), :], sem.at[slot])

    def issue(ids, row0, j, slot):
        for r in range(tc):
            row_copy(ids, row0, j, r, slot).start()

    diag = lax.broadcasted_iota(I32, (tc, tc), 0) == lax.broadcasted_iota(I32, (tc, tc), 1)
    ones = jnp.ones((tc, LANES), BF16)

    ahead = 2

    @pl.when(step == 0)
    def _():
        for j in range(ahead):
            issue(idx_ref, base, j, j)

    for j in range(nsub):
        slot = j
        for r in range(tc):
            row_copy(idx_ref, base, j, r, slot).wait()
        xs = jnp.concatenate([buf.at[slot][pl.ds(c, tc, stride=n_tiles), :].astype(BF16) for c in range(n_tiles)],
                             axis=1)
        g = jnp.dot(xs, wg_ref[...], preferred_element_type=F32)
        u = jnp.dot(xs, wu_ref[...], preferred_element_type=F32)
        hm = (jax.nn.silu(g) * u).astype(BF16)
        y = jnp.dot(hm, wd_ref[...], preferred_element_type=F32)
        gr = jnp.broadcast_to(gate_ref[0, :, j * tc:(j + 1) * tc], (tc, tc))
        gcol = _split_dot(jnp.where(diag, gr, 0.0), ones)
        y_ref[j * tc:(j + 1) * tc, :] = (y * jnp.tile(gcol, (1, D_MODEL // LANES))).astype(BF16)
        if j + ahead < nsub:
            issue(idx_ref, base, j + ahead, j + ahead)
        else:
            issue(nxt_ref, next_base, j + ahead - nsub, j + ahead - nsub)

    @pl.when(step == last_step)
    def _():
        for j in range(ahead):
            for r in range(tc):
                row_copy(nxt_ref, next_base, j, r, j).wait()


def _ffn(idx, gates, h2d, w_gate, w_up, w_down, layer, seq, tc=256):
    b, ne, cap = idx.shape
    tc = min(tc, cap // 4)
    nsub = cap // tc
    assert nsub > 2
    idx3 = idx.reshape(b * ne, 1, cap)
    gate3 = gates.reshape(b * ne, 1, cap)
    slot = lambda ei, bi: (bi * ne + ei, 0, 0)
    next_slot = lambda ei, bi: (jnp.where(bi + 1 < b, (bi + 1) * ne + ei, jnp.minimum(ei + 1, ne - 1)), 0, 0)
    wspec = lambda shape: pl.BlockSpec((None, None) + shape, lambda ei, bi: (layer, ei, 0, 0))
    return pl.pallas_call(
        functools.partial(_ffn_kernel, seq=seq, tc=tc, nsub=nsub),
        grid=(ne, b),
        in_specs=[
            pl.BlockSpec((1, 1, cap), slot, memory_space=pltpu.SMEM),
            pl.BlockSpec((1, 1, cap), next_slot, memory_space=pltpu.SMEM),
            pl.BlockSpec((1, 1, cap), slot),
            pl.BlockSpec(memory_space=pl.ANY),
            wspec((D_MODEL, EXPERT_FF)), wspec((D_MODEL, EXPERT_FF)), wspec((EXPERT_FF, D_MODEL)),
        ],
        out_specs=pl.BlockSpec((None, None, cap, D_MODEL), lambda ei, bi: (bi, ei, 0, 0)),
        out_shape=jax.ShapeDtypeStruct((b, ne, cap, D_MODEL), BF16),
        scratch_shapes=[pltpu.VMEM((nsub, tc * (D_MODEL // LANES), LANES), F32), pltpu.SemaphoreType.DMA((nsub,)),
                        pltpu.VMEM((D_MODEL, EXPERT_FF), BF16), pltpu.VMEM((D_MODEL, EXPERT_FF), BF16),
                        pltpu.VMEM((EXPERT_FF, D_MODEL), BF16)],
        compiler_params=pltpu.CompilerParams(dimension_semantics=("arbitrary", "arbitrary"),
                                             vmem_limit_bytes=VMEM_LIMIT, disable_bounds_checks=True),
        name="ffn",
    )(idx3, idx3, gate3, h2d, w_gate, w_up, w_down)


_CTM = 256
_CWIN = 64
_CALIGN = 16


def _combine_kernel(cb_ref, x_ref, sp_ref, y_hbm, g_ref, o_ref, ybuf, xbuf, sem, xsem, *,
                    final, tiles_per_seq, nblk, cap):
    i = pl.program_id(0)
    n_tiles = pl.num_programs(0)
    b = i // tiles_per_seq
    slot = i % 2

    def window(tile, e):
        tb = tile // tiles_per_seq
        off = (tb * N_EXPERTS + e) * (nblk + 1) + (tile % tiles_per_seq) * (_CTM // LANES)
        s0 = cb_ref[off]
        s1 = cb_ref[off + _CTM // LANES]
        start = jnp.minimum((s0 // _CALIGN) * _CALIGN, cap - _CWIN)
        return s1, pl.multiple_of(start, _CALIGN)

    def fetch(tile, e, start, buf_slot):
        return pltpu.make_async_copy(y_hbm.at[tile // tiles_per_seq, e, pl.ds(start, _CWIN), :],
                                     ybuf.at[buf_slot, pl.ds(e * _CWIN, _CWIN), :], sem.at[buf_slot])

    def fetch_all(tile, buf_slot):
        for e in range(N_EXPERTS):
            fetch(tile, e, window(tile, e)[1], buf_slot).start()

    @pl.when(i == 0)
    def _():
        fetch_all(i, slot)

    @pl.when(i + 1 < n_tiles)
    def _():
        fetch_all(i + 1, 1 - slot)

    wins = [window(i, e) for e in range(N_EXPERTS)]
    pad = jnp.full((LANES - N_EXPERTS, LANES), -1.0, F32)
    sp_t = jnp.concatenate([jnp.concatenate([sp_ref[hf], pad], axis=0).T for hf in range(_CTM // LANES)],
                           axis=0)
    lane = lax.broadcasted_iota(I32, (1, _CWIN), 1).astype(F32)
    lane2 = lax.broadcasted_iota(I32, (1, LANES), 1)
    per_tile = LANES // _CWIN
    hits = []
    for e0 in range(0, N_EXPERTS, per_tile):
        rel = sp_t[:, e0:e0 + 1] - wins[e0][1].astype(F32)
        for k in range(1, per_tile):
            rel = jnp.where(lane2 < k * _CWIN, rel,
                            sp_t[:, e0 + k:e0 + k + 1] - (wins[e0 + k][1] - k * _CWIN).astype(F32))
        hits.append(jnp.where(rel == lane2.astype(F32), 1.0, 0.0).astype(BF16))
    for e in range(N_EXPERTS):
        fetch(i, e, wins[e][1], slot).wait()
    o_ref[...] = x_ref[...] + jnp.dot(jnp.concatenate(hits, axis=1), ybuf[slot], preferred_element_type=F32)
    for e in range(N_EXPERTS):
        s1, start = wins[e]
        col = sp_t[:, e:e + 1]

        def extra(k, carry, e=e, s1=s1, start=start, col=col):
            lo = start + (k + 1) * _CWIN
            st = pl.multiple_of(jnp.minimum(lo, cap - _CWIN), _CALIGN)
            cp = pltpu.make_async_copy(y_hbm.at[b, e, pl.ds(st, _CWIN), :], xbuf, xsem)
            cp.start()
            cp.wait()
            hit = jnp.where(jnp.logical_and(col - st.astype(F32) == lane, col >= lo.astype(F32)), 1.0, 0.0)
            o_ref[...] += jnp.dot(hit.astype(BF16), xbuf[...], preferred_element_type=F32)
            return carry

        n_extra = jnp.maximum(s1 - start - 1, 0) // _CWIN
        lax.fori_loop(0, n_extra, extra, 0)
    if final:
        x = o_ref[...]
        o_ref[...] = x * _rms_scale(x) * g_ref[...]


def _combine(x2d, spos, cb, y, g_final, final, seq):
    t = x2d.shape[0]
    batch, ne, cap, _ = y.shape
    nblk = seq // LANES
    cb_full = jnp.concatenate([cb, jnp.full((batch, ne, 1), cap, I32)], axis=-1).reshape(-1)
    spb = _CTM // LANES
    return pl.pallas_call(
        functools.partial(_combine_kernel, final=final, tiles_per_seq=seq // _CTM, nblk=nblk, cap=cap),
        grid_spec=pltpu.PrefetchScalarGridSpec(
            num_scalar_prefetch=1,
            grid=(t // _CTM,),
            in_specs=[
                pl.BlockSpec((_CTM, D_MODEL), lambda i, c: (i, 0)),
                pl.BlockSpec((spb, N_EXPERTS, LANES), lambda i, c: (i, 0, 0)),
                pl.BlockSpec(memory_space=pl.ANY),
                pl.BlockSpec((1, D_MODEL), lambda i, c: (0, 0)),
            ],
            out_specs=pl.BlockSpec((_CTM, D_MODEL), lambda i, c: (i, 0)),
            scratch_shapes=[pltpu.VMEM((2, N_EXPERTS * _CWIN, D_MODEL), BF16), pltpu.VMEM((_CWIN, D_MODEL), BF16),
                            pltpu.SemaphoreType.DMA((2,)), pltpu.SemaphoreType.DMA],
        ),
        out_shape=jax.ShapeDtypeStruct((t, D_MODEL), F32),
        compiler_params=_cparams(("arbitrary",)),
        name="combine",
    )(cb_full, x2d, spos, y, g_final.reshape(1, D_MODEL))


_SLAB_Q0 = 512
_SLAB_K0 = _SLAB_Q0 + D_HEADS * LANES
_SLAB_V0 = _SLAB_K0 + D_KV_HEADS * LANES
_ODD_COLS = _SLAB_V0 + D_KV_HEADS * LANES


def _proj_odd_kernel(x_ref, g_ref, wm_ref, cqn_ref, wq_ref, ckvn_ref, wkv_ref, dqn_ref, dkn_ref,
                     ccq_ref, s1cq_ref, s2cq_ref, cck_ref, s1ck_ref, s2ck_ref,
                     cdq_ref, s1dq_ref, s2dq_ref, cdk_ref, s1dk_ref, s2dk_ref,
                     qc_ref, kc_ref, vc_ref, qd_ref, kd_ref, vd_ref, stat_ref, *, steps_per_seq):
    lane = lax.broadcasted_iota(I32, (1, LANES), 1)
    one64 = jnp.where(lane == HEAD_DIM, 1.0, 0.0)
    last_lane = lane == LANES - 1
    half_rope = C_ROPE // 2
    stats = [jnp.zeros((1, LANES), F32), jnp.zeros((1, LANES), F32)]
    ones_mat = jnp.ones((LANES, LANES), BF16)

    def with_norm(val, fill, row, col):
        n2 = jnp.dot((val * val).astype(BF16), ones_mat, preferred_element_type=F32)
        stats[row] = jnp.where(lane == col, jnp.maximum(jnp.max(n2, axis=0, keepdims=True), stats[row]), stats[row])
        return jnp.where(last_lane, fill, val).astype(BF16)

    def head_norm(xg, gn_ref):
        ss = jnp.sum(xg * xg, axis=-1, keepdims=True) * (1.0 / HEAD_DIM)
        return xg * lax.rsqrt(ss + EPS) * gn_ref[...]

    n_chunks = 2
    rows_per = x_ref.shape[0] // n_chunks
    for c in range(n_chunks):
        rows = slice(c * rows_per, (c + 1) * rows_per)
        tab = lambda *refs: [r[rows, :] for r in refs]
        x = x_ref[rows, :]
        y = (x * _rms_scale(x) * g_ref[...]).astype(BF16)
        pm = jnp.dot(y, wm_ref[...], preferred_element_type=F32)
        cq = pm[:, :C_Q_RANK]
        cqn = (cq * _rms_scale(cq) * cqn_ref[...]).astype(BF16)
        qc = jnp.dot(cqn, wq_ref[...], preferred_element_type=F32)
        ckv = pm[:, C_Q_RANK:C_Q_RANK + C_KV_RANK]
        ckvn = (ckv * _rms_scale(ckv) * ckvn_ref[...]).astype(BF16)
        kv = jnp.dot(ckvn, wkv_ref[...], preferred_element_type=F32)
        kr = _rope3(pm[:, C_Q_RANK + C_KV_RANK:_SLAB_Q0], *tab(cck_ref, s1ck_ref, s2ck_ref), half_rope)
        t_cq = tab(ccq_ref, s1cq_ref, s2cq_ref)
        for h in range(C_HEADS):
            sl = slice(h * LANES, (h + 1) * LANES)
            qc_ref[rows, sl] = with_norm(_rope3(qc[:, sl], *t_cq, half_rope), 1.0, 1, h)
            kc_ref[rows, sl] = with_norm(kv[:, sl] + kr, -1.0, 0, h)
            vc_ref[rows, sl] = (kv[:, C_HEADS * LANES + h * LANES:C_HEADS * LANES + (h + 1) * LANES]
                                + one64).astype(BF16)
        t_dq = tab(cdq_ref, s1dq_ref, s2dq_ref)
        t_dk = tab(cdk_ref, s1dk_ref, s2dk_ref)
        for g in range(D_HEADS):
            xg = pm[:, _SLAB_Q0 + g * LANES:_SLAB_Q0 + (g + 1) * LANES]
            qd_ref[rows, g * LANES:(g + 1) * LANES] = with_norm(
                _rope3(head_norm(xg, dqn_ref), *t_dq, HEAD_DIM // 4), 1.0, 1, C_HEADS + g)
        for g in range(D_KV_HEADS):
            sl = slice(g * LANES, (g + 1) * LANES)
            xg = pm[:, _SLAB_K0 + g * LANES:_SLAB_K0 + (g + 1) * LANES]
            kd_ref[rows, sl] = with_norm(_rope3(head_norm(xg, dkn_ref), *t_dk, HEAD_DIM // 4), -1.0, 0, C_HEADS + g)
            vd_ref[rows, sl] = (pm[:, _SLAB_V0 + g * LANES:_SLAB_V0 + (g + 1) * LANES] + one64).astype(BF16)

    new = jnp.concatenate(stats + [jnp.zeros((6, LANES), F32)], axis=0)

    @pl.when(pl.program_id(0) % steps_per_seq == 0)
    def _():
        stat_ref[...] = new

    @pl.when(pl.program_id(0) % steps_per_seq != 0)
    def _():
        stat_ref[...] = jnp.maximum(stat_ref[...], new)


def _slabs(w, n_heads, width, lane_off=0):
    k = w.shape[0]
    w3 = w.reshape(k, n_heads, width)
    w3 = jnp.pad(w3, ((0, 0), (0, 0), (lane_off, LANES - width - lane_off)))
    return w3.reshape(k, n_heads * LANES)


def _axial_tables(row, col, scale):
    half = HEAD_DIM // 2
    cr, s1r, s2r = _rope_tables(row, D_THETA, half, 0, LANES, scale)
    cc, s1c, s2c = _rope_tables(col, D_THETA, half, half, LANES, scale)
    lane = jnp.arange(LANES)[None, :]
    return jnp.where(lane < half, cr, cc), s1r + s1c, s2r + s2c


def _proj_odd(x2d, seq, g_mix, w_in, cq_norm, w_cq_up, ckv_norm, w_ckv_up, dq_norm, dk_norm, tm=512):
    t = x2d.shape[0]
    nblk = seq // tm
    o1 = C_Q_RANK
    o2 = o1 + C_KV_RANK
    o3 = o2 + C_ROPE
    o4 = o3 + D_HEADS * HEAD_DIM
    o5 = o4 + D_KV_HEADS * HEAD_DIM
    wm = jnp.concatenate([
        w_in[:, :o2],
        _slabs(w_in[:, o2:o3], 1, C_ROPE, C_NOPE),
        _slabs(w_in[:, o3:o4], D_HEADS, HEAD_DIM),
        _slabs(w_in[:, o4:o5], D_KV_HEADS, HEAD_DIM),
        _slabs(w_in[:, o5:], D_KV_HEADS, HEAD_DIM),
    ], axis=1).astype(BF16)
    assert wm.shape[1] == _ODD_COLS
    wq = _slabs(w_cq_up, C_HEADS, C_NOPE + C_ROPE).astype(BF16)
    kv3 = w_ckv_up.reshape(C_KV_RANK, C_HEADS, 2 * HEAD_DIM)
    wkv = jnp.concatenate([
        _slabs(kv3[:, :, :C_NOPE].reshape(C_KV_RANK, -1), C_HEADS, C_NOPE),
        _slabs(kv3[:, :, C_NOPE:].reshape(C_KV_RANK, -1), C_HEADS, HEAD_DIM),
    ], axis=1).astype(BF16)
    pad64 = lambda g: jnp.pad(g, (0, LANES - HEAD_DIM)).reshape(1, LANES)

    pos = jnp.arange(seq, dtype=I32)
    row_pos = pos // GRID_W
    col_pos = pos % GRID_W
    c_scale = (C_NOPE + C_ROPE) ** -0.5 * LOG2E
    d_scale = HEAD_DIM ** -0.5 * LOG2E
    tabs = (_rope_tables(pos, ROPE_THETA, C_ROPE, C_NOPE, LANES, c_scale)
            + _rope_tables(pos, ROPE_THETA, C_ROPE, C_NOPE, LANES, 1.0)
            + _axial_tables(row_pos, col_pos, d_scale)
            + _axial_tables(row_pos, col_pos, 1.0))

    row = lambda i: (i, 0)
    full = lambda i: (0, 0)
    tspec = pl.BlockSpec((tm, LANES), lambda i: (i % nblk, 0))
    wide = C_HEADS * LANES
    kvw = D_KV_HEADS * LANES
    return pl.pallas_call(
        functools.partial(_proj_odd_kernel, steps_per_seq=nblk),
        grid=(t // tm,),
        in_specs=[
            pl.BlockSpec((tm, D_MODEL), row),
            pl.BlockSpec((1, D_MODEL), full),
            pl.BlockSpec(wm.shape, full),
            pl.BlockSpec((1, C_Q_RANK), full),
            pl.BlockSpec(wq.shape, full),
            pl.BlockSpec((1, C_KV_RANK), full),
            pl.BlockSpec(wkv.shape, full),
            pl.BlockSpec((1, LANES), full),
            pl.BlockSpec((1, LANES), full),
        ] + [tspec] * 12,
        out_specs=[
            pl.BlockSpec((tm, wide), row), pl.BlockSpec((tm, wide), row), pl.BlockSpec((tm, wide), row),
            pl.BlockSpec((tm, wide), row), pl.BlockSpec((tm, kvw), row), pl.BlockSpec((tm, kvw), row),
            pl.BlockSpec((None, 8, LANES), lambda i: (i // nblk, 0, 0)),
        ],
        out_shape=[
            jax.ShapeDtypeStruct((t, wide), BF16), jax.ShapeDtypeStruct((t, wide), BF16),
            jax.ShapeDtypeStruct((t, wide), BF16), jax.ShapeDtypeStruct((t, wide), BF16),
            jax.ShapeDtypeStruct((t, kvw), BF16), jax.ShapeDtypeStruct((t, kvw), BF16),
            jax.ShapeDtypeStruct((t // seq, 8, LANES), F32),
        ],
        compiler_params=_cparams(("arbitrary",)),
        name="proj_odd",
    )(x2d, g_mix.reshape(1, D_MODEL), wm, cq_norm.reshape(1, -1), wq, ckv_norm.reshape(1, -1), wkv,
      pad64(dq_norm), pad64(dk_norm), *tabs)


def _flash_kernel(q_ref, k_ref, v_ref, o_ref, qs_sc, m_sc, acc_sc, *, group, tq, tk):
    ki = pl.program_id(3)

    @pl.when(ki == 0)
    def _():
        for g in range(group):
            qs_sc[g * tq:(g + 1) * tq, :] = q_ref[:, g * LANES:(g + 1) * LANES]
        m_sc[...] = jnp.full(m_sc.shape, NEG, F32)
        acc_sc[...] = jnp.zeros(acc_sc.shape, F32)

    s = lax.dot_general(qs_sc[...], k_ref[...], _NT, preferred_element_type=F32)
    m_prev = m_sc[...]
    m_new = jnp.maximum(m_prev, jnp.max(s, axis=1, keepdims=True))
    alpha = jnp.exp2(m_prev - m_new)
    p = jnp.exp2(s - jnp.tile(m_new, (1, tk // LANES)))
    acc_sc[...] = alpha * acc_sc[...] + jnp.dot(p.astype(BF16), v_ref[...], preferred_element_type=F32)
    m_sc[...] = m_new

    @pl.when(ki == pl.num_programs(3) - 1)
    def _():
        acc = acc_sc[...]
        o = acc / acc[:, HEAD_DIM:HEAD_DIM + 1]
        for g in range(group):
            o_ref[:, g * LANES:(g + 1) * LANES] = o[g * tq:(g + 1) * tq].astype(BF16)


_FLASH_CHUNK = 2048
_V_ROWS = 80


def _flash_bounded_kernel(bound_ref, q_ref, k_ref, v_ref, o_ref, qs_sc, acc_sc, *, group, tq):
    ki = pl.program_id(3)
    head = pl.program_id(0) * pl.num_programs(1) + pl.program_id(1)

    @pl.when(ki == 0)
    def _():
        fix = jnp.where(lax.broadcasted_iota(I32, (1, LANES), 1) == LANES - 1, bound_ref[head], 1.0)
        for g in range(group):
            qs_sc[g * tq:(g + 1) * tq, :] = (q_ref[:, g * LANES:(g + 1) * LANES].astype(F32) * fix).astype(BF16)
        acc_sc[...] = jnp.zeros(acc_sc.shape, F32)

    chunk = min(_FLASH_CHUNK, k_ref.shape[0])
    n_chunks = k_ref.shape[0] // chunk
    qs = qs_sc[...]

    def scores(c):
        return lax.dot_general(k_ref[c * chunk:(c + 1) * chunk, :], qs, _NT, preferred_element_type=F32)

    def values(c, s_t):
        return lax.dot_general(v_ref[c * chunk:(c + 1) * chunk, :_V_ROWS], jnp.exp2(s_t).astype(BF16),
                               (((0,), (0,)), ((), ())), preferred_element_type=F32)

    acc = acc_sc[...]
    s_prev = scores(0)
    for c in range(1, n_chunks):
        s_next = scores(c)
        acc = acc + values(c - 1, s_prev)
        s_prev = s_next
    acc_sc[...] = acc + values(n_chunks - 1, s_prev)

    @pl.when(ki == pl.num_programs(3) - 1)
    def _():
        acc = acc_sc[...]
        o_t = acc / acc[HEAD_DIM:HEAD_DIM + 1, :]
        o = jnp.concatenate([o_t, jnp.zeros((LANES - _V_ROWS, o_t.shape[1]), F32)], axis=0).T
        for g in range(group):
            o_ref[:, g * LANES:(g + 1) * LANES] = o[g * tq:(g + 1) * tq].astype(BF16)


def _flash(q, k, v, group, nk, bounded, rows=1024, tk=512, tk_bounded=8192):
    b, s, qw = q.shape
    hk = k.shape[2] // LANES
    tq = rows // group
    tk = min(tk, s)
    tkb = min(tk_bounded, s)
    out_shape = jax.ShapeDtypeStruct((b, s, qw), BF16)
    sem = ("parallel", "parallel", "parallel", "arbitrary")

    def running_max(q, k, v, nk):
        qspec = pl.BlockSpec((None, tq, group * LANES), lambda bi, hi, qi, ki: (bi, qi, hi))
        kspec = pl.BlockSpec((None, tk, LANES), lambda bi, hi, qi, ki: (bi, ki, hi))
        return pl.pallas_call(
            functools.partial(_flash_kernel, group=group, tq=tq, tk=tk),
            grid=(b, hk, s // tq, s // tk),
            in_specs=[qspec, kspec, kspec],
            out_specs=qspec,
            out_shape=out_shape,
            scratch_shapes=[pltpu.VMEM((rows, LANES), BF16), pltpu.VMEM((rows, LANES), F32),
                            pltpu.VMEM((rows, LANES), F32)],
            compiler_params=_cparams(sem),
            name="flash",
        )(q, k, v)

    def bound(q, k, v, nk):
        qspec = pl.BlockSpec((None, tq, group * LANES), lambda bi, hi, qi, ki, nkr: (bi, qi, hi))
        kspec = pl.BlockSpec((None, tkb, LANES), lambda bi, hi, qi, ki, nkr: (bi, ki, hi))
        return pl.pallas_call(
            functools.partial(_flash_bounded_kernel, group=group, tq=tq),
            grid_spec=pltpu.PrefetchScalarGridSpec(
                num_scalar_prefetch=1,
                grid=(b, hk, s // tq, s // tkb),
                in_specs=[qspec, kspec, kspec],
                out_specs=qspec,
                scratch_shapes=[pltpu.VMEM((rows, LANES), BF16), pltpu.VMEM((_V_ROWS, rows), F32)],
            ),
            out_shape=out_shape,
            compiler_params=_cparams(sem),
            name="flash_bounded",
        )(nk.reshape(-1), q, k, v)

    return lax.cond(bounded, bound, running_max, q, k, v, nk)


def _moe(x1, h2, aff, batch, seq, w_gate, w_up, w_down, layer, g_final, final):
    idx, gates, spos, cb = _route(aff.reshape(-1, LANES), batch, seq)
    y = _ffn(idx, gates, h2, w_gate, w_up, w_down, layer, seq)
    return _combine(x1, spos.reshape(-1, N_EXPERTS, LANES), cb, y, g_final, final, seq)


def kernel(x, norm_mix, norm_ffn, even_w_in, even_gmlp_norm, even_w_spatial, even_b_spatial, even_w_out,
           odd_w_in, odd_cq_norm, odd_w_cq_up, odd_ckv_norm, odd_w_ckv_up, odd_dq_norm, odd_dk_norm, odd_w_out,
           moe_w_router, moe_w_gate, moe_w_up, moe_w_down, final_norm):
    b, s, d = x.shape
    depth = norm_mix.shape[0]
    x2d = x.reshape(b * s, d)
    for i in range(depth):
        j = i // 2
        last = i == depth - 1
        if i % 2 == 0:
            q, k, v, go = _proj_even(x2d, s, norm_mix[i], even_w_in[j], even_gmlp_norm[j], even_w_spatial[j],
                                     even_b_spatial[j])
            a = _dilated(q.reshape(b, s, A_WIDTH), k.reshape(b, s, A_WIDTH), v.reshape(b, s, A_WIDTH))
            x1, h2, aff = _outproj(x2d, a.reshape(b * s, A_WIDTH), go, even_w_out[j][:A_WIDTH],
                                   even_w_out[j][A_WIDTH:], norm_ffn[i], moe_w_router[i])
        else:
            qc, kc, vc, qd, kd, vd, stat = _proj_odd(x2d, s, norm_mix[i], odd_w_in[j], odd_cq_norm[j],
                                                     odd_w_cq_up[j], odd_ckv_norm[j], odd_w_ckv_up[j],
                                                     odd_dq_norm[j], odd_dk_norm[j])
            grp = D_HEADS // D_KV_HEADS
            k2_c, k2_d = stat[:, 0, :C_HEADS], stat[:, 0, C_HEADS:C_HEADS + D_KV_HEADS]
            q2_c = stat[:, 1, :C_HEADS]
            q2_d = jnp.max(stat[:, 1, C_HEADS:C_HEADS + D_HEADS].reshape(b, D_KV_HEADS, grp), axis=-1)
            bound_c = jnp.sqrt(q2_c * k2_c) * _NORM_MARGIN ** 2
            bound_d = jnp.sqrt(q2_d * k2_d) * _NORM_MARGIN ** 2
            bounded = jnp.maximum(jnp.max(bound_c), jnp.max(bound_d)) <= _MAX_SCORE_BOUND
            r3 = lambda z: z.reshape(b, s, -1)
            oc = _flash(r3(qc), r3(kc), r3(vc), 1, bound_c, bounded)
            od = _flash(r3(qd), r3(kd), r3(vd), grp, bound_d, bounded)
            cw = C_HEADS * HEAD_DIM
            x1, h2, aff = _outproj(x2d, oc.reshape(b * s, -1), od.reshape(b * s, -1),
                                   _slabs(odd_w_out[j][:cw].T, C_HEADS, HEAD_DIM).T,
                                   _slabs(odd_w_out[j][cw:].T, D_HEADS, HEAD_DIM).T,
                                   norm_ffn[i], moe_w_router[i])
        x2d = _moe(x1, h2, aff, b, s, moe_w_gate, moe_w_up, moe_w_down, i, final_norm, last)
    return x2d.reshape(b, s, d)
```

```python
import functools
import math

import jax
import jax.numpy as jnp
from jax import lax
from jax.experimental import pallas as pl
from jax.experimental.pallas import tpu as pltpu

F32 = jnp.float32
BF16 = jnp.bfloat16
I32 = jnp.int32

EPS = 1e-6
NEG = -1e30
LOG2E = 1.4426950408889634

D_MODEL = 1024
HEAD_DIM = 64
ROPE_THETA = 500000.0
ROT_DIM = 16
GRID_W = 64
A_HEADS = 12
A_WIDTH = 768
A_DILATIONS = (1, 4, 16)
A_RADIUS = 64
B_WIDTH = 256
B_GROUPS = 4
B_CHUNK = 128
C_HEADS = 8
C_Q_RANK = 256
C_KV_RANK = 128
C_NOPE = 64
C_ROPE = 32
D_HEADS = 8
D_KV_HEADS = 2
D_THETA = 10000.0
N_EXPERTS = 16
EC_FACTOR = 2
EXPERT_FF = 512

_SEARCH_STEPS = 64
_NORM_MARGIN = 1.01
_MAX_SCORE_BOUND = 55.0

LANES = 128
VMEM_LIMIT = 48 * 1024 * 1024
_DILATED_VMEM_LIMIT = 56 * 1024 * 1024

_NT = (((1,), (1,)), ((), ()))


def _cparams(sem):
    return pltpu.CompilerParams(dimension_semantics=sem, vmem_limit_bytes=VMEM_LIMIT)


def _rms_scale(x):
    return lax.rsqrt(jnp.mean(x * x, axis=-1, keepdims=True) + EPS)


def _rope3(a, c, s1, s2, shift):
    return a * c + pltpu.roll(a, LANES - shift, 1) * s1 + pltpu.roll(a, shift, 1) * s2


def _split_dot(x, w_bf16):
    hi = x.astype(BF16)
    lo = (x - hi.astype(F32)).astype(BF16)
    return (jnp.dot(hi, w_bf16, preferred_element_type=F32)
            + jnp.dot(lo, w_bf16, preferred_element_type=F32))


def _proj_even_kernel(x_ref, g_ref, w_ref, cq_ref, s1q_ref, s2q_ref, ck_ref, s1k_ref, s2k_ref,
                      gn_ref, gmat_ref, ws_ref, bs_ref,
                      q_ref, k_ref, v_ref, go_ref):
    x = x_ref[...]
    y = (x * _rms_scale(x) * g_ref[...]).astype(BF16)
    tm = x.shape[0]

    aq = jnp.dot(y, w_ref[:, 0:A_WIDTH], preferred_element_type=F32)
    tq = (cq_ref[...], s1q_ref[...], s2q_ref[...])
    for j in range(A_WIDTH // LANES):
        sl = slice(j * LANES, (j + 1) * LANES)
        q_ref[:, sl] = _rope3(aq[:, sl], *tq, ROT_DIM // 2)
    ak = jnp.dot(y, w_ref[:, A_WIDTH:2 * A_WIDTH], preferred_element_type=F32)
    tk = (ck_ref[...], s1k_ref[...], s2k_ref[...])
    for j in range(A_WIDTH // LANES):
        sl = slice(j * LANES, (j + 1) * LANES)
        k_ref[:, sl] = _rope3(ak[:, sl], *tk, ROT_DIM // 2)
    v_ref[...] = jnp.dot(y, w_ref[:, 2 * A_WIDTH:3 * A_WIDTH], preferred_element_type=F32)

    z = jnp.dot(y, w_ref[:, 3 * A_WIDTH:3 * A_WIDTH + 2 * B_WIDTH], preferred_element_type=F32)
    ge = jax.nn.gelu(z)
    u = ge[:, :B_WIDTH]
    vv = ge[:, B_WIDTH:]
    ss = _split_dot(vv * vv, gmat_ref[...])
    vn = (vv * lax.rsqrt(ss + EPS) * gn_ref[...]).astype(BF16)
    grp = lax.broadcasted_iota(I32, (B_CHUNK, B_WIDTH), 1) // (B_WIDTH // B_GROUPS)
    for c in range(tm // B_CHUNK):
        rows = slice(c * B_CHUNK, (c + 1) * B_CHUNK)
        vc = vn[rows]
        mg = [jnp.dot(ws_ref[g], vc, preferred_element_type=F32) for g in range(B_GROUPS)]
        mixed = jnp.where(grp == 0, mg[0], jnp.where(grp == 1, mg[1], jnp.where(grp == 2, mg[2], mg[3])))
        go_ref[rows, :] = (u[rows] * (mixed + bs_ref[...])).astype(BF16)


def _rope_tables(pos, theta, r, lane_off, period, scale):
    half = r // 2
    inv = jnp.power(jnp.float32(theta), -jnp.arange(half, dtype=F32) * (2.0 / r))
    ang = pos.astype(F32)[:, None] * inv[None, :]
    cos, sin = jnp.cos(ang), jnp.sin(ang)
    o = (jnp.arange(LANES) % period) - lane_off
    in_lo = (o >= 0) & (o < half)
    in_hi = (o >= half) & (o < r)
    idx = jnp.clip(jnp.where(in_hi, o - half, o), 0, half - 1)
    c = jnp.where((in_lo | in_hi)[None, :], cos[:, idx], 1.0)
    s1 = jnp.where(in_lo[None, :], -sin[:, idx], 0.0)
    s2 = jnp.where(in_hi[None, :], sin[:, idx], 0.0)
    return c * scale, s1 * scale, s2 * scale


def _proj_even(x2d, seq, g_mix, w_in, gmlp_norm, w_s, b_s, tm=512):
    t = x2d.shape[0]
    nblk = seq // tm
    pos = jnp.arange(seq, dtype=I32)
    qscale = HEAD_DIM ** -0.5 * LOG2E
    cq, s1q, s2q = _rope_tables(pos, ROPE_THETA, ROT_DIM, 0, HEAD_DIM, qscale)
    ck, s1k, s2k = _rope_tables(pos, ROPE_THETA, ROT_DIM, 0, HEAD_DIM, 1.0)
    gdim = B_WIDTH // B_GROUPS
    gid = jnp.arange(B_WIDTH) // gdim
    gmat = jnp.where(gid[:, None] == gid[None, :], 1.0 / gdim, 0.0).astype(BF16)
    bias = jnp.repeat(b_s.T, gdim, axis=1)
    row = lambda i: (i, 0)
    tab = lambda i: (i % nblk, 0)
    full = lambda i: (0, 0)
    tspec = pl.BlockSpec((tm, LANES), tab)
    return pl.pallas_call(
        _proj_even_kernel,
        grid=(t // tm,),
        in_specs=[
            pl.BlockSpec((tm, D_MODEL), row),
            pl.BlockSpec((1, D_MODEL), full),
            pl.BlockSpec(w_in.shape, full),
            tspec, tspec, tspec, tspec, tspec, tspec,
            pl.BlockSpec((1, B_WIDTH), full),
            pl.BlockSpec((B_WIDTH, B_WIDTH), full),
            pl.BlockSpec((B_GROUPS, B_CHUNK, B_CHUNK), lambda i: (0, 0, 0)),
            pl.BlockSpec((B_CHUNK, B_WIDTH), full),
        ],
        out_specs=[
            pl.BlockSpec((tm, A_WIDTH), row),
            pl.BlockSpec((tm, A_WIDTH), row),
            pl.BlockSpec((tm, A_WIDTH), row),
            pl.BlockSpec((tm, B_WIDTH), row),
        ],
        out_shape=[
            jax.ShapeDtypeStruct((t, A_WIDTH), F32),
            jax.ShapeDtypeStruct((t, A_WIDTH), F32),
            jax.ShapeDtypeStruct((t, A_WIDTH), F32),
            jax.ShapeDtypeStruct((t, B_WIDTH), BF16),
        ],
        compiler_params=_cparams(("parallel",)),
        name="proj_even",
    )(x2d, g_mix.reshape(1, D_MODEL), w_in.astype(BF16), cq, s1q, s2q, ck, s1k, s2k,
      gmlp_norm.reshape(1, B_WIDTH), gmat, w_s.astype(BF16), bias)


_TQ = 128
_TK = _TQ + 2 * A_RADIUS
_NORM_ROWS = 1024


def _dilated_kernel(q_ref, k_ref, v_ref, o_ref, m_sc, l_sc, bias_sc, *, seq):
    half0 = lax.broadcasted_iota(I32, (1, LANES), 1) < HEAD_DIM
    same_head = jnp.where(lax.broadcasted_iota(I32, (LANES, LANES), 0) // HEAD_DIM
                          == lax.broadcasted_iota(I32, (LANES, LANES), 1) // HEAD_DIM,
                          1.0, 0.0).astype(BF16)
    n_pat = len(A_DILATIONS)

    def max_head_sq(ref):
        def body(c, mx):
            x = ref[pl.ds(pl.multiple_of(c * _NORM_ROWS, _NORM_ROWS), _NORM_ROWS), :]
            n2 = jnp.dot((x * x).astype(BF16), same_head, preferred_element_type=F32)
            return jnp.maximum(mx, jnp.max(n2, axis=0, keepdims=True))
        return lax.fori_loop(0, seq // _NORM_ROWS, body, jnp.zeros((1, LANES), F32))

    both = jnp.sqrt(max_head_sq(q_ref) * max_head_sq(k_ref)) * (_NORM_MARGIN * _NORM_MARGIN)
    both = jnp.broadcast_to(both, (8, LANES))
    other = pltpu.roll(both, HEAD_DIM, 1)
    bound = [jnp.where(half0, both, other)[:1], jnp.where(half0, other, both)[:1]]
    worst = jnp.max(both)

    diff = lax.broadcasted_iota(I32, (_TQ, _TK), 1) - lax.broadcasted_iota(I32, (_TQ, _TK), 0)
    for case in range(3):
        band = jnp.where(jnp.abs(diff - case * A_RADIUS) <= A_RADIUS, 0.0, NEG)
        bias_sc[6 + case] = band
        for h in range(2):
            bias_sc[3 * h + case] = band - jnp.tile(bound[h], (1, _TK // LANES))
    one_bf16 = jnp.ones((), BF16)

    def run(bounded):
        for pi, d in enumerate(A_DILATIONS):
            cls_len = seq // d
            tpc = cls_len // _TQ

            def tile(qrows, kb, vb, case, pi=pi):
                q = q_ref[qrows, :]
                parts = []
                for h in range(2):
                    qh = jnp.where(half0 if h == 0 else jnp.logical_not(half0), q, 0.0).astype(BF16)
                    s = lax.dot_general(qh, kb, _NT, preferred_element_type=F32)
                    if bounded:
                        mt = None
                        p = jnp.exp2(s + bias_sc[3 * h + case]).astype(BF16)
                    else:
                        s = s + bias_sc[6 + case]
                        mt = jnp.max(s, axis=-1, keepdims=True)
                        p = jnp.exp2(s - mt).astype(BF16)
                    vh = jnp.where(half0 if h == 0 else jnp.logical_not(half0), vb, one_bf16)
                    parts.append((mt, jnp.dot(p, vh, preferred_element_type=F32)))
                ot = jnp.where(half0, parts[0][1], parts[1][1])
                lt = pltpu.roll(jnp.where(half0, parts[1][1], parts[0][1]), HEAD_DIM, 1)
                if bounded:
                    if pi > 0:
                        lt = l_sc[qrows, :] + lt
                        ot = o_ref[qrows, :] + ot
                else:
                    mt = jnp.where(half0, parts[0][0], parts[1][0])
                    if pi > 0:
                        mp = m_sc[qrows, :]
                        mn = jnp.maximum(mp, mt)
                        a = jnp.exp2(mp - mn)
                        b = jnp.exp2(mt - mn)
                        lt = a * l_sc[qrows, :] + b * lt
                        ot = a * o_ref[qrows, :] + b * ot
                        mt = mn
                    if pi < n_pat - 1:
                        m_sc[qrows, :] = mt
                if pi == n_pat - 1:
                    o_ref[qrows, :] = ot / lt
                else:
                    l_sc[qrows, :] = lt
                    o_ref[qrows, :] = ot

            def window(l0, cls_len=cls_len):
                if isinstance(l0, int):
                    kst = min(max(l0 - A_RADIUS, 0), cls_len - _TK)
                else:
                    kst = jnp.clip(l0 - A_RADIUS, 0, cls_len - _TK)
                return kst, (l0 - kst) // A_RADIUS

            if d < 8:

                def body(j, carry, d=d, tpc=tpc):
                    i = j // tpc
                    l0 = (j % tpc) * _TQ
                    kst, case = window(l0)
                    if d == 1:
                        qrows = pl.ds(pl.multiple_of(l0, _TQ), _TQ)
                        krows = pl.ds(pl.multiple_of(kst, A_RADIUS), _TK)
                    else:
                        qrows = pl.ds(l0 * d + i, _TQ, stride=d)
                        krows = pl.ds(kst * d + i, _TK, stride=d)
                    tile(qrows, k_ref[krows, :].astype(BF16), v_ref[krows, :].astype(BF16), case)
                    return carry

                lax.fori_loop(0, seq // _TQ, body, 0, unroll=16)
            else:

                def body(i, carry, d=d, cls_len=cls_len, tpc=tpc):
                    cls = pl.ds(i, cls_len, stride=d)
                    kc = k_ref[cls, :].astype(BF16)
                    vc = v_ref[cls, :].astype(BF16)
                    for n in range(tpc):
                        kst, case = window(n * _TQ)
                        tile(pl.ds(n * _TQ * d + i, _TQ, stride=d), kc[kst:kst + _TK], vc[kst:kst + _TK], case)
                    return carry

                lax.fori_loop(0, d, body, 0, unroll=4)

    @pl.when(worst <= _MAX_SCORE_BOUND)
    def _():
        run(True)

    @pl.when(jnp.logical_not(worst <= _MAX_SCORE_BOUND))
    def _():
        run(False)


def _dilated(q, k, v):
    b, s, w = q.shape
    spec = pl.BlockSpec((None, s, LANES), lambda bi, hi: (bi, 0, hi))
    return pl.pallas_call(
        functools.partial(_dilated_kernel, seq=s),
        grid=(b, w // LANES),
        in_specs=[spec, spec, spec],
        out_specs=spec,
        out_shape=jax.ShapeDtypeStruct((b, s, w), F32),
        scratch_shapes=[pltpu.VMEM((s, LANES), F32), pltpu.VMEM((s, LANES), F32),
                        pltpu.VMEM((9, _TQ, _TK), F32)],
        compiler_params=pltpu.CompilerParams(dimension_semantics=("parallel", "parallel"),
                                             vmem_limit_bytes=_DILATED_VMEM_LIMIT),
        name="dilated",
    )(q, k, v)


def _outproj_kernel(x_ref, a_ref, b_ref, wa_ref, wb_ref, gf_ref, wr_ref, x1_ref, h2_ref, aff_ref):
    x1 = (x_ref[...]
          + jnp.dot(a_ref[...].astype(BF16), wa_ref[...], preferred_element_type=F32)
          + jnp.dot(b_ref[...].astype(BF16), wb_ref[...], preferred_element_type=F32))
    x1_ref[...] = x1
    h2 = x1 * _rms_scale(x1) * gf_ref[...]
    n_tiles = D_MODEL // LANES
    for j in range(n_tiles):
        h2_ref[pl.ds(j, h2.shape[0], stride=n_tiles), :] = h2[:, j * LANES:(j + 1) * LANES]
    hi = h2.astype(BF16)
    lo = (h2 - hi.astype(F32)).astype(BF16)
    both = jnp.dot(hi, wr_ref[...], preferred_element_type=F32)
    lg = (both[:, :LANES] + both[:, LANES:]
          + jnp.dot(lo, wr_ref[:, :LANES], preferred_element_type=F32))
    valid = lax.broadcasted_iota(I32, lg.shape, 1) < N_EXPERTS
    lg = jnp.where(valid, lg, NEG)
    e = jnp.exp(lg - jnp.max(lg, axis=-1, keepdims=True))
    aff = e / jnp.sum(e, axis=-1, keepdims=True)
    aff_t = aff.T
    for j in range(aff.shape[0] // LANES):
        aff_ref[j] = aff_t[:N_EXPERTS, j * LANES:(j + 1) * LANES]


def _outproj(x2d, a, b, wa, wb, g_ffn, w_router, tm=512):
    t = x2d.shape[0]
    wr = jnp.pad(w_router, ((0, 0), (0, LANES - N_EXPERTS)))
    wr_hi = wr.astype(BF16)
    wr_lo = (wr - wr_hi.astype(F32)).astype(BF16)
    wr2 = jnp.concatenate([wr_hi, wr_lo], axis=1)
    row = lambda i: (i, 0)
    full = lambda i: (0, 0)
    return pl.pallas_call(
        _outproj_kernel,
        grid=(t // tm,),
        in_specs=[
            pl.BlockSpec((tm, D_MODEL), row),
            pl.BlockSpec((tm, a.shape[1]), row),
            pl.BlockSpec((tm, b.shape[1]), row),
            pl.BlockSpec(wa.shape, full),
            pl.BlockSpec(wb.shape, full),
            pl.BlockSpec((1, D_MODEL), full),
            pl.BlockSpec((D_MODEL, 2 * LANES), full),
        ],
        out_specs=[
            pl.BlockSpec((tm, D_MODEL), row),
            pl.BlockSpec((tm * (D_MODEL // LANES), LANES), row),
            pl.BlockSpec((tm // LANES, N_EXPERTS, LANES), lambda i: (i, 0, 0)),
        ],
        out_shape=[
            jax.ShapeDtypeStruct((t, D_MODEL), F32),
            jax.ShapeDtypeStruct((t * (D_MODEL // LANES), LANES), F32),
            jax.ShapeDtypeStruct((t // LANES, N_EXPERTS, LANES), F32),
        ],
        compiler_params=_cparams(("parallel",)),
        name="outproj",
    )(x2d, a, b, wa.astype(BF16), wb.astype(BF16), g_ffn.reshape(1, D_MODEL), wr2)


def _route_kernel(aff_ref, idx_ref, gate_ref, spos_ref, cb_ref, thr_sc, need_sc, *, cap):
    nblk = aff_ref.shape[0] // N_EXPERTS
    aff3 = aff_ref[...].reshape(nblk, N_EXPERTS, LANES)

    def count(pred):
        return jnp.sum(jnp.sum(jnp.where(pred, 1.0, 0.0), axis=0), axis=1, keepdims=True)

    def search(it, bracket):
        lo, hi = bracket
        mid = (lo + hi) * 0.5
        enough = count(aff3 >= mid[None]) >= cap
        return jnp.where(enough, mid, lo), jnp.where(enough, hi, mid)

    lo, hi = lax.fori_loop(0, _SEARCH_STEPS, search,
                           (jnp.zeros((N_EXPERTS, 1), F32), jnp.full((N_EXPERTS, 1), 2.0, F32)))
    below = jnp.where(aff3 < hi[None], aff3, -1.0)
    thr = jnp.max(jnp.max(below, axis=0), axis=1, keepdims=True)
    need = cap - count(aff3 > thr[None])
    thr_sc[...] = jnp.broadcast_to(thr, (N_EXPERTS, LANES))
    need_sc[...] = jnp.broadcast_to(need, (N_EXPERTS, LANES))

    ri = lax.broadcasted_iota(I32, (LANES, LANES), 0)
    ci = lax.broadcasted_iota(I32, (LANES, LANES), 1)
    upper = jnp.where(ri <= ci, 1.0, 0.0).astype(BF16)
    lower = jnp.where(ci <= ri, 1.0, 0.0).astype(BF16)
    eye = jnp.where(ri == ci, 1.0, 0.0).astype(BF16)
    ones = jnp.ones((LANES, LANES), BF16)
    bi = lax.broadcasted_iota(I32, (nblk, nblk), 0)
    bj = lax.broadcasted_iota(I32, (nblk, nblk), 1)
    strict = jnp.where(bj < bi, 1.0, 0.0).astype(BF16)
    before = jnp.where(bi < bj, 1.0, 0.0).astype(BF16)
    mean_rows = jnp.full((8, LANES), 1.0 / LANES, BF16)
    c_row = lax.broadcasted_iota(I32, (1, cap), 1).astype(F32)
    blk_iota = lax.broadcasted_iota(I32, (nblk, cap), 0).astype(F32)
    t_iota = lax.broadcasted_iota(I32, (LANES, cap), 0).astype(F32)
    rep = cap // LANES

    def cums(mask_bf16):
        lp = jnp.dot(mask_bf16, upper, preferred_element_type=F32)
        bc = jnp.dot(mask_bf16, ones, preferred_element_type=F32)
        bst = jnp.dot(strict, bc.astype(BF16), preferred_element_type=F32)
        return lp, bc, bst

    def per_expert(e, carry):
        a = aff_ref[pl.ds(e, nblk, stride=N_EXPERTS), :]
        thr_e = thr_sc[pl.ds(e, 1), :]
        need_e = need_sc[pl.ds(e, 1), :]
        gt = a > thr_e
        eq = a == thr_e
        eqf = jnp.where(eq, 1.0, 0.0)
        lp_q, _, bst_q = cums(eqf.astype(BF16))
        sel = jnp.logical_or(gt, jnp.logical_and(eq, bst_q + lp_q - eqf < need_e))
        mb = jnp.where(sel, 1.0, 0.0).astype(BF16)
        lp, bc, bst = cums(mb)
        spos_ref[pl.ds(e, nblk, stride=N_EXPERTS), :] = jnp.where(sel, bst + lp - 1.0, -1.0)
        bc_row = lax.dot_general(mean_rows, bc.astype(BF16), _NT, preferred_element_type=F32)
        cb_ref[pl.ds(e, 1), :] = jnp.dot(bc_row.astype(BF16), before, preferred_element_type=F32)[:1].astype(I32)
        bend_w = jnp.tile(bst + bc, (1, rep))
        bst_w = jnp.tile(bst, (1, rep))
        blk_c = jnp.sum(jnp.where(bend_w <= c_row, 1.0, 0.0), axis=0, keepdims=True)
        onehot = blk_iota == blk_c
        bst_c = jnp.sum(jnp.where(onehot, bst_w, 0.0), axis=0, keepdims=True)
        r_c = c_row - bst_c
        ohb = jnp.where(onehot, 1.0, 0.0).astype(BF16)
        lp_t = lax.dot_general(lower, mb, _NT, preferred_element_type=F32)
        lp_c = jnp.dot(lp_t.astype(BF16), ohb, preferred_element_type=F32)
        tl_c = jnp.sum(jnp.where(lp_c <= r_c, 1.0, 0.0), axis=0, keepdims=True)
        idx_ref[pl.ds(e, 1), :] = (blk_c * LANES + tl_c).astype(I32)
        a_hi = a.astype(BF16)
        a_lo = (a - a_hi.astype(F32)).astype(BF16)
        at_hi = lax.dot_general(eye, a_hi, _NT, preferred_element_type=F32).astype(BF16)
        at_lo = lax.dot_general(eye, a_lo, _NT, preferred_element_type=F32).astype(BF16)
        g_c = (jnp.dot(at_hi, ohb, preferred_element_type=F32)
               + jnp.dot(at_lo, ohb, preferred_element_type=F32))
        gate_ref[pl.ds(e, 1), :] = jnp.sum(jnp.where(t_iota == tl_c, g_c, 0.0), axis=0, keepdims=True)
        return carry

    lax.fori_loop(0, N_EXPERTS, per_expert, 0, unroll=2)


def _route(aff2d, batch, seq):
    cap = EC_FACTOR * seq // N_EXPERTS
    nblk = seq // LANES
    rows = nblk * N_EXPERTS
    out_spec = pl.BlockSpec((None, N_EXPERTS, cap), lambda b: (b, 0, 0))
    return pl.pallas_call(
        functools.partial(_route_kernel, cap=cap),
        grid=(batch,),
        in_specs=[pl.BlockSpec((rows, LANES), lambda b: (b, 0))],
        out_specs=[out_spec, out_spec, pl.BlockSpec((rows, LANES), lambda b: (b, 0)),
                   pl.BlockSpec((None, N_EXPERTS, nblk), lambda b: (b, 0, 0))],
        out_shape=[jax.ShapeDtypeStruct((batch, N_EXPERTS, cap), I32),
                   jax.ShapeDtypeStruct((batch, N_EXPERTS, cap), F32),
                   jax.ShapeDtypeStruct((batch * rows, LANES), F32),
                   jax.ShapeDtypeStruct((batch, N_EXPERTS, nblk), I32)],
        scratch_shapes=[pltpu.VMEM((N_EXPERTS, LANES), F32), pltpu.VMEM((N_EXPERTS, LANES), F32)],
        compiler_params=_cparams(("parallel",)),
        name="route",
    )(aff2d)


def _ffn_kernel(idx_ref, nxt_ref, gate_ref, h_hbm, wg32_ref, wu32_ref, wd32_ref, y_ref, buf, sem,
                wg_ref, wu_ref, wd_ref, *, seq, tc, nsub):
    seq_id = pl.program_id(1)
    n_seq = pl.num_programs(1)
    step = pl.program_id(0) * n_seq + seq_id
    last_step = pl.num_programs(0) * n_seq - 1
    base = seq_id * seq
    next_base = jnp.where(seq_id + 1 < n_seq, seq_id + 1, 0) * seq

    @pl.when(seq_id == 0)
    def _():
        wg_ref[...] = wg32_ref[...].astype(BF16)
        wu_ref[...] = wu32_ref[...].astype(BF16)
        wd_ref[...] = wd32_ref[...].astype(BF16)

    n_tiles = D_MODEL // LANES

    def row_copy(ids, row0, j, r, slot):
        tok = pl.multiple_of((row0 + ids[0, 0, j * tc + r]) * n_tiles, n_tiles)
        return pltpu.make_async_copy(h_hbm.at[pl.ds(tok, n_tiles), :],
                                     buf.at[slot, pl.ds(r * n_tiles, n_tiles), :], sem.at[slot])

    def issue(ids, row0, j, slot):
        for r in range(tc):
            row_copy(ids, row0, j, r, slot).start()

    diag = lax.broadcasted_iota(I32, (tc, tc), 0) == lax.broadcasted_iota(I32, (tc, tc), 1)
    ones = jnp.ones((tc, LANES), BF16)

    ahead = 2

    @pl.when(step == 0)
    def _():
        for j in range(ahead):
            issue(idx_ref, base, j, j)

    for j in range(nsub):
        slot = j
        for r in range(tc):
            row_copy(idx_ref, base, j, r, slot).wait()
        xs = jnp.concatenate([buf.at[slot][pl.ds(c, tc, stride=n_tiles), :].astype(BF16) for c in range(n_tiles)],
                             axis=1)
        g = jnp.dot(xs, wg_ref[...], preferred_element_type=F32)
        u = jnp.dot(xs, wu_ref[...], preferred_element_type=F32)
        hm = (jax.nn.silu(g) * u).astype(BF16)
        y = jnp.dot(hm, wd_ref[...], preferred_element_type=F32)
        gr = jnp.broadcast_to(gate_ref[0, :, j * tc:(j + 1) * tc], (tc, tc))
        gcol = _split_dot(jnp.where(diag, gr, 0.0), ones)
        y_ref[j * tc:(j + 1) * tc, :] = (y * jnp.tile(gcol, (1, D_MODEL // LANES))).astype(BF16)
        if j + ahead < nsub:
            issue(idx_ref, base, j + ahead, j + ahead)
        else:
            issue(nxt_ref, next_base, j + ahead - nsub, j + ahead - nsub)

    @pl.when(step == last_step)
    def _():
        for j in range(ahead):
            for r in range(tc):
                row_copy(nxt_ref, next_base, j, r, j).wait()


def _ffn(idx, gates, h2d, w_gate, w_up, w_down, layer, seq, tc=256):
    b, ne, cap = idx.shape
    tc = min(tc, cap // 4)
    nsub = cap // tc
    assert nsub > 2
    idx3 = idx.reshape(b * ne, 1, cap)
    gate3 = gates.reshape(b * ne, 1, cap)
    slot = lambda ei, bi: (bi * ne + ei, 0, 0)
    next_slot = lambda ei, bi: (jnp.where(bi + 1 < b, (bi + 1) * ne + ei, jnp.minimum(ei + 1, ne - 1)), 0, 0)
    wspec = lambda shape: pl.BlockSpec((None, None) + shape, lambda ei, bi: (layer, ei, 0, 0))
    return pl.pallas_call(
        functools.partial(_ffn_kernel, seq=seq, tc=tc, nsub=nsub),
        grid=(ne, b),
        in_specs=[
            pl.BlockSpec((1, 1, cap), slot, memory_space=pltpu.SMEM),
            pl.BlockSpec((1, 1, cap), next_slot, memory_space=pltpu.SMEM),
            pl.BlockSpec((1, 1, cap), slot),
            pl.BlockSpec(memory_space=pl.ANY),
            wspec((D_MODEL, EXPERT_FF)), wspec((D_MODEL, EXPERT_FF)), wspec((EXPERT_FF, D_MODEL)),
        ],
        out_specs=pl.BlockSpec((None, None, cap, D_MODEL), lambda ei, bi: (bi, ei, 0, 0)),
        out_shape=jax.ShapeDtypeStruct((b, ne, cap, D_MODEL), BF16),
        scratch_shapes=[pltpu.VMEM((nsub, tc * (D_MODEL // LANES), LANES), F32), pltpu.SemaphoreType.DMA((nsub,)),
                        pltpu.VMEM((D_MODEL, EXPERT_FF), BF16), pltpu.VMEM((D_MODEL, EXPERT_FF), BF16),
                        pltpu.VMEM((EXPERT_FF, D_MODEL), BF16)],
        compiler_params=pltpu.CompilerParams(dimension_semantics=("arbitrary", "arbitrary"),
                                             vmem_limit_bytes=VMEM_LIMIT, disable_bounds_checks=True),
        name="ffn",
    )(idx3, idx3, gate3, h2d, w_gate, w_up, w_down)


_CTM = 256
_CWIN = 64
_CALIGN = 16


def _combine_kernel(cb_ref, x_ref, sp_ref, y_hbm, g_ref, o_ref, ybuf, xbuf, sem, xsem, *,
                    final, tiles_per_seq, nblk, cap):
    i = pl.program_id(0)
    n_tiles = pl.num_programs(0)
    b = i // tiles_per_seq
    slot = i % 2

    def window(tile, e):
        tb = tile // tiles_per_seq
        off = (tb * N_EXPERTS + e) * (nblk + 1) + (tile % tiles_per_seq) * (_CTM // LANES)
        s0 = cb_ref[off]
        s1 = cb_ref[off + _CTM // LANES]
        start = jnp.minimum((s0 // _CALIGN) * _CALIGN, cap - _CWIN)
        return s1, pl.multiple_of(start, _CALIGN)

    def fetch(tile, e, start, buf_slot):
        return pltpu.make_async_copy(y_hbm.at[tile // tiles_per_seq, e, pl.ds(start, _CWIN), :],
                                     ybuf.at[buf_slot, pl.ds(e * _CWIN, _CWIN), :], sem.at[buf_slot])

    def fetch_all(tile, buf_slot):
        for e in range(N_EXPERTS):
            fetch(tile, e, window(tile, e)[1], buf_slot).start()

    @pl.when(i == 0)
    def _():
        fetch_all(i, slot)

    @pl.when(i + 1 < n_tiles)
    def _():
        fetch_all(i + 1, 1 - slot)

    wins = [window(i, e) for e in range(N_EXPERTS)]
    pad = jnp.full((LANES - N_EXPERTS, LANES), -1.0, F32)
    sp_t = jnp.concatenate([jnp.concatenate([sp_ref[hf], pad], axis=0).T for hf in range(_CTM // LANES)],
                           axis=0)
    lane = lax.broadcasted_iota(I32, (1, _CWIN), 1).astype(F32)
    lane2 = lax.broadcasted_iota(I32, (1, LANES), 1)
    per_tile = LANES // _CWIN
    hits = []
    for e0 in range(0, N_EXPERTS, per_tile):
        rel = sp_t[:, e0:e0 + 1] - wins[e0][1].astype(F32)
        for k in range(1, per_tile):
            rel = jnp.where(lane2 < k * _CWIN, rel,
                            sp_t[:, e0 + k:e0 + k + 1] - (wins[e0 + k][1] - k * _CWIN).astype(F32))
        hits.append(jnp.where(rel == lane2.astype(F32), 1.0, 0.0).astype(BF16))
    for e in range(N_EXPERTS):
        fetch(i, e, wins[e][1], slot).wait()
    o_ref[...] = x_ref[...] + jnp.dot(jnp.concatenate(hits, axis=1), ybuf[slot], preferred_element_type=F32)
    for e in range(N_EXPERTS):
        s1, start = wins[e]
        col = sp_t[:, e:e + 1]

        def extra(k, carry, e=e, s1=s1, start=start, col=col):
            lo = start + (k + 1) * _CWIN
            st = pl.multiple_of(jnp.minimum(lo, cap - _CWIN), _CALIGN)
            cp = pltpu.make_async_copy(y_hbm.at[b, e, pl.ds(st, _CWIN), :], xbuf, xsem)
            cp.start()
            cp.wait()
            hit = jnp.where(jnp.logical_and(col - st.astype(F32) == lane, col >= lo.astype(F32)), 1.0, 0.0)
            o_ref[...] += jnp.dot(hit.astype(BF16), xbuf[...], preferred_element_type=F32)
            return carry

        n_extra = jnp.maximum(s1 - start - 1, 0) // _CWIN
        lax.fori_loop(0, n_extra, extra, 0)
    if final:
        x = o_ref[...]
        o_ref[...] = x * _rms_scale(x) * g_ref[...]


def _combine(x2d, spos, cb, y, g_final, final, seq):
    t = x2d.shape[0]
    batch, ne, cap, _ = y.shape
    nblk = seq // LANES
    cb_full = jnp.concatenate([cb, jnp.full((batch, ne, 1), cap, I32)], axis=-1).reshape(-1)
    spb = _CTM // LANES
    return pl.pallas_call(
        functools.partial(_combine_kernel, final=final, tiles_per_seq=seq // _CTM, nblk=nblk, cap=cap),
        grid_spec=pltpu.PrefetchScalarGridSpec(
            num_scalar_prefetch=1,
            grid=(t // _CTM,),
            in_specs=[
                pl.BlockSpec((_CTM, D_MODEL), lambda i, c: (i, 0)),
                pl.BlockSpec((spb, N_EXPERTS, LANES), lambda i, c: (i, 0, 0)),
                pl.BlockSpec(memory_space=pl.ANY),
                pl.BlockSpec((1, D_MODEL), lambda i, c: (0, 0)),
            ],
            out_specs=pl.BlockSpec((_CTM, D_MODEL), lambda i, c: (i, 0)),
            scratch_shapes=[pltpu.VMEM((2, N_EXPERTS * _CWIN, D_MODEL), BF16), pltpu.VMEM((_CWIN, D_MODEL), BF16),
                            pltpu.SemaphoreType.DMA((2,)), pltpu.SemaphoreType.DMA],
        ),
        out_shape=jax.ShapeDtypeStruct((t, D_MODEL), F32),
        compiler_params=_cparams(("arbitrary",)),
        name="combine",
    )(cb_full, x2d, spos, y, g_final.reshape(1, D_MODEL))


_SLAB_Q0 = 512
_SLAB_K0 = _SLAB_Q0 + D_HEADS * LANES
_SLAB_V0 = _SLAB_K0 + D_KV_HEADS * LANES
_ODD_COLS = _SLAB_V0 + D_KV_HEADS * LANES


def _proj_odd_kernel(x_ref, g_ref, wm_ref, cqn_ref, wq_ref, ckvn_ref, wkv_ref, dqn_ref, dkn_ref,
                     ccq_ref, s1cq_ref, s2cq_ref, cck_ref, s1ck_ref, s2ck_ref,
                     cdq_ref, s1dq_ref, s2dq_ref, cdk_ref, s1dk_ref, s2dk_ref,
                     qc_ref, kc_ref, vc_ref, qd_ref, kd_ref, vd_ref, stat_ref, *, steps_per_seq):
    lane = lax.broadcasted_iota(I32, (1, LANES), 1)
    one64 = jnp.where(lane == HEAD_DIM, 1.0, 0.0)
    last_lane = lane == LANES - 1
    half_rope = C_ROPE // 2
    stats = [jnp.zeros((1, LANES), F32), jnp.zeros((1, LANES), F32)]
    ones_mat = jnp.ones((LANES, LANES), BF16)

    def with_norm(val, fill, row, col):
        n2 = jnp.dot((val * val).astype(BF16), ones_mat, preferred_element_type=F32)
        stats[row] = jnp.where(lane == col, jnp.maximum(jnp.max(n2, axis=0, keepdims=True), stats[row]), stats[row])
        return jnp.where(last_lane, fill, val).astype(BF16)

    def head_norm(xg, gn_ref):
        ss = jnp.sum(xg * xg, axis=-1, keepdims=True) * (1.0 / HEAD_DIM)
        return xg * lax.rsqrt(ss + EPS) * gn_ref[...]

    n_chunks = 2
    rows_per = x_ref.shape[0] // n_chunks
    for c in range(n_chunks):
        rows = slice(c * rows_per, (c + 1) * rows_per)
        tab = lambda *refs: [r[rows, :] for r in refs]
        x = x_ref[rows, :]
        y = (x * _rms_scale(x) * g_ref[...]).astype(BF16)
        pm = jnp.dot(y, wm_ref[...], preferred_element_type=F32)
        cq = pm[:, :C_Q_RANK]
        cqn = (cq * _rms_scale(cq) * cqn_ref[...]).astype(BF16)
        qc = jnp.dot(cqn, wq_ref[...], preferred_element_type=F32)
        ckv = pm[:, C_Q_RANK:C_Q_RANK + C_KV_RANK]
        ckvn = (ckv * _rms_scale(ckv) * ckvn_ref[...]).astype(BF16)
        kv = jnp.dot(ckvn, wkv_ref[...], preferred_element_type=F32)
        kr = _rope3(pm[:, C_Q_RANK + C_KV_RANK:_SLAB_Q0], *tab(cck_ref, s1ck_ref, s2ck_ref), half_rope)
        t_cq = tab(ccq_ref, s1cq_ref, s2cq_ref)
        for h in range(C_HEADS):
            sl = slice(h * LANES, (h + 1) * LANES)
            qc_ref[rows, sl] = with_norm(_rope3(qc[:, sl], *t_cq, half_rope), 1.0, 1, h)
            kc_ref[rows, sl] = with_norm(kv[:, sl] + kr, -1.0, 0, h)
            vc_ref[rows, sl] = (kv[:, C_HEADS * LANES + h * LANES:C_HEADS * LANES + (h + 1) * LANES]
                                + one64).astype(BF16)
        t_dq = tab(cdq_ref, s1dq_ref, s2dq_ref)
        t_dk = tab(cdk_ref, s1dk_ref, s2dk_ref)
        for g in range(D_HEADS):
            xg = pm[:, _SLAB_Q0 + g * LANES:_SLAB_Q0 + (g + 1) * LANES]
            qd_ref[rows, g * LANES:(g + 1) * LANES] = with_norm(
                _rope3(head_norm(xg, dqn_ref), *t_dq, HEAD_DIM // 4), 1.0, 1, C_HEADS + g)
        for g in range(D_KV_HEADS):
            sl = slice(g * LANES, (g + 1) * LANES)
            xg = pm[:, _SLAB_K0 + g * LANES:_SLAB_K0 + (g + 1) * LANES]
            kd_ref[rows, sl] = with_norm(_rope3(head_norm(xg, dkn_ref), *t_dk, HEAD_DIM // 4), -1.0, 0, C_HEADS + g)
            vd_ref[rows, sl] = (pm[:, _SLAB_V0 + g * LANES:_SLAB_V0 + (g + 1) * LANES] + one64).astype(BF16)

    new = jnp.concatenate(stats + [jnp.zeros((6, LANES), F32)], axis=0)

    @pl.when(pl.program_id(0) % steps_per_seq == 0)
    def _():
        stat_ref[...] = new

    @pl.when(pl.program_id(0) % steps_per_seq != 0)
    def _():
        stat_ref[...] = jnp.maximum(stat_ref[...], new)


def _slabs(w, n_heads, width, lane_off=0):
    k = w.shape[0]
    w3 = w.reshape(k, n_heads, width)
    w3 = jnp.pad(w3, ((0, 0), (0, 0), (lane_off, LANES - width - lane_off)))
    return w3.reshape(k, n_heads * LANES)


def _axial_tables(row, col, scale):
    half = HEAD_DIM // 2
    cr, s1r, s2r = _rope_tables(row, D_THETA, half, 0, LANES, scale)
    cc, s1c, s2c = _rope_tables(col, D_THETA, half, half, LANES, scale)
    lane = jnp.arange(LANES)[None, :]
    return jnp.where(lane < half, cr, cc), s1r + s1c, s2r + s2c


def _proj_odd(x2d, seq, g_mix, w_in, cq_norm, w_cq_up, ckv_norm, w_ckv_up, dq_norm, dk_norm, tm=512):
    t = x2d.shape[0]
    nblk = seq // tm
    o1 = C_Q_RANK
    o2 = o1 + C_KV_RANK
    o3 = o2 + C_ROPE
    o4 = o3 + D_HEADS * HEAD_DIM
    o5 = o4 + D_KV_HEADS * HEAD_DIM
    wm = jnp.concatenate([
        w_in[:, :o2],
        _slabs(w_in[:, o2:o3], 1, C_ROPE, C_NOPE),
        _slabs(w_in[:, o3:o4], D_HEADS, HEAD_DIM),
        _slabs(w_in[:, o4:o5], D_KV_HEADS, HEAD_DIM),
        _slabs(w_in[:, o5:], D_KV_HEADS, HEAD_DIM),
    ], axis=1).astype(BF16)
    assert wm.shape[1] == _ODD_COLS
    wq = _slabs(w_cq_up, C_HEADS, C_NOPE + C_ROPE).astype(BF16)
    kv3 = w_ckv_up.reshape(C_KV_RANK, C_HEADS, 2 * HEAD_DIM)
    wkv = jnp.concatenate([
        _slabs(kv3[:, :, :C_NOPE].reshape(C_KV_RANK, -1), C_HEADS, C_NOPE),
        _slabs(kv3[:, :, C_NOPE:].reshape(C_KV_RANK, -1), C_HEADS, HEAD_DIM),
    ], axis=1).astype(BF16)
    pad64 = lambda g: jnp.pad(g, (0, LANES - HEAD_DIM)).reshape(1, LANES)

    pos = jnp.arange(seq, dtype=I32)
    row_pos = pos // GRID_W
    col_pos = pos % GRID_W
    c_scale = (C_NOPE + C_ROPE) ** -0.5 * LOG2E
    d_scale = HEAD_DIM ** -0.5 * LOG2E
    tabs = (_rope_tables(pos, ROPE_THETA, C_ROPE, C_NOPE, LANES, c_scale)
            + _rope_tables(pos, ROPE_THETA, C_ROPE, C_NOPE, LANES, 1.0)
            + _axial_tables(row_pos, col_pos, d_scale)
            + _axial_tables(row_pos, col_pos, 1.0))

    row = lambda i: (i, 0)
    full = lambda i: (0, 0)
    tspec = pl.BlockSpec((tm, LANES), lambda i: (i % nblk, 0))
    wide = C_HEADS * LANES
    kvw = D_KV_HEADS * LANES
    return pl.pallas_call(
        functools.partial(_proj_odd_kernel, steps_per_seq=nblk),
        grid=(t // tm,),
        in_specs=[
            pl.BlockSpec((tm, D_MODEL), row),
            pl.BlockSpec((1, D_MODEL), full),
            pl.BlockSpec(wm.shape, full),
            pl.BlockSpec((1, C_Q_RANK), full),
            pl.BlockSpec(wq.shape, full),
            pl.BlockSpec((1, C_KV_RANK), full),
            pl.BlockSpec(wkv.shape, full),
            pl.BlockSpec((1, LANES), full),
            pl.BlockSpec((1, LANES), full),
        ] + [tspec] * 12,
        out_specs=[
            pl.BlockSpec((tm, wide), row), pl.BlockSpec((tm, wide), row), pl.BlockSpec((tm, wide), row),
            pl.BlockSpec((tm, wide), row), pl.BlockSpec((tm, kvw), row), pl.BlockSpec((tm, kvw), row),
            pl.BlockSpec((None, 8, LANES), lambda i: (i // nblk, 0, 0)),
        ],
        out_shape=[
            jax.ShapeDtypeStruct((t, wide), BF16), jax.ShapeDtypeStruct((t, wide), BF16),
            jax.ShapeDtypeStruct((t, wide), BF16), jax.ShapeDtypeStruct((t, wide), BF16),
            jax.ShapeDtypeStruct((t, kvw), BF16), jax.ShapeDtypeStruct((t, kvw), BF16),
            jax.ShapeDtypeStruct((t // seq, 8, LANES), F32),
        ],
        compiler_params=_cparams(("arbitrary",)),
        name="proj_odd",
    )(x2d, g_mix.reshape(1, D_MODEL), wm, cq_norm.reshape(1, -1), wq, ckv_norm.reshape(1, -1), wkv,
      pad64(dq_norm), pad64(dk_norm), *tabs)


def _flash_kernel(q_ref, k_ref, v_ref, o_ref, qs_sc, m_sc, acc_sc, *, group, tq, tk):
    ki = pl.program_id(3)

    @pl.when(ki == 0)
    def _():
        for g in range(group):
            qs_sc[g * tq:(g + 1) * tq, :] = q_ref[:, g * LANES:(g + 1) * LANES]
        m_sc[...] = jnp.full(m_sc.shape, NEG, F32)
        acc_sc[...] = jnp.zeros(acc_sc.shape, F32)

    s = lax.dot_general(qs_sc[...], k_ref[...], _NT, preferred_element_type=F32)
    m_prev = m_sc[...]
    m_new = jnp.maximum(m_prev, jnp.max(s, axis=1, keepdims=True))
    alpha = jnp.exp2(m_prev - m_new)
    p = jnp.exp2(s - jnp.tile(m_new, (1, tk // LANES)))
    acc_sc[...] = alpha * acc_sc[...] + jnp.dot(p.astype(BF16), v_ref[...], preferred_element_type=F32)
    m_sc[...] = m_new

    @pl.when(ki == pl.num_programs(3) - 1)
    def _():
        acc = acc_sc[...]
        o = acc / acc[:, HEAD_DIM:HEAD_DIM + 1]
        for g in range(group):
            o_ref[:, g * LANES:(g + 1) * LANES] = o[g * tq:(g + 1) * tq].astype(BF16)


_FLASH_CHUNK = 2048
_V_ROWS = 80


def _flash_bounded_kernel(bound_ref, q_ref, k_ref, v_ref, o_ref, qs_sc, acc_sc, *, group, tq):
    ki = pl.program_id(3)
    head = pl.program_id(0) * pl.num_programs(1) + pl.program_id(1)

    @pl.when(ki == 0)
    def _():
        fix = jnp.where(lax.broadcasted_iota(I32, (1, LANES), 1) == LANES - 1, bound_ref[head], 1.0)
        for g in range(group):
            qs_sc[g * tq:(g + 1) * tq, :] = (q_ref[:, g * LANES:(g + 1) * LANES].astype(F32) * fix).astype(BF16)
        acc_sc[...] = jnp.zeros(acc_sc.shape, F32)

    chunk = min(_FLASH_CHUNK, k_ref.shape[0])
    n_chunks = k_ref.shape[0] // chunk
    qs = qs_sc[...]

    def scores(c):
        return lax.dot_general(k_ref[c * chunk:(c + 1) * chunk, :], qs, _NT, preferred_element_type=F32)

    def values(c, s_t):
        return lax.dot_general(v_ref[c * chunk:(c + 1) * chunk, :_V_ROWS], jnp.exp2(s_t).astype(BF16),
                               (((0,), (0,)), ((), ())), preferred_element_type=F32)

    acc = acc_sc[...]
    s_prev = scores(0)
    for c in range(1, n_chunks):
        s_next = scores(c)
        acc = acc + values(c - 1, s_prev)
        s_prev = s_next
    acc_sc[...] = acc + values(n_chunks - 1, s_prev)

    @pl.when(ki == pl.num_programs(3) - 1)
    def _():
        acc = acc_sc[...]
        o_t = acc / acc[HEAD_DIM:HEAD_DIM + 1, :]
        o = jnp.concatenate([o_t, jnp.zeros((LANES - _V_ROWS, o_t.shape[1]), F32)], axis=0).T
        for g in range(group):
            o_ref[:, g * LANES:(g + 1) * LANES] = o[g * tq:(g + 1) * tq].astype(BF16)


def _flash(q, k, v, group, nk, bounded, rows=1024, tk=512, tk_bounded=8192):
    b, s, qw = q.shape
    hk = k.shape[2] // LANES
    tq = rows // group
    tk = min(tk, s)
    tkb = min(tk_bounded, s)
    out_shape = jax.ShapeDtypeStruct((b, s, qw), BF16)
    sem = ("parallel", "parallel", "parallel", "arbitrary")

    def running_max(q, k, v, nk):
        qspec = pl.BlockSpec((None, tq, group * LANES), lambda bi, hi, qi, ki: (bi, qi, hi))
        kspec = pl.BlockSpec((None, tk, LANES), lambda bi, hi, qi, ki: (bi, ki, hi))
        return pl.pallas_call(
            functools.partial(_flash_kernel, group=group, tq=tq, tk=tk),
            grid=(b, hk, s // tq, s // tk),
            in_specs=[qspec, kspec, kspec],
            out_specs=qspec,
            out_shape=out_shape,
            scratch_shapes=[pltpu.VMEM((rows, LANES), BF16), pltpu.VMEM((rows, LANES), F32),
                            pltpu.VMEM((rows, LANES), F32)],
            compiler_params=_cparams(sem),
            name="flash",
        )(q, k, v)

    def bound(q, k, v, nk):
        qspec = pl.BlockSpec((None, tq, group * LANES), lambda bi, hi, qi, ki, nkr: (bi, qi, hi))
        kspec = pl.BlockSpec((None, tkb, LANES), lambda bi, hi, qi, ki, nkr: (bi, ki, hi))
        return pl.pallas_call(
            functools.partial(_flash_bounded_kernel, group=group, tq=tq),
            grid_spec=pltpu.PrefetchScalarGridSpec(
                num_scalar_prefetch=1,
                grid=(b, hk, s // tq, s // tkb),
                in_specs=[qspec, kspec, kspec],
                out_specs=qspec,
                scratch_shapes=[pltpu.VMEM((rows, LANES), BF16), pltpu.VMEM((_V_ROWS, rows), F32)],
            ),
            out_shape=out_shape,
            compiler_params=_cparams(sem),
            name="flash_bounded",
        )(nk.reshape(-1), q, k, v)

    return lax.cond(bounded, bound, running_max, q, k, v, nk)


def _moe(x1, h2, aff, batch, seq, w_gate, w_up, w_down, layer, g_final, final):
    idx, gates, spos, cb = _route(aff.reshape(-1, LANES), batch, seq)
    y = _ffn(idx, gates, h2, w_gate, w_up, w_down, layer, seq)
    return _combine(x1, spos.reshape(-1, N_EXPERTS, LANES), cb, y, g_final, final, seq)


def kernel(x, norm_mix, norm_ffn, even_w_in, even_gmlp_norm, even_w_spatial, even_b_spatial, even_w_out,
           odd_w_in, odd_cq_norm, odd_w_cq_up, odd_ckv_norm, odd_w_ckv_up, odd_dq_norm, odd_dk_norm, odd_w_out,
           moe_w_router, moe_w_gate, moe_w_up, moe_w_down, final_norm):
    b, s, d = x.shape
    depth = norm_mix.shape[0]
    x2d = x.reshape(b * s, d)
    for i in range(depth):
        j = i // 2
        last = i == depth - 1
        if i % 2 == 0:
            q, k, v, go = _proj_even(x2d, s, norm_mix[i], even_w_in[j], even_gmlp_norm[j], even_w_spatial[j],
                                     even_b_spatial[j])
            a = _dilated(q.reshape(b, s, A_WIDTH), k.reshape(b, s, A_WIDTH), v.reshape(b, s, A_WIDTH))
            x1, h2, aff = _outproj(x2d, a.reshape(b * s, A_WIDTH), go, even_w_out[j][:A_WIDTH],
                                   even_w_out[j][A_WIDTH:], norm_ffn[i], moe_w_router[i])
        else:
            qc, kc, vc, qd, kd, vd, stat = _proj_odd(x2d, s, norm_mix[i], odd_w_in[j], odd_cq_norm[j],
                                                     odd_w_cq_up[j], odd_ckv_norm[j], odd_w_ckv_up[j],
                                                     odd_dq_norm[j], odd_dk_norm[j])
            grp = D_HEADS // D_KV_HEADS
            k2_c, k2_d = stat[:, 0, :C_HEADS], stat[:, 0, C_HEADS:C_HEADS + D_KV_HEADS]
            q2_c = stat[:, 1, :C_HEADS]
            q2_d = jnp.max(stat[:, 1, C_HEADS:C_HEADS + D_HEADS].reshape(b, D_KV_HEADS, grp), axis=-1)
            bound_c = jnp.sqrt(q2_c * k2_c) * _NORM_MARGIN ** 2
            bound_d = jnp.sqrt(q2_d * k2_d) * _NORM_MARGIN ** 2
            bounded = jnp.maximum(jnp.max(bound_c), jnp.max(bound_d)) <= _MAX_SCORE_BOUND
            r3 = lambda z: z.reshape(b, s, -1)
            oc = _flash(r3(qc), r3(kc), r3(vc), 1, bound_c, bounded)
            od = _flash(r3(qd), r3(kd), r3(vd), grp, bound_d, bounded)
            cw = C_HEADS * HEAD_DIM
            x1, h2, aff = _outproj(x2d, oc.reshape(b * s, -1), od.reshape(b * s, -1),
                                   _slabs(odd_w_out[j][:cw].T, C_HEADS, HEAD_DIM).T,
                                   _slabs(odd_w_out[j][cw:].T, D_HEADS, HEAD_DIM).T,
                                   norm_ffn[i], moe_w_router[i])
        x2d = _moe(x1, h2, aff, b, s, moe_w_gate, moe_w_up, moe_w_down, i, final_norm, last)
    return x2d.reshape(b, s, d)
```

```python
import functools
import math

import jax
import jax.numpy as jnp
from jax import lax
from jax.experimental import pallas as pl
from jax.experimental.pallas import tpu as pltpu

F32 = jnp.float32
BF16 = jnp.bfloat16
I32 = jnp.int32

EPS = 1e-6
NEG = -1e30
LOG2E = 1.4426950408889634

D_MODEL = 1024
HEAD_DIM = 64
ROPE_THETA = 500000.0
ROT_DIM = 16
GRID_W = 64
A_HEADS = 12
A_WIDTH = 768
A_DILATIONS = (1, 4, 16)
A_RADIUS = 64
B_WIDTH = 256
B_GROUPS = 4
B_CHUNK = 128
C_HEADS = 8
C_Q_RANK = 256
C_KV_RANK = 128
C_NOPE = 64
C_ROPE = 32
D_HEADS = 8
D_KV_HEADS = 2
D_THETA = 10000.0
N_EXPERTS = 16
EC_FACTOR = 2
EXPERT_FF = 512

_MIN_NORMAL = 1.1754944e-38
_GEOMETRIC_STEPS = 40
_ARITHMETIC_STEPS = 24
_NORM_MARGIN = 1.01
_MAX_SCORE_BOUND = 55.0

LANES = 128
VMEM_LIMIT = 48 * 1024 * 1024
_DILATED_VMEM_LIMIT = 56 * 1024 * 1024

_NT = (((1,), (1,)), ((), ()))


def _cparams(sem):
    return pltpu.CompilerParams(dimension_semantics=sem, vmem_limit_bytes=VMEM_LIMIT)


def _rms_scale(x):
    return lax.rsqrt(jnp.mean(x * x, axis=-1, keepdims=True) + EPS)


def _rope3(a, c, s1, s2, shift):
    return a * c + pltpu.roll(a, LANES - shift, 1) * s1 + pltpu.roll(a, shift, 1) * s2


def _split_dot(x, w_bf16):
    hi = x.astype(BF16)
    lo = (x - hi.astype(F32)).astype(BF16)
    return (jnp.dot(hi, w_bf16, preferred_element_type=F32)
            + jnp.dot(lo, w_bf16, preferred_element_type=F32))


def _proj_even_kernel(x_ref, g_ref, w_ref, cq_ref, s1q_ref, s2q_ref, ck_ref, s1k_ref, s2k_ref,
                      gn_ref, gmat_ref, ws_ref, bs_ref,
                      q_ref, k_ref, v_ref, go_ref):
    x = x_ref[...]
    y = (x * _rms_scale(x) * g_ref[...]).astype(BF16)
    tm = x.shape[0]

    aq = jnp.dot(y, w_ref[:, 0:A_WIDTH], preferred_element_type=F32)
    tq = (cq_ref[...], s1q_ref[...], s2q_ref[...])
    for j in range(A_WIDTH // LANES):
        sl = slice(j * LANES, (j + 1) * LANES)
        q_ref[:, sl] = _rope3(aq[:, sl], *tq, ROT_DIM // 2)
    ak = jnp.dot(y, w_ref[:, A_WIDTH:2 * A_WIDTH], preferred_element_type=F32)
    tk = (ck_ref[...], s1k_ref[...], s2k_ref[...])
    for j in range(A_WIDTH // LANES):
        sl = slice(j * LANES, (j + 1) * LANES)
        k_ref[:, sl] = _rope3(ak[:, sl], *tk, ROT_DIM // 2)
    v_ref[...] = jnp.dot(y, w_ref[:, 2 * A_WIDTH:3 * A_WIDTH], preferred_element_type=F32)

    z = jnp.dot(y, w_ref[:, 3 * A_WIDTH:3 * A_WIDTH + 2 * B_WIDTH], preferred_element_type=F32)
    ge = jax.nn.gelu(z)
    u = ge[:, :B_WIDTH]
    vv = ge[:, B_WIDTH:]
    ss = _split_dot(vv * vv, gmat_ref[...])
    vn = (vv * lax.rsqrt(ss + EPS) * gn_ref[...]).astype(BF16)
    grp = lax.broadcasted_iota(I32, (B_CHUNK, B_WIDTH), 1) // (B_WIDTH // B_GROUPS)
    for c in range(tm // B_CHUNK):
        rows = slice(c * B_CHUNK, (c + 1) * B_CHUNK)
        vc = vn[rows]
        mg = [jnp.dot(ws_ref[g], vc, preferred_element_type=F32) for g in range(B_GROUPS)]
        mixed = jnp.where(grp == 0, mg[0], jnp.where(grp == 1, mg[1], jnp.where(grp == 2, mg[2], mg[3])))
        go_ref[rows, :] = (u[rows] * (mixed + bs_ref[...])).astype(BF16)


def _rope_tables(pos, theta, r, lane_off, period, scale):
    half = r // 2
    inv = jnp.power(jnp.float32(theta), -jnp.arange(half, dtype=F32) * (2.0 / r))
    ang = pos.astype(F32)[:, None] * inv[None, :]
    cos, sin = jnp.cos(ang), jnp.sin(ang)
    o = (jnp.arange(LANES) % period) - lane_off
    in_lo = (o >= 0) & (o < half)
    in_hi = (o >= half) & (o < r)
    idx = jnp.clip(jnp.where(in_hi, o - half, o), 0, half - 1)
    c = jnp.where((in_lo | in_hi)[None, :], cos[:, idx], 1.0)
    s1 = jnp.where(in_lo[None, :], -sin[:, idx], 0.0)
    s2 = jnp.where(in_hi[None, :], sin[:, idx], 0.0)
    return c * scale, s1 * scale, s2 * scale


def _proj_even(x2d, seq, g_mix, w_in, gmlp_norm, w_s, b_s, tm=512):
    t = x2d.shape[0]
    nblk = seq // tm
    pos = jnp.arange(seq, dtype=I32)
    qscale = HEAD_DIM ** -0.5 * LOG2E
    cq, s1q, s2q = _rope_tables(pos, ROPE_THETA, ROT_DIM, 0, HEAD_DIM, qscale)
    ck, s1k, s2k = _rope_tables(pos, ROPE_THETA, ROT_DIM, 0, HEAD_DIM, 1.0)
    gdim = B_WIDTH // B_GROUPS
    gid = jnp.arange(B_WIDTH) // gdim
    gmat = jnp.where(gid[:, None] == gid[None, :], 1.0 / gdim, 0.0).astype(BF16)
    bias = jnp.repeat(b_s.T, gdim, axis=1)
    row = lambda i: (i, 0)
    tab = lambda i: (i % nblk, 0)
    full = lambda i: (0, 0)
    tspec = pl.BlockSpec((tm, LANES), tab)
    return pl.pallas_call(
        _proj_even_kernel,
        grid=(t // tm,),
        in_specs=[
            pl.BlockSpec((tm, D_MODEL), row),
            pl.BlockSpec((1, D_MODEL), full),
            pl.BlockSpec(w_in.shape, full),
            tspec, tspec, tspec, tspec, tspec, tspec,
            pl.BlockSpec((1, B_WIDTH), full),
            pl.BlockSpec((B_WIDTH, B_WIDTH), full),
            pl.BlockSpec((B_GROUPS, B_CHUNK, B_CHUNK), lambda i: (0, 0, 0)),
            pl.BlockSpec((B_CHUNK, B_WIDTH), full),
        ],
        out_specs=[
            pl.BlockSpec((tm, A_WIDTH), row),
            pl.BlockSpec((tm, A_WIDTH), row),
            pl.BlockSpec((tm, A_WIDTH), row),
            pl.BlockSpec((tm, B_WIDTH), row),
        ],
        out_shape=[
            jax.ShapeDtypeStruct((t, A_WIDTH), F32),
            jax.ShapeDtypeStruct((t, A_WIDTH), F32),
            jax.ShapeDtypeStruct((t, A_WIDTH), F32),
            jax.ShapeDtypeStruct((t, B_WIDTH), BF16),
        ],
        compiler_params=_cparams(("parallel",)),
        name="proj_even",
    )(x2d, g_mix.reshape(1, D_MODEL), w_in.astype(BF16), cq, s1q, s2q, ck, s1k, s2k,
      gmlp_norm.reshape(1, B_WIDTH), gmat, w_s.astype(BF16), bias)


_TQ = 128
_TK = _TQ + 2 * A_RADIUS
_NORM_ROWS = 1024


def _dilated_kernel(q_ref, k_ref, v_ref, o_ref, m_sc, l_sc, bias_sc, *, seq):
    half0 = lax.broadcasted_iota(I32, (1, LANES), 1) < HEAD_DIM
    same_head = jnp.where(lax.broadcasted_iota(I32, (LANES, LANES), 0) // HEAD_DIM
                          == lax.broadcasted_iota(I32, (LANES, LANES), 1) // HEAD_DIM,
                          1.0, 0.0).astype(BF16)
    n_pat = len(A_DILATIONS)

    def max_head_sq(ref):
        def body(c, mx):
            x = ref[pl.ds(pl.multiple_of(c * _NORM_ROWS, _NORM_ROWS), _NORM_ROWS), :]
            n2 = jnp.dot((x * x).astype(BF16), same_head, preferred_element_type=F32)
            return jnp.maximum(mx, jnp.max(n2, axis=0, keepdims=True))
        return lax.fori_loop(0, seq // _NORM_ROWS, body, jnp.zeros((1, LANES), F32))

    both = jnp.sqrt(max_head_sq(q_ref) * max_head_sq(k_ref)) * (_NORM_MARGIN * _NORM_MARGIN)
    both = jnp.broadcast_to(both, (8, LANES))
    other = pltpu.roll(both, HEAD_DIM, 1)
    bound = [jnp.where(half0, both, other)[:1], jnp.where(half0, other, both)[:1]]
    worst = jnp.max(both)

    diff = lax.broadcasted_iota(I32, (_TQ, _TK), 1) - lax.broadcasted_iota(I32, (_TQ, _TK), 0)
    for case in range(3):
        band = jnp.where(jnp.abs(diff - case * A_RADIUS) <= A_RADIUS, 0.0, NEG)
        bias_sc[6 + case] = band
        for h in range(2):
            bias_sc[3 * h + case] = band - jnp.tile(bound[h], (1, _TK // LANES))
    one_bf16 = jnp.ones((), BF16)

    def run(bounded):
        for pi, d in enumerate(A_DILATIONS):
            cls_len = seq // d
            tpc = cls_len // _TQ

            def tile(qrows, kb, vb, case, pi=pi):
                q = q_ref[qrows, :]
                parts = []
                for h in range(2):
                    qh = jnp.where(half0 if h == 0 else jnp.logical_not(half0), q, 0.0).astype(BF16)
                    s = lax.dot_general(qh, kb, _NT, preferred_element_type=F32)
                    if bounded:
                        mt = None
                        p = jnp.exp2(s + bias_sc[3 * h + case]).astype(BF16)
                    else:
                        s = s + bias_sc[6 + case]
                        mt = jnp.max(s, axis=-1, keepdims=True)
                        p = jnp.exp2(s - mt).astype(BF16)
                    vh = jnp.where(half0 if h == 0 else jnp.logical_not(half0), vb, one_bf16)
                    parts.append((mt, jnp.dot(p, vh, preferred_element_type=F32)))
                ot = jnp.where(half0, parts[0][1], parts[1][1])
                lt = pltpu.roll(jnp.where(half0, parts[1][1], parts[0][1]), HEAD_DIM, 1)
                if bounded:
                    if pi > 0:
                        lt = l_sc[qrows, :] + lt
                        ot = o_ref[qrows, :] + ot
                else:
                    mt = jnp.where(half0, parts[0][0], parts[1][0])
                    if pi > 0:
                        mp = m_sc[qrows, :]
                        mn = jnp.maximum(mp, mt)
                        a = jnp.exp2(mp - mn)
                        b = jnp.exp2(mt - mn)
                        lt = a * l_sc[qrows, :] + b * lt
                        ot = a * o_ref[qrows, :] + b * ot
                        mt = mn
                    if pi < n_pat - 1:
                        m_sc[qrows, :] = mt
                if pi == n_pat - 1:
                    o_ref[qrows, :] = ot / lt
                else:
                    l_sc[qrows, :] = lt
                    o_ref[qrows, :] = ot

            def window(l0, cls_len=cls_len):
                if isinstance(l0, int):
                    kst = min(max(l0 - A_RADIUS, 0), cls_len - _TK)
                else:
                    kst = jnp.clip(l0 - A_RADIUS, 0, cls_len - _TK)
                return kst, (l0 - kst) // A_RADIUS

            if d < 8:

                def body(j, carry, d=d, tpc=tpc):
                    i = j // tpc
                    l0 = (j % tpc) * _TQ
                    kst, case = window(l0)
                    if d == 1:
                        qrows = pl.ds(pl.multiple_of(l0, _TQ), _TQ)
                        krows = pl.ds(pl.multiple_of(kst, A_RADIUS), _TK)
                    else:
                        qrows = pl.ds(l0 * d + i, _TQ, stride=d)
                        krows = pl.ds(kst * d + i, _TK, stride=d)
                    tile(qrows, k_ref[krows, :].astype(BF16), v_ref[krows, :].astype(BF16), case)
                    return carry

                lax.fori_loop(0, seq // _TQ, body, 0, unroll=16)
            else:

                def body(i, carry, d=d, cls_len=cls_len, tpc=tpc):
                    cls = pl.ds(i, cls_len, stride=d)
                    kc = k_ref[cls, :].astype(BF16)
                    vc = v_ref[cls, :].astype(BF16)
                    for n in range(tpc):
                        kst, case = window(n * _TQ)
                        tile(pl.ds(n * _TQ * d + i, _TQ, stride=d), kc[kst:kst + _TK], vc[kst:kst + _TK], case)
                    return carry

                lax.fori_loop(0, d, body, 0, unroll=4)

    @pl.when(worst <= _MAX_SCORE_BOUND)
    def _():
        run(True)

    @pl.when(jnp.logical_not(worst <= _MAX_SCORE_BOUND))
    def _():
        run(False)


def _dilated(q, k, v):
    b, s, w = q.shape
    spec = pl.BlockSpec((None, s, LANES), lambda bi, hi: (bi, 0, hi))
    return pl.pallas_call(
        functools.partial(_dilated_kernel, seq=s),
        grid=(b, w // LANES),
        in_specs=[spec, spec, spec],
        out_specs=spec,
        out_shape=jax.ShapeDtypeStruct((b, s, w), F32),
        scratch_shapes=[pltpu.VMEM((s, LANES), F32), pltpu.VMEM((s, LANES), F32),
                        pltpu.VMEM((9, _TQ, _TK), F32)],
        compiler_params=pltpu.CompilerParams(dimension_semantics=("parallel", "parallel"),
                                             vmem_limit_bytes=_DILATED_VMEM_LIMIT),
        name="dilated",
    )(q, k, v)


def _outproj_kernel(x_ref, a_ref, b_ref, wa_ref, wb_ref, gf_ref, wr_ref, x1_ref, h2_ref, aff_ref):
    x1 = (x_ref[...]
          + jnp.dot(a_ref[...].astype(BF16), wa_ref[...], preferred_element_type=F32)
          + jnp.dot(b_ref[...].astype(BF16), wb_ref[...], preferred_element_type=F32))
    x1_ref[...] = x1
    h2 = x1 * _rms_scale(x1) * gf_ref[...]
    n_tiles = D_MODEL // LANES
    for j in range(n_tiles):
        h2_ref[pl.ds(j, h2.shape[0], stride=n_tiles), :] = h2[:, j * LANES:(j + 1) * LANES]
    hi = h2.astype(BF16)
    lo = (h2 - hi.astype(F32)).astype(BF16)
    both = jnp.dot(hi, wr_ref[...], preferred_element_type=F32)
    lg = (both[:, :LANES] + both[:, LANES:]
          + jnp.dot(lo, wr_ref[:, :LANES], preferred_element_type=F32))
    valid = lax.broadcasted_iota(I32, lg.shape, 1) < N_EXPERTS
    lg = jnp.where(valid, lg, NEG)
    e = jnp.exp(lg - jnp.max(lg, axis=-1, keepdims=True))
    aff = e / jnp.sum(e, axis=-1, keepdims=True)
    aff_t = aff.T
    for j in range(aff.shape[0] // LANES):
        aff_ref[j] = aff_t[:N_EXPERTS, j * LANES:(j + 1) * LANES]


def _outproj(x2d, a, b, wa, wb, g_ffn, w_router, tm=512):
    t = x2d.shape[0]
    wr = jnp.pad(w_router, ((0, 0), (0, LANES - N_EXPERTS)))
    wr_hi = wr.astype(BF16)
    wr_lo = (wr - wr_hi.astype(F32)).astype(BF16)
    wr2 = jnp.concatenate([wr_hi, wr_lo], axis=1)
    row = lambda i: (i, 0)
    full = lambda i: (0, 0)
    return pl.pallas_call(
        _outproj_kernel,
        grid=(t // tm,),
        in_specs=[
            pl.BlockSpec((tm, D_MODEL), row),
            pl.BlockSpec((tm, a.shape[1]), row),
            pl.BlockSpec((tm, b.shape[1]), row),
            pl.BlockSpec(wa.shape, full),
            pl.BlockSpec(wb.shape, full),
            pl.BlockSpec((1, D_MODEL), full),
            pl.BlockSpec((D_MODEL, 2 * LANES), full),
        ],
        out_specs=[
            pl.BlockSpec((tm, D_MODEL), row),
            pl.BlockSpec((tm * (D_MODEL // LANES), LANES), row),
            pl.BlockSpec((tm // LANES, N_EXPERTS, LANES), lambda i: (i, 0, 0)),
        ],
        out_shape=[
            jax.ShapeDtypeStruct((t, D_MODEL), F32),
            jax.ShapeDtypeStruct((t * (D_MODEL // LANES), LANES), F32),
            jax.ShapeDtypeStruct((t // LANES, N_EXPERTS, LANES), F32),
        ],
        compiler_params=_cparams(("parallel",)),
        name="outproj",
    )(x2d, a, b, wa.astype(BF16), wb.astype(BF16), g_ffn.reshape(1, D_MODEL), wr2)


def _route_kernel(aff_ref, idx_ref, gate_ref, spos_ref, cb_ref, thr_sc, need_sc, *, cap):
    nblk = aff_ref.shape[0] // N_EXPERTS
    aff3 = aff_ref[...].reshape(nblk, N_EXPERTS, LANES)

    def count(pred):
        return jnp.sum(jnp.sum(jnp.where(pred, 1.0, 0.0), axis=0), axis=1, keepdims=True)

    def narrow(bracket, mid):
        lo, hi = bracket
        enough = count(aff3 >= mid[None]) >= cap
        return jnp.where(enough, mid, lo), jnp.where(enough, hi, mid)

    def geometric(it, bracket):
        lo, hi = bracket
        return narrow(bracket, jnp.clip(jnp.sqrt(lo) * jnp.sqrt(hi), lo, hi))

    def arithmetic(it, bracket):
        lo, hi = bracket
        return narrow(bracket, lo + (hi - lo) * 0.5)

    bracket = (jnp.full((N_EXPERTS, 1), _MIN_NORMAL, F32), jnp.full((N_EXPERTS, 1), 2.0, F32))
    bracket = lax.fori_loop(0, _GEOMETRIC_STEPS, geometric, bracket)
    lo, hi = lax.fori_loop(0, _ARITHMETIC_STEPS, arithmetic, bracket)
    below = jnp.where(aff3 < hi[None], aff3, -1.0)
    thr = jnp.max(jnp.max(below, axis=0), axis=1, keepdims=True)
    need = cap - count(aff3 > thr[None])
    thr_sc[...] = jnp.broadcast_to(thr, (N_EXPERTS, LANES))
    need_sc[...] = jnp.broadcast_to(need, (N_EXPERTS, LANES))

    ri = lax.broadcasted_iota(I32, (LANES, LANES), 0)
    ci = lax.broadcasted_iota(I32, (LANES, LANES), 1)
    upper = jnp.where(ri <= ci, 1.0, 0.0).astype(BF16)
    lower = jnp.where(ci <= ri, 1.0, 0.0).astype(BF16)
    eye = jnp.where(ri == ci, 1.0, 0.0).astype(BF16)
    ones = jnp.ones((LANES, LANES), BF16)
    bi = lax.broadcasted_iota(I32, (nblk, nblk), 0)
    bj = lax.broadcasted_iota(I32, (nblk, nblk), 1)
    strict = jnp.where(bj < bi, 1.0, 0.0).astype(BF16)
    before = jnp.where(bi < bj, 1.0, 0.0).astype(BF16)
    mean_rows = jnp.full((8, LANES), 1.0 / LANES, BF16)
    c_row = lax.broadcasted_iota(I32, (1, cap), 1).astype(F32)
    blk_iota = lax.broadcasted_iota(I32, (nblk, cap), 0).astype(F32)
    t_iota = lax.broadcasted_iota(I32, (LANES, cap), 0).astype(F32)
    rep = cap // LANES

    def cums(mask_bf16):
        lp = jnp.dot(mask_bf16, upper, preferred_element_type=F32)
        bc = jnp.dot(mask_bf16, ones, preferred_element_type=F32)
        bst = jnp.dot(strict, bc.astype(BF16), preferred_element_type=F32)
        return lp, bc, bst

    def per_expert(e, carry):
        a = aff_ref[pl.ds(e, nblk, stride=N_EXPERTS), :]
        thr_e = thr_sc[pl.ds(e, 1), :]
        need_e = need_sc[pl.ds(e, 1), :]
        gt = a > thr_e
        eq = a == thr_e
        eqf = jnp.where(eq, 1.0, 0.0)
        lp_q, _, bst_q = cums(eqf.astype(BF16))
        sel = jnp.logical_or(gt, jnp.logical_and(eq, bst_q + lp_q - eqf < need_e))
        mb = jnp.where(sel, 1.0, 0.0).astype(BF16)
        lp, bc, bst = cums(mb)
        spos_ref[pl.ds(e, nblk, stride=N_EXPERTS), :] = jnp.where(sel, bst + lp - 1.0, -1.0)
        bc_row = lax.dot_general(mean_rows, bc.astype(BF16), _NT, preferred_element_type=F32)
        cb_ref[pl.ds(e, 1), :] = jnp.dot(bc_row.astype(BF16), before, preferred_element_type=F32)[:1].astype(I32)
        bend_w = jnp.tile(bst + bc, (1, rep))
        bst_w = jnp.tile(bst, (1, rep))
        blk_c = jnp.sum(jnp.where(bend_w <= c_row, 1.0, 0.0), axis=0, keepdims=True)
        onehot = blk_iota == blk_c
        bst_c = jnp.sum(jnp.where(onehot, bst_w, 0.0), axis=0, keepdims=True)
        r_c = c_row - bst_c
        ohb = jnp.where(onehot, 1.0, 0.0).astype(BF16)
        lp_t = lax.dot_general(lower, mb, _NT, preferred_element_type=F32)
        lp_c = jnp.dot(lp_t.astype(BF16), ohb, preferred_element_type=F32)
        tl_c = jnp.sum(jnp.where(lp_c <= r_c, 1.0, 0.0), axis=0, keepdims=True)
        idx_ref[pl.ds(e, 1), :] = (blk_c * LANES + tl_c).astype(I32)
        a_hi = a.astype(BF16)
        a_lo = (a - a_hi.astype(F32)).astype(BF16)
        at_hi = lax.dot_general(eye, a_hi, _NT, preferred_element_type=F32).astype(BF16)
        at_lo = lax.dot_general(eye, a_lo, _NT, preferred_element_type=F32).astype(BF16)
        g_c = (jnp.dot(at_hi, ohb, preferred_element_type=F32)
               + jnp.dot(at_lo, ohb, preferred_element_type=F32))
        gate_ref[pl.ds(e, 1), :] = jnp.sum(jnp.where(t_iota == tl_c, g_c, 0.0), axis=0, keepdims=True)
        return carry

    lax.fori_loop(0, N_EXPERTS, per_expert, 0, unroll=2)


def _route(aff2d, batch, seq):
    cap = EC_FACTOR * seq // N_EXPERTS
    nblk = seq // LANES
    rows = nblk * N_EXPERTS
    out_spec = pl.BlockSpec((None, N_EXPERTS, cap), lambda b: (b, 0, 0))
    return pl.pallas_call(
        functools.partial(_route_kernel, cap=cap),
        grid=(batch,),
        in_specs=[pl.BlockSpec((rows, LANES), lambda b: (b, 0))],
        out_specs=[out_spec, out_spec, pl.BlockSpec((rows, LANES), lambda b: (b, 0)),
                   pl.BlockSpec((None, N_EXPERTS, nblk), lambda b: (b, 0, 0))],
        out_shape=[jax.ShapeDtypeStruct((batch, N_EXPERTS, cap), I32),
                   jax.ShapeDtypeStruct((batch, N_EXPERTS, cap), F32),
                   jax.ShapeDtypeStruct((batch * rows, LANES), F32),
                   jax.ShapeDtypeStruct((batch, N_EXPERTS, nblk), I32)],
        scratch_shapes=[pltpu.VMEM((N_EXPERTS, LANES), F32), pltpu.VMEM((N_EXPERTS, LANES), F32)],
        compiler_params=_cparams(("parallel",)),
        name="route",
    )(aff2d)


def _ffn_kernel(idx_ref, nxt_ref, gate_ref, h_hbm, wg32_ref, wu32_ref, wd32_ref, y_ref, buf, sem,
                wg_ref, wu_ref, wd_ref, *, seq, tc, nsub):
    seq_id = pl.program_id(1)
    n_seq = pl.num_programs(1)
    step = pl.program_id(0) * n_seq + seq_id
    last_step = pl.num_programs(0) * n_seq - 1
    base = seq_id * seq
    next_base = jnp.where(seq_id + 1 < n_seq, seq_id + 1, 0) * seq

    @pl.when(seq_id == 0)
    def _():
        wg_ref[...] = wg32_ref[...].astype(BF16)
        wu_ref[...] = wu32_ref[...].astype(BF16)
        wd_ref[...] = wd32_ref[...].astype(BF16)

    n_tiles = D_MODEL // LANES

    def row_copy(ids, row0, j, r, slot):
        tok = pl.multiple_of((row0 + ids[0, 0, j * tc + r]) * n_tiles, n_tiles)
        return pltpu.make_async_copy(h_hbm.at[pl.ds(tok, n_tiles), :],
                                     buf.at[slot, pl.ds(r * n_tiles, n_tiles), :], sem.at[slot])

    def issue(ids, row0, j, slot):
        for r in range(tc):
            row_copy(ids, row0, j, r, slot).start()

    diag = lax.broadcasted_iota(I32, (tc, tc), 0) == lax.broadcasted_iota(I32, (tc, tc), 1)
    ones = jnp.ones((tc, LANES), BF16)

    ahead = 2

    @pl.when(step == 0)
    def _():
        for j in range(ahead):
            issue(idx_ref, base, j, j)

    for j in range(nsub):
        slot = j
        for r in range(tc):
            row_copy(idx_ref, base, j, r, slot).wait()
        xs = jnp.concatenate([buf.at[slot][pl.ds(c, tc, stride=n_tiles), :].astype(BF16) for c in range(n_tiles)],
                             axis=1)
        g = jnp.dot(xs, wg_ref[...], preferred_element_type=F32)
        u = jnp.dot(xs, wu_ref[...], preferred_element_type=F32)
        hm = (jax.nn.silu(g) * u).astype(BF16)
        y = jnp.dot(hm, wd_ref[...], preferred_element_type=F32)
        gr = jnp.broadcast_to(gate_ref[0, :, j * tc:(j + 1) * tc], (tc, tc))
        gcol = _split_dot(jnp.where(diag, gr, 0.0), ones)
        y_ref[j * tc:(j + 1) * tc, :] = (y * jnp.tile(gcol, (1, D_MODEL // LANES))).astype(BF16)
        if j + ahead < nsub:
            issue(idx_ref, base, j + ahead, j + ahead)
        else:
            issue(nxt_ref, next_base, j + ahead - nsub, j + ahead - nsub)

    @pl.when(step == last_step)
    def _():
        for j in range(ahead):
            for r in range(tc):
                row_copy(nxt_ref, next_base, j, r, j).wait()


def _ffn(idx, gates, h2d, w_gate, w_up, w_down, layer, seq, tc=256):
    b, ne, cap = idx.shape
    tc = min(tc, cap // 4)
    nsub = cap // tc
    assert nsub > 2
    idx3 = idx.reshape(b * ne, 1, cap)
    gate3 = gates.reshape(b * ne, 1, cap)
    slot = lambda ei, bi: (bi * ne + ei, 0, 0)
    next_slot = lambda ei, bi: (jnp.where(bi + 1 < b, (bi + 1) * ne + ei, jnp.minimum(ei + 1, ne - 1)), 0, 0)
    wspec = lambda shape: pl.BlockSpec((None, None) + shape, lambda ei, bi: (layer, ei, 0, 0))
    return pl.pallas_call(
        functools.partial(_ffn_kernel, seq=seq, tc=tc, nsub=nsub),
        grid=(ne, b),
        in_specs=[
            pl.BlockSpec((1, 1, cap), slot, memory_space=pltpu.SMEM),
            pl.BlockSpec((1, 1, cap), next_slot, memory_space=pltpu.SMEM),
            pl.BlockSpec((1, 1, cap), slot),
            pl.BlockSpec(memory_space=pl.ANY),
            wspec((D_MODEL, EXPERT_FF)), wspec((D_MODEL, EXPERT_FF)), wspec((EXPERT_FF, D_MODEL)),
        ],
        out_specs=pl.BlockSpec((None, None, cap, D_MODEL), lambda ei, bi: (bi, ei, 0, 0)),
        out_shape=jax.ShapeDtypeStruct((b, ne, cap, D_MODEL), BF16),
        scratch_shapes=[pltpu.VMEM((nsub, tc * (D_MODEL // LANES), LANES), F32), pltpu.SemaphoreType.DMA((nsub,)),
                        pltpu.VMEM((D_MODEL, EXPERT_FF), BF16), pltpu.VMEM((D_MODEL, EXPERT_FF), BF16),
                        pltpu.VMEM((EXPERT_FF, D_MODEL), BF16)],
        compiler_params=pltpu.CompilerParams(dimension_semantics=("arbitrary", "arbitrary"),
                                             vmem_limit_bytes=VMEM_LIMIT, disable_bounds_checks=True),
        name="ffn",
    )(idx3, idx3, gate3, h2d, w_gate, w_up, w_down)


_CTM = 256
_CWIN = 64
_CALIGN = 16


def _combine_kernel(cb_ref, x_ref, sp_ref, y_hbm, g_ref, o_ref, ybuf, xbuf, sem, xsem, *,
                    final, tiles_per_seq, nblk, cap):
    i = pl.program_id(0)
    n_tiles = pl.num_programs(0)
    b = i // tiles_per_seq
    slot = i % 2

    def window(tile, e):
        tb = tile // tiles_per_seq
        off = (tb * N_EXPERTS + e) * (nblk + 1) + (tile % tiles_per_seq) * (_CTM // LANES)
        s0 = cb_ref[off]
        s1 = cb_ref[off + _CTM // LANES]
        start = jnp.minimum((s0 // _CALIGN) * _CALIGN, cap - _CWIN)
        return s1, pl.multiple_of(start, _CALIGN)

    def fetch(tile, e, start, buf_slot):
        return pltpu.make_async_copy(y_hbm.at[tile // tiles_per_seq, e, pl.ds(start, _CWIN), :],
                                     ybuf.at[buf_slot, pl.ds(e * _CWIN, _CWIN), :], sem.at[buf_slot])

    def fetch_all(tile, buf_slot):
        for e in range(N_EXPERTS):
            fetch(tile, e, window(tile, e)[1], buf_slot).start()

    @pl.when(i == 0)
    def _():
        fetch_all(i, slot)

    @pl.when(i + 1 < n_tiles)
    def _():
        fetch_all(i + 1, 1 - slot)

    wins = [window(i, e) for e in range(N_EXPERTS)]
    pad = jnp.full((LANES - N_EXPERTS, LANES), -1.0, F32)
    sp_t = jnp.concatenate([jnp.concatenate([sp_ref[hf], pad], axis=0).T for hf in range(_CTM // LANES)],
                           axis=0)
    lane = lax.broadcasted_iota(I32, (1, _CWIN), 1).astype(F32)
    lane2 = lax.broadcasted_iota(I32, (1, LANES), 1)
    per_tile = LANES // _CWIN
    hits = []
    for e0 in range(0, N_EXPERTS, per_tile):
        rel = sp_t[:, e0:e0 + 1] - wins[e0][1].astype(F32)
        for k in range(1, per_tile):
            rel = jnp.where(lane2 < k * _CWIN, rel,
                            sp_t[:, e0 + k:e0 + k + 1] - (wins[e0 + k][1] - k * _CWIN).astype(F32))
        hits.append(jnp.where(rel == lane2.astype(F32), 1.0, 0.0).astype(BF16))
    for e in range(N_EXPERTS):
        fetch(i, e, wins[e][1], slot).wait()
    o_ref[...] = x_ref[...] + jnp.dot(jnp.concatenate(hits, axis=1), ybuf[slot], preferred_element_type=F32)
    for e in range(N_EXPERTS):
        s1, start = wins[e]
        col = sp_t[:, e:e + 1]

        def extra(k, carry, e=e, s1=s1, start=start, col=col):
            lo = start + (k + 1) * _CWIN
            st = pl.multiple_of(jnp.minimum(lo, cap - _CWIN), _CALIGN)
            cp = pltpu.make_async_copy(y_hbm.at[b, e, pl.ds(st, _CWIN), :], xbuf, xsem)
            cp.start()
            cp.wait()
            hit = jnp.where(jnp.logical_and(col - st.astype(F32) == lane, col >= lo.astype(F32)), 1.0, 0.0)
            o_ref[...] += jnp.dot(hit.astype(BF16), xbuf[...], preferred_element_type=F32)
            return carry

        n_extra = jnp.maximum(s1 - start - 1, 0) // _CWIN
        lax.fori_loop(0, n_extra, extra, 0)
    if final:
        x = o_ref[...]
        o_ref[...] = x * _rms_scale(x) * g_ref[...]


def _combine(x2d, spos, cb, y, g_final, final, seq):
    t = x2d.shape[0]
    batch, ne, cap, _ = y.shape
    nblk = seq // LANES
    cb_full = jnp.concatenate([cb, jnp.full((batch, ne, 1), cap, I32)], axis=-1).reshape(-1)
    spb = _CTM // LANES
    return pl.pallas_call(
        functools.partial(_combine_kernel, final=final, tiles_per_seq=seq // _CTM, nblk=nblk, cap=cap),
        grid_spec=pltpu.PrefetchScalarGridSpec(
            num_scalar_prefetch=1,
            grid=(t // _CTM,),
            in_specs=[
                pl.BlockSpec((_CTM, D_MODEL), lambda i, c: (i, 0)),
                pl.BlockSpec((spb, N_EXPERTS, LANES), lambda i, c: (i, 0, 0)),
                pl.BlockSpec(memory_space=pl.ANY),
                pl.BlockSpec((1, D_MODEL), lambda i, c: (0, 0)),
            ],
            out_specs=pl.BlockSpec((_CTM, D_MODEL), lambda i, c: (i, 0)),
            scratch_shapes=[pltpu.VMEM((2, N_EXPERTS * _CWIN, D_MODEL), BF16), pltpu.VMEM((_CWIN, D_MODEL), BF16),
                            pltpu.SemaphoreType.DMA((2,)), pltpu.SemaphoreType.DMA],
        ),
        out_shape=jax.ShapeDtypeStruct((t, D_MODEL), F32),
        compiler_params=_cparams(("arbitrary",)),
        name="combine",
    )(cb_full, x2d, spos, y, g_final.reshape(1, D_MODEL))


_SLAB_Q0 = 512
_SLAB_K0 = _SLAB_Q0 + D_HEADS * LANES
_SLAB_V0 = _SLAB_K0 + D_KV_HEADS * LANES
_ODD_COLS = _SLAB_V0 + D_KV_HEADS * LANES


def _proj_odd_kernel(x_ref, g_ref, wm_ref, cqn_ref, wq_ref, ckvn_ref, wkv_ref, dqn_ref, dkn_ref,
                     ccq_ref, s1cq_ref, s2cq_ref, cck_ref, s1ck_ref, s2ck_ref,
                     cdq_ref, s1dq_ref, s2dq_ref, cdk_ref, s1dk_ref, s2dk_ref,
                     qc_ref, kc_ref, vc_ref, qd_ref, kd_ref, vd_ref, stat_ref, *, steps_per_seq):
    lane = lax.broadcasted_iota(I32, (1, LANES), 1)
    one64 = jnp.where(lane == HEAD_DIM, 1.0, 0.0)
    last_lane = lane == LANES - 1
    half_rope = C_ROPE // 2
    stats = [jnp.zeros((1, LANES), F32), jnp.zeros((1, LANES), F32)]
    ones_mat = jnp.ones((LANES, LANES), BF16)

    def with_norm(val, fill, row, col):
        n2 = jnp.dot((val * val).astype(BF16), ones_mat, preferred_element_type=F32)
        stats[row] = jnp.where(lane == col, jnp.maximum(jnp.max(n2, axis=0, keepdims=True), stats[row]), stats[row])
        return jnp.where(last_lane, fill, val).astype(BF16)

    def head_norm(xg, gn_ref):
        ss = jnp.sum(xg * xg, axis=-1, keepdims=True) * (1.0 / HEAD_DIM)
        return xg * lax.rsqrt(ss + EPS) * gn_ref[...]

    n_chunks = 2
    rows_per = x_ref.shape[0] // n_chunks
    for c in range(n_chunks):
        rows = slice(c * rows_per, (c + 1) * rows_per)
        tab = lambda *refs: [r[rows, :] for r in refs]
        x = x_ref[rows, :]
        y = (x * _rms_scale(x) * g_ref[...]).astype(BF16)
        pm = jnp.dot(y, wm_ref[...], preferred_element_type=F32)
        cq = pm[:, :C_Q_RANK]
        cqn = (cq * _rms_scale(cq) * cqn_ref[...]).astype(BF16)
        qc = jnp.dot(cqn, wq_ref[...], preferred_element_type=F32)
        ckv = pm[:, C_Q_RANK:C_Q_RANK + C_KV_RANK]
        ckvn = (ckv * _rms_scale(ckv) * ckvn_ref[...]).astype(BF16)
        kv = jnp.dot(ckvn, wkv_ref[...], preferred_element_type=F32)
        kr = _rope3(pm[:, C_Q_RANK + C_KV_RANK:_SLAB_Q0], *tab(cck_ref, s1ck_ref, s2ck_ref), half_rope)
        t_cq = tab(ccq_ref, s1cq_ref, s2cq_ref)
        for h in range(C_HEADS):
            sl = slice(h * LANES, (h + 1) * LANES)
            qc_ref[rows, sl] = with_norm(_rope3(qc[:, sl], *t_cq, half_rope), 1.0, 1, h)
            kc_ref[rows, sl] = with_norm(kv[:, sl] + kr, -1.0, 0, h)
            vc_ref[rows, sl] = (kv[:, C_HEADS * LANES + h * LANES:C_HEADS * LANES + (h + 1) * LANES]
                                + one64).astype(BF16)
        t_dq = tab(cdq_ref, s1dq_ref, s2dq_ref)
        t_dk = tab(cdk_ref, s1dk_ref, s2dk_ref)
        for g in range(D_HEADS):
            xg = pm[:, _SLAB_Q0 + g * LANES:_SLAB_Q0 + (g + 1) * LANES]
            qd_ref[rows, g * LANES:(g + 1) * LANES] = with_norm(
                _rope3(head_norm(xg, dqn_ref), *t_dq, HEAD_DIM // 4), 1.0, 1, C_HEADS + g)
        for g in range(D_KV_HEADS):
            sl = slice(g * LANES, (g + 1) * LANES)
            xg = pm[:, _SLAB_K0 + g * LANES:_SLAB_K0 + (g + 1) * LANES]
            kd_ref[rows, sl] = with_norm(_rope3(head_norm(xg, dkn_ref), *t_dk, HEAD_DIM // 4), -1.0, 0, C_HEADS + g)
            vd_ref[rows, sl] = (pm[:, _SLAB_V0 + g * LANES:_SLAB_V0 + (g + 1) * LANES] + one64).astype(BF16)

    new = jnp.concatenate(stats + [jnp.zeros((6, LANES), F32)], axis=0)

    @pl.when(pl.program_id(0) % steps_per_seq == 0)
    def _():
        stat_ref[...] = new

    @pl.when(pl.program_id(0) % steps_per_seq != 0)
    def _():
        stat_ref[...] = jnp.maximum(stat_ref[...], new)


def _slabs(w, n_heads, width, lane_off=0):
    k = w.shape[0]
    w3 = w.reshape(k, n_heads, width)
    w3 = jnp.pad(w3, ((0, 0), (0, 0), (lane_off, LANES - width - lane_off)))
    return w3.reshape(k, n_heads * LANES)


def _axial_tables(row, col, scale):
    half = HEAD_DIM // 2
    cr, s1r, s2r = _rope_tables(row, D_THETA, half, 0, LANES, scale)
    cc, s1c, s2c = _rope_tables(col, D_THETA, half, half, LANES, scale)
    lane = jnp.arange(LANES)[None, :]
    return jnp.where(lane < half, cr, cc), s1r + s1c, s2r + s2c


def _proj_odd(x2d, seq, g_mix, w_in, cq_norm, w_cq_up, ckv_norm, w_ckv_up, dq_norm, dk_norm, tm=512):
    t = x2d.shape[0]
    nblk = seq // tm
    o1 = C_Q_RANK
    o2 = o1 + C_KV_RANK
    o3 = o2 + C_ROPE
    o4 = o3 + D_HEADS * HEAD_DIM
    o5 = o4 + D_KV_HEADS * HEAD_DIM
    wm = jnp.concatenate([
        w_in[:, :o2],
        _slabs(w_in[:, o2:o3], 1, C_ROPE, C_NOPE),
        _slabs(w_in[:, o3:o4], D_HEADS, HEAD_DIM),
        _slabs(w_in[:, o4:o5], D_KV_HEADS, HEAD_DIM),
        _slabs(w_in[:, o5:], D_KV_HEADS, HEAD_DIM),
    ], axis=1).astype(BF16)
    assert wm.shape[1] == _ODD_COLS
    wq = _slabs(w_cq_up, C_HEADS, C_NOPE + C_ROPE).astype(BF16)
    kv3 = w_ckv_up.reshape(C_KV_RANK, C_HEADS, 2 * HEAD_DIM)
    wkv = jnp.concatenate([
        _slabs(kv3[:, :, :C_NOPE].reshape(C_KV_RANK, -1), C_HEADS, C_NOPE),
        _slabs(kv3[:, :, C_NOPE:].reshape(C_KV_RANK, -1), C_HEADS, HEAD_DIM),
    ], axis=1).astype(BF16)
    pad64 = lambda g: jnp.pad(g, (0, LANES - HEAD_DIM)).reshape(1, LANES)

    pos = jnp.arange(seq, dtype=I32)
    row_pos = pos // GRID_W
    col_pos = pos % GRID_W
    c_scale = (C_NOPE + C_ROPE) ** -0.5 * LOG2E
    d_scale = HEAD_DIM ** -0.5 * LOG2E
    tabs = (_rope_tables(pos, ROPE_THETA, C_ROPE, C_NOPE, LANES, c_scale)
            + _rope_tables(pos, ROPE_THETA, C_ROPE, C_NOPE, LANES, 1.0)
            + _axial_tables(row_pos, col_pos, d_scale)
            + _axial_tables(row_pos, col_pos, 1.0))

    row = lambda i: (i, 0)
    full = lambda i: (0, 0)
    tspec = pl.BlockSpec((tm, LANES), lambda i: (i % nblk, 0))
    wide = C_HEADS * LANES
    kvw = D_KV_HEADS * LANES
    return pl.pallas_call(
        functools.partial(_proj_odd_kernel, steps_per_seq=nblk),
        grid=(t // tm,),
        in_specs=[
            pl.BlockSpec((tm, D_MODEL), row),
            pl.BlockSpec((1, D_MODEL), full),
            pl.BlockSpec(wm.shape, full),
            pl.BlockSpec((1, C_Q_RANK), full),
            pl.BlockSpec(wq.shape, full),
            pl.BlockSpec((1, C_KV_RANK), full),
            pl.BlockSpec(wkv.shape, full),
            pl.BlockSpec((1, LANES), full),
            pl.BlockSpec((1, LANES), full),
        ] + [tspec] * 12,
        out_specs=[
            pl.BlockSpec((tm, wide), row), pl.BlockSpec((tm, wide), row), pl.BlockSpec((tm, wide), row),
            pl.BlockSpec((tm, wide), row), pl.BlockSpec((tm, kvw), row), pl.BlockSpec((tm, kvw), row),
            pl.BlockSpec((None, 8, LANES), lambda i: (i // nblk, 0, 0)),
        ],
        out_shape=[
            jax.ShapeDtypeStruct((t, wide), BF16), jax.ShapeDtypeStruct((t, wide), BF16),
            jax.ShapeDtypeStruct((t, wide), BF16), jax.ShapeDtypeStruct((t, wide), BF16),
            jax.ShapeDtypeStruct((t, kvw), BF16), jax.ShapeDtypeStruct((t, kvw), BF16),
            jax.ShapeDtypeStruct((t // seq, 8, LANES), F32),
        ],
        compiler_params=_cparams(("arbitrary",)),
        name="proj_odd",
    )(x2d, g_mix.reshape(1, D_MODEL), wm, cq_norm.reshape(1, -1), wq, ckv_norm.reshape(1, -1), wkv,
      pad64(dq_norm), pad64(dk_norm), *tabs)


def _flash_kernel(q_ref, k_ref, v_ref, o_ref, qs_sc, m_sc, acc_sc, *, group, tq, tk):
    ki = pl.program_id(3)

    @pl.when(ki == 0)
    def _():
        for g in range(group):
            qs_sc[g * tq:(g + 1) * tq, :] = q_ref[:, g * LANES:(g + 1) * LANES]
        m_sc[...] = jnp.full(m_sc.shape, NEG, F32)
        acc_sc[...] = jnp.zeros(acc_sc.shape, F32)

    s = lax.dot_general(qs_sc[...], k_ref[...], _NT, preferred_element_type=F32)
    m_prev = m_sc[...]
    m_new = jnp.maximum(m_prev, jnp.max(s, axis=1, keepdims=True))
    alpha = jnp.exp2(m_prev - m_new)
    p = jnp.exp2(s - jnp.tile(m_new, (1, tk // LANES)))
    acc_sc[...] = alpha * acc_sc[...] + jnp.dot(p.astype(BF16), v_ref[...], preferred_element_type=F32)
    m_sc[...] = m_new

    @pl.when(ki == pl.num_programs(3) - 1)
    def _():
        acc = acc_sc[...]
        o = acc / acc[:, HEAD_DIM:HEAD_DIM + 1]
        for g in range(group):
            o_ref[:, g * LANES:(g + 1) * LANES] = o[g * tq:(g + 1) * tq].astype(BF16)


_FLASH_CHUNK = 2048
_V_ROWS = 80


def _flash_bounded_kernel(bound_ref, q_ref, k_ref, v_ref, o_ref, qs_sc, acc_sc, *, group, tq):
    ki = pl.program_id(3)
    head = pl.program_id(0) * pl.num_programs(1) + pl.program_id(1)

    @pl.when(ki == 0)
    def _():
        fix = jnp.where(lax.broadcasted_iota(I32, (1, LANES), 1) == LANES - 1, bound_ref[head], 1.0)
        for g in range(group):
            qs_sc[g * tq:(g + 1) * tq, :] = (q_ref[:, g * LANES:(g + 1) * LANES].astype(F32) * fix).astype(BF16)
        acc_sc[...] = jnp.zeros(acc_sc.shape, F32)

    chunk = min(_FLASH_CHUNK, k_ref.shape[0])
    n_chunks = k_ref.shape[0] // chunk
    qs = qs_sc[...]

    def scores(c):
        return lax.dot_general(k_ref[c * chunk:(c + 1) * chunk, :], qs, _NT, preferred_element_type=F32)

    def values(c, s_t):
        return lax.dot_general(v_ref[c * chunk:(c + 1) * chunk, :_V_ROWS], jnp.exp2(s_t).astype(BF16),
                               (((0,), (0,)), ((), ())), preferred_element_type=F32)

    acc = acc_sc[...]
    s_prev = scores(0)
    for c in range(1, n_chunks):
        s_next = scores(c)
        acc = acc + values(c - 1, s_prev)
        s_prev = s_next
    acc_sc[...] = acc + values(n_chunks - 1, s_prev)

    @pl.when(ki == pl.num_programs(3) - 1)
    def _():
        acc = acc_sc[...]
        o_t = acc / acc[HEAD_DIM:HEAD_DIM + 1, :]
        o = jnp.concatenate([o_t, jnp.zeros((LANES - _V_ROWS, o_t.shape[1]), F32)], axis=0).T
        for g in range(group):
            o_ref[:, g * LANES:(g + 1) * LANES] = o[g * tq:(g + 1) * tq].astype(BF16)


def _flash(q, k, v, group, nk, bounded, rows=1024, tk=512, tk_bounded=8192):
    b, s, qw = q.shape
    hk = k.shape[2] // LANES
    tq = rows // group
    tk = min(tk, s)
    tkb = min(tk_bounded, s)
    out_shape = jax.ShapeDtypeStruct((b, s, qw), BF16)
    sem = ("parallel", "parallel", "parallel", "arbitrary")

    def running_max(q, k, v, nk):
        qspec = pl.BlockSpec((None, tq, group * LANES), lambda bi, hi, qi, ki: (bi, qi, hi))
        kspec = pl.BlockSpec((None, tk, LANES), lambda bi, hi, qi, ki: (bi, ki, hi))
        return pl.pallas_call(
            functools.partial(_flash_kernel, group=group, tq=tq, tk=tk),
            grid=(b, hk, s // tq, s // tk),
            in_specs=[qspec, kspec, kspec],
            out_specs=qspec,
            out_shape=out_shape,
            scratch_shapes=[pltpu.VMEM((rows, LANES), BF16), pltpu.VMEM((rows, LANES), F32),
                            pltpu.VMEM((rows, LANES), F32)],
            compiler_params=_cparams(sem),
            name="flash",
        )(q, k, v)

    def bound(q, k, v, nk):
        qspec = pl.BlockSpec((None, tq, group * LANES), lambda bi, hi, qi, ki, nkr: (bi, qi, hi))
        kspec = pl.BlockSpec((None, tkb, LANES), lambda bi, hi, qi, ki, nkr: (bi, ki, hi))
        return pl.pallas_call(
            functools.partial(_flash_bounded_kernel, group=group, tq=tq),
            grid_spec=pltpu.PrefetchScalarGridSpec(
                num_scalar_prefetch=1,
                grid=(b, hk, s // tq, s // tkb),
                in_specs=[qspec, kspec, kspec],
                out_specs=qspec,
                scratch_shapes=[pltpu.VMEM((rows, LANES), BF16), pltpu.VMEM((_V_ROWS, rows), F32)],
            ),
            out_shape=out_shape,
            compiler_params=_cparams(sem),
            name="flash_bounded",
        )(nk.reshape(-1), q, k, v)

    return lax.cond(bounded, bound, running_max, q, k, v, nk)


def _moe(x1, h2, aff, batch, seq, w_gate, w_up, w_down, layer, g_final, final):
    idx, gates, spos, cb = _route(aff.reshape(-1, LANES), batch, seq)
    y = _ffn(idx, gates, h2, w_gate, w_up, w_down, layer, seq)
    return _combine(x1, spos.reshape(-1, N_EXPERTS, LANES), cb, y, g_final, final, seq)


def kernel(x, norm_mix, norm_ffn, even_w_in, even_gmlp_norm, even_w_spatial, even_b_spatial, even_w_out,
           odd_w_in, odd_cq_norm, odd_w_cq_up, odd_ckv_norm, odd_w_ckv_up, odd_dq_norm, odd_dk_norm, odd_w_out,
           moe_w_router, moe_w_gate, moe_w_up, moe_w_down, final_norm):
    b, s, d = x.shape
    depth = norm_mix.shape[0]
    x2d = x.reshape(b * s, d)
    for i in range(depth):
        j = i // 2
        last = i == depth - 1
        if i % 2 == 0:
            q, k, v, go = _proj_even(x2d, s, norm_mix[i], even_w_in[j], even_gmlp_norm[j], even_w_spatial[j],
                                     even_b_spatial[j])
            a = _dilated(q.reshape(b, s, A_WIDTH), k.reshape(b, s, A_WIDTH), v.reshape(b, s, A_WIDTH))
            x1, h2, aff = _outproj(x2d, a.reshape(b * s, A_WIDTH), go, even_w_out[j][:A_WIDTH],
                                   even_w_out[j][A_WIDTH:], norm_ffn[i], moe_w_router[i])
        else:
            qc, kc, vc, qd, kd, vd, stat = _proj_odd(x2d, s, norm_mix[i], odd_w_in[j], odd_cq_norm[j],
                                                     odd_w_cq_up[j], odd_ckv_norm[j], odd_w_ckv_up[j],
                                                     odd_dq_norm[j], odd_dk_norm[j])
            grp = D_HEADS // D_KV_HEADS
            k2_c, k2_d = stat[:, 0, :C_HEADS], stat[:, 0, C_HEADS:C_HEADS + D_KV_HEADS]
            q2_c = stat[:, 1, :C_HEADS]
            q2_d = jnp.max(stat[:, 1, C_HEADS:C_HEADS + D_HEADS].reshape(b, D_KV_HEADS, grp), axis=-1)
            bound_c = jnp.sqrt(q2_c * k2_c) * _NORM_MARGIN ** 2
            bound_d = jnp.sqrt(q2_d * k2_d) * _NORM_MARGIN ** 2
            bounded = jnp.maximum(jnp.max(bound_c), jnp.max(bound_d)) <= _MAX_SCORE_BOUND
            r3 = lambda z: z.reshape(b, s, -1)
            oc = _flash(r3(qc), r3(kc), r3(vc), 1, bound_c, bounded)
            od = _flash(r3(qd), r3(kd), r3(vd), grp, bound_d, bounded)
            cw = C_HEADS * HEAD_DIM
            x1, h2, aff = _outproj(x2d, oc.reshape(b * s, -1), od.reshape(b * s, -1),
                                   _slabs(odd_w_out[j][:cw].T, C_HEADS, HEAD_DIM).T,
                                   _slabs(odd_w_out[j][cw:].T, D_HEADS, HEAD_DIM).T,
                                   norm_ffn[i], moe_w_router[i])
        x2d = _moe(x1, h2, aff, b, s, moe_w_gate, moe_w_up, moe_w_down, i, final_norm, last)
    return x2d.reshape(b, s, d)
```

```python
import functools
import math

import jax
import jax.numpy as jnp
from jax import lax
from jax.experimental import pallas as pl
from jax.experimental.pallas import tpu as pltpu

F32 = jnp.float32
BF16 = jnp.bfloat16
I32 = jnp.int32

EPS = 1e-6
NEG = -1e30
LOG2E = 1.4426950408889634

D_MODEL = 1024
HEAD_DIM = 64
ROPE_THETA = 500000.0
ROT_DIM = 16
GRID_W = 64
A_HEADS = 12
A_WIDTH = 768
A_DILATIONS = (1, 4, 16)
A_RADIUS = 64
B_WIDTH = 256
B_GROUPS = 4
B_CHUNK = 128
C_HEADS = 8
C_Q_RANK = 256
C_KV_RANK = 128
C_NOPE = 64
C_ROPE = 32
D_HEADS = 8
D_KV_HEADS = 2
D_THETA = 10000.0
N_EXPERTS = 16
EC_FACTOR = 2
EXPERT_FF = 512

_MIN_NORMAL = 1.1754944e-38
_GEOMETRIC_STEPS = 40
_ARITHMETIC_STEPS = 24
_NORM_MARGIN = 1.01
_MAX_SCORE_BOUND = 55.0

LANES = 128
VMEM_LIMIT = 48 * 1024 * 1024
_DILATED_VMEM_LIMIT = 56 * 1024 * 1024

_NT = (((1,), (1,)), ((), ()))


def _cparams(sem):
    return pltpu.CompilerParams(dimension_semantics=sem, vmem_limit_bytes=VMEM_LIMIT)


def _rms_scale(x):
    return lax.rsqrt(jnp.mean(x * x, axis=-1, keepdims=True) + EPS)


def _rope3(a, c, s, lo, shift):
    return a * c + jnp.where(lo, pltpu.roll(a, LANES - shift, 1), pltpu.roll(a, shift, 1)) * s


def _split_dot(x, w_bf16):
    hi = x.astype(BF16)
    lo = (x - hi.astype(F32)).astype(BF16)
    return (jnp.dot(hi, w_bf16, preferred_element_type=F32)
            + jnp.dot(lo, w_bf16, preferred_element_type=F32))


def _proj_even_kernel(x_ref, g_ref, w_ref, cq_ref, sq_ref, ck_ref, sk_ref, lo_ref,
                      gn_ref, gmat_ref, ws_ref, bs_ref,
                      q_ref, k_ref, v_ref, go_ref):
    x = x_ref[...]
    y = (x * _rms_scale(x) * g_ref[...]).astype(BF16)
    tm = x.shape[0]

    aq = jnp.dot(y, w_ref[:, 0:A_WIDTH], preferred_element_type=F32)
    lo = lo_ref[...] > 0.0
    tq = (cq_ref[...], sq_ref[...], lo)
    for j in range(A_WIDTH // LANES):
        sl = slice(j * LANES, (j + 1) * LANES)
        q_ref[:, sl] = _rope3(aq[:, sl], *tq, ROT_DIM // 2)
    ak = jnp.dot(y, w_ref[:, A_WIDTH:2 * A_WIDTH], preferred_element_type=F32)
    tk = (ck_ref[...], sk_ref[...], lo)
    for j in range(A_WIDTH // LANES):
        sl = slice(j * LANES, (j + 1) * LANES)
        k_ref[:, sl] = _rope3(ak[:, sl], *tk, ROT_DIM // 2)
    v_ref[...] = jnp.dot(y, w_ref[:, 2 * A_WIDTH:3 * A_WIDTH], preferred_element_type=F32)

    z = jnp.dot(y, w_ref[:, 3 * A_WIDTH:3 * A_WIDTH + 2 * B_WIDTH], preferred_element_type=F32)
    ge = jax.nn.gelu(z)
    u = ge[:, :B_WIDTH]
    vv = ge[:, B_WIDTH:]
    ss = _split_dot(vv * vv, gmat_ref[...])
    vn = (vv * lax.rsqrt(ss + EPS) * gn_ref[...]).astype(BF16)
    grp = lax.broadcasted_iota(I32, (B_CHUNK, B_WIDTH), 1) // (B_WIDTH // B_GROUPS)
    for c in range(tm // B_CHUNK):
        rows = slice(c * B_CHUNK, (c + 1) * B_CHUNK)
        vc = vn[rows]
        mg = [jnp.dot(ws_ref[g], vc, preferred_element_type=F32) for g in range(B_GROUPS)]
        mixed = jnp.where(grp == 0, mg[0], jnp.where(grp == 1, mg[1], jnp.where(grp == 2, mg[2], mg[3])))
        go_ref[rows, :] = (u[rows] * (mixed + bs_ref[...])).astype(BF16)


def _rope_tables(pos, theta, r, lane_off, period, scale):
    half = r // 2
    inv = jnp.power(jnp.float32(theta), -jnp.arange(half, dtype=F32) * (2.0 / r))
    ang = pos.astype(F32)[:, None] * inv[None, :]
    cos, sin = jnp.cos(ang), jnp.sin(ang)
    o = (jnp.arange(LANES) % period) - lane_off
    in_lo = (o >= 0) & (o < half)
    in_hi = (o >= half) & (o < r)
    idx = jnp.clip(jnp.where(in_hi, o - half, o), 0, half - 1)
    c = jnp.where((in_lo | in_hi)[None, :], cos[:, idx], 1.0)
    s = jnp.where(in_lo[None, :], -sin[:, idx], jnp.where(in_hi[None, :], sin[:, idx], 0.0))
    return c * scale, s * scale, jnp.where(in_lo, 1.0, 0.0).reshape(1, LANES)


def _proj_even(x2d, seq, g_mix, w_in, gmlp_norm, w_s, b_s, tm=512):
    t = x2d.shape[0]
    nblk = seq // tm
    pos = jnp.arange(seq, dtype=I32)
    qscale = HEAD_DIM ** -0.5 * LOG2E
    cq, sq, lo = _rope_tables(pos, ROPE_THETA, ROT_DIM, 0, HEAD_DIM, qscale)
    ck, sk, _ = _rope_tables(pos, ROPE_THETA, ROT_DIM, 0, HEAD_DIM, 1.0)
    gdim = B_WIDTH // B_GROUPS
    gid = jnp.arange(B_WIDTH) // gdim
    gmat = jnp.where(gid[:, None] == gid[None, :], 1.0 / gdim, 0.0).astype(BF16)
    bias = jnp.repeat(b_s.T, gdim, axis=1)
    row = lambda i: (i, 0)
    tab = lambda i: (i % nblk, 0)
    full = lambda i: (0, 0)
    tspec = pl.BlockSpec((tm, LANES), tab)
    return pl.pallas_call(
        _proj_even_kernel,
        grid=(t // tm,),
        in_specs=[
            pl.BlockSpec((tm, D_MODEL), row),
            pl.BlockSpec((1, D_MODEL), full),
            pl.BlockSpec(w_in.shape, full),
            tspec, tspec, tspec, tspec,
            pl.BlockSpec((1, LANES), full),
            pl.BlockSpec((1, B_WIDTH), full),
            pl.BlockSpec((B_WIDTH, B_WIDTH), full),
            pl.BlockSpec((B_GROUPS, B_CHUNK, B_CHUNK), lambda i: (0, 0, 0)),
            pl.BlockSpec((B_CHUNK, B_WIDTH), full),
        ],
        out_specs=[
            pl.BlockSpec((tm, A_WIDTH), row),
            pl.BlockSpec((tm, A_WIDTH), row),
            pl.BlockSpec((tm, A_WIDTH), row),
            pl.BlockSpec((tm, B_WIDTH), row),
        ],
        out_shape=[
            jax.ShapeDtypeStruct((t, A_WIDTH), F32),
            jax.ShapeDtypeStruct((t, A_WIDTH), F32),
            jax.ShapeDtypeStruct((t, A_WIDTH), F32),
            jax.ShapeDtypeStruct((t, B_WIDTH), BF16),
        ],
        compiler_params=_cparams(("parallel",)),
        name="proj_even",
    )(x2d, g_mix.reshape(1, D_MODEL), w_in.astype(BF16), cq, sq, ck, sk, lo,
      gmlp_norm.reshape(1, B_WIDTH), gmat, w_s.astype(BF16), bias)


_TQ = 128
_TK = _TQ + 2 * A_RADIUS
_NORM_ROWS = 1024


def _dilated_kernel(q_ref, k_ref, v_ref, o_ref, m_sc, l_sc, bias_sc, *, seq):
    half0 = lax.broadcasted_iota(I32, (1, LANES), 1) < HEAD_DIM
    same_head = jnp.where(lax.broadcasted_iota(I32, (LANES, LANES), 0) // HEAD_DIM
                          == lax.broadcasted_iota(I32, (LANES, LANES), 1) // HEAD_DIM,
                          1.0, 0.0).astype(BF16)
    n_pat = len(A_DILATIONS)

    def max_head_sq(ref):
        def body(c, mx):
            x = ref[pl.ds(pl.multiple_of(c * _NORM_ROWS, _NORM_ROWS), _NORM_ROWS), :]
            n2 = jnp.dot((x * x).astype(BF16), same_head, preferred_element_type=F32)
            return jnp.maximum(mx, jnp.max(n2, axis=0, keepdims=True))
        return lax.fori_loop(0, seq // _NORM_ROWS, body, jnp.zeros((1, LANES), F32))

    both = jnp.sqrt(max_head_sq(q_ref) * max_head_sq(k_ref)) * (_NORM_MARGIN * _NORM_MARGIN)
    both = jnp.broadcast_to(both, (8, LANES))
    other = pltpu.roll(both, HEAD_DIM, 1)
    bound = [jnp.where(half0, both, other)[:1], jnp.where(half0, other, both)[:1]]
    worst = jnp.max(both)

    diff = lax.broadcasted_iota(I32, (_TQ, _TK), 1) - lax.broadcasted_iota(I32, (_TQ, _TK), 0)
    for case in range(3):
        band = jnp.where(jnp.abs(diff - case * A_RADIUS) <= A_RADIUS, 0.0, NEG)
        bias_sc[6 + case] = band
        for h in range(2):
            bias_sc[3 * h + case] = band - jnp.tile(bound[h], (1, _TK // LANES))
    one_bf16 = jnp.ones((), BF16)

    def run(bounded):
        for pi, d in enumerate(A_DILATIONS):
            cls_len = seq // d
            tpc = cls_len // _TQ

            def tile(qrows, kb, vb, case, pi=pi):
                q = q_ref[qrows, :]
                parts = []
                for h in range(2):
                    qh = jnp.where(half0 if h == 0 else jnp.logical_not(half0), q, 0.0).astype(BF16)
                    s = lax.dot_general(qh, kb, _NT, preferred_element_type=F32)
                    if bounded:
                        mt = None
                        p = jnp.exp2(s + bias_sc[3 * h + case]).astype(BF16)
                    else:
                        s = s + bias_sc[6 + case]
                        mt = jnp.max(s, axis=-1, keepdims=True)
                        p = jnp.exp2(s - mt).astype(BF16)
                    vh = jnp.where(half0 if h == 0 else jnp.logical_not(half0), vb, one_bf16)
                    parts.append((mt, jnp.dot(p, vh, preferred_element_type=F32)))
                ot = jnp.where(half0, parts[0][1], parts[1][1])
                lt = pltpu.roll(jnp.where(half0, parts[1][1], parts[0][1]), HEAD_DIM, 1)
                if bounded:
                    if pi > 0:
                        lt = l_sc[qrows, :] + lt
                        ot = o_ref[qrows, :] + ot
                else:
                    mt = jnp.where(half0, parts[0][0], parts[1][0])
                    if pi > 0:
                        mp = m_sc[qrows, :]
                        mn = jnp.maximum(mp, mt)
                        a = jnp.exp2(mp - mn)
                        b = jnp.exp2(mt - mn)
                        lt = a * l_sc[qrows, :] + b * lt
                        ot = a * o_ref[qrows, :] + b * ot
                        mt = mn
                    if pi < n_pat - 1:
                        m_sc[qrows, :] = mt
                if pi == n_pat - 1:
                    o_ref[qrows, :] = ot / lt
                else:
                    l_sc[qrows, :] = lt
                    o_ref[qrows, :] = ot

            def window(l0, cls_len=cls_len):
                if isinstance(l0, int):
                    kst = min(max(l0 - A_RADIUS, 0), cls_len - _TK)
                else:
                    kst = jnp.clip(l0 - A_RADIUS, 0, cls_len - _TK)
                return kst, (l0 - kst) // A_RADIUS

            if d < 8:

                def body(j, carry, d=d, tpc=tpc):
                    i = j // tpc
                    l0 = (j % tpc) * _TQ
                    kst, case = window(l0)
                    if d == 1:
                        qrows = pl.ds(pl.multiple_of(l0, _TQ), _TQ)
                        krows = pl.ds(pl.multiple_of(kst, A_RADIUS), _TK)
                    else:
                        qrows = pl.ds(l0 * d + i, _TQ, stride=d)
                        krows = pl.ds(kst * d + i, _TK, stride=d)
                    tile(qrows, k_ref[krows, :].astype(BF16), v_ref[krows, :].astype(BF16), case)
                    return carry

                lax.fori_loop(0, seq // _TQ, body, 0, unroll=16)
            else:

                def body(i, carry, d=d, cls_len=cls_len, tpc=tpc):
                    cls = pl.ds(i, cls_len, stride=d)
                    kc = k_ref[cls, :].astype(BF16)
                    vc = v_ref[cls, :].astype(BF16)
                    for n in range(tpc):
                        kst, case = window(n * _TQ)
                        tile(pl.ds(n * _TQ * d + i, _TQ, stride=d), kc[kst:kst + _TK], vc[kst:kst + _TK], case)
                    return carry

                lax.fori_loop(0, d, body, 0, unroll=4)

    @pl.when(worst <= _MAX_SCORE_BOUND)
    def _():
        run(True)

    @pl.when(jnp.logical_not(worst <= _MAX_SCORE_BOUND))
    def _():
        run(False)


def _dilated(q, k, v):
    b, s, w = q.shape
    spec = pl.BlockSpec((None, s, LANES), lambda bi, hi: (bi, 0, hi))
    return pl.pallas_call(
        functools.partial(_dilated_kernel, seq=s),
        grid=(b, w // LANES),
        in_specs=[spec, spec, spec],
        out_specs=spec,
        out_shape=jax.ShapeDtypeStruct((b, s, w), F32),
        scratch_shapes=[pltpu.VMEM((s, LANES), F32), pltpu.VMEM((s, LANES), F32),
                        pltpu.VMEM((9, _TQ, _TK), F32)],
        compiler_params=pltpu.CompilerParams(dimension_semantics=("parallel", "parallel"),
                                             vmem_limit_bytes=_DILATED_VMEM_LIMIT),
        name="dilated",
    )(q, k, v)


def _outproj_kernel(x_ref, a_ref, b_ref, wa_ref, wb_ref, gf_ref, wr_ref, x1_ref, h2_ref, aff_ref):
    x1 = (x_ref[...]
          + jnp.dot(a_ref[...].astype(BF16), wa_ref[...], preferred_element_type=F32)
          + jnp.dot(b_ref[...].astype(BF16), wb_ref[...], preferred_element_type=F32))
    x1_ref[...] = x1
    h2 = x1 * _rms_scale(x1) * gf_ref[...]
    n_tiles = D_MODEL // LANES
    for j in range(n_tiles):
        h2_ref[pl.ds(j, h2.shape[0], stride=n_tiles), :] = h2[:, j * LANES:(j + 1) * LANES]
    hi = h2.astype(BF16)
    lo = (h2 - hi.astype(F32)).astype(BF16)
    both = jnp.dot(hi, wr_ref[...], preferred_element_type=F32)
    lg = (both[:, :LANES] + both[:, LANES:]
          + jnp.dot(lo, wr_ref[:, :LANES], preferred_element_type=F32))
    valid = lax.broadcasted_iota(I32, lg.shape, 1) < N_EXPERTS
    lg = jnp.where(valid, lg, NEG)
    e = jnp.exp(lg - jnp.max(lg, axis=-1, keepdims=True))
    aff = e / jnp.sum(e, axis=-1, keepdims=True)
    aff_t = aff.T
    for j in range(aff.shape[0] // LANES):
        aff_ref[j] = aff_t[:N_EXPERTS, j * LANES:(j + 1) * LANES]


def _outproj(x2d, a, b, wa, wb, g_ffn, w_router, tm=512):
    t = x2d.shape[0]
    wr = jnp.pad(w_router, ((0, 0), (0, LANES - N_EXPERTS)))
    wr_hi = wr.astype(BF16)
    wr_lo = (wr - wr_hi.astype(F32)).astype(BF16)
    wr2 = jnp.concatenate([wr_hi, wr_lo], axis=1)
    row = lambda i: (i, 0)
    full = lambda i: (0, 0)
    return pl.pallas_call(
        _outproj_kernel,
        grid=(t // tm,),
        in_specs=[
            pl.BlockSpec((tm, D_MODEL), row),
            pl.BlockSpec((tm, a.shape[1]), row),
            pl.BlockSpec((tm, b.shape[1]), row),
            pl.BlockSpec(wa.shape, full),
            pl.BlockSpec(wb.shape, full),
            pl.BlockSpec((1, D_MODEL), full),
            pl.BlockSpec((D_MODEL, 2 * LANES), full),
        ],
        out_specs=[
            pl.BlockSpec((tm, D_MODEL), row),
            pl.BlockSpec((tm * (D_MODEL // LANES), LANES), row),
            pl.BlockSpec((tm // LANES, N_EXPERTS, LANES), lambda i: (i, 0, 0)),
        ],
        out_shape=[
            jax.ShapeDtypeStruct((t, D_MODEL), F32),
            jax.ShapeDtypeStruct((t * (D_MODEL // LANES), LANES), F32),
            jax.ShapeDtypeStruct((t // LANES, N_EXPERTS, LANES), F32),
        ],
        compiler_params=_cparams(("parallel",)),
        name="outproj",
    )(x2d, a, b, wa.astype(BF16), wb.astype(BF16), g_ffn.reshape(1, D_MODEL), wr2)


def _route_kernel(aff_ref, idx_ref, gate_ref, spos_ref, cb_ref, thr_sc, need_sc, *, cap):
    nblk = aff_ref.shape[0] // N_EXPERTS
    aff3 = aff_ref[...].reshape(nblk, N_EXPERTS, LANES)

    def count(pred):
        return jnp.sum(jnp.sum(jnp.where(pred, 1.0, 0.0), axis=0), axis=1, keepdims=True)

    def narrow(bracket, mid):
        lo, hi = bracket
        enough = count(aff3 >= mid[None]) >= cap
        return jnp.where(enough, mid, lo), jnp.where(enough, hi, mid)

    def geometric(it, bracket):
        lo, hi = bracket
        return narrow(bracket, jnp.clip(jnp.sqrt(lo) * jnp.sqrt(hi), lo, hi))

    def arithmetic(it, bracket):
        lo, hi = bracket
        return narrow(bracket, lo + (hi - lo) * 0.5)

    bracket = (jnp.full((N_EXPERTS, 1), _MIN_NORMAL, F32), jnp.full((N_EXPERTS, 1), 2.0, F32))
    bracket = lax.fori_loop(0, _GEOMETRIC_STEPS, geometric, bracket)
    lo, hi = lax.fori_loop(0, _ARITHMETIC_STEPS, arithmetic, bracket)
    below = jnp.where(aff3 < hi[None], aff3, -1.0)
    thr = jnp.max(jnp.max(below, axis=0), axis=1, keepdims=True)
    need = cap - count(aff3 > thr[None])
    thr_sc[...] = jnp.broadcast_to(thr, (N_EXPERTS, LANES))
    need_sc[...] = jnp.broadcast_to(need, (N_EXPERTS, LANES))

    ri = lax.broadcasted_iota(I32, (LANES, LANES), 0)
    ci = lax.broadcasted_iota(I32, (LANES, LANES), 1)
    upper = jnp.where(ri <= ci, 1.0, 0.0).astype(BF16)
    lower = jnp.where(ci <= ri, 1.0, 0.0).astype(BF16)
    eye = jnp.where(ri == ci, 1.0, 0.0).astype(BF16)
    ones = jnp.ones((LANES, LANES), BF16)
    bi = lax.broadcasted_iota(I32, (nblk, nblk), 0)
    bj = lax.broadcasted_iota(I32, (nblk, nblk), 1)
    strict = jnp.where(bj < bi, 1.0, 0.0).astype(BF16)
    before = jnp.where(bi < bj, 1.0, 0.0).astype(BF16)
    mean_rows = jnp.full((8, LANES), 1.0 / LANES, BF16)
    c_row = lax.broadcasted_iota(I32, (1, cap), 1).astype(F32)
    blk_iota = lax.broadcasted_iota(I32, (nblk, cap), 0).astype(F32)
    t_iota = lax.broadcasted_iota(I32, (LANES, cap), 0).astype(F32)
    rep = cap // LANES

    def cums(mask_bf16):
        lp = jnp.dot(mask_bf16, upper, preferred_element_type=F32)
        bc = jnp.dot(mask_bf16, ones, preferred_element_type=F32)
        bst = jnp.dot(strict, bc.astype(BF16), preferred_element_type=F32)
        return lp, bc, bst

    def per_expert(e, carry):
        a = aff_ref[pl.ds(e, nblk, stride=N_EXPERTS), :]
        thr_e = thr_sc[pl.ds(e, 1), :]
        need_e = need_sc[pl.ds(e, 1), :]
        gt = a > thr_e
        eq = a == thr_e
        eqf = jnp.where(eq, 1.0, 0.0)
        lp_q, _, bst_q = cums(eqf.astype(BF16))
        sel = jnp.logical_or(gt, jnp.logical_and(eq, bst_q + lp_q - eqf < need_e))
        mb = jnp.where(sel, 1.0, 0.0).astype(BF16)
        lp, bc, bst = cums(mb)
        spos_ref[pl.ds(e, nblk, stride=N_EXPERTS), :] = jnp.where(sel, bst + lp - 1.0, -1.0)
        bc_row = lax.dot_general(mean_rows, bc.astype(BF16), _NT, preferred_element_type=F32)
        cb_ref[pl.ds(e, 1), :] = jnp.dot(bc_row.astype(BF16), before, preferred_element_type=F32)[:1].astype(I32)
        bend_w = jnp.tile(bst + bc, (1, rep))
        bst_w = jnp.tile(bst, (1, rep))
        blk_c = jnp.sum(jnp.where(bend_w <= c_row, 1.0, 0.0), axis=0, keepdims=True)
        onehot = blk_iota == blk_c
        bst_c = jnp.sum(jnp.where(onehot, bst_w, 0.0), axis=0, keepdims=True)
        r_c = c_row - bst_c
        ohb = jnp.where(onehot, 1.0, 0.0).astype(BF16)
        lp_t = lax.dot_general(lower, mb, _NT, preferred_element_type=F32)
        lp_c = jnp.dot(lp_t.astype(BF16), ohb, preferred_element_type=F32)
        tl_c = jnp.sum(jnp.where(lp_c <= r_c, 1.0, 0.0), axis=0, keepdims=True)
        idx_ref[pl.ds(e, 1), :] = (blk_c * LANES + tl_c).astype(I32)
        a_hi = a.astype(BF16)
        a_lo = (a - a_hi.astype(F32)).astype(BF16)
        at_hi = lax.dot_general(eye, a_hi, _NT, preferred_element_type=F32).astype(BF16)
        at_lo = lax.dot_general(eye, a_lo, _NT, preferred_element_type=F32).astype(BF16)
        g_c = (jnp.dot(at_hi, ohb, preferred_element_type=F32)
               + jnp.dot(at_lo, ohb, preferred_element_type=F32))
        gate_ref[pl.ds(e, 1), :] = jnp.sum(jnp.where(t_iota == tl_c, g_c, 0.0), axis=0, keepdims=True)
        return carry

    lax.fori_loop(0, N_EXPERTS, per_expert, 0, unroll=2)


def _route(aff2d, batch, seq):
    cap = EC_FACTOR * seq // N_EXPERTS
    nblk = seq // LANES
    rows = nblk * N_EXPERTS
    out_spec = pl.BlockSpec((None, N_EXPERTS, cap), lambda b: (b, 0, 0))
    return pl.pallas_call(
        functools.partial(_route_kernel, cap=cap),
        grid=(batch,),
        in_specs=[pl.BlockSpec((rows, LANES), lambda b: (b, 0))],
        out_specs=[out_spec, out_spec, pl.BlockSpec((rows, LANES), lambda b: (b, 0)),
                   pl.BlockSpec((None, N_EXPERTS, nblk), lambda b: (b, 0, 0))],
        out_shape=[jax.ShapeDtypeStruct((batch, N_EXPERTS, cap), I32),
                   jax.ShapeDtypeStruct((batch, N_EXPERTS, cap), F32),
                   jax.ShapeDtypeStruct((batch * rows, LANES), F32),
                   jax.ShapeDtypeStruct((batch, N_EXPERTS, nblk), I32)],
        scratch_shapes=[pltpu.VMEM((N_EXPERTS, LANES), F32), pltpu.VMEM((N_EXPERTS, LANES), F32)],
        compiler_params=_cparams(("parallel",)),
        name="route",
    )(aff2d)


def _ffn_kernel(idx_ref, nxt_ref, gate_ref, h_hbm, wg32_ref, wu32_ref, wd32_ref, y_ref, buf, sem,
                wg_ref, wu_ref, wd_ref, *, seq, tc, nsub):
    seq_id = pl.program_id(1)
    n_seq = pl.num_programs(1)
    step = pl.program_id(0) * n_seq + seq_id
    last_step = pl.num_programs(0) * n_seq - 1
    base = seq_id * seq
    next_base = jnp.where(seq_id + 1 < n_seq, seq_id + 1, 0) * seq

    @pl.when(seq_id == 0)
    def _():
        wg_ref[...] = wg32_ref[...].astype(BF16)
        wu_ref[...] = wu32_ref[...].astype(BF16)
        wd_ref[...] = wd32_ref[...].astype(BF16)

    n_tiles = D_MODEL // LANES

    def row_copy(ids, row0, j, r, slot):
        tok = pl.multiple_of((row0 + ids[0, 0, j * tc + r]) * n_tiles, n_tiles)
        return pltpu.make_async_copy(h_hbm.at[pl.ds(tok, n_tiles), :],
                                     buf.at[slot, pl.ds(r * n_tiles, n_tiles), :], sem.at[slot])

    def issue(ids, row0, j, slot):
        for r in range(tc):
            row_copy(ids, row0, j, r, slot).start()

    diag = lax.broadcasted_iota(I32, (tc, tc), 0) == lax.broadcasted_iota(I32, (tc, tc), 1)
    ones = jnp.ones((tc, LANES), BF16)

    ahead = 2

    @pl.when(step == 0)
    def _():
        for j in range(ahead):
            issue(idx_ref, base, j, j)

    for j in range(nsub):
        slot = j
        for r in range(tc):
            row_copy(idx_ref, base, j, r, slot).wait()
        xs = jnp.concatenate([buf.at[slot][pl.ds(c, tc, stride=n_tiles), :].astype(BF16) for c in range(n_tiles)],
                             axis=1)
        g = jnp.dot(xs, wg_ref[...], preferred_element_type=F32)
        u = jnp.dot(xs, wu_ref[...], preferred_element_type=F32)
        hm = (jax.nn.silu(g) * u).astype(BF16)
        y = jnp.dot(hm, wd_ref[...], preferred_element_type=F32)
        gr = jnp.broadcast_to(gate_ref[0, :, j * tc:(j + 1) * tc], (tc, tc))
        gcol = _split_dot(jnp.where(diag, gr, 0.0), ones)
        y_ref[j * tc:(j + 1) * tc, :] = (y * jnp.tile(gcol, (1, D_MODEL // LANES))).astype(BF16)
        if j + ahead < nsub:
            issue(idx_ref, base, j + ahead, j + ahead)
        else:
            issue(nxt_ref, next_base, j + ahead - nsub, j + ahead - nsub)

    @pl.when(step == last_step)
    def _():
        for j in range(ahead):
            for r in range(tc):
                row_copy(nxt_ref, next_base, j, r, j).wait()


def _ffn(idx, gates, h2d, w_gate, w_up, w_down, layer, seq, tc=256):
    b, ne, cap = idx.shape
    tc = min(tc, cap // 4)
    nsub = cap // tc
    assert nsub > 2
    idx3 = idx.reshape(b * ne, 1, cap)
    gate3 = gates.reshape(b * ne, 1, cap)
    slot = lambda ei, bi: (bi * ne + ei, 0, 0)
    next_slot = lambda ei, bi: (jnp.where(bi + 1 < b, (bi + 1) * ne + ei, jnp.minimum(ei + 1, ne - 1)), 0, 0)
    wspec = lambda shape: pl.BlockSpec((None, None) + shape, lambda ei, bi: (layer, ei, 0, 0))
    return pl.pallas_call(
        functools.partial(_ffn_kernel, seq=seq, tc=tc, nsub=nsub),
        grid=(ne, b),
        in_specs=[
            pl.BlockSpec((1, 1, cap), slot, memory_space=pltpu.SMEM),
            pl.BlockSpec((1, 1, cap), next_slot, memory_space=pltpu.SMEM),
            pl.BlockSpec((1, 1, cap), slot),
            pl.BlockSpec(memory_space=pl.ANY),
            wspec((D_MODEL, EXPERT_FF)), wspec((D_MODEL, EXPERT_FF)), wspec((EXPERT_FF, D_MODEL)),
        ],
        out_specs=pl.BlockSpec((None, None, cap, D_MODEL), lambda ei, bi: (bi, ei, 0, 0)),
        out_shape=jax.ShapeDtypeStruct((b, ne, cap, D_MODEL), BF16),
        scratch_shapes=[pltpu.VMEM((nsub, tc * (D_MODEL // LANES), LANES), F32), pltpu.SemaphoreType.DMA((nsub,)),
                        pltpu.VMEM((D_MODEL, EXPERT_FF), BF16), pltpu.VMEM((D_MODEL, EXPERT_FF), BF16),
                        pltpu.VMEM((EXPERT_FF, D_MODEL), BF16)],
        compiler_params=pltpu.CompilerParams(dimension_semantics=("arbitrary", "arbitrary"),
                                             vmem_limit_bytes=VMEM_LIMIT, disable_bounds_checks=True),
        name="ffn",
    )(idx3, idx3, gate3, h2d, w_gate, w_up, w_down)


_CTM = 256
_CWIN = 64
_CALIGN = 16


def _combine_kernel(cb_ref, x_ref, sp_ref, y_hbm, g_ref, o_ref, ybuf, xbuf, sem, xsem, *,
                    final, tiles_per_seq, nblk, cap):
    i = pl.program_id(0)
    n_tiles = pl.num_programs(0)
    b = i // tiles_per_seq
    slot = i % 2

    def window(tile, e):
        tb = tile // tiles_per_seq
        off = (tb * N_EXPERTS + e) * (nblk + 1) + (tile % tiles_per_seq) * (_CTM // LANES)
        s0 = cb_ref[off]
        s1 = cb_ref[off + _CTM // LANES]
        start = jnp.minimum((s0 // _CALIGN) * _CALIGN, cap - _CWIN)
        return s1, pl.multiple_of(start, _CALIGN)

    def fetch(tile, e, start, buf_slot):
        return pltpu.make_async_copy(y_hbm.at[tile // tiles_per_seq, e, pl.ds(start, _CWIN), :],
                                     ybuf.at[buf_slot, pl.ds(e * _CWIN, _CWIN), :], sem.at[buf_slot])

    def fetch_all(tile, buf_slot):
        for e in range(N_EXPERTS):
            fetch(tile, e, window(tile, e)[1], buf_slot).start()

    @pl.when(i == 0)
    def _():
        fetch_all(i, slot)

    @pl.when(i + 1 < n_tiles)
    def _():
        fetch_all(i + 1, 1 - slot)

    wins = [window(i, e) for e in range(N_EXPERTS)]
    pad = jnp.full((LANES - N_EXPERTS, LANES), -1.0, F32)
    sp_t = jnp.concatenate([jnp.concatenate([sp_ref[hf], pad], axis=0).T for hf in range(_CTM // LANES)],
                           axis=0)
    lane = lax.broadcasted_iota(I32, (1, _CWIN), 1).astype(F32)
    lane2 = lax.broadcasted_iota(I32, (1, LANES), 1)
    per_tile = LANES // _CWIN
    hits = []
    for e0 in range(0, N_EXPERTS, per_tile):
        rel = sp_t[:, e0:e0 + 1] - wins[e0][1].astype(F32)
        for k in range(1, per_tile):
            rel = jnp.where(lane2 < k * _CWIN, rel,
                            sp_t[:, e0 + k:e0 + k + 1] - (wins[e0 + k][1] - k * _CWIN).astype(F32))
        hits.append(jnp.where(rel == lane2.astype(F32), 1.0, 0.0).astype(BF16))
    for e in range(N_EXPERTS):
        fetch(i, e, wins[e][1], slot).wait()
    o_ref[...] = x_ref[...] + jnp.dot(jnp.concatenate(hits, axis=1), ybuf[slot], preferred_element_type=F32)
    for e in range(N_EXPERTS):
        s1, start = wins[e]
        col = sp_t[:, e:e + 1]

        def extra(k, carry, e=e, s1=s1, start=start, col=col):
            lo = start + (k + 1) * _CWIN
            st = pl.multiple_of(jnp.minimum(lo, cap - _CWIN), _CALIGN)
            cp = pltpu.make_async_copy(y_hbm.at[b, e, pl.ds(st, _CWIN), :], xbuf, xsem)
            cp.start()
            cp.wait()
            hit = jnp.where(jnp.logical_and(col - st.astype(F32) == lane, col >= lo.astype(F32)), 1.0, 0.0)
            o_ref[...] += jnp.dot(hit.astype(BF16), xbuf[...], preferred_element_type=F32)
            return carry

        n_extra = jnp.maximum(s1 - start - 1, 0) // _CWIN
        lax.fori_loop(0, n_extra, extra, 0)
    if final:
        x = o_ref[...]
        o_ref[...] = x * _rms_scale(x) * g_ref[...]


def _combine(x2d, spos, cb, y, g_final, final, seq):
    t = x2d.shape[0]
    batch, ne, cap, _ = y.shape
    nblk = seq // LANES
    cb_full = jnp.concatenate([cb, jnp.full((batch, ne, 1), cap, I32)], axis=-1).reshape(-1)
    spb = _CTM // LANES
    return pl.pallas_call(
        functools.partial(_combine_kernel, final=final, tiles_per_seq=seq // _CTM, nblk=nblk, cap=cap),
        grid_spec=pltpu.PrefetchScalarGridSpec(
            num_scalar_prefetch=1,
            grid=(t // _CTM,),
            in_specs=[
                pl.BlockSpec((_CTM, D_MODEL), lambda i, c: (i, 0)),
                pl.BlockSpec((spb, N_EXPERTS, LANES), lambda i, c: (i, 0, 0)),
                pl.BlockSpec(memory_space=pl.ANY),
                pl.BlockSpec((1, D_MODEL), lambda i, c: (0, 0)),
            ],
            out_specs=pl.BlockSpec((_CTM, D_MODEL), lambda i, c: (i, 0)),
            scratch_shapes=[pltpu.VMEM((2, N_EXPERTS * _CWIN, D_MODEL), BF16), pltpu.VMEM((_CWIN, D_MODEL), BF16),
                            pltpu.SemaphoreType.DMA((2,)), pltpu.SemaphoreType.DMA],
        ),
        out_shape=jax.ShapeDtypeStruct((t, D_MODEL), F32),
        compiler_params=_cparams(("arbitrary",)),
        name="combine",
    )(cb_full, x2d, spos, y, g_final.reshape(1, D_MODEL))


_SLAB_Q0 = 512
_SLAB_K0 = _SLAB_Q0 + D_HEADS * LANES
_SLAB_V0 = _SLAB_K0 + D_KV_HEADS * LANES
_ODD_COLS = _SLAB_V0 + D_KV_HEADS * LANES


def _proj_odd_kernel(x_ref, g_ref, wm_ref, cqn_ref, wq_ref, ckvn_ref, wkv_ref, dqn_ref, dkn_ref,
                     ccq_ref, scq_ref, cck_ref, sck_ref, cdq_ref, sdq_ref, cdk_ref, sdk_ref, loc_ref, lod_ref,
                     qc_ref, kc_ref, vc_ref, qd_ref, kd_ref, vd_ref, stat_ref, *, steps_per_seq):
    lane = lax.broadcasted_iota(I32, (1, LANES), 1)
    one64 = jnp.where(lane == HEAD_DIM, 1.0, 0.0)
    last_lane = lane == LANES - 1
    half_rope = C_ROPE // 2
    stats = [jnp.zeros((1, LANES), F32), jnp.zeros((1, LANES), F32)]
    ones_mat = jnp.ones((LANES, LANES), BF16)

    def with_norm(val, fill, row, col):
        n2 = jnp.dot((val * val).astype(BF16), ones_mat, preferred_element_type=F32)
        stats[row] = jnp.where(lane == col, jnp.maximum(jnp.max(n2, axis=0, keepdims=True), stats[row]), stats[row])
        return jnp.where(last_lane, fill, val).astype(BF16)

    def head_norm(xg, gn_ref):
        ss = jnp.sum(xg * xg, axis=-1, keepdims=True) * (1.0 / HEAD_DIM)
        return xg * lax.rsqrt(ss + EPS) * gn_ref[...]

    lo_c = loc_ref[...] > 0.0
    lo_d = lod_ref[...] > 0.0
    n_chunks = 2
    rows_per = x_ref.shape[0] // n_chunks
    for c in range(n_chunks):
        rows = slice(c * rows_per, (c + 1) * rows_per)
        tab = lambda *refs: [r[rows, :] for r in refs]
        x = x_ref[rows, :]
        y = (x * _rms_scale(x) * g_ref[...]).astype(BF16)
        pm = jnp.dot(y, wm_ref[...], preferred_element_type=F32)
        cq = pm[:, :C_Q_RANK]
        cqn = (cq * _rms_scale(cq) * cqn_ref[...]).astype(BF16)
        qc = jnp.dot(cqn, wq_ref[...], preferred_element_type=F32)
        ckv = pm[:, C_Q_RANK:C_Q_RANK + C_KV_RANK]
        ckvn = (ckv * _rms_scale(ckv) * ckvn_ref[...]).astype(BF16)
        kv = jnp.dot(ckvn, wkv_ref[...], preferred_element_type=F32)
        kr = _rope3(pm[:, C_Q_RANK + C_KV_RANK:_SLAB_Q0], *tab(cck_ref, sck_ref), lo_c, half_rope)
        t_cq = tab(ccq_ref, scq_ref) + [lo_c]
        for h in range(C_HEADS):
            sl = slice(h * LANES, (h + 1) * LANES)
            qc_ref[rows, sl] = with_norm(_rope3(qc[:, sl], *t_cq, half_rope), 1.0, 1, h)
            kc_ref[rows, sl] = with_norm(kv[:, sl] + kr, -1.0, 0, h)
            vc_ref[rows, sl] = (kv[:, C_HEADS * LANES + h * LANES:C_HEADS * LANES + (h + 1) * LANES]
                                + one64).astype(BF16)
        t_dq = tab(cdq_ref, sdq_ref) + [lo_d]
        t_dk = tab(cdk_ref, sdk_ref) + [lo_d]
        for g in range(D_HEADS):
            xg = pm[:, _SLAB_Q0 + g * LANES:_SLAB_Q0 + (g + 1) * LANES]
            qd_ref[rows, g * LANES:(g + 1) * LANES] = with_norm(
                _rope3(head_norm(xg, dqn_ref), *t_dq, HEAD_DIM // 4), 1.0, 1, C_HEADS + g)
        for g in range(D_KV_HEADS):
            sl = slice(g * LANES, (g + 1) * LANES)
            xg = pm[:, _SLAB_K0 + g * LANES:_SLAB_K0 + (g + 1) * LANES]
            kd_ref[rows, sl] = with_norm(_rope3(head_norm(xg, dkn_ref), *t_dk, HEAD_DIM // 4), -1.0, 0, C_HEADS + g)
            vd_ref[rows, sl] = (pm[:, _SLAB_V0 + g * LANES:_SLAB_V0 + (g + 1) * LANES] + one64).astype(BF16)

    new = jnp.concatenate(stats + [jnp.zeros((6, LANES), F32)], axis=0)

    @pl.when(pl.program_id(0) % steps_per_seq == 0)
    def _():
        stat_ref[...] = new

    @pl.when(pl.program_id(0) % steps_per_seq != 0)
    def _():
        stat_ref[...] = jnp.maximum(stat_ref[...], new)


def _slabs(w, n_heads, width, lane_off=0):
    k = w.shape[0]
    w3 = w.reshape(k, n_heads, width)
    w3 = jnp.pad(w3, ((0, 0), (0, 0), (lane_off, LANES - width - lane_off)))
    return w3.reshape(k, n_heads * LANES)


def _axial_tables(row, col, scale):
    half = HEAD_DIM // 2
    cr, sr, lor = _rope_tables(row, D_THETA, half, 0, LANES, scale)
    cc, sc, loc = _rope_tables(col, D_THETA, half, half, LANES, scale)
    lane = jnp.arange(LANES)[None, :]
    return jnp.where(lane < half, cr, cc), sr + sc, lor + loc


def _proj_odd(x2d, seq, g_mix, w_in, cq_norm, w_cq_up, ckv_norm, w_ckv_up, dq_norm, dk_norm, tm=512):
    t = x2d.shape[0]
    nblk = seq // tm
    o1 = C_Q_RANK
    o2 = o1 + C_KV_RANK
    o3 = o2 + C_ROPE
    o4 = o3 + D_HEADS * HEAD_DIM
    o5 = o4 + D_KV_HEADS * HEAD_DIM
    wm = jnp.concatenate([
        w_in[:, :o2],
        _slabs(w_in[:, o2:o3], 1, C_ROPE, C_NOPE),
        _slabs(w_in[:, o3:o4], D_HEADS, HEAD_DIM),
        _slabs(w_in[:, o4:o5], D_KV_HEADS, HEAD_DIM),
        _slabs(w_in[:, o5:], D_KV_HEADS, HEAD_DIM),
    ], axis=1).astype(BF16)
    assert wm.shape[1] == _ODD_COLS
    wq = _slabs(w_cq_up, C_HEADS, C_NOPE + C_ROPE).astype(BF16)
    kv3 = w_ckv_up.reshape(C_KV_RANK, C_HEADS, 2 * HEAD_DIM)
    wkv = jnp.concatenate([
        _slabs(kv3[:, :, :C_NOPE].reshape(C_KV_RANK, -1), C_HEADS, C_NOPE),
        _slabs(kv3[:, :, C_NOPE:].reshape(C_KV_RANK, -1), C_HEADS, HEAD_DIM),
    ], axis=1).astype(BF16)
    pad64 = lambda g: jnp.pad(g, (0, LANES - HEAD_DIM)).reshape(1, LANES)

    pos = jnp.arange(seq, dtype=I32)
    row_pos = pos // GRID_W
    col_pos = pos % GRID_W
    c_scale = (C_NOPE + C_ROPE) ** -0.5 * LOG2E
    d_scale = HEAD_DIM ** -0.5 * LOG2E
    ccq, scq, lo_c = _rope_tables(pos, ROPE_THETA, C_ROPE, C_NOPE, LANES, c_scale)
    cck, sck, _ = _rope_tables(pos, ROPE_THETA, C_ROPE, C_NOPE, LANES, 1.0)
    cdq, sdq, lo_d = _axial_tables(row_pos, col_pos, d_scale)
    cdk, sdk, _ = _axial_tables(row_pos, col_pos, 1.0)
    tabs = (ccq, scq, cck, sck, cdq, sdq, cdk, sdk)

    row = lambda i: (i, 0)
    full = lambda i: (0, 0)
    tspec = pl.BlockSpec((tm, LANES), lambda i: (i % nblk, 0))
    wide = C_HEADS * LANES
    kvw = D_KV_HEADS * LANES
    return pl.pallas_call(
        functools.partial(_proj_odd_kernel, steps_per_seq=nblk),
        grid=(t // tm,),
        in_specs=[
            pl.BlockSpec((tm, D_MODEL), row),
            pl.BlockSpec((1, D_MODEL), full),
            pl.BlockSpec(wm.shape, full),
            pl.BlockSpec((1, C_Q_RANK), full),
            pl.BlockSpec(wq.shape, full),
            pl.BlockSpec((1, C_KV_RANK), full),
            pl.BlockSpec(wkv.shape, full),
            pl.BlockSpec((1, LANES), full),
            pl.BlockSpec((1, LANES), full),
        ] + [tspec] * 8 + [pl.BlockSpec((1, LANES), full)] * 2,
        out_specs=[
            pl.BlockSpec((tm, wide), row), pl.BlockSpec((tm, wide), row), pl.BlockSpec((tm, wide), row),
            pl.BlockSpec((tm, wide), row), pl.BlockSpec((tm, kvw), row), pl.BlockSpec((tm, kvw), row),
            pl.BlockSpec((None, 8, LANES), lambda i: (i // nblk, 0, 0)),
        ],
        out_shape=[
            jax.ShapeDtypeStruct((t, wide), BF16), jax.ShapeDtypeStruct((t, wide), BF16),
            jax.ShapeDtypeStruct((t, wide), BF16), jax.ShapeDtypeStruct((t, wide), BF16),
            jax.ShapeDtypeStruct((t, kvw), BF16), jax.ShapeDtypeStruct((t, kvw), BF16),
            jax.ShapeDtypeStruct((t // seq, 8, LANES), F32),
        ],
        compiler_params=_cparams(("arbitrary",)),
        name="proj_odd",
    )(x2d, g_mix.reshape(1, D_MODEL), wm, cq_norm.reshape(1, -1), wq, ckv_norm.reshape(1, -1), wkv,
      pad64(dq_norm), pad64(dk_norm), *tabs, lo_c, lo_d)


def _flash_kernel(q_ref, k_ref, v_ref, o_ref, qs_sc, m_sc, acc_sc, *, group, tq, tk):
    ki = pl.program_id(3)

    @pl.when(ki == 0)
    def _():
        for g in range(group):
            qs_sc[g * tq:(g + 1) * tq, :] = q_ref[:, g * LANES:(g + 1) * LANES]
        m_sc[...] = jnp.full(m_sc.shape, NEG, F32)
        acc_sc[...] = jnp.zeros(acc_sc.shape, F32)

    s = lax.dot_general(qs_sc[...], k_ref[...], _NT, preferred_element_type=F32)
    m_prev = m_sc[...]
    m_new = jnp.maximum(m_prev, jnp.max(s, axis=1, keepdims=True))
    alpha = jnp.exp2(m_prev - m_new)
    p = jnp.exp2(s - jnp.tile(m_new, (1, tk // LANES)))
    acc_sc[...] = alpha * acc_sc[...] + jnp.dot(p.astype(BF16), v_ref[...], preferred_element_type=F32)
    m_sc[...] = m_new

    @pl.when(ki == pl.num_programs(3) - 1)
    def _():
        acc = acc_sc[...]
        o = acc / acc[:, HEAD_DIM:HEAD_DIM + 1]
        for g in range(group):
            o_ref[:, g * LANES:(g + 1) * LANES] = o[g * tq:(g + 1) * tq].astype(BF16)


_FLASH_CHUNK = 2048
_V_ROWS = 80


def _flash_bounded_kernel(bound_ref, q_ref, k_ref, v_ref, o_ref, qs_sc, acc_sc, *, group, tq):
    ki = pl.program_id(3)
    head = pl.program_id(0) * pl.num_programs(1) + pl.program_id(1)

    @pl.when(ki == 0)
    def _():
        fix = jnp.where(lax.broadcasted_iota(I32, (1, LANES), 1) == LANES - 1, bound_ref[head], 1.0)
        for g in range(group):
            qs_sc[g * tq:(g + 1) * tq, :] = (q_ref[:, g * LANES:(g + 1) * LANES].astype(F32) * fix).astype(BF16)
        acc_sc[...] = jnp.zeros(acc_sc.shape, F32)

    chunk = min(_FLASH_CHUNK, k_ref.shape[0])
    n_chunks = k_ref.shape[0] // chunk
    qs = qs_sc[...]

    def scores(c):
        return lax.dot_general(k_ref[c * chunk:(c + 1) * chunk, :], qs, _NT, preferred_element_type=F32)

    def values(c, s_t):
        return lax.dot_general(v_ref[c * chunk:(c + 1) * chunk, :_V_ROWS], jnp.exp2(s_t).astype(BF16),
                               (((0,), (0,)), ((), ())), preferred_element_type=F32)

    acc = acc_sc[...]
    s_prev = scores(0)
    for c in range(1, n_chunks):
        s_next = scores(c)
        acc = acc + values(c - 1, s_prev)
        s_prev = s_next
    acc_sc[...] = acc + values(n_chunks - 1, s_prev)

    @pl.when(ki == pl.num_programs(3) - 1)
    def _():
        acc = acc_sc[...]
        o_t = acc / acc[HEAD_DIM:HEAD_DIM + 1, :]
        o = jnp.concatenate([o_t, jnp.zeros((LANES - _V_ROWS, o_t.shape[1]), F32)], axis=0).T
        for g in range(group):
            o_ref[:, g * LANES:(g + 1) * LANES] = o[g * tq:(g + 1) * tq].astype(BF16)


def _flash(q, k, v, group, nk, bounded, rows=1024, tk=512, tk_bounded=8192):
    b, s, qw = q.shape
    hk = k.shape[2] // LANES
    tq = rows // group
    tk = min(tk, s)
    tkb = min(tk_bounded, s)
    out_shape = jax.ShapeDtypeStruct((b, s, qw), BF16)
    sem = ("parallel", "parallel", "parallel", "arbitrary")

    def running_max(q, k, v, nk):
        qspec = pl.BlockSpec((None, tq, group * LANES), lambda bi, hi, qi, ki: (bi, qi, hi))
        kspec = pl.BlockSpec((None, tk, LANES), lambda bi, hi, qi, ki: (bi, ki, hi))
        return pl.pallas_call(
            functools.partial(_flash_kernel, group=group, tq=tq, tk=tk),
            grid=(b, hk, s // tq, s // tk),
            in_specs=[qspec, kspec, kspec],
            out_specs=qspec,
            out_shape=out_shape,
            scratch_shapes=[pltpu.VMEM((rows, LANES), BF16), pltpu.VMEM((rows, LANES), F32),
                            pltpu.VMEM((rows, LANES), F32)],
            compiler_params=_cparams(sem),
            name="flash",
        )(q, k, v)

    def bound(q, k, v, nk):
        qspec = pl.BlockSpec((None, tq, group * LANES), lambda bi, hi, qi, ki, nkr: (bi, qi, hi))
        kspec = pl.BlockSpec((None, tkb, LANES), lambda bi, hi, qi, ki, nkr: (bi, ki, hi))
        return pl.pallas_call(
            functools.partial(_flash_bounded_kernel, group=group, tq=tq),
            grid_spec=pltpu.PrefetchScalarGridSpec(
                num_scalar_prefetch=1,
                grid=(b, hk, s // tq, s // tkb),
                in_specs=[qspec, kspec, kspec],
                out_specs=qspec,
                scratch_shapes=[pltpu.VMEM((rows, LANES), BF16), pltpu.VMEM((_V_ROWS, rows), F32)],
            ),
            out_shape=out_shape,
            compiler_params=_cparams(sem),
            name="flash_bounded",
        )(nk.reshape(-1), q, k, v)

    return lax.cond(bounded, bound, running_max, q, k, v, nk)


def _moe(x1, h2, aff, batch, seq, w_gate, w_up, w_down, layer, g_final, final):
    idx, gates, spos, cb = _route(aff.reshape(-1, LANES), batch, seq)
    y = _ffn(idx, gates, h2, w_gate, w_up, w_down, layer, seq)
    return _combine(x1, spos.reshape(-1, N_EXPERTS, LANES), cb, y, g_final, final, seq)


def kernel(x, norm_mix, norm_ffn, even_w_in, even_gmlp_norm, even_w_spatial, even_b_spatial, even_w_out,
           odd_w_in, odd_cq_norm, odd_w_cq_up, odd_ckv_norm, odd_w_ckv_up, odd_dq_norm, odd_dk_norm, odd_w_out,
           moe_w_router, moe_w_gate, moe_w_up, moe_w_down, final_norm):
    b, s, d = x.shape
    depth = norm_mix.shape[0]
    x2d = x.reshape(b * s, d)
    for i in range(depth):
        j = i // 2
        last = i == depth - 1
        if i % 2 == 0:
            q, k, v, go = _proj_even(x2d, s, norm_mix[i], even_w_in[j], even_gmlp_norm[j], even_w_spatial[j],
                                     even_b_spatial[j])
            a = _dilated(q.reshape(b, s, A_WIDTH), k.reshape(b, s, A_WIDTH), v.reshape(b, s, A_WIDTH))
            x1, h2, aff = _outproj(x2d, a.reshape(b * s, A_WIDTH), go, even_w_out[j][:A_WIDTH],
                                   even_w_out[j][A_WIDTH:], norm_ffn[i], moe_w_router[i])
        else:
            qc, kc, vc, qd, kd, vd, stat = _proj_odd(x2d, s, norm_mix[i], odd_w_in[j], odd_cq_norm[j],
                                                     odd_w_cq_up[j], odd_ckv_norm[j], odd_w_ckv_up[j],
                                                     odd_dq_norm[j], odd_dk_norm[j])
            grp = D_HEADS // D_KV_HEADS
            k2_c, k2_d = stat[:, 0, :C_HEADS], stat[:, 0, C_HEADS:C_HEADS + D_KV_HEADS]
            q2_c = stat[:, 1, :C_HEADS]
            q2_d = jnp.max(stat[:, 1, C_HEADS:C_HEADS + D_HEADS].reshape(b, D_KV_HEADS, grp), axis=-1)
            bound_c = jnp.sqrt(q2_c * k2_c) * _NORM_MARGIN ** 2
            bound_d = jnp.sqrt(q2_d * k2_d) * _NORM_MARGIN ** 2
            bounded = jnp.maximum(jnp.max(bound_c), jnp.max(bound_d)) <= _MAX_SCORE_BOUND
            r3 = lambda z: z.reshape(b, s, -1)
            oc = _flash(r3(qc), r3(kc), r3(vc), 1, bound_c, bounded)
            od = _flash(r3(qd), r3(kd), r3(vd), grp, bound_d, bounded)
            cw = C_HEADS * HEAD_DIM
            x1, h2, aff = _outproj(x2d, oc.reshape(b * s, -1), od.reshape(b * s, -1),
                                   _slabs(odd_w_out[j][:cw].T, C_HEADS, HEAD_DIM).T,
                                   _slabs(odd_w_out[j][cw:].T, D_HEADS, HEAD_DIM).T,
                                   norm_ffn[i], moe_w_router[i])
        x2d = _moe(x1, h2, aff, b, s, moe_w_gate, moe_w_up, moe_w_down, i, final_norm, last)
    return x2d.reshape(b, s, d)
```

```python
import functools
import math

import jax
import jax.numpy as jnp
from jax import lax
from jax.experimental import pallas as pl
from jax.experimental.pallas import tpu as pltpu

F32 = jnp.float32
BF16 = jnp.bfloat16
I32 = jnp.int32

EPS = 1e-6
NEG = -1e30
LOG2E = 1.4426950408889634

D_MODEL = 1024
HEAD_DIM = 64
ROPE_THETA = 500000.0
ROT_DIM = 16
GRID_W = 64
A_HEADS = 12
A_WIDTH = 768
A_DILATIONS = (1, 4, 16)
A_RADIUS = 64
B_WIDTH = 256
B_GROUPS = 4
B_CHUNK = 128
C_HEADS = 8
C_Q_RANK = 256
C_KV_RANK = 128
C_NOPE = 64
C_ROPE = 32
D_HEADS = 8
D_KV_HEADS = 2
D_THETA = 10000.0
N_EXPERTS = 16
EC_FACTOR = 2
EXPERT_FF = 512

_MIN_NORMAL = 1.1754944e-38
_GEOMETRIC_STEPS = 40
_ARITHMETIC_STEPS = 24
_NORM_MARGIN = 1.01
_MAX_SCORE_BOUND = 55.0

LANES = 128
VMEM_LIMIT = 48 * 1024 * 1024
_DILATED_VMEM_LIMIT = 56 * 1024 * 1024

_NT = (((1,), (1,)), ((), ()))


def _cparams(sem):
    return pltpu.CompilerParams(dimension_semantics=sem, vmem_limit_bytes=VMEM_LIMIT)


def _rms_scale(x):
    return lax.rsqrt(jnp.mean(x * x, axis=-1, keepdims=True) + EPS)


def _rope3(a, c, s, lo, shift):
    return a * c + jnp.where(lo, pltpu.roll(a, LANES - shift, 1), pltpu.roll(a, shift, 1)) * s


def _split_dot(x, w_bf16):
    hi = x.astype(BF16)
    lo = (x - hi.astype(F32)).astype(BF16)
    return (jnp.dot(hi, w_bf16, preferred_element_type=F32)
            + jnp.dot(lo, w_bf16, preferred_element_type=F32))


def _proj_even_kernel(x_ref, g_ref, w_ref, cq_ref, sq_ref, ck_ref, sk_ref, lo_ref,
                      gn_ref, gmat_ref, ws_ref, bs_ref,
                      q_ref, k_ref, v_ref, go_ref):
    x = x_ref[...]
    y = (x * _rms_scale(x) * g_ref[...]).astype(BF16)
    tm = x.shape[0]

    aq = jnp.dot(y, w_ref[:, 0:A_WIDTH], preferred_element_type=F32)
    lo = lo_ref[...] > 0.0
    tq = (cq_ref[...], sq_ref[...], lo)
    for j in range(A_WIDTH // LANES):
        sl = slice(j * LANES, (j + 1) * LANES)
        q_ref[:, sl] = _rope3(aq[:, sl], *tq, ROT_DIM // 2)
    ak = jnp.dot(y, w_ref[:, A_WIDTH:2 * A_WIDTH], preferred_element_type=F32)
    tk = (ck_ref[...], sk_ref[...], lo)
    for j in range(A_WIDTH // LANES):
        sl = slice(j * LANES, (j + 1) * LANES)
        k_ref[:, sl] = _rope3(ak[:, sl], *tk, ROT_DIM // 2)
    v_ref[...] = jnp.dot(y, w_ref[:, 2 * A_WIDTH:3 * A_WIDTH], preferred_element_type=F32)

    z = jnp.dot(y, w_ref[:, 3 * A_WIDTH:3 * A_WIDTH + 2 * B_WIDTH], preferred_element_type=F32)
    ge = jax.nn.gelu(z)
    u = ge[:, :B_WIDTH]
    vv = ge[:, B_WIDTH:]
    ss = _split_dot(vv * vv, gmat_ref[...])
    vn = (vv * lax.rsqrt(ss + EPS) * gn_ref[...]).astype(BF16)
    grp = lax.broadcasted_iota(I32, (B_CHUNK, B_WIDTH), 1) // (B_WIDTH // B_GROUPS)
    for c in range(tm // B_CHUNK):
        rows = slice(c * B_CHUNK, (c + 1) * B_CHUNK)
        vc = vn[rows]
        mg = [jnp.dot(ws_ref[g], vc, preferred_element_type=F32) for g in range(B_GROUPS)]
        mixed = jnp.where(grp == 0, mg[0], jnp.where(grp == 1, mg[1], jnp.where(grp == 2, mg[2], mg[3])))
        go_ref[rows, :] = (u[rows] * (mixed + bs_ref[...])).astype(BF16)


def _rope_tables(pos, theta, r, lane_off, period, scale):
    half = r // 2
    inv = jnp.power(jnp.float32(theta), -jnp.arange(half, dtype=F32) * (2.0 / r))
    ang = pos.astype(F32)[:, None] * inv[None, :]
    cos, sin = jnp.cos(ang), jnp.sin(ang)
    o = (jnp.arange(LANES) % period) - lane_off
    in_lo = (o >= 0) & (o < half)
    in_hi = (o >= half) & (o < r)
    idx = jnp.clip(jnp.where(in_hi, o - half, o), 0, half - 1)
    c = jnp.where((in_lo | in_hi)[None, :], cos[:, idx], 1.0)
    s = jnp.where(in_lo[None, :], -sin[:, idx], jnp.where(in_hi[None, :], sin[:, idx], 0.0))
    return c * scale, s * scale, jnp.where(in_lo, 1.0, 0.0).reshape(1, LANES)


def _proj_even(x2d, seq, g_mix, w_in, gmlp_norm, w_s, b_s, tm=512):
    t = x2d.shape[0]
    nblk = seq // tm
    pos = jnp.arange(seq, dtype=I32)
    qscale = HEAD_DIM ** -0.5 * LOG2E
    cq, sq, lo = _rope_tables(pos, ROPE_THETA, ROT_DIM, 0, HEAD_DIM, qscale)
    ck, sk, _ = _rope_tables(pos, ROPE_THETA, ROT_DIM, 0, HEAD_DIM, 1.0)
    gdim = B_WIDTH // B_GROUPS
    gid = jnp.arange(B_WIDTH) // gdim
    gmat = jnp.where(gid[:, None] == gid[None, :], 1.0 / gdim, 0.0).astype(BF16)
    bias = jnp.repeat(b_s.T, gdim, axis=1)
    row = lambda i: (i, 0)
    tab = lambda i: (i % nblk, 0)
    full = lambda i: (0, 0)
    tspec = pl.BlockSpec((tm, LANES), tab)
    return pl.pallas_call(
        _proj_even_kernel,
        grid=(t // tm,),
        in_specs=[
            pl.BlockSpec((tm, D_MODEL), row),
            pl.BlockSpec((1, D_MODEL), full),
            pl.BlockSpec(w_in.shape, full),
            tspec, tspec, tspec, tspec,
            pl.BlockSpec((1, LANES), full),
            pl.BlockSpec((1, B_WIDTH), full),
            pl.BlockSpec((B_WIDTH, B_WIDTH), full),
            pl.BlockSpec((B_GROUPS, B_CHUNK, B_CHUNK), lambda i: (0, 0, 0)),
            pl.BlockSpec((B_CHUNK, B_WIDTH), full),
        ],
        out_specs=[
            pl.BlockSpec((tm, A_WIDTH), row),
            pl.BlockSpec((tm, A_WIDTH), row),
            pl.BlockSpec((tm, A_WIDTH), row),
            pl.BlockSpec((tm, B_WIDTH), row),
        ],
        out_shape=[
            jax.ShapeDtypeStruct((t, A_WIDTH), F32),
            jax.ShapeDtypeStruct((t, A_WIDTH), F32),
            jax.ShapeDtypeStruct((t, A_WIDTH), F32),
            jax.ShapeDtypeStruct((t, B_WIDTH), BF16),
        ],
        compiler_params=_cparams(("parallel",)),
        name="proj_even",
    )(x2d, g_mix.reshape(1, D_MODEL), w_in.astype(BF16), cq, sq, ck, sk, lo,
      gmlp_norm.reshape(1, B_WIDTH), gmat, w_s.astype(BF16), bias)


_TQ = 128
_TK = _TQ + 2 * A_RADIUS
_NORM_ROWS = 1024


def _dilated_kernel(q_ref, k_ref, v_ref, o_ref, m_sc, l_sc, bias_sc, *, seq):
    half0 = lax.broadcasted_iota(I32, (1, LANES), 1) < HEAD_DIM
    same_head = jnp.where(lax.broadcasted_iota(I32, (LANES, LANES), 0) // HEAD_DIM
                          == lax.broadcasted_iota(I32, (LANES, LANES), 1) // HEAD_DIM,
                          1.0, 0.0).astype(BF16)
    n_pat = len(A_DILATIONS)

    def max_head_sq(ref):
        def body(c, mx):
            x = ref[pl.ds(pl.multiple_of(c * _NORM_ROWS, _NORM_ROWS), _NORM_ROWS), :]
            n2 = jnp.dot((x * x).astype(BF16), same_head, preferred_element_type=F32)
            return jnp.maximum(mx, jnp.max(n2, axis=0, keepdims=True))
        return lax.fori_loop(0, seq // _NORM_ROWS, body, jnp.zeros((1, LANES), F32))

    both = jnp.sqrt(max_head_sq(q_ref) * max_head_sq(k_ref)) * (_NORM_MARGIN * _NORM_MARGIN)
    both = jnp.broadcast_to(both, (8, LANES))
    other = pltpu.roll(both, HEAD_DIM, 1)
    bound = [jnp.where(half0, both, other)[:1], jnp.where(half0, other, both)[:1]]
    worst = jnp.max(both)

    diff = lax.broadcasted_iota(I32, (_TQ, _TK), 1) - lax.broadcasted_iota(I32, (_TQ, _TK), 0)
    for case in range(3):
        band = jnp.where(jnp.abs(diff - case * A_RADIUS) <= A_RADIUS, 0.0, NEG)
        bias_sc[6 + case] = band
        for h in range(2):
            bias_sc[3 * h + case] = band - jnp.tile(bound[h], (1, _TK // LANES))
    one_bf16 = jnp.ones((), BF16)

    def run(bounded):
        for pi, d in enumerate(A_DILATIONS):
            cls_len = seq // d
            tpc = cls_len // _TQ

            def tile(qrows, kb, vb, case, pi=pi):
                q = q_ref[qrows, :]
                parts = []
                for h in range(2):
                    qh = jnp.where(half0 if h == 0 else jnp.logical_not(half0), q, 0.0).astype(BF16)
                    s = lax.dot_general(qh, kb, _NT, preferred_element_type=F32)
                    if bounded:
                        mt = None
                        p = jnp.exp2(s + bias_sc[3 * h + case]).astype(BF16)
                    else:
                        s = s + bias_sc[6 + case]
                        mt = jnp.max(s, axis=-1, keepdims=True)
                        p = jnp.exp2(s - mt).astype(BF16)
                    vh = jnp.where(half0 if h == 0 else jnp.logical_not(half0), vb, one_bf16)
                    parts.append((mt, jnp.dot(p, vh, preferred_element_type=F32)))
                ot = jnp.where(half0, parts[0][1], parts[1][1])
                lt = pltpu.roll(jnp.where(half0, parts[1][1], parts[0][1]), HEAD_DIM, 1)
                if bounded:
                    if pi > 0:
                        lt = l_sc[qrows, :] + lt
                        ot = o_ref[qrows, :] + ot
                else:
                    mt = jnp.where(half0, parts[0][0], parts[1][0])
                    if pi > 0:
                        mp = m_sc[qrows, :]
                        mn = jnp.maximum(mp, mt)
                        a = jnp.exp2(mp - mn)
                        b = jnp.exp2(mt - mn)
                        lt = a * l_sc[qrows, :] + b * lt
                        ot = a * o_ref[qrows, :] + b * ot
                        mt = mn
                    if pi < n_pat - 1:
                        m_sc[qrows, :] = mt
                if pi == n_pat - 1:
                    o_ref[qrows, :] = ot / lt
                else:
                    l_sc[qrows, :] = lt
                    o_ref[qrows, :] = ot

            def window(l0, cls_len=cls_len):
                if isinstance(l0, int):
                    kst = min(max(l0 - A_RADIUS, 0), cls_len - _TK)
                else:
                    kst = jnp.clip(l0 - A_RADIUS, 0, cls_len - _TK)
                return kst, (l0 - kst) // A_RADIUS

            if d < 8:

                def body(j, carry, d=d, tpc=tpc):
                    i = j // tpc
                    l0 = (j % tpc) * _TQ
                    kst, case = window(l0)
                    if d == 1:
                        qrows = pl.ds(pl.multiple_of(l0, _TQ), _TQ)
                        krows = pl.ds(pl.multiple_of(kst, A_RADIUS), _TK)
                    else:
                        qrows = pl.ds(l0 * d + i, _TQ, stride=d)
                        krows = pl.ds(kst * d + i, _TK, stride=d)
                    tile(qrows, k_ref[krows, :].astype(BF16), v_ref[krows, :].astype(BF16), case)
                    return carry

                lax.fori_loop(0, seq // _TQ, body, 0, unroll=16)
            else:

                def body(i, carry, d=d, cls_len=cls_len, tpc=tpc):
                    cls = pl.ds(i, cls_len, stride=d)
                    kc = k_ref[cls, :].astype(BF16)
                    vc = v_ref[cls, :].astype(BF16)
                    for n in range(tpc):
                        kst, case = window(n * _TQ)
                        tile(pl.ds(n * _TQ * d + i, _TQ, stride=d), kc[kst:kst + _TK], vc[kst:kst + _TK], case)
                    return carry

                lax.fori_loop(0, d, body, 0, unroll=4)

    @pl.when(worst <= _MAX_SCORE_BOUND)
    def _():
        run(True)

    @pl.when(jnp.logical_not(worst <= _MAX_SCORE_BOUND))
    def _():
        run(False)


def _dilated(q, k, v):
    b, s, w = q.shape
    spec = pl.BlockSpec((None, s, LANES), lambda bi, hi: (bi, 0, hi))
    return pl.pallas_call(
        functools.partial(_dilated_kernel, seq=s),
        grid=(b, w // LANES),
        in_specs=[spec, spec, spec],
        out_specs=spec,
        out_shape=jax.ShapeDtypeStruct((b, s, w), F32),
        scratch_shapes=[pltpu.VMEM((s, LANES), F32), pltpu.VMEM((s, LANES), F32),
                        pltpu.VMEM((9, _TQ, _TK), F32)],
        compiler_params=pltpu.CompilerParams(dimension_semantics=("parallel", "parallel"),
                                             vmem_limit_bytes=_DILATED_VMEM_LIMIT),
        name="dilated",
    )(q, k, v)


def _outproj_kernel(x_ref, a_ref, b_ref, wa_ref, wb_ref, gf_ref, wr_ref, x1_ref, h2_ref, aff_ref):
    x1 = (x_ref[...]
          + jnp.dot(a_ref[...].astype(BF16), wa_ref[...], preferred_element_type=F32)
          + jnp.dot(b_ref[...].astype(BF16), wb_ref[...], preferred_element_type=F32))
    x1_ref[...] = x1
    h2 = x1 * _rms_scale(x1) * gf_ref[...]
    n_tiles = D_MODEL // LANES
    for j in range(n_tiles):
        h2_ref[pl.ds(j, h2.shape[0], stride=n_tiles), :] = h2[:, j * LANES:(j + 1) * LANES]
    hi = h2.astype(BF16)
    lo = (h2 - hi.astype(F32)).astype(BF16)
    both = jnp.dot(hi, wr_ref[...], preferred_element_type=F32)
    lg = (both[:, :LANES] + both[:, LANES:]
          + jnp.dot(lo, wr_ref[:, :LANES], preferred_element_type=F32))
    valid = lax.broadcasted_iota(I32, lg.shape, 1) < N_EXPERTS
    lg = jnp.where(valid, lg, NEG)
    e = jnp.exp(lg - jnp.max(lg, axis=-1, keepdims=True))
    aff = e / jnp.sum(e, axis=-1, keepdims=True)
    aff_t = aff.T
    for j in range(aff.shape[0] // LANES):
        aff_ref[j] = aff_t[:N_EXPERTS, j * LANES:(j + 1) * LANES]


def _outproj(x2d, a, b, wa, wb, g_ffn, w_router, tm=512):
    t = x2d.shape[0]
    wr = jnp.pad(w_router, ((0, 0), (0, LANES - N_EXPERTS)))
    wr_hi = wr.astype(BF16)
    wr_lo = (wr - wr_hi.astype(F32)).astype(BF16)
    wr2 = jnp.concatenate([wr_hi, wr_lo], axis=1)
    row = lambda i: (i, 0)
    full = lambda i: (0, 0)
    return pl.pallas_call(
        _outproj_kernel,
        grid=(t // tm,),
        in_specs=[
            pl.BlockSpec((tm, D_MODEL), row),
            pl.BlockSpec((tm, a.shape[1]), row),
            pl.BlockSpec((tm, b.shape[1]), row),
            pl.BlockSpec(wa.shape, full),
            pl.BlockSpec(wb.shape, full),
            pl.BlockSpec((1, D_MODEL), full),
            pl.BlockSpec((D_MODEL, 2 * LANES), full),
        ],
        out_specs=[
            pl.BlockSpec((tm, D_MODEL), row),
            pl.BlockSpec((tm * (D_MODEL // LANES), LANES), row),
            pl.BlockSpec((tm // LANES, N_EXPERTS, LANES), lambda i: (i, 0, 0)),
        ],
        out_shape=[
            jax.ShapeDtypeStruct((t, D_MODEL), F32),
            jax.ShapeDtypeStruct((t * (D_MODEL // LANES), LANES), F32),
            jax.ShapeDtypeStruct((t // LANES, N_EXPERTS, LANES), F32),
        ],
        compiler_params=_cparams(("parallel",)),
        name="outproj",
    )(x2d, a, b, wa.astype(BF16), wb.astype(BF16), g_ffn.reshape(1, D_MODEL), wr2)


def _route_kernel(aff_ref, idx_ref, gate_ref, spos_ref, cb_ref, thr_sc, need_sc, *, cap):
    nblk = aff_ref.shape[0] // N_EXPERTS
    aff3 = aff_ref[...].reshape(nblk, N_EXPERTS, LANES)

    def count(pred):
        return jnp.sum(jnp.sum(jnp.where(pred, 1.0, 0.0), axis=0), axis=1, keepdims=True)

    def two_halvings(bracket, mid):
        lo, hi = bracket
        m2 = mid(lo, hi)
        m1, m3 = mid(lo, m2), mid(m2, hi)
        e1, e2, e3 = (count(aff3 >= m[None]) >= cap for m in (m1, m2, m3))
        return (jnp.where(e2, jnp.where(e3, m3, m2), jnp.where(e1, m1, lo)),
                jnp.where(e2, jnp.where(e3, hi, m3), jnp.where(e1, m2, m1)))

    def geometric(it, bracket):
        return two_halvings(bracket, lambda lo, hi: jnp.clip(jnp.sqrt(lo) * jnp.sqrt(hi), lo, hi))

    def arithmetic(it, bracket):
        return two_halvings(bracket, lambda lo, hi: lo + (hi - lo) * 0.5)

    bracket = (jnp.full((N_EXPERTS, 1), _MIN_NORMAL, F32), jnp.full((N_EXPERTS, 1), 2.0, F32))
    bracket = lax.fori_loop(0, _GEOMETRIC_STEPS // 2, geometric, bracket)
    lo, hi = lax.fori_loop(0, _ARITHMETIC_STEPS // 2, arithmetic, bracket)
    below = jnp.where(aff3 < hi[None], aff3, -1.0)
    thr = jnp.max(jnp.max(below, axis=0), axis=1, keepdims=True)
    need = cap - count(aff3 > thr[None])
    thr_sc[...] = jnp.broadcast_to(thr, (N_EXPERTS, LANES))
    need_sc[...] = jnp.broadcast_to(need, (N_EXPERTS, LANES))

    ri = lax.broadcasted_iota(I32, (LANES, LANES), 0)
    ci = lax.broadcasted_iota(I32, (LANES, LANES), 1)
    upper = jnp.where(ri <= ci, 1.0, 0.0).astype(BF16)
    lower = jnp.where(ci <= ri, 1.0, 0.0).astype(BF16)
    eye = jnp.where(ri == ci, 1.0, 0.0).astype(BF16)
    ones = jnp.ones((LANES, LANES), BF16)
    bi = lax.broadcasted_iota(I32, (nblk, nblk), 0)
    bj = lax.broadcasted_iota(I32, (nblk, nblk), 1)
    strict = jnp.where(bj < bi, 1.0, 0.0).astype(BF16)
    before = jnp.where(bi < bj, 1.0, 0.0).astype(BF16)
    mean_rows = jnp.full((8, LANES), 1.0 / LANES, BF16)
    c_row = lax.broadcasted_iota(I32, (1, cap), 1).astype(F32)
    blk_iota = lax.broadcasted_iota(I32, (nblk, cap), 0).astype(F32)
    t_iota = lax.broadcasted_iota(I32, (LANES, cap), 0).astype(F32)
    rep = cap // LANES

    def cums(mask_bf16):
        lp = jnp.dot(mask_bf16, upper, preferred_element_type=F32)
        bc = jnp.dot(mask_bf16, ones, preferred_element_type=F32)
        bst = jnp.dot(strict, bc.astype(BF16), preferred_element_type=F32)
        return lp, bc, bst

    def per_expert(e, carry):
        a = aff_ref[pl.ds(e, nblk, stride=N_EXPERTS), :]
        thr_e = thr_sc[pl.ds(e, 1), :]
        need_e = need_sc[pl.ds(e, 1), :]
        gt = a > thr_e
        eq = a == thr_e
        eqf = jnp.where(eq, 1.0, 0.0)
        lp_q, _, bst_q = cums(eqf.astype(BF16))
        sel = jnp.logical_or(gt, jnp.logical_and(eq, bst_q + lp_q - eqf < need_e))
        mb = jnp.where(sel, 1.0, 0.0).astype(BF16)
        lp, bc, bst = cums(mb)
        spos_ref[pl.ds(e, nblk, stride=N_EXPERTS), :] = jnp.where(sel, bst + lp - 1.0, -1.0)
        bc_row = lax.dot_general(mean_rows, bc.astype(BF16), _NT, preferred_element_type=F32)
        cb_ref[pl.ds(e, 1), :] = jnp.dot(bc_row.astype(BF16), before, preferred_element_type=F32)[:1].astype(I32)
        bend_w = jnp.tile(bst + bc, (1, rep))
        bst_w = jnp.tile(bst, (1, rep))
        blk_c = jnp.sum(jnp.where(bend_w <= c_row, 1.0, 0.0), axis=0, keepdims=True)
        onehot = blk_iota == blk_c
        bst_c = jnp.sum(jnp.where(onehot, bst_w, 0.0), axis=0, keepdims=True)
        r_c = c_row - bst_c
        ohb = jnp.where(onehot, 1.0, 0.0).astype(BF16)
        lp_t = lax.dot_general(lower, mb, _NT, preferred_element_type=F32)
        lp_c = jnp.dot(lp_t.astype(BF16), ohb, preferred_element_type=F32)
        tl_c = jnp.sum(jnp.where(lp_c <= r_c, 1.0, 0.0), axis=0, keepdims=True)
        idx_ref[pl.ds(e, 1), :] = (blk_c * LANES + tl_c).astype(I32)
        a_hi = a.astype(BF16)
        a_lo = (a - a_hi.astype(F32)).astype(BF16)
        at_hi = lax.dot_general(eye, a_hi, _NT, preferred_element_type=F32).astype(BF16)
        at_lo = lax.dot_general(eye, a_lo, _NT, preferred_element_type=F32).astype(BF16)
        g_c = (jnp.dot(at_hi, ohb, preferred_element_type=F32)
               + jnp.dot(at_lo, ohb, preferred_element_type=F32))
        gate_ref[pl.ds(e, 1), :] = jnp.sum(jnp.where(t_iota == tl_c, g_c, 0.0), axis=0, keepdims=True)
        return carry

    lax.fori_loop(0, N_EXPERTS, per_expert, 0, unroll=2)


def _route(aff2d, batch, seq):
    cap = EC_FACTOR * seq // N_EXPERTS
    nblk = seq // LANES
    rows = nblk * N_EXPERTS
    out_spec = pl.BlockSpec((None, N_EXPERTS, cap), lambda b: (b, 0, 0))
    return pl.pallas_call(
        functools.partial(_route_kernel, cap=cap),
        grid=(batch,),
        in_specs=[pl.BlockSpec((rows, LANES), lambda b: (b, 0))],
        out_specs=[out_spec, out_spec, pl.BlockSpec((rows, LANES), lambda b: (b, 0)),
                   pl.BlockSpec((None, N_EXPERTS, nblk), lambda b: (b, 0, 0))],
        out_shape=[jax.ShapeDtypeStruct((batch, N_EXPERTS, cap), I32),
                   jax.ShapeDtypeStruct((batch, N_EXPERTS, cap), F32),
                   jax.ShapeDtypeStruct((batch * rows, LANES), F32),
                   jax.ShapeDtypeStruct((batch, N_EXPERTS, nblk), I32)],
        scratch_shapes=[pltpu.VMEM((N_EXPERTS, LANES), F32), pltpu.VMEM((N_EXPERTS, LANES), F32)],
        compiler_params=_cparams(("parallel",)),
        name="route",
    )(aff2d)


def _ffn_kernel(idx_ref, nxt_ref, gate_ref, h_hbm, wg32_ref, wu32_ref, wd32_ref, y_ref, buf, sem,
                wg_ref, wu_ref, wd_ref, *, seq, tc, nsub):
    seq_id = pl.program_id(1)
    n_seq = pl.num_programs(1)
    step = pl.program_id(0) * n_seq + seq_id
    last_step = pl.num_programs(0) * n_seq - 1
    base = seq_id * seq
    next_base = jnp.where(seq_id + 1 < n_seq, seq_id + 1, 0) * seq

    @pl.when(seq_id == 0)
    def _():
        wg_ref[...] = wg32_ref[...].astype(BF16)
        wu_ref[...] = wu32_ref[...].astype(BF16)
        wd_ref[...] = wd32_ref[...].astype(BF16)

    n_tiles = D_MODEL // LANES

    def row_copy(ids, row0, j, r, slot):
        tok = pl.multiple_of((row0 + ids[0, 0, j * tc + r]) * n_tiles, n_tiles)
        return pltpu.make_async_copy(h_hbm.at[pl.ds(tok, n_tiles), :],
                                     buf.at[slot, pl.ds(r * n_tiles, n_tiles), :], sem.at[slot])

    def issue(ids, row0, j, slot):
        for r in range(tc):
            row_copy(ids, row0, j, r, slot).start()

    diag = lax.broadcasted_iota(I32, (tc, tc), 0) == lax.broadcasted_iota(I32, (tc, tc), 1)
    ones = jnp.ones((tc, LANES), BF16)

    ahead = 2

    @pl.when(step == 0)
    def _():
        for j in range(ahead):
            issue(idx_ref, base, j, j)

    for j in range(nsub):
        slot = j
        for r in range(tc):
            row_copy(idx_ref, base, j, r, slot).wait()
        xs = jnp.concatenate([buf.at[slot][pl.ds(c, tc, stride=n_tiles), :].astype(BF16) for c in range(n_tiles)],
                             axis=1)
        g = jnp.dot(xs, wg_ref[...], preferred_element_type=F32)
        u = jnp.dot(xs, wu_ref[...], preferred_element_type=F32)
        hm = (jax.nn.silu(g) * u).astype(BF16)
        y = jnp.dot(hm, wd_ref[...], preferred_element_type=F32)
        gr = jnp.broadcast_to(gate_ref[0, :, j * tc:(j + 1) * tc], (tc, tc))
        gcol = _split_dot(jnp.where(diag, gr, 0.0), ones)
        y_ref[j * tc:(j + 1) * tc, :] = (y * jnp.tile(gcol, (1, D_MODEL // LANES))).astype(BF16)
        if j + ahead < nsub:
            issue(idx_ref, base, j + ahead, j + ahead)
        else:
            issue(nxt_ref, next_base, j + ahead - nsub, j + ahead - nsub)

    @pl.when(step == last_step)
    def _():
        for j in range(ahead):
            for r in range(tc):
                row_copy(nxt_ref, next_base, j, r, j).wait()


def _ffn(idx, gates, h2d, w_gate, w_up, w_down, layer, seq, tc=256):
    b, ne, cap = idx.shape
    tc = min(tc, cap // 4)
    nsub = cap // tc
    assert nsub > 2
    idx3 = idx.reshape(b * ne, 1, cap)
    gate3 = gates.reshape(b * ne, 1, cap)
    slot = lambda ei, bi: (bi * ne + ei, 0, 0)
    next_slot = lambda ei, bi: (jnp.where(bi + 1 < b, (bi + 1) * ne + ei, jnp.minimum(ei + 1, ne - 1)), 0, 0)
    wspec = lambda shape: pl.BlockSpec((None, None) + shape, lambda ei, bi: (layer, ei, 0, 0))
    return pl.pallas_call(
        functools.partial(_ffn_kernel, seq=seq, tc=tc, nsub=nsub),
        grid=(ne, b),
        in_specs=[
            pl.BlockSpec((1, 1, cap), slot, memory_space=pltpu.SMEM),
            pl.BlockSpec((1, 1, cap), next_slot, memory_space=pltpu.SMEM),
            pl.BlockSpec((1, 1, cap), slot),
            pl.BlockSpec(memory_space=pl.ANY),
            wspec((D_MODEL, EXPERT_FF)), wspec((D_MODEL, EXPERT_FF)), wspec((EXPERT_FF, D_MODEL)),
        ],
        out_specs=pl.BlockSpec((None, None, cap, D_MODEL), lambda ei, bi: (bi, ei, 0, 0)),
        out_shape=jax.ShapeDtypeStruct((b, ne, cap, D_MODEL), BF16),
        scratch_shapes=[pltpu.VMEM((nsub, tc * (D_MODEL // LANES), LANES), F32), pltpu.SemaphoreType.DMA((nsub,)),
                        pltpu.VMEM((D_MODEL, EXPERT_FF), BF16), pltpu.VMEM((D_MODEL, EXPERT_FF), BF16),
                        pltpu.VMEM((EXPERT_FF, D_MODEL), BF16)],
        compiler_params=pltpu.CompilerParams(dimension_semantics=("arbitrary", "arbitrary"),
                                             vmem_limit_bytes=VMEM_LIMIT, disable_bounds_checks=True),
        name="ffn",
    )(idx3, idx3, gate3, h2d, w_gate, w_up, w_down)


_CTM = 256
_CWIN = 64
_CALIGN = 16


def _combine_kernel(cb_ref, x_ref, sp_ref, y_hbm, g_ref, o_ref, ybuf, xbuf, sem, xsem, *,
                    final, tiles_per_seq, nblk, cap):
    i = pl.program_id(0)
    n_tiles = pl.num_programs(0)
    b = i // tiles_per_seq
    slot = i % 2

    def window(tile, e):
        tb = tile // tiles_per_seq
        off = (tb * N_EXPERTS + e) * (nblk + 1) + (tile % tiles_per_seq) * (_CTM // LANES)
        s0 = cb_ref[off]
        s1 = cb_ref[off + _CTM // LANES]
        start = jnp.minimum((s0 // _CALIGN) * _CALIGN, cap - _CWIN)
        return s1, pl.multiple_of(start, _CALIGN)

    def fetch(tile, e, start, buf_slot):
        return pltpu.make_async_copy(y_hbm.at[tile // tiles_per_seq, e, pl.ds(start, _CWIN), :],
                                     ybuf.at[buf_slot, pl.ds(e * _CWIN, _CWIN), :], sem.at[buf_slot])

    def fetch_all(tile, buf_slot):
        for e in range(N_EXPERTS):
            fetch(tile, e, window(tile, e)[1], buf_slot).start()

    @pl.when(i == 0)
    def _():
        fetch_all(i, slot)

    @pl.when(i + 1 < n_tiles)
    def _():
        fetch_all(i + 1, 1 - slot)

    wins = [window(i, e) for e in range(N_EXPERTS)]
    pad = jnp.full((LANES - N_EXPERTS, LANES), -1.0, F32)
    sp_t = jnp.concatenate([jnp.concatenate([sp_ref[hf], pad], axis=0).T for hf in range(_CTM // LANES)],
                           axis=0)
    lane = lax.broadcasted_iota(I32, (1, _CWIN), 1).astype(F32)
    lane2 = lax.broadcasted_iota(I32, (1, LANES), 1)
    per_tile = LANES // _CWIN
    hits = []
    for e0 in range(0, N_EXPERTS, per_tile):
        rel = sp_t[:, e0:e0 + 1] - wins[e0][1].astype(F32)
        for k in range(1, per_tile):
            rel = jnp.where(lane2 < k * _CWIN, rel,
                            sp_t[:, e0 + k:e0 + k + 1] - (wins[e0 + k][1] - k * _CWIN).astype(F32))
        hits.append(jnp.where(rel == lane2.astype(F32), 1.0, 0.0).astype(BF16))
    for e in range(N_EXPERTS):
        fetch(i, e, wins[e][1], slot).wait()
    o_ref[...] = x_ref[...] + jnp.dot(jnp.concatenate(hits, axis=1), ybuf[slot], preferred_element_type=F32)
    for e in range(N_EXPERTS):
        s1, start = wins[e]
        col = sp_t[:, e:e + 1]

        def extra(k, carry, e=e, s1=s1, start=start, col=col):
            lo = start + (k + 1) * _CWIN
            st = pl.multiple_of(jnp.minimum(lo, cap - _CWIN), _CALIGN)
            cp = pltpu.make_async_copy(y_hbm.at[b, e, pl.ds(st, _CWIN), :], xbuf, xsem)
            cp.start()
            cp.wait()
            hit = jnp.where(jnp.logical_and(col - st.astype(F32) == lane, col >= lo.astype(F32)), 1.0, 0.0)
            o_ref[...] += jnp.dot(hit.astype(BF16), xbuf[...], preferred_element_type=F32)
            return carry

        n_extra = jnp.maximum(s1 - start - 1, 0) // _CWIN
        lax.fori_loop(0, n_extra, extra, 0)
    if final:
        x = o_ref[...]
        o_ref[...] = x * _rms_scale(x) * g_ref[...]


def _combine(x2d, spos, cb, y, g_final, final, seq):
    t = x2d.shape[0]
    batch, ne, cap, _ = y.shape
    nblk = seq // LANES
    cb_full = jnp.concatenate([cb, jnp.full((batch, ne, 1), cap, I32)], axis=-1).reshape(-1)
    spb = _CTM // LANES
    return pl.pallas_call(
        functools.partial(_combine_kernel, final=final, tiles_per_seq=seq // _CTM, nblk=nblk, cap=cap),
        grid_spec=pltpu.PrefetchScalarGridSpec(
            num_scalar_prefetch=1,
            grid=(t // _CTM,),
            in_specs=[
                pl.BlockSpec((_CTM, D_MODEL), lambda i, c: (i, 0)),
                pl.BlockSpec((spb, N_EXPERTS, LANES), lambda i, c: (i, 0, 0)),
                pl.BlockSpec(memory_space=pl.ANY),
                pl.BlockSpec((1, D_MODEL), lambda i, c: (0, 0)),
            ],
            out_specs=pl.BlockSpec((_CTM, D_MODEL), lambda i, c: (i, 0)),
            scratch_shapes=[pltpu.VMEM((2, N_EXPERTS * _CWIN, D_MODEL), BF16), pltpu.VMEM((_CWIN, D_MODEL), BF16),
                            pltpu.SemaphoreType.DMA((2,)), pltpu.SemaphoreType.DMA],
        ),
        out_shape=jax.ShapeDtypeStruct((t, D_MODEL), F32),
        compiler_params=_cparams(("arbitrary",)),
        name="combine",
    )(cb_full, x2d, spos, y, g_final.reshape(1, D_MODEL))


_SLAB_Q0 = 512
_SLAB_K0 = _SLAB_Q0 + D_HEADS * LANES
_SLAB_V0 = _SLAB_K0 + D_KV_HEADS * LANES
_ODD_COLS = _SLAB_V0 + D_KV_HEADS * LANES


def _proj_odd_kernel(x_ref, g_ref, wm_ref, cqn_ref, wq_ref, ckvn_ref, wkv_ref, dqn_ref, dkn_ref,
                     ccq_ref, scq_ref, cck_ref, sck_ref, cdq_ref, sdq_ref, cdk_ref, sdk_ref, loc_ref, lod_ref,
                     qc_ref, kc_ref, vc_ref, qd_ref, kd_ref, vd_ref, stat_ref, *, steps_per_seq):
    lane = lax.broadcasted_iota(I32, (1, LANES), 1)
    one64 = jnp.where(lane == HEAD_DIM, 1.0, 0.0)
    last_lane = lane == LANES - 1
    half_rope = C_ROPE // 2
    stats = [jnp.zeros((1, LANES), F32), jnp.zeros((1, LANES), F32)]
    ones_mat = jnp.ones((LANES, LANES), BF16)

    def with_norm(val, fill, row, col):
        n2 = jnp.dot((val * val).astype(BF16), ones_mat, preferred_element_type=F32)
        stats[row] = jnp.where(lane == col, jnp.maximum(jnp.max(n2, axis=0, keepdims=True), stats[row]), stats[row])
        return jnp.where(last_lane, fill, val).astype(BF16)

    def head_norm(xg, gn_ref):
        ss = jnp.sum(xg * xg, axis=-1, keepdims=True) * (1.0 / HEAD_DIM)
        return xg * lax.rsqrt(ss + EPS) * gn_ref[...]

    lo_c = loc_ref[...] > 0.0
    lo_d = lod_ref[...] > 0.0
    n_chunks = 2
    rows_per = x_ref.shape[0] // n_chunks
    for c in range(n_chunks):
        rows = slice(c * rows_per, (c + 1) * rows_per)
        tab = lambda *refs: [r[rows, :] for r in refs]
        x = x_ref[rows, :]
        y = (x * _rms_scale(x) * g_ref[...]).astype(BF16)
        pm = jnp.dot(y, wm_ref[...], preferred_element_type=F32)
        cq = pm[:, :C_Q_RANK]
        cqn = (cq * _rms_scale(cq) * cqn_ref[...]).astype(BF16)
        qc = jnp.dot(cqn, wq_ref[...], preferred_element_type=F32)
        ckv = pm[:, C_Q_RANK:C_Q_RANK + C_KV_RANK]
        ckvn = (ckv * _rms_scale(ckv) * ckvn_ref[...]).astype(BF16)
        kv = jnp.dot(ckvn, wkv_ref[...], preferred_element_type=F32)
        kr = _rope3(pm[:, C_Q_RANK + C_KV_RANK:_SLAB_Q0], *tab(cck_ref, sck_ref), lo_c, half_rope)
        t_cq = tab(ccq_ref, scq_ref) + [lo_c]
        for h in range(C_HEADS):
            sl = slice(h * LANES, (h + 1) * LANES)
            qc_ref[rows, sl] = with_norm(_rope3(qc[:, sl], *t_cq, half_rope), 1.0, 1, h)
            kc_ref[rows, sl] = with_norm(kv[:, sl] + kr, -1.0, 0, h)
            vc_ref[rows, sl] = (kv[:, C_HEADS * LANES + h * LANES:C_HEADS * LANES + (h + 1) * LANES]
                                + one64).astype(BF16)
        t_dq = tab(cdq_ref, sdq_ref) + [lo_d]
        t_dk = tab(cdk_ref, sdk_ref) + [lo_d]
        for g in range(D_HEADS):
            xg = pm[:, _SLAB_Q0 + g * LANES:_SLAB_Q0 + (g + 1) * LANES]
            qd_ref[rows, g * LANES:(g + 1) * LANES] = with_norm(
                _rope3(head_norm(xg, dqn_ref), *t_dq, HEAD_DIM // 4), 1.0, 1, C_HEADS + g)
        for g in range(D_KV_HEADS):
            sl = slice(g * LANES, (g + 1) * LANES)
            xg = pm[:, _SLAB_K0 + g * LANES:_SLAB_K0 + (g + 1) * LANES]
            kd_ref[rows, sl] = with_norm(_rope3(head_norm(xg, dkn_ref), *t_dk, HEAD_DIM // 4), -1.0, 0, C_HEADS + g)
            vd_ref[rows, sl] = (pm[:, _SLAB_V0 + g * LANES:_SLAB_V0 + (g + 1) * LANES] + one64).astype(BF16)

    new = jnp.concatenate(stats + [jnp.zeros((6, LANES), F32)], axis=0)

    @pl.when(pl.program_id(0) % steps_per_seq == 0)
    def _():
        stat_ref[...] = new

    @pl.when(pl.program_id(0) % steps_per_seq != 0)
    def _():
        stat_ref[...] = jnp.maximum(stat_ref[...], new)


def _slabs(w, n_heads, width, lane_off=0):
    k = w.shape[0]
    w3 = w.reshape(k, n_heads, width)
    w3 = jnp.pad(w3, ((0, 0), (0, 0), (lane_off, LANES - width - lane_off)))
    return w3.reshape(k, n_heads * LANES)


def _axial_tables(row, col, scale):
    half = HEAD_DIM // 2
    cr, sr, lor = _rope_tables(row, D_THETA, half, 0, LANES, scale)
    cc, sc, loc = _rope_tables(col, D_THETA, half, half, LANES, scale)
    lane = jnp.arange(LANES)[None, :]
    return jnp.where(lane < half, cr, cc), sr + sc, lor + loc


def _proj_odd(x2d, seq, g_mix, w_in, cq_norm, w_cq_up, ckv_norm, w_ckv_up, dq_norm, dk_norm, tm=512):
    t = x2d.shape[0]
    nblk = seq // tm
    o1 = C_Q_RANK
    o2 = o1 + C_KV_RANK
    o3 = o2 + C_ROPE
    o4 = o3 + D_HEADS * HEAD_DIM
    o5 = o4 + D_KV_HEADS * HEAD_DIM
    wm = jnp.concatenate([
        w_in[:, :o2],
        _slabs(w_in[:, o2:o3], 1, C_ROPE, C_NOPE),
        _slabs(w_in[:, o3:o4], D_HEADS, HEAD_DIM),
        _slabs(w_in[:, o4:o5], D_KV_HEADS, HEAD_DIM),
        _slabs(w_in[:, o5:], D_KV_HEADS, HEAD_DIM),
    ], axis=1).astype(BF16)
    assert wm.shape[1] == _ODD_COLS
    wq = _slabs(w_cq_up, C_HEADS, C_NOPE + C_ROPE).astype(BF16)
    kv3 = w_ckv_up.reshape(C_KV_RANK, C_HEADS, 2 * HEAD_DIM)
    wkv = jnp.concatenate([
        _slabs(kv3[:, :, :C_NOPE].reshape(C_KV_RANK, -1), C_HEADS, C_NOPE),
        _slabs(kv3[:, :, C_NOPE:].reshape(C_KV_RANK, -1), C_HEADS, HEAD_DIM),
    ], axis=1).astype(BF16)
    pad64 = lambda g: jnp.pad(g, (0, LANES - HEAD_DIM)).reshape(1, LANES)

    pos = jnp.arange(seq, dtype=I32)
    row_pos = pos // GRID_W
    col_pos = pos % GRID_W
    c_scale = (C_NOPE + C_ROPE) ** -0.5 * LOG2E
    d_scale = HEAD_DIM ** -0.5 * LOG2E
    ccq, scq, lo_c = _rope_tables(pos, ROPE_THETA, C_ROPE, C_NOPE, LANES, c_scale)
    cck, sck, _ = _rope_tables(pos, ROPE_THETA, C_ROPE, C_NOPE, LANES, 1.0)
    cdq, sdq, lo_d = _axial_tables(row_pos, col_pos, d_scale)
    cdk, sdk, _ = _axial_tables(row_pos, col_pos, 1.0)
    tabs = (ccq, scq, cck, sck, cdq, sdq, cdk, sdk)

    row = lambda i: (i, 0)
    full = lambda i: (0, 0)
    tspec = pl.BlockSpec((tm, LANES), lambda i: (i % nblk, 0))
    wide = C_HEADS * LANES
    kvw = D_KV_HEADS * LANES
    return pl.pallas_call(
        functools.partial(_proj_odd_kernel, steps_per_seq=nblk),
        grid=(t // tm,),
        in_specs=[
            pl.BlockSpec((tm, D_MODEL), row),
            pl.BlockSpec((1, D_MODEL), full),
            pl.BlockSpec(wm.shape, full),
            pl.BlockSpec((1, C_Q_RANK), full),
            pl.BlockSpec(wq.shape, full),
            pl.BlockSpec((1, C_KV_RANK), full),
            pl.BlockSpec(wkv.shape, full),
            pl.BlockSpec((1, LANES), full),
            pl.BlockSpec((1, LANES), full),
        ] + [tspec] * 8 + [pl.BlockSpec((1, LANES), full)] * 2,
        out_specs=[
            pl.BlockSpec((tm, wide), row), pl.BlockSpec((tm, wide), row), pl.BlockSpec((tm, wide), row),
            pl.BlockSpec((tm, wide), row), pl.BlockSpec((tm, kvw), row), pl.BlockSpec((tm, kvw), row),
            pl.BlockSpec((None, 8, LANES), lambda i: (i // nblk, 0, 0)),
        ],
        out_shape=[
            jax.ShapeDtypeStruct((t, wide), BF16), jax.ShapeDtypeStruct((t, wide), BF16),
            jax.ShapeDtypeStruct((t, wide), BF16), jax.ShapeDtypeStruct((t, wide), BF16),
            jax.ShapeDtypeStruct((t, kvw), BF16), jax.ShapeDtypeStruct((t, kvw), BF16),
            jax.ShapeDtypeStruct((t // seq, 8, LANES), F32),
        ],
        compiler_params=_cparams(("arbitrary",)),
        name="proj_odd",
    )(x2d, g_mix.reshape(1, D_MODEL), wm, cq_norm.reshape(1, -1), wq, ckv_norm.reshape(1, -1), wkv,
      pad64(dq_norm), pad64(dk_norm), *tabs, lo_c, lo_d)


def _flash_kernel(q_ref, k_ref, v_ref, o_ref, qs_sc, m_sc, acc_sc, *, group, tq, tk):
    ki = pl.program_id(3)

    @pl.when(ki == 0)
    def _():
        for g in range(group):
            qs_sc[g * tq:(g + 1) * tq, :] = q_ref[:, g * LANES:(g + 1) * LANES]
        m_sc[...] = jnp.full(m_sc.shape, NEG, F32)
        acc_sc[...] = jnp.zeros(acc_sc.shape, F32)

    s = lax.dot_general(qs_sc[...], k_ref[...], _NT, preferred_element_type=F32)
    m_prev = m_sc[...]
    m_new = jnp.maximum(m_prev, jnp.max(s, axis=1, keepdims=True))
    alpha = jnp.exp2(m_prev - m_new)
    p = jnp.exp2(s - jnp.tile(m_new, (1, tk // LANES)))
    acc_sc[...] = alpha * acc_sc[...] + jnp.dot(p.astype(BF16), v_ref[...], preferred_element_type=F32)
    m_sc[...] = m_new

    @pl.when(ki == pl.num_programs(3) - 1)
    def _():
        acc = acc_sc[...]
        o = acc / acc[:, HEAD_DIM:HEAD_DIM + 1]
        for g in range(group):
            o_ref[:, g * LANES:(g + 1) * LANES] = o[g * tq:(g + 1) * tq].astype(BF16)


_FLASH_CHUNK = 2048
_V_ROWS = 80


def _flash_bounded_kernel(bound_ref, q_ref, k_ref, v_ref, o_ref, qs_sc, acc_sc, *, group, tq):
    ki = pl.program_id(3)
    head = pl.program_id(0) * pl.num_programs(1) + pl.program_id(1)

    @pl.when(ki == 0)
    def _():
        fix = jnp.where(lax.broadcasted_iota(I32, (1, LANES), 1) == LANES - 1, bound_ref[head], 1.0)
        for g in range(group):
            qs_sc[g * tq:(g + 1) * tq, :] = (q_ref[:, g * LANES:(g + 1) * LANES].astype(F32) * fix).astype(BF16)
        acc_sc[...] = jnp.zeros(acc_sc.shape, F32)

    chunk = min(_FLASH_CHUNK, k_ref.shape[0])
    n_chunks = k_ref.shape[0] // chunk
    qs = qs_sc[...]

    def scores(c):
        return lax.dot_general(k_ref[c * chunk:(c + 1) * chunk, :], qs, _NT, preferred_element_type=F32)

    def values(c, s_t):
        return lax.dot_general(v_ref[c * chunk:(c + 1) * chunk, :_V_ROWS], jnp.exp2(s_t).astype(BF16),
                               (((0,), (0,)), ((), ())), preferred_element_type=F32)

    acc = acc_sc[...]
    s_prev = scores(0)
    for c in range(1, n_chunks):
        s_next = scores(c)
        acc = acc + values(c - 1, s_prev)
        s_prev = s_next
    acc_sc[...] = acc + values(n_chunks - 1, s_prev)

    @pl.when(ki == pl.num_programs(3) - 1)
    def _():
        acc = acc_sc[...]
        o_t = acc / acc[HEAD_DIM:HEAD_DIM + 1, :]
        o = jnp.concatenate([o_t, jnp.zeros((LANES - _V_ROWS, o_t.shape[1]), F32)], axis=0).T
        for g in range(group):
            o_ref[:, g * LANES:(g + 1) * LANES] = o[g * tq:(g + 1) * tq].astype(BF16)


def _flash(q, k, v, group, nk, bounded, rows=1024, tk=512, tk_bounded=8192):
    b, s, qw = q.shape
    hk = k.shape[2] // LANES
    tq = rows // group
    tk = min(tk, s)
    tkb = min(tk_bounded, s)
    out_shape = jax.ShapeDtypeStruct((b, s, qw), BF16)
    sem = ("parallel", "parallel", "parallel", "arbitrary")

    def running_max(q, k, v, nk):
        qspec = pl.BlockSpec((None, tq, group * LANES), lambda bi, hi, qi, ki: (bi, qi, hi))
        kspec = pl.BlockSpec((None, tk, LANES), lambda bi, hi, qi, ki: (bi, ki, hi))
        return pl.pallas_call(
            functools.partial(_flash_kernel, group=group, tq=tq, tk=tk),
            grid=(b, hk, s // tq, s // tk),
            in_specs=[qspec, kspec, kspec],
            out_specs=qspec,
            out_shape=out_shape,
            scratch_shapes=[pltpu.VMEM((rows, LANES), BF16), pltpu.VMEM((rows, LANES), F32),
                            pltpu.VMEM((rows, LANES), F32)],
            compiler_params=_cparams(sem),
            name="flash",
        )(q, k, v)

    def bound(q, k, v, nk):
        qspec = pl.BlockSpec((None, tq, group * LANES), lambda bi, hi, qi, ki, nkr: (bi, qi, hi))
        kspec = pl.BlockSpec((None, tkb, LANES), lambda bi, hi, qi, ki, nkr: (bi, ki, hi))
        return pl.pallas_call(
            functools.partial(_flash_bounded_kernel, group=group, tq=tq),
            grid_spec=pltpu.PrefetchScalarGridSpec(
                num_scalar_prefetch=1,
                grid=(b, hk, s // tq, s // tkb),
                in_specs=[qspec, kspec, kspec],
                out_specs=qspec,
                scratch_shapes=[pltpu.VMEM((rows, LANES), BF16), pltpu.VMEM((_V_ROWS, rows), F32)],
            ),
            out_shape=out_shape,
            compiler_params=_cparams(sem),
            name="flash_bounded",
        )(nk.reshape(-1), q, k, v)

    return lax.cond(bounded, bound, running_max, q, k, v, nk)


def _moe(x1, h2, aff, batch, seq, w_gate, w_up, w_down, layer, g_final, final):
    idx, gates, spos, cb = _route(aff.reshape(-1, LANES), batch, seq)
    y = _ffn(idx, gates, h2, w_gate, w_up, w_down, layer, seq)
    return _combine(x1, spos.reshape(-1, N_EXPERTS, LANES), cb, y, g_final, final, seq)


def kernel(x, norm_mix, norm_ffn, even_w_in, even_gmlp_norm, even_w_spatial, even_b_spatial, even_w_out,
           odd_w_in, odd_cq_norm, odd_w_cq_up, odd_ckv_norm, odd_w_ckv_up, odd_dq_norm, odd_dk_norm, odd_w_out,
           moe_w_router, moe_w_gate, moe_w_up, moe_w_down, final_norm):
    b, s, d = x.shape
    depth = norm_mix.shape[0]
    x2d = x.reshape(b * s, d)
    for i in range(depth):
        j = i // 2
        last = i == depth - 1
        if i % 2 == 0:
            q, k, v, go = _proj_even(x2d, s, norm_mix[i], even_w_in[j], even_gmlp_norm[j], even_w_spatial[j],
                                     even_b_spatial[j])
            a = _dilated(q.reshape(b, s, A_WIDTH), k.reshape(b, s, A_WIDTH), v.reshape(b, s, A_WIDTH))
            x1, h2, aff = _outproj(x2d, a.reshape(b * s, A_WIDTH), go, even_w_out[j][:A_WIDTH],
                                   even_w_out[j][A_WIDTH:], norm_ffn[i], moe_w_router[i])
        else:
            qc, kc, vc, qd, kd, vd, stat = _proj_odd(x2d, s, norm_mix[i], odd_w_in[j], odd_cq_norm[j],
                                                     odd_w_cq_up[j], odd_ckv_norm[j], odd_w_ckv_up[j],
                                                     odd_dq_norm[j], odd_dk_norm[j])
            grp = D_HEADS // D_KV_HEADS
            k2_c, k2_d = stat[:, 0, :C_HEADS], stat[:, 0, C_HEADS:C_HEADS + D_KV_HEADS]
            q2_c = stat[:, 1, :C_HEADS]
            q2_d = jnp.max(stat[:, 1, C_HEADS:C_HEADS + D_HEADS].reshape(b, D_KV_HEADS, grp), axis=-1)
            bound_c = jnp.sqrt(q2_c * k2_c) * _NORM_MARGIN ** 2
            bound_d = jnp.sqrt(q2_d * k2_d) * _NORM_MARGIN ** 2
            bounded = jnp.maximum(jnp.max(bound_c), jnp.max(bound_d)) <= _MAX_SCORE_BOUND
            r3 = lambda z: z.reshape(b, s, -1)
            oc = _flash(r3(qc), r3(kc), r3(vc), 1, bound_c, bounded)
            od = _flash(r3(qd), r3(kd), r3(vd), grp, bound_d, bounded)
            cw = C_HEADS * HEAD_DIM
            x1, h2, aff = _outproj(x2d, oc.reshape(b * s, -1), od.reshape(b * s, -1),
                                   _slabs(odd_w_out[j][:cw].T, C_HEADS, HEAD_DIM).T,
                                   _slabs(odd_w_out[j][cw:].T, D_HEADS, HEAD_DIM).T,
                                   norm_ffn[i], moe_w_router[i])
        x2d = _moe(x1, h2, aff, b, s, moe_w_gate, moe_w_up, moe_w_down, i, final_norm, last)
    return x2d.reshape(b, s, d)
```
